```python
import math
import jax, jax.numpy as jnp
from jax import lax
import numpy as np

D_MODEL = 1024
BATCH = 8
SEQ = 8192
DEPTH = 2

MIX_WIDTH = D_MODEL
N_MIXERS = 4
GROUP_WIDTH = MIX_WIDTH // N_MIXERS
BLOCK = 128
ROPE_THETA = 500000.0
NEG_INF = -1e30
EPS = 1e-6

DIFF_HEADS = 4
DIFF_DV = GROUP_WIDTH // DIFF_HEADS
DIFF_DK = DIFF_DV // 2
DIFF_ROPE = DIFF_DK // 4

MLA_HEADS = 4
MLA_NOPE = GROUP_WIDTH // MLA_HEADS
MLA_ROPE = MLA_NOPE // 2
MLA_V = GROUP_WIDTH // MLA_HEADS
MLA_Q_LORA = GROUP_WIDTH
MLA_KV_LORA = GROUP_WIDTH // 2

SGU_GROUPS = 4
SGU_CHUNK = 128
SGU_GROUP_DIM = GROUP_WIDTH // SGU_GROUPS

DIL_HEADS = 4
DIL_HD = GROUP_WIDTH // DIL_HEADS
DIL_ROPE = DIL_HD // 4
DIL_BRANCHES = ((128, 1), (512, 4), (2048, 16))

PEER_HEADS = 8
PEER_TOPK = 16
N_KEYS = 128
N_EXPERTS = N_KEYS * N_KEYS
PEER_DQ = 128
PEER_CHUNK = 128

A_QK = DIFF_HEADS * 2 * DIFF_DK
A_V = DIFF_HEADS * DIFF_DV
IN_SPLIT_SIZES = (A_QK, A_QK, A_V,
                  MLA_Q_LORA, MLA_KV_LORA, MLA_ROPE,
                  2 * GROUP_WIDTH,
                  GROUP_WIDTH, GROUP_WIDTH, GROUP_WIDTH)
IN_WIDTH = sum(IN_SPLIT_SIZES)

kernel_name = 'hybrid_parallel_heads_peer'


def _split_cols(z, sizes):
    out, start = [], 0
    for n in sizes:
        out.append(z[..., start:start + n])
        start += n
    return out


def _rmsnorm(x, g):
    xf = x.astype(jnp.float32)
    y = xf * lax.rsqrt(jnp.mean(xf * xf, axis=-1, keepdims=True) + EPS)
    return (y * g.astype(jnp.float32)).astype(x.dtype)


def _rope(x, n_rot):
    S = x.shape[1]
    half = n_rot // 2
    pos = jnp.arange(S, dtype=jnp.float32)
    inv = ROPE_THETA ** (-jnp.arange(half, dtype=jnp.float32) * (2.0 / n_rot))
    ang = pos[:, None] * inv[None, :]
    cos = jnp.cos(ang)[None, :, None, :]
    sin = jnp.sin(ang)[None, :, None, :]
    xf = x.astype(jnp.float32)
    x1 = xf[..., :half]
    x2 = xf[..., half:n_rot]
    out = jnp.concatenate([x1 * cos - x2 * sin, x2 * cos + x1 * sin, xf[..., n_rot:]], axis=-1)
    return out.astype(x.dtype)


def _to_blocks(a):
    B, S = a.shape[:2]
    return jnp.swapaxes(a.reshape(B, S // BLOCK, BLOCK, *a.shape[2:]), 0, 1)


def _from_blocks(a):
    nb, B = a.shape[:2]
    a = jnp.swapaxes(a, 0, 1)
    return a.reshape(B, nb * BLOCK, *a.shape[3:])


def _diff_attention(q, k, v, lam):
    S = q.shape[1]
    nb = S // BLOCK
    scale = DIFF_DK ** -0.5
    kpos = jnp.arange(S)

    def block(args):
        i, qi = args
        qpos = i * BLOCK + jnp.arange(BLOCK)
        s = jnp.einsum('bqhmd,bkhmd->bhmqk', qi, k).astype(jnp.float32) * scale
        s = jnp.where(kpos[None, :] <= qpos[:, None], s, NEG_INF)
        p = jax.nn.softmax(s, axis=-1)
        p = p[:, :, 0] - lam * p[:, :, 1]
        return jnp.einsum('bhqk,bkhd->bqhd', p.astype(v.dtype), v)

    o = lax.map(block, (jnp.arange(nb), _to_blocks(q)))
    return _from_blocks(o)


def _causal_attention(q, k, v, scale):
    S = q.shape[1]
    nb = S // BLOCK
    kpos = jnp.arange(S)

    def block(args):
        i, qi = args
        qpos = i * BLOCK + jnp.arange(BLOCK)
        s = jnp.einsum('bqhd,bkhd->bhqk', qi, k).astype(jnp.float32) * scale
        s = jnp.where(kpos[None, :] <= qpos[:, None], s, NEG_INF)
        p = jax.nn.softmax(s, axis=-1)
        return jnp.einsum('bhqk,bkhd->bqhd', p.astype(v.dtype), v)

    o = lax.map(block, (jnp.arange(nb), _to_blocks(q)))
    return _from_blocks(o)


def _spatial_gating(zc, v_norm, w_s, b_s):
    B, S, _ = zc.shape
    u, v = zc[..., :GROUP_WIDTH], zc[..., GROUP_WIDTH:]
    v = _rmsnorm(v, v_norm)
    v = v.reshape(B, S // SGU_CHUNK, SGU_CHUNK, SGU_GROUPS, SGU_GROUP_DIM)
    w = w_s * jnp.tril(jnp.ones((SGU_CHUNK, SGU_CHUNK), w_s.dtype))
    sv = jnp.einsum('gts,bcsge->bctge', w, v) + b_s.T[:, :, None]
    return u * sv.reshape(B, S, GROUP_WIDTH)


def _dilated_attention(q, k, v):
    S = q.shape[1]
    nb = S // BLOCK
    scale = DIL_HD ** -0.5

    def block(args):
        i, qi = args
        qpos = i * BLOCK + jnp.arange(BLOCK)
        outs, lses = [], []
        for window, dil in DIL_BRANCHES:
            n = window // dil + 1
            idx = qpos[:, None] - dil * jnp.arange(n)[None, :]
            valid = idx >= 0
            idx = jnp.maximum(idx, 0)
            kg = k[:, idx]
            vg = v[:, idx]
            s = jnp.einsum('bqhe,bqjhe->bqhj', qi, kg).astype(jnp.float32) * scale
            s = jnp.where(valid[None, :, None, :], s, NEG_INF)
            m = jnp.max(s, axis=-1, keepdims=True)
            e = jnp.exp(s - m)
            den = jnp.sum(e, axis=-1, keepdims=True)
            outs.append(jnp.einsum('bqhj,bqjhe->bqhe', (e / den).astype(v.dtype), vg))
            lses.append((m + jnp.log(den))[..., 0])
        wts = jax.nn.softmax(jnp.stack(lses, axis=0), axis=0)
        o = jnp.sum(wts[..., None] * jnp.stack(outs, axis=0).astype(jnp.float32), axis=0)
        return o.astype(v.dtype)

    o = lax.map(block, (jnp.arange(nb), _to_blocks(q)))
    return _from_blocks(o)


def _peer(h, w_q, sub_keys, exp_u, exp_v):
    B, S, D = h.shape
    T = B * S
    half = PEER_DQ // 2
    hc = h.reshape(T // PEER_CHUNK, PEER_CHUNK, D)

    def chunk(xc):
        q = (xc @ w_q).reshape(PEER_CHUNK, PEER_HEADS, 2, half)
        s = jnp.einsum('thce,cne->thcn', q, sub_keys).astype(jnp.float32)
        sv, si = lax.top_k(s, PEER_TOPK)
        cand = (sv[:, :, 0, :, None] + sv[:, :, 1, None, :]).reshape(PEER_CHUNK, PEER_HEADS, PEER_TOPK * PEER_TOPK)
        cidx = (si[:, :, 0, :, None] * N_KEYS + si[:, :, 1, None, :]).reshape(PEER_CHUNK, PEER_HEADS, PEER_TOPK * PEER_TOPK)
        top, pos = lax.top_k(cand, PEER_TOPK)
        eidx = jnp.take_along_axis(cidx, pos, axis=-1)
        g = jax.nn.softmax(top, axis=-1)
        a = jax.nn.gelu(jnp.einsum('td,thkd->thk', xc, exp_u[eidx]).astype(jnp.float32))
        return jnp.einsum('thk,thkd->td', (g * a).astype(exp_v.dtype), exp_v[eidx])

    return lax.map(chunk, hc).reshape(B, S, D)


def setup_inputs(seed: int = 0) -> dict:
    key = jax.random.key(seed)
    ks = jax.random.split(key, 24)

    def nrm(k, shape, scale):
        return jax.random.normal(k, shape, jnp.float32) * scale

    def gain(k, shape):
        return 1.0 + 0.02 * jax.random.normal(k, shape, jnp.float32)

    L = DEPTH
    return {
        'x': nrm(ks[0], (BATCH, SEQ, D_MODEL), 1.0),
        'w_in': nrm(ks[1], (L, D_MODEL, IN_WIDTH), D_MODEL ** -0.5),
        'w_out': nrm(ks[2], (L, MIX_WIDTH, D_MODEL), MIX_WIDTH ** -0.5),
        'norm_mix': gain(ks[3], (L, D_MODEL)),
        'norm_ffn': gain(ks[4], (L, D_MODEL)),
        'diff_lambda': nrm(ks[5], (L, 4, DIFF_DK), 0.1),
        'diff_head_norm': gain(ks[6], (L, A_V)),
        'mla_q_norm': gain(ks[7], (L, MLA_Q_LORA)),
        'mla_kv_norm': gain(ks[8], (L, MLA_KV_LORA)),
        'mla_w_uq': nrm(ks[9], (L, MLA_Q_LORA, MLA_HEADS * (MLA_NOPE + MLA_ROPE)), MLA_Q_LORA ** -0.5),
        'mla_w_ukv': nrm(ks[10], (L, MLA_KV_LORA, MLA_HEADS * (MLA_NOPE + MLA_V)), MLA_KV_LORA ** -0.5),
        'mla_out_norm': gain(ks[11], (L, GROUP_WIDTH)),
        'sgu_v_norm': gain(ks[12], (L, GROUP_WIDTH)),
        'sgu_w': nrm(ks[13], (L, SGU_GROUPS, SGU_CHUNK, SGU_CHUNK), SGU_CHUNK ** -0.5),
        'sgu_b': gain(ks[14], (L, SGU_GROUPS, SGU_CHUNK)),
        'sgu_out_norm': gain(ks[15], (L, GROUP_WIDTH)),
        'dil_out_norm': gain(ks[16], (L, GROUP_WIDTH)),
        'peer_w_q': nrm(ks[17], (L, D_MODEL, PEER_HEADS * PEER_DQ), D_MODEL ** -0.5),
        'peer_sub_keys': nrm(ks[18], (L, 2, N_KEYS, PEER_DQ // 2), (PEER_DQ // 2) ** -0.5),
        'peer_u': nrm(ks[19], (L, N_EXPERTS, D_MODEL), D_MODEL ** -0.5),
        'peer_v': nrm(ks[20], (L, N_EXPERTS, D_MODEL), PEER_HEADS ** -0.5),
        'final_norm': gain(ks[21], (D_MODEL,)),
    }


def reference(x, w_in, w_out, norm_mix, norm_ffn, diff_lambda, diff_head_norm,
              mla_q_norm, mla_kv_norm, mla_w_uq, mla_w_ukv, mla_out_norm,
              sgu_v_norm, sgu_w, sgu_b, sgu_out_norm, dil_out_norm,
              peer_w_q, peer_sub_keys, peer_u, peer_v, final_norm):
    B, S, _ = x.shape
    for l in range(DEPTH):
        h = _rmsnorm(x, norm_mix[l])
        z = h @ w_in[l]
        aq, ak, av, cq, ckv, kpe, zc, dq, dk, dv = _split_cols(z, IN_SPLIT_SIZES)

        lam_init = 0.8 - 0.6 * math.exp(-0.3 * l)
        lf = diff_lambda[l].astype(jnp.float32)
        lam = jnp.exp(jnp.sum(lf[0] * lf[1])) - jnp.exp(jnp.sum(lf[2] * lf[3])) + lam_init
        aq = _rope(aq.reshape(B, S, DIFF_HEADS * 2, DIFF_DK), DIFF_ROPE).reshape(B, S, DIFF_HEADS, 2, DIFF_DK)
        ak = _rope(ak.reshape(B, S, DIFF_HEADS * 2, DIFF_DK), DIFF_ROPE).reshape(B, S, DIFF_HEADS, 2, DIFF_DK)
        oa = _diff_attention(aq, ak, av.reshape(B, S, DIFF_HEADS, DIFF_DV), lam)
        oa = (_rmsnorm(oa, diff_head_norm[l].reshape(DIFF_HEADS, DIFF_DV)) * (1.0 - lam_init)).reshape(B, S, GROUP_WIDTH)

        cq = _rmsnorm(cq, mla_q_norm[l])
        ckv = _rmsnorm(ckv, mla_kv_norm[l])
        q = (cq @ mla_w_uq[l]).reshape(B, S, MLA_HEADS, MLA_NOPE + MLA_ROPE)
        q = jnp.concatenate([q[..., :MLA_NOPE], _rope(q[..., MLA_NOPE:], MLA_ROPE)], axis=-1)
        kv = (ckv @ mla_w_ukv[l]).reshape(B, S, MLA_HEADS, MLA_NOPE + MLA_V)
        k_pe = jnp.broadcast_to(_rope(kpe[:, :, None, :], MLA_ROPE), (B, S, MLA_HEADS, MLA_ROPE))
        k = jnp.concatenate([kv[..., :MLA_NOPE], k_pe], axis=-1)
        ob = _causal_attention(q, k, kv[..., MLA_NOPE:], (MLA_NOPE + MLA_ROPE) ** -0.5)
        ob = _rmsnorm(ob.reshape(B, S, GROUP_WIDTH), mla_out_norm[l])

        oc = _spatial_gating(jax.nn.gelu(zc), sgu_v_norm[l], sgu_w[l], sgu_b[l])
        oc = _rmsnorm(oc, sgu_out_norm[l])

        dq = _rope(dq.reshape(B, S, DIL_HEADS, DIL_HD), DIL_ROPE)
        dk = _rope(dk.reshape(B, S, DIL_HEADS, DIL_HD), DIL_ROPE)
        od = _dilated_attention(dq, dk, dv.reshape(B, S, DIL_HEADS, DIL_HD))
        od = _rmsnorm(od.reshape(B, S, GROUP_WIDTH), dil_out_norm[l])

        x = x + jnp.concatenate([oa, ob, oc, od], axis=-1) @ w_out[l]

        x = x + _peer(_rmsnorm(x, norm_ffn[l]), peer_w_q[l], peer_sub_keys[l], peer_u[l], peer_v[l])
    return _rmsnorm(x, final_norm)
```

```python
import functools
import math

import numpy as np
import jax
import jax.numpy as jnp
from jax import lax
from jax.experimental import pallas as pl
from jax.experimental.pallas import tpu as pltpu

D_MODEL = 1024
GROUP = 256
ROPE_THETA = 500000.0
NEG = -1e30
EPS = 1e-6
LANES = 128

N_KEYS = 128
PEER_HEADS = 8
PEER_TOP = 16

BF = jnp.bfloat16
F32 = jnp.float32

C_AQ, C_AK, C_AV, C_CQ, C_CKV, C_KPE, C_ZC, C_DQ, C_DK, C_DV, C_END = (
    0, 256, 512, 768, 1024, 1152, 1664, 2176, 2432, 2688, 2944)

VMEM_LIMIT = 56 * 1024 * 1024


def _rms(x, g):
    return x * lax.rsqrt(jnp.mean(x * x, axis=-1, keepdims=True) + EPS) * g


def _nt_dot(a, b):
    return lax.dot_general(a, b, (((1,), (1,)), ((), ())), preferred_element_type=F32)


def _dot(a, b):
    return jnp.dot(a, b, preferred_element_type=F32)


def _rope_tables(seq, n_rot, period, offset):
    half = n_rot // 2
    pos = jnp.arange(seq, dtype=F32)
    inv = ROPE_THETA ** (-jnp.arange(half, dtype=F32) * (2.0 / n_rot))
    ang = pos[:, None] * inv[None, :]
    cos, sin = jnp.cos(ang), jnp.sin(ang)
    g = np.arange(LANES) % period - offset
    rot = (g >= 0) & (g < n_rot)
    idx = np.where(rot, g % half, 0)
    sign = np.where(g < half, -1.0, 1.0).astype(np.float32)
    cos_t = jnp.where(rot[None, :], cos[:, idx], 1.0)
    sin_t = jnp.where(rot[None, :], sin[:, idx] * sign[None, :], 0.0)
    return cos_t.astype(F32), sin_t.astype(F32)


def _rope_apply(x, cos_t, sin_t, n_rot, period, offset):
    half = n_rot // 2
    lane = lax.broadcasted_iota(jnp.int32, (1, LANES), 1)
    first = (lane % period - offset) < half
    outs = []
    for c in range(x.shape[1] // LANES):
        xc = x[:, c * LANES:(c + 1) * LANES]
        fwd = pltpu.roll(xc, LANES - half, 1)
        bwd = pltpu.roll(xc, half, 1)
        outs.append(xc * cos_t + jnp.where(first, fwd, bwd) * sin_t)
    return outs[0] if len(outs) == 1 else jnp.concatenate(outs, axis=1)


def _inproj_body(x_ref, gmix_ref, w_ref, ca_ref, sa_ref, cb_ref, sb_ref, cd_ref, sd_ref,
                 gq_ref, gkv_ref, wuq_ref, wuk_ref, wuv_ref,
                 gsv_ref, sguw_ref, sgub_ref, gso_ref,
                 qa_ref, ka_ref, va_ref, qb_ref, kb_ref, vb_ref, oc_ref,
                 qd_ref, kd_ref, vd_ref, *, tm):
    x = x_ref[0]
    h = _rms(x, gmix_ref[...]).astype(BF)

    def proj(a, b):
        return _dot(h, w_ref[:, a:b])

    ca, sa = ca_ref[...], sa_ref[...]
    aq = _rope_apply(proj(C_AQ, C_AK), ca, sa, 8, 32, 0)
    qa_ref[0] = (aq * (32.0 ** -0.5)).astype(BF)
    ka_ref[0] = _rope_apply(proj(C_AK, C_AV), ca, sa, 8, 32, 0).astype(BF)
    va_ref[0] = proj(C_AV, C_CQ).astype(BF)

    cb, sb = cb_ref[...], sb_ref[...]
    cq = _rms(proj(C_CQ, C_CKV), gq_ref[...]).astype(BF)
    qb = _rope_apply(_dot(cq, wuq_ref[...]), cb, sb, 32, 128, 64)
    qb_ref[0] = (qb * (96.0 ** -0.5)).astype(BF)
    ckv = _rms(proj(C_CKV, C_KPE), gkv_ref[...]).astype(BF)
    kpe = _rope_apply(proj(C_KPE, C_ZC), cb, sb, 32, 128, 64)
    kb_ref[0] = (_dot(ckv, wuk_ref[...]) + kpe).astype(BF)
    vb_ref[0] = _dot(ckv, wuv_ref[...]).astype(BF)

    zc = jax.nn.gelu(proj(C_ZC, C_DQ))
    u = zc[:, :GROUP]
    vn = _rms(zc[:, GROUP:], gsv_ref[...]).astype(BF)
    r = lax.broadcasted_iota(jnp.int32, (128, 128), 0)
    c = lax.broadcasted_iota(jnp.int32, (128, 128), 1)
    wcat = jnp.concatenate(
        [jnp.where(r >= c, sguw_ref[g], 0.0).astype(BF) for g in range(4)], axis=1)
    lane = lax.broadcasted_iota(jnp.int32, (1, GROUP), 1)
    bias = sgub_ref[...]
    gso = gso_ref[...]
    for ch in range(tm // 128):
        vc = vn[ch * 128:(ch + 1) * 128]
        vst = jnp.concatenate(
            [jnp.where(lane // 64 == g, vc, jnp.zeros_like(vc)) for g in range(4)], axis=0)
        sv = _dot(wcat, vst) + bias
        oc = u[ch * 128:(ch + 1) * 128] * sv
        oc_ref[0, ch * 128:(ch + 1) * 128, :] = _rms(oc, gso).astype(BF)

    cd, sd = cd_ref[...], sd_ref[...]
    dq = _rope_apply(proj(C_DQ, C_DK), cd, sd, 16, 64, 0)
    qd_ref[0] = (dq * 0.125).astype(BF)
    kd_ref[0] = _rope_apply(proj(C_DK, C_DV), cd, sd, 16, 64, 0).astype(BF)
    vd_ref[0] = proj(C_DV, C_END).astype(BF)


def _inproj(x, gmix, w_cat, tabs, gq, gkv, wuq, wuk, wuv, gsv, sguw, sgub, gso, *, tm):
    B, S, D = x.shape
    ns = S // tm
    full = lambda shape: pl.BlockSpec(shape, lambda s, b: (0,) * len(shape))
    tab = pl.BlockSpec((tm, LANES), lambda s, b: (s, 0))
    seq = lambda w: pl.BlockSpec((1, tm, w), lambda s, b: (b, s, 0))
    out_w = [256, 256, 256, 512, 512, 256, 256, 256, 256, 256]
    return pl.pallas_call(
        functools.partial(_inproj_body, tm=tm),
        grid=(ns, B),
        in_specs=[seq(D), full((1, D)), full((D, C_END))] + [tab] * 6 + [
            full((1, 256)), full((1, 128)), full((256, 512)), full((128, 512)), full((128, 256)),
            full((1, 256)), full((4, 128, 128)), full((128, 256)), full((1, 256))],
        out_specs=[seq(w) for w in out_w],
        out_shape=[jax.ShapeDtypeStruct((B, S, w), BF) for w in out_w],
        compiler_params=pltpu.CompilerParams(
            dimension_semantics=("arbitrary", "arbitrary"), vmem_limit_bytes=VMEM_LIMIT),
        name="inproj",
    )(x, gmix, w_cat, *tabs, gq, gkv, wuq, wuk, wuv, gsv, sguw, sgub, gso)


def _head_expand(cols, lane, width):
    out = cols[-1]
    for h in range(len(cols) - 2, -1, -1):
        out = jnp.where(lane // width == h, cols[h], out)
    return out


def _vstack(v, lane):
    return jnp.concatenate(
        [jnp.where(lane // 64 == h, v, jnp.zeros_like(v)) for h in range(4)], axis=0)


def _attn_a_body(lam_ref, gh_ref, q_ref, k_ref, v_ref, o_ref,
                 qs_ref, m_ref, l_ref, acc_ref, *, t, lam_init):
    qi = pl.program_id(1)
    ki = pl.program_id(2)
    lane = lax.broadcasted_iota(jnp.int32, (1, GROUP), 1)

    @pl.when(ki == 0)
    def _init():
        m_ref[...] = jnp.full(m_ref.shape, NEG, F32)
        l_ref[...] = jnp.zeros(l_ref.shape, F32)
        acc_ref[...] = jnp.zeros(acc_ref.shape, F32)
        q = q_ref[0]
        for j in range(8):
            qs_ref[j] = jnp.where(lane // 32 == j, q, jnp.zeros_like(q))

    def step(masked):
        k = k_ref[0]
        vst = _vstack(v_ref[0], lane)
        if masked:
            rr = lax.broadcasted_iota(jnp.int32, (t, t), 0)
            cc = lax.broadcasted_iota(jnp.int32, (t, t), 1)
            causal = rr >= cc
        for mp in range(2):
            es, alphas = [], []
            for h in range(4):
                j = 2 * h + mp
                s = _nt_dot(qs_ref[j], k)
                if masked:
                    s = jnp.where(causal, s, NEG)
                m_prev = m_ref[j]
                m_new = jnp.maximum(m_prev, jnp.max(s, axis=-1, keepdims=True))
                alpha = jnp.exp(m_prev - m_new)
                e = jnp.exp(s - m_new)
                l_ref[j] = alpha * l_ref[j] + jnp.sum(e, axis=-1, keepdims=True)
                m_ref[j] = m_new
                es.append(e.astype(BF))
                alphas.append(alpha)
            pv = _dot(jnp.concatenate(es, axis=1), vst)
            acc_ref[mp] = acc_ref[mp] * _head_expand(alphas, lane, 64) + pv

    @pl.when(ki < qi)
    def _off():
        step(False)

    @pl.when(ki == qi)
    def _diag():
        step(True)
        lf = lam_ref[...]
        lam = (jnp.exp(jnp.sum(lf[0:1] * lf[1:2], axis=-1, keepdims=True))
               - jnp.exp(jnp.sum(lf[2:3] * lf[3:4], axis=-1, keepdims=True)) + lam_init)
        il0 = _head_expand([1.0 / l_ref[2 * h] for h in range(4)], lane, 64)
        il1 = _head_expand([1.0 / l_ref[2 * h + 1] for h in range(4)], lane, 64)
        o = acc_ref[0] * il0 - lam * (acc_ref[1] * il1)
        o2 = o * o
        inv = jnp.zeros_like(o)
        for h in range(4):
            hm = lane // 64 == h
            ms = jnp.sum(jnp.where(hm, o2, 0.0), axis=-1, keepdims=True) * (1.0 / 64)
            inv = jnp.where(hm, lax.rsqrt(ms + EPS), inv)
        o_ref[0] = (o * inv * gh_ref[...] * (1.0 - lam_init)).astype(BF)


def _attn_a(lam_p, gh, q, k, v, *, t, lam_init):
    B, S, _ = q.shape
    n = S // t
    return pl.pallas_call(
        functools.partial(_attn_a_body, t=t, lam_init=lam_init),
        grid=(B, n, n),
        in_specs=[pl.BlockSpec((4, 32), lambda b, i, j: (0, 0)),
                  pl.BlockSpec((1, GROUP), lambda b, i, j: (0, 0)),
                  pl.BlockSpec((1, t, GROUP), lambda b, i, j: (b, i, 0)),
                  pl.BlockSpec((1, t, GROUP), lambda b, i, j: (b, jnp.minimum(i, j), 0)),
                  pl.BlockSpec((1, t, GROUP), lambda b, i, j: (b, jnp.minimum(i, j), 0))],
        out_specs=pl.BlockSpec((1, t, GROUP), lambda b, i, j: (b, i, 0)),
        out_shape=jax.ShapeDtypeStruct((B, S, GROUP), BF),
        scratch_shapes=[pltpu.VMEM((8, t, GROUP), BF),
                        pltpu.VMEM((8, t, 1), F32),
                        pltpu.VMEM((8, t, 1), F32),
                        pltpu.VMEM((2, t, GROUP), F32)],
        compiler_params=pltpu.CompilerParams(
            dimension_semantics=("arbitrary", "arbitrary", "arbitrary"),
            vmem_limit_bytes=VMEM_LIMIT),
        name="attn_diff",
    )(lam_p, gh, q, k, v)


def _attn_b_body(g_ref, q_ref, k_ref, v_ref, o_ref, m_ref, l_ref, acc_ref, *, t):
    qi = pl.program_id(1)
    ki = pl.program_id(2)
    lane = lax.broadcasted_iota(jnp.int32, (1, GROUP), 1)

    @pl.when(ki == 0)
    def _init():
        m_ref[...] = jnp.full(m_ref.shape, NEG, F32)
        l_ref[...] = jnp.zeros(l_ref.shape, F32)
        acc_ref[...] = jnp.zeros(acc_ref.shape, F32)

    def step(masked):
        vst = _vstack(v_ref[0], lane)
        if masked:
            rr = lax.broadcasted_iota(jnp.int32, (t, t), 0)
            cc = lax.broadcasted_iota(jnp.int32, (t, t), 1)
            causal = rr >= cc
        es, alphas = [], []
        for h in range(4):
            s = _nt_dot(q_ref[0, :, h * 128:(h + 1) * 128], k_ref[0, :, h * 128:(h + 1) * 128])
            if masked:
                s = jnp.where(causal, s, NEG)
            m_prev = m_ref[h]
            m_new = jnp.maximum(m_prev, jnp.max(s, axis=-1, keepdims=True))
            alpha = jnp.exp(m_prev - m_new)
            e = jnp.exp(s - m_new)
            l_ref[h] = alpha * l_ref[h] + jnp.sum(e, axis=-1, keepdims=True)
            m_ref[h] = m_new
            es.append(e.astype(BF))
            alphas.append(alpha)
        pv = _dot(jnp.concatenate(es, axis=1), vst)
        acc_ref[...] = acc_ref[...] * _head_expand(alphas, lane, 64) + pv

    @pl.when(ki < qi)
    def _off():
        step(False)

    @pl.when(ki == qi)
    def _diag():
        step(True)
        il = _head_expand([1.0 / l_ref[h] for h in range(4)], lane, 64)
        o_ref[0] = _rms(acc_ref[...] * il, g_ref[...]).astype(BF)


def _attn_b(g, q, k, v, *, t):
    B, S, _ = q.shape
    n = S // t
    return pl.pallas_call(
        functools.partial(_attn_b_body, t=t),
        grid=(B, n, n),
        in_specs=[pl.BlockSpec((1, GROUP), lambda b, i, j: (0, 0)),
                  pl.BlockSpec((1, t, 512), lambda b, i, j: (b, i, 0)),
                  pl.BlockSpec((1, t, 512), lambda b, i, j: (b, jnp.minimum(i, j), 0)),
                  pl.BlockSpec((1, t, GROUP), lambda b, i, j: (b, jnp.minimum(i, j), 0))],
        out_specs=pl.BlockSpec((1, t, GROUP), lambda b, i, j: (b, i, 0)),
        out_shape=jax.ShapeDtypeStruct((B, S, GROUP), BF),
        scratch_shapes=[pltpu.VMEM((4, t, 1), F32),
                        pltpu.VMEM((4, t, 1), F32),
                        pltpu.VMEM((t, GROUP), F32)],
        compiler_params=pltpu.CompilerParams(
            dimension_semantics=("arbitrary", "arbitrary", "arbitrary"),
            vmem_limit_bytes=VMEM_LIMIT),
        name="attn_mla",
    )(g, q, k, v)


def _swa_body(q_ref, kp_ref, vp_ref, kc_ref, vc_ref, o_ref, lse_ref, *, tq):
    i = pl.program_id(1)
    lane = lax.broadcasted_iota(jnp.int32, (1, GROUP), 1)
    q = q_ref[0]
    kw = jnp.concatenate([kp_ref[0], kc_ref[0]], axis=0)
    vw = jnp.concatenate([vp_ref[0], vc_ref[0]], axis=0)
    a = lax.broadcasted_iota(jnp.int32, (128, 256), 0)
    c = lax.broadcasted_iota(jnp.int32, (128, 256), 1)
    d = c - a
    band = jnp.where(d >= 0, d, 1000) <= 128
    cmin = jnp.where(i > 0, 0, 128)
    band0 = jnp.where(c >= cmin, jnp.where(d >= 0, d, 1000), 1000) <= 128
    for sb in range(tq // 128):
        qs = q[sb * 128:(sb + 1) * 128]
        kwin = kw[sb * 128:sb * 128 + 256]
        vst = _vstack(vw[sb * 128:sb * 128 + 256], lane)
        ok = band0 if sb == 0 else band
        es, ils, lses = [], [], []
        for h in range(4):
            s = _nt_dot(jnp.where(lane // 64 == h, qs, jnp.zeros_like(qs)), kwin)
            s = jnp.where(ok, s, NEG)
            m = jnp.max(s, axis=-1, keepdims=True)
            e = jnp.exp(s - m)
            den = jnp.sum(e, axis=-1, keepdims=True)
            es.append(e.astype(BF))
            ils.append(1.0 / den)
            lses.append(m + jnp.log(den))
        pv = _dot(jnp.concatenate(es, axis=1), vst)
        o_ref[0, sb * 128:(sb + 1) * 128, :] = pv * _head_expand(ils, lane, 64)
        lse_ref[0, sb * 128:(sb + 1) * 128, :] = _head_expand(lses, lane, 64)


def _swa(q, k, v, *, tq):
    N, L, _ = q.shape
    nb = tq // 128
    cur = pl.BlockSpec((1, tq, GROUP), lambda n, i: (n, i, 0))
    prev = pl.BlockSpec((1, 128, GROUP), lambda n, i: (n, jnp.maximum(i * nb - 1, 0), 0))
    return pl.pallas_call(
        functools.partial(_swa_body, tq=tq),
        grid=(N, L // tq),
        in_specs=[cur, prev, prev, cur, cur],
        out_specs=[cur, cur],
        out_shape=[jax.ShapeDtypeStruct((N, L, GROUP), F32)] * 2,
        compiler_params=pltpu.CompilerParams(
            dimension_semantics=("arbitrary", "arbitrary"), vmem_limit_bytes=VMEM_LIMIT),
        name="swa",
    )(q, k, v, k, v)


def _outproj_body(x_ref, oa_ref, ob_ref, oc_ref, o1_ref, l1_ref, o2_ref, l2_ref, o3_ref, l3_ref,
                  gd_ref, w_ref, y_ref):
    l1, l2, l3 = l1_ref[...], l2_ref[...], l3_ref[...]
    mx = jnp.maximum(jnp.maximum(l1, l2), l3)
    w1, w2, w3 = jnp.exp(l1 - mx), jnp.exp(l2 - mx), jnp.exp(l3 - mx)
    od = (w1 * o1_ref[...] + w2 * o2_ref[...] + w3 * o3_ref[...]) / (w1 + w2 + w3)
    od = _rms(od, gd_ref[...]).astype(BF)
    y = _dot(oa_ref[...], w_ref[0:256, :])
    y += _dot(ob_ref[...], w_ref[256:512, :])
    y += _dot(oc_ref[...], w_ref[512:768, :])
    y += _dot(od, w_ref[768:1024, :])
    y_ref[...] = x_ref[...] + y


def _outproj(x, oa, ob, oc, branches, gd, w_out, *, tm):
    T, D = x.shape
    row = lambda w: pl.BlockSpec((tm, w), lambda i: (i, 0))
    full = lambda shape: pl.BlockSpec(shape, lambda i: (0,) * len(shape))
    return pl.pallas_call(
        _outproj_body,
        grid=(T // tm,),
        in_specs=[row(D)] + [row(GROUP)] * 9 + [full((1, GROUP)), full((D, D))],
        out_specs=row(D),
        out_shape=jax.ShapeDtypeStruct((T, D), F32),
        compiler_params=pltpu.CompilerParams(
            dimension_semantics=("arbitrary",), vmem_limit_bytes=VMEM_LIMIT),
        name="outproj",
    )(x, oa, ob, oc, *branches, gd, w_out)


PEER_EB = 1024
N_CAND = PEER_TOP + 1


def _peer_body(x_ref, g_ref, wqt_ref, kpad_ref, u_ref, vt_ref, gf_ref, o_ref,
               ht_ref, st_ref, e2_ref, th_ref, cw_ref, at_ref, hw_ref, acc_ref,
               *, tt, final):
    e = pl.program_id(1)
    nch = tt // LANES

    @pl.when(e == 0)
    def _prologue():
        h = _rms(x_ref[...], g_ref[...])
        ht_ref[...] = h.T.astype(BF)
        qt = _dot(wqt_ref[...], ht_ref[...]).astype(BF)
        st_ref[...] = _dot(kpad_ref[...], qt)
        acc_ref[...] = jnp.zeros(acc_ref.shape, F32)

        def top_vals(load, n_rows, n_out):
            vals = []
            prev = None
            for _ in range(n_out):
                accs = [None] * 4
                for r in range(n_rows):
                    v = load(r)
                    if prev is not None:
                        v = jnp.where(v < prev, v, NEG)
                    a = accs[r % 4]
                    accs[r % 4] = v if a is None else jnp.maximum(a, v)
                prev = jnp.maximum(jnp.maximum(accs[0], accs[1]), jnp.maximum(accs[2], accs[3]))
                vals.append(prev)
            return vals

        def token_chunk(tc, carry):
            ls = pl.ds(pl.multiple_of(tc * LANES, LANES), LANES)
            v1 = top_vals(lambda r: st_ref[r * 8:(r + 1) * 8, ls], N_KEYS, N_CAND)
            v2 = top_vals(lambda r: st_ref[1024 + r * 8:1024 + (r + 1) * 8, ls], N_KEYS, N_CAND)
            cands = [v1[a] + v2[b] for a in range(N_CAND) for b in range(N_CAND)
                     if (a + 1) * (b + 1) <= N_CAND]
            top = top_vals(lambda r: cands[r], len(cands), N_CAND)
            z = jnp.ones_like(top[0])
            for r in range(1, PEER_TOP):
                z = z + jnp.exp(top[r] - top[0])
            tau = 0.5 * (top[PEER_TOP - 1] + top[PEER_TOP])
            iz = 1.0 / z
            for r in range(N_KEYS):
                s1 = st_ref[r * 8:(r + 1) * 8, ls]
                th_ref[r, :, ls] = tau - s1
                cw_ref[r, :, ls] = jnp.exp(s1 - v1[0]) * iz
            for hh in range(PEER_HEADS):
                rows = slice(2048 + hh * 128, 2048 + (hh + 1) * 128)
                e2_ref[hh * 128:(hh + 1) * 128, ls] = jnp.exp(st_ref[rows, ls] - v2[0][hh:hh + 1, :])
            return carry

        lax.fori_loop(0, nch, token_chunk, 0)

    at_ref[...] = _dot(u_ref[...], ht_ref[...])

    def row_block(r, carry):
        i = e * (PEER_EB // N_KEYS) + r
        th = th_ref[i]
        cw = cw_ref[i]
        rs = pl.ds(pl.multiple_of(r * N_KEYS, N_KEYS), N_KEYS)
        for tc in range(nch):
            ls = slice(tc * LANES, (tc + 1) * LANES)
            w = jnp.zeros((N_KEYS, LANES), F32)
            for hh in range(PEER_HEADS):
                s2 = st_ref[2048 + hh * 128:2048 + (hh + 1) * 128, ls]
                sel = jnp.where(s2 >= th[hh:hh + 1, ls], e2_ref[hh * 128:(hh + 1) * 128, ls], 0.0)
                w = w + sel * cw[hh:hh + 1, ls]
            hw_ref[rs, ls] = (jax.nn.gelu(at_ref[rs, ls]) * w).astype(BF)
        return carry

    lax.fori_loop(0, PEER_EB // N_KEYS, row_block, 0)
    acc_ref[...] += _dot(vt_ref[...], hw_ref[...])

    @pl.when(e == pl.num_programs(1) - 1)
    def _finish():
        y = x_ref[...] + acc_ref[...].T
        if final:
            y = _rms(y, gf_ref[...])
        o_ref[...] = y


def _peer(x, g, wqt, kpad, u, vt, gf, *, tt, final):
    T, D = x.shape
    ne = u.shape[0] // PEER_EB
    const = lambda shape: pl.BlockSpec(shape, lambda i, e: (0,) * len(shape),
                                       pipeline_mode=pl.Buffered(1))
    return pl.pallas_call(
        functools.partial(_peer_body, tt=tt, final=final),
        grid=(T // tt, ne),
        in_specs=[pl.BlockSpec((tt, D), lambda i, e: (i, 0)),
                  const((1, D)), const((D, D)), const((3 * 1024, D)),
                  pl.BlockSpec((PEER_EB, D), lambda i, e: (e, 0)),
                  pl.BlockSpec((D, PEER_EB), lambda i, e: (0, e)),
                  const((1, D))],
        out_specs=pl.BlockSpec((tt, D), lambda i, e: (i, 0)),
        out_shape=jax.ShapeDtypeStruct((T, D), F32),
        scratch_shapes=[pltpu.VMEM((D, tt), BF),
                        pltpu.VMEM((3 * 1024, tt), F32),
                        pltpu.VMEM((1024, tt), F32),
                        pltpu.VMEM((N_KEYS, PEER_HEADS, tt), F32),
                        pltpu.VMEM((N_KEYS, PEER_HEADS, tt), F32),
                        pltpu.VMEM((PEER_EB, tt), F32),
                        pltpu.VMEM((PEER_EB, tt), BF),
                        pltpu.VMEM((D, tt), F32)],
        compiler_params=pltpu.CompilerParams(
            dimension_semantics=("arbitrary", "arbitrary"), vmem_limit_bytes=VMEM_LIMIT),
        name="peer",
    )(x, g, wqt, kpad, u, vt, gf)


def _pack_in_weights(w_in):
    aq, ak, av, cq, ckv, kpe, zc, dq, dk, dv = jnp.split(
        w_in, np.cumsum([256, 256, 256, 256, 128, 32, 512, 256, 256])[:].tolist(), axis=1)
    kpe_rep = jnp.zeros((D_MODEL, 4, 128), w_in.dtype).at[:, :, 64:96].set(kpe[:, None, :])
    return jnp.concatenate(
        [aq, ak, av, cq, ckv, kpe_rep.reshape(D_MODEL, 512), zc, dq, dk, dv], axis=1).astype(BF)


def _pack_mla_weights(w_uq, w_ukv):
    uq = jnp.pad(w_uq.reshape(256, 4, 96), ((0, 0), (0, 0), (0, 32))).reshape(256, 512)
    ukv = w_ukv.reshape(128, 4, 128)
    uk = jnp.pad(ukv[:, :, :64], ((0, 0), (0, 0), (0, 64))).reshape(128, 512)
    uv = ukv[:, :, 64:].reshape(128, 256)
    return uq.astype(BF), uk.astype(BF), uv.astype(BF)


def _pack_sub_keys(sub_keys):
    eye = jnp.eye(PEER_HEADS, dtype=sub_keys.dtype)
    def block(c, key_major):
        k = sub_keys[c]
        sel = jnp.zeros((2,), sub_keys.dtype).at[c].set(1.0)
        if key_major:
            m = jnp.einsum('nd,hg,c->nhgcd', k, eye, sel)
        else:
            m = jnp.einsum('nd,hg,c->hngcd', k, eye, sel)
        return m.reshape(1024, 1024)
    return jnp.concatenate([block(0, True), block(1, True), block(1, False)], axis=0).astype(BF)


def _deinterleave(a, d):
    B, S, W = a.shape
    return a.reshape(B, S // d, d, W).transpose(0, 2, 1, 3).reshape(B * d, S // d, W)


def _interleave(a, d, B):
    N, L, W = a.shape
    return a.reshape(B, d, L, W).transpose(0, 2, 1, 3).reshape(B, L * d, W)


def kernel(x, w_in, w_out, norm_mix, norm_ffn, diff_lambda, diff_head_norm, mla_q_norm, mla_kv_norm, mla_w_uq, mla_w_ukv, mla_out_norm, sgu_v_norm, sgu_w, sgu_b, sgu_out_norm, dil_out_norm, peer_w_q, peer_sub_keys, peer_u, peer_v, final_norm):
    B, S, D = x.shape
    depth = w_in.shape[0]
    T = B * S
    tm = min(512, S)
    ta = min(512, S)
    tt = min(512, T)
    row = lambda a: a.reshape(1, -1)

    tabs = (_rope_tables(S, 8, 32, 0) + _rope_tables(S, 32, 128, 64) + _rope_tables(S, 16, 64, 0))

    for l in range(depth):
        lam_init = 0.8 - 0.6 * math.exp(-0.3 * l)
        w_cat = _pack_in_weights(w_in[l])
        wuq, wuk, wuv = _pack_mla_weights(mla_w_uq[l], mla_w_ukv[l])
        sgub = jnp.repeat(sgu_b[l].T, 64, axis=1)
        qa, ka, va, qb, kb, vb, oc, qd, kd, vd = _inproj(
            x, row(norm_mix[l]), w_cat, tabs, row(mla_q_norm[l]), row(mla_kv_norm[l]),
            wuq, wuk, wuv, row(sgu_v_norm[l]), sgu_w[l], sgub, row(sgu_out_norm[l]), tm=tm)

        oa = _attn_a(diff_lambda[l], row(diff_head_norm[l]), qa, ka, va, t=ta, lam_init=lam_init)
        ob = _attn_b(row(mla_out_norm[l]), qb, kb, vb, t=ta)

        branches = []
        for dil in (1, 4, 16):
            if dil == 1:
                o, lse = _swa(qd, kd, vd, tq=min(512, S))
            else:
                qq, kk, vv = (_deinterleave(a, dil) for a in (qd, kd, vd))
                o, lse = _swa(qq, kk, vv, tq=min(512, S // dil))
                o, lse = _interleave(o, dil, B), _interleave(lse, dil, B)
            branches += [o.reshape(T, GROUP), lse.reshape(T, GROUP)]

        x2 = _outproj(x.reshape(T, D), oa.reshape(T, GROUP), ob.reshape(T, GROUP),
                      oc.reshape(T, GROUP), branches, row(dil_out_norm[l]),
                      w_out[l].astype(BF), tm=min(512, T))

        x2 = _peer(x2, row(norm_ffn[l]), peer_w_q[l].T.astype(BF), _pack_sub_keys(peer_sub_keys[l]),
                   peer_u[l].astype(BF), peer_v[l].T.astype(BF), row(final_norm),
                   tt=tt, final=(l == depth - 1))
        x = x2.reshape(B, S, D)
    return x
```

```python
import functools
import math

import numpy as np
import jax
import jax.numpy as jnp
from jax import lax
from jax.experimental import pallas as pl
from jax.experimental.pallas import tpu as pltpu

D_MODEL = 1024
GROUP = 256
ROPE_THETA = 500000.0
NEG = -1e30
EPS = 1e-6
LANES = 128
LOG2E = math.log2(math.e)

N_KEYS = 128
PEER_HEADS = 8
PEER_TOP = 16

BF = jnp.bfloat16
F32 = jnp.float32

C_AQ, C_AK, C_AV, C_CQ, C_CKV, C_KPE, C_ZC, C_DQ, C_DK, C_DV, C_END = (
    0, 256, 512, 768, 1024, 1152, 1664, 2176, 2432, 2688, 2944)

VMEM_LIMIT = 56 * 1024 * 1024


def _rms(x, g):
    return x * lax.rsqrt(jnp.mean(x * x, axis=-1, keepdims=True) + EPS) * g


def _nt_dot(a, b):
    return lax.dot_general(a, b, (((1,), (1,)), ((), ())), preferred_element_type=F32)


def _dot(a, b):
    return jnp.dot(a, b, preferred_element_type=F32)


def _rope_tables(seq, n_rot, period, offset):
    half = n_rot // 2
    pos = jnp.arange(seq, dtype=F32)
    inv = ROPE_THETA ** (-jnp.arange(half, dtype=F32) * (2.0 / n_rot))
    ang = pos[:, None] * inv[None, :]
    cos, sin = jnp.cos(ang), jnp.sin(ang)
    g = np.arange(LANES) % period - offset
    rot = (g >= 0) & (g < n_rot)
    idx = np.where(rot, g % half, 0)
    sign = np.where(g < half, -1.0, 1.0).astype(np.float32)
    cos_t = jnp.where(rot[None, :], cos[:, idx], 1.0)
    sin_t = jnp.where(rot[None, :], sin[:, idx] * sign[None, :], 0.0)
    return cos_t.astype(F32), sin_t.astype(F32)


def _rope_apply(x, cos_t, sin_t, n_rot, period, offset):
    half = n_rot // 2
    lane = lax.broadcasted_iota(jnp.int32, (1, LANES), 1)
    first = (lane % period - offset) < half
    outs = []
    for c in range(x.shape[1] // LANES):
        xc = x[:, c * LANES:(c + 1) * LANES]
        fwd = pltpu.roll(xc, LANES - half, 1)
        bwd = pltpu.roll(xc, half, 1)
        outs.append(xc * cos_t + jnp.where(first, fwd, bwd) * sin_t)
    return outs[0] if len(outs) == 1 else jnp.concatenate(outs, axis=1)


def _inproj_body(x_ref, gmix_ref, w_ref, ca_ref, sa_ref, cb_ref, sb_ref, cd_ref, sd_ref,
                 gq_ref, gkv_ref, wuq_ref, wuk_ref, wuv_ref,
                 gsv_ref, sguw_ref, sgub_ref, gso_ref,
                 qa_ref, ka_ref, va_ref, qb_ref, kb_ref, vb_ref, oc_ref,
                 qd_ref, kd_ref, vd_ref, *, tm):
    x = x_ref[0]
    h = _rms(x, gmix_ref[...]).astype(BF)

    def proj(a, b):
        return _dot(h, w_ref[:, a:b])

    ca, sa = ca_ref[...], sa_ref[...]
    aq = _rope_apply(proj(C_AQ, C_AK), ca, sa, 8, 32, 0)
    qa_ref[0] = (aq * (32.0 ** -0.5 * LOG2E)).astype(BF)
    ka_ref[0] = _rope_apply(proj(C_AK, C_AV), ca, sa, 8, 32, 0).astype(BF)
    va_ref[0] = proj(C_AV, C_CQ).astype(BF)

    cb, sb = cb_ref[...], sb_ref[...]
    cq = _rms(proj(C_CQ, C_CKV), gq_ref[...]).astype(BF)
    qb = _rope_apply(_dot(cq, wuq_ref[...]), cb, sb, 32, 128, 64)
    qb_ref[0] = (qb * (96.0 ** -0.5 * LOG2E)).astype(BF)
    ckv = _rms(proj(C_CKV, C_KPE), gkv_ref[...]).astype(BF)
    kpe = _rope_apply(proj(C_KPE, C_ZC), cb, sb, 32, 128, 64)
    kb_ref[0] = (_dot(ckv, wuk_ref[...]) + kpe).astype(BF)
    vb_ref[0] = _dot(ckv, wuv_ref[...]).astype(BF)

    zc = jax.nn.gelu(proj(C_ZC, C_DQ))
    u = zc[:, :GROUP]
    vn = _rms(zc[:, GROUP:], gsv_ref[...]).astype(BF)
    r = lax.broadcasted_iota(jnp.int32, (128, 128), 0)
    c = lax.broadcasted_iota(jnp.int32, (128, 128), 1)
    wcat = jnp.concatenate(
        [jnp.where(r >= c, sguw_ref[g], 0.0).astype(BF) for g in range(4)], axis=1)
    lane = lax.broadcasted_iota(jnp.int32, (1, GROUP), 1)
    bias = sgub_ref[...]
    gso = gso_ref[...]
    for ch in range(tm // 128):
        vc = vn[ch * 128:(ch + 1) * 128]
        vst = jnp.concatenate(
            [jnp.where(lane // 64 == g, vc, jnp.zeros_like(vc)) for g in range(4)], axis=0)
        sv = _dot(wcat, vst) + bias
        oc = u[ch * 128:(ch + 1) * 128] * sv
        oc_ref[0, ch * 128:(ch + 1) * 128, :] = _rms(oc, gso).astype(BF)

    cd, sd = cd_ref[...], sd_ref[...]
    dq = _rope_apply(proj(C_DQ, C_DK), cd, sd, 16, 64, 0)
    qd_ref[0] = (dq * 0.125).astype(BF)
    kd_ref[0] = _rope_apply(proj(C_DK, C_DV), cd, sd, 16, 64, 0).astype(BF)
    vd_ref[0] = proj(C_DV, C_END).astype(BF)


def _inproj(x, gmix, w_cat, tabs, gq, gkv, wuq, wuk, wuv, gsv, sguw, sgub, gso, *, tm):
    B, S, D = x.shape
    ns = S // tm
    full = lambda shape: pl.BlockSpec(shape, lambda s, b: (0,) * len(shape))
    tab = pl.BlockSpec((tm, LANES), lambda s, b: (s, 0))
    seq = lambda w: pl.BlockSpec((1, tm, w), lambda s, b: (b, s, 0))
    out_w = [256, 256, 256, 512, 512, 256, 256, 256, 256, 256]
    return pl.pallas_call(
        functools.partial(_inproj_body, tm=tm),
        grid=(ns, B),
        in_specs=[seq(D), full((1, D)), full((D, C_END))] + [tab] * 6 + [
            full((1, 256)), full((1, 128)), full((256, 512)), full((128, 512)), full((128, 256)),
            full((1, 256)), full((4, 128, 128)), full((128, 256)), full((1, 256))],
        out_specs=[seq(w) for w in out_w],
        out_shape=[jax.ShapeDtypeStruct((B, S, w), BF) for w in out_w],
        compiler_params=pltpu.CompilerParams(
            dimension_semantics=("arbitrary", "arbitrary"), vmem_limit_bytes=VMEM_LIMIT),
        name="inproj",
    )(x, gmix, w_cat, *tabs, gq, gkv, wuq, wuk, wuv, gsv, sguw, sgub, gso)


def _head_expand(cols, lane, width):
    out = cols[-1]
    for h in range(len(cols) - 2, -1, -1):
        out = jnp.where(lane // width == h, cols[h], out)
    return out


def _vstack(v, lane):
    return jnp.concatenate(
        [jnp.where(lane // 64 == h, v, jnp.zeros_like(v)) for h in range(4)], axis=0)


def _head_expand_rep(cols, lane128):
    lo = jnp.where(lane128 < 64, cols[0], cols[1])
    hi = jnp.where(lane128 < 64, cols[2], cols[3])
    return jnp.concatenate([lo, hi], axis=1)


def _online_softmax(s, m_ref, l_ref, j):
    m_prev = m_ref[j]
    m_new = jnp.maximum(m_prev, jnp.max(s, axis=-1, keepdims=True))
    alpha = jnp.exp2(m_prev - m_new)
    cols = [jnp.exp2(s[:, c * LANES:(c + 1) * LANES] - m_new) for c in range(s.shape[1] // LANES)]
    rs = cols[0]
    for col in cols[1:]:
        rs = rs + col
    l_ref[j] = alpha * l_ref[j] + jnp.sum(rs, axis=-1, keepdims=True)
    m_ref[j] = m_new
    return [col.astype(BF) for col in cols], alpha


def _attn_a_body(lam_ref, gh_ref, q_ref, k_ref, v_ref, o_ref,
                 qs_ref, m_ref, l_ref, acc_ref, *, t, lam_init):
    qi = pl.program_id(1)
    ki = pl.program_id(2)
    lane = lax.broadcasted_iota(jnp.int32, (1, GROUP), 1)
    lane128 = lax.broadcasted_iota(jnp.int32, (1, LANES), 1)

    @pl.when(ki == 0)
    def _init():
        m_ref[...] = jnp.full(m_ref.shape, NEG, F32)
        l_ref[...] = jnp.zeros(l_ref.shape, F32)
        acc_ref[...] = jnp.zeros(acc_ref.shape, F32)
        q = q_ref[0]
        for j in range(8):
            qs_ref[j] = jnp.where(lane // 32 == j, q, jnp.zeros_like(q))

    def step(masked):
        k = k_ref[0]
        vst = _vstack(v_ref[0], lane)
        if masked:
            rr = lax.broadcasted_iota(jnp.int32, (t, t), 0)
            cc = lax.broadcasted_iota(jnp.int32, (t, t), 1)
            causal = rr >= cc
        for mp in range(2):
            es, alphas = [], []
            for h in range(4):
                j = 2 * h + mp
                s = _nt_dot(qs_ref[j], k)
                if masked:
                    s = jnp.where(causal, s, NEG)
                cols, alpha = _online_softmax(s, m_ref, l_ref, j)
                es += cols
                alphas.append(alpha)
            pv = _dot(jnp.concatenate(es, axis=1), vst)
            acc_ref[mp] = acc_ref[mp] * _head_expand_rep(alphas, lane128) + pv

    @pl.when(ki < qi)
    def _off():
        step(False)

    @pl.when(ki == qi)
    def _diag():
        step(True)
        lf = lam_ref[...]
        lam = (jnp.exp(jnp.sum(lf[0:1] * lf[1:2], axis=-1, keepdims=True))
               - jnp.exp(jnp.sum(lf[2:3] * lf[3:4], axis=-1, keepdims=True)) + lam_init)
        il0 = _head_expand_rep([1.0 / l_ref[2 * h] for h in range(4)], lane128)
        il1 = _head_expand_rep([1.0 / l_ref[2 * h + 1] for h in range(4)], lane128)
        o = acc_ref[0] * il0 - lam * (acc_ref[1] * il1)
        o2 = o * o
        inv = jnp.zeros_like(o)
        for h in range(4):
            hm = lane // 64 == h
            ms = jnp.sum(jnp.where(hm, o2, 0.0), axis=-1, keepdims=True) * (1.0 / 64)
            inv = jnp.where(hm, lax.rsqrt(ms + EPS), inv)
        o_ref[0] = (o * inv * gh_ref[...] * (1.0 - lam_init)).astype(BF)


def _attn_a(lam_p, gh, q, k, v, *, t, lam_init):
    B, S, _ = q.shape
    n = S // t
    return pl.pallas_call(
        functools.partial(_attn_a_body, t=t, lam_init=lam_init),
        grid=(B, n, n),
        in_specs=[pl.BlockSpec((4, 32), lambda b, i, j: (0, 0)),
                  pl.BlockSpec((1, GROUP), lambda b, i, j: (0, 0)),
                  pl.BlockSpec((1, t, GROUP), lambda b, i, j: (b, i, 0)),
                  pl.BlockSpec((1, t, GROUP), lambda b, i, j: (b, jnp.minimum(i, j), 0)),
                  pl.BlockSpec((1, t, GROUP), lambda b, i, j: (b, jnp.minimum(i, j), 0))],
        out_specs=pl.BlockSpec((1, t, GROUP), lambda b, i, j: (b, i, 0)),
        out_shape=jax.ShapeDtypeStruct((B, S, GROUP), BF),
        scratch_shapes=[pltpu.VMEM((8, t, GROUP), BF),
                        pltpu.VMEM((8, t, LANES), F32),
                        pltpu.VMEM((8, t, LANES), F32),
                        pltpu.VMEM((2, t, GROUP), F32)],
        compiler_params=pltpu.CompilerParams(
            dimension_semantics=("arbitrary", "arbitrary", "arbitrary"),
            vmem_limit_bytes=VMEM_LIMIT),
        name="attn_diff",
    )(lam_p, gh, q, k, v)


def _attn_b_body(g_ref, q_ref, k_ref, v_ref, o_ref, m_ref, l_ref, acc_ref, *, t):
    qi = pl.program_id(1)
    ki = pl.program_id(2)
    lane = lax.broadcasted_iota(jnp.int32, (1, GROUP), 1)
    lane128 = lax.broadcasted_iota(jnp.int32, (1, LANES), 1)

    @pl.when(ki == 0)
    def _init():
        m_ref[...] = jnp.full(m_ref.shape, NEG, F32)
        l_ref[...] = jnp.zeros(l_ref.shape, F32)
        acc_ref[...] = jnp.zeros(acc_ref.shape, F32)

    def step(masked):
        vst = _vstack(v_ref[0], lane)
        if masked:
            rr = lax.broadcasted_iota(jnp.int32, (t, t), 0)
            cc = lax.broadcasted_iota(jnp.int32, (t, t), 1)
            causal = rr >= cc
        es, alphas = [], []
        for h in range(4):
            s = _nt_dot(q_ref[0, :, h * 128:(h + 1) * 128], k_ref[0, :, h * 128:(h + 1) * 128])
            if masked:
                s = jnp.where(causal, s, NEG)
            cols, alpha = _online_softmax(s, m_ref, l_ref, h)
            es += cols
            alphas.append(alpha)
        pv = _dot(jnp.concatenate(es, axis=1), vst)
        acc_ref[...] = acc_ref[...] * _head_expand_rep(alphas, lane128) + pv

    @pl.when(ki < qi)
    def _off():
        step(False)

    @pl.when(ki == qi)
    def _diag():
        step(True)
        il = _head_expand_rep([1.0 / l_ref[h] for h in range(4)], lane128)
        o_ref[0] = _rms(acc_ref[...] * il, g_ref[...]).astype(BF)


def _attn_b(g, q, k, v, *, t):
    B, S, _ = q.shape
    n = S // t
    return pl.pallas_call(
        functools.partial(_attn_b_body, t=t),
        grid=(B, n, n),
        in_specs=[pl.BlockSpec((1, GROUP), lambda b, i, j: (0, 0)),
                  pl.BlockSpec((1, t, 512), lambda b, i, j: (b, i, 0)),
                  pl.BlockSpec((1, t, 512), lambda b, i, j: (b, jnp.minimum(i, j), 0)),
                  pl.BlockSpec((1, t, GROUP), lambda b, i, j: (b, jnp.minimum(i, j), 0))],
        out_specs=pl.BlockSpec((1, t, GROUP), lambda b, i, j: (b, i, 0)),
        out_shape=jax.ShapeDtypeStruct((B, S, GROUP), BF),
        scratch_shapes=[pltpu.VMEM((4, t, LANES), F32),
                        pltpu.VMEM((4, t, LANES), F32),
                        pltpu.VMEM((t, GROUP), F32)],
        compiler_params=pltpu.CompilerParams(
            dimension_semantics=("arbitrary", "arbitrary", "arbitrary"),
            vmem_limit_bytes=VMEM_LIMIT),
        name="attn_mla",
    )(g, q, k, v)


def _swa_body(q_ref, kp_ref, vp_ref, kc_ref, vc_ref, o_ref, lse_ref, *, tq):
    i = pl.program_id(1)
    lane = lax.broadcasted_iota(jnp.int32, (1, GROUP), 1)
    q = q_ref[0]
    kw = jnp.concatenate([kp_ref[0], kc_ref[0]], axis=0)
    vw = jnp.concatenate([vp_ref[0], vc_ref[0]], axis=0)
    a = lax.broadcasted_iota(jnp.int32, (128, 256), 0)
    c = lax.broadcasted_iota(jnp.int32, (128, 256), 1)
    d = c - a
    band = jnp.where(d >= 0, d, 1000) <= 128
    cmin = jnp.where(i > 0, 0, 128)
    band0 = jnp.where(c >= cmin, jnp.where(d >= 0, d, 1000), 1000) <= 128
    for sb in range(tq // 128):
        qs = q[sb * 128:(sb + 1) * 128]
        kwin = kw[sb * 128:sb * 128 + 256]
        vst = _vstack(vw[sb * 128:sb * 128 + 256], lane)
        ok = band0 if sb == 0 else band
        es, ils, lses = [], [], []
        for h in range(4):
            s = _nt_dot(jnp.where(lane // 64 == h, qs, jnp.zeros_like(qs)), kwin)
            s = jnp.where(ok, s, NEG)
            m = jnp.max(s, axis=-1, keepdims=True)
            e = jnp.exp(s - m)
            den = jnp.sum(e, axis=-1, keepdims=True)
            es.append(e.astype(BF))
            ils.append(1.0 / den)
            lses.append(m + jnp.log(den))
        pv = _dot(jnp.concatenate(es, axis=1), vst)
        o_ref[0, sb * 128:(sb + 1) * 128, :] = pv * _head_expand(ils, lane, 64)
        lse_ref[0, sb * 128:(sb + 1) * 128, :] = _head_expand(lses, lane, 64)


def _swa(q, k, v, *, tq):
    N, L, _ = q.shape
    nb = tq // 128
    cur = pl.BlockSpec((1, tq, GROUP), lambda n, i: (n, i, 0))
    prev = pl.BlockSpec((1, 128, GROUP), lambda n, i: (n, jnp.maximum(i * nb - 1, 0), 0))
    return pl.pallas_call(
        functools.partial(_swa_body, tq=tq),
        grid=(N, L // tq),
        in_specs=[cur, prev, prev, cur, cur],
        out_specs=[cur, cur],
        out_shape=[jax.ShapeDtypeStruct((N, L, GROUP), F32)] * 2,
        compiler_params=pltpu.CompilerParams(
            dimension_semantics=("arbitrary", "arbitrary"), vmem_limit_bytes=VMEM_LIMIT),
        name="swa",
    )(q, k, v, k, v)


def _outproj_body(x_ref, oa_ref, ob_ref, oc_ref, o1_ref, l1_ref, o2_ref, l2_ref, o3_ref, l3_ref,
                  gd_ref, w_ref, y_ref):
    l1, l2, l3 = l1_ref[...], l2_ref[...], l3_ref[...]
    mx = jnp.maximum(jnp.maximum(l1, l2), l3)
    w1, w2, w3 = jnp.exp(l1 - mx), jnp.exp(l2 - mx), jnp.exp(l3 - mx)
    od = (w1 * o1_ref[...] + w2 * o2_ref[...] + w3 * o3_ref[...]) / (w1 + w2 + w3)
    od = _rms(od, gd_ref[...]).astype(BF)
    y = _dot(oa_ref[...], w_ref[0:256, :])
    y += _dot(ob_ref[...], w_ref[256:512, :])
    y += _dot(oc_ref[...], w_ref[512:768, :])
    y += _dot(od, w_ref[768:1024, :])
    y_ref[...] = x_ref[...] + y


def _outproj(x, oa, ob, oc, branches, gd, w_out, *, tm):
    T, D = x.shape
    row = lambda w: pl.BlockSpec((tm, w), lambda i: (i, 0))
    full = lambda shape: pl.BlockSpec(shape, lambda i: (0,) * len(shape))
    return pl.pallas_call(
        _outproj_body,
        grid=(T // tm,),
        in_specs=[row(D)] + [row(GROUP)] * 9 + [full((1, GROUP)), full((D, D))],
        out_specs=row(D),
        out_shape=jax.ShapeDtypeStruct((T, D), F32),
        compiler_params=pltpu.CompilerParams(
            dimension_semantics=("arbitrary",), vmem_limit_bytes=VMEM_LIMIT),
        name="outproj",
    )(x, oa, ob, oc, *branches, gd, w_out)


PEER_EB = 1024
N_CAND = PEER_TOP + 1


def _peer_body(x_ref, g_ref, wqt_ref, kpad_ref, u_ref, vt_ref, gf_ref, o_ref,
               ht_ref, st_ref, e2_ref, th_ref, cw_ref, hw_ref, acc_ref,
               *, tt, final):
    e = pl.program_id(1)
    nch = tt // LANES

    @pl.when(e == 0)
    def _prologue():
        h = _rms(x_ref[...], g_ref[...])
        ht_ref[...] = h.T.astype(BF)
        qt = _dot(wqt_ref[...], ht_ref[...]).astype(BF)
        st_ref[...] = _dot(kpad_ref[...], qt)
        acc_ref[...] = jnp.zeros(acc_ref.shape, F32)

        def top_vals(load, n_rows, n_out):
            vals = []
            prev = None
            for _ in range(n_out):
                accs = [None] * 4
                for r in range(n_rows):
                    v = load(r)
                    if prev is not None:
                        v = jnp.where(v < prev, v, NEG)
                    a = accs[r % 4]
                    accs[r % 4] = v if a is None else jnp.maximum(a, v)
                prev = jnp.maximum(jnp.maximum(accs[0], accs[1]), jnp.maximum(accs[2], accs[3]))
                vals.append(prev)
            return vals

        def token_chunk(tc, carry):
            ls = pl.ds(pl.multiple_of(tc * LANES, LANES), LANES)
            v1 = top_vals(lambda r: st_ref[r * 8:(r + 1) * 8, ls], N_KEYS, N_CAND)
            v2 = top_vals(lambda r: st_ref[1024 + r * 8:1024 + (r + 1) * 8, ls], N_KEYS, N_CAND)
            cands = [v1[a] + v2[b] for a in range(N_CAND) for b in range(N_CAND)
                     if (a + 1) * (b + 1) <= N_CAND]
            top = top_vals(lambda r: cands[r], len(cands), N_CAND)
            z = jnp.ones_like(top[0])
            for r in range(1, PEER_TOP):
                z = z + jnp.exp(top[r] - top[0])
            tau = 0.5 * (top[PEER_TOP - 1] + top[PEER_TOP])
            iz = 1.0 / z
            for r in range(N_KEYS):
                s1 = st_ref[r * 8:(r + 1) * 8, ls]
                th_ref[r, :, ls] = tau - s1
                cw_ref[r, :, ls] = jnp.exp(s1 - v1[0]) * iz
            for hh in range(PEER_HEADS):
                rows = slice(2048 + hh * 128, 2048 + (hh + 1) * 128)
                e2_ref[hh * 128:(hh + 1) * 128, ls] = jnp.exp(st_ref[rows, ls] - v2[0][hh:hh + 1, :])
            return carry

        lax.fori_loop(0, nch, token_chunk, 0)

    nblk = PEER_EB // 256

    def pre_act(b):
        return _dot(u_ref[b * 256:(b + 1) * 256, :], ht_ref[...])

    def gate(b, a):
        i0 = e * (PEER_EB // N_KEYS) + 2 * b
        th = [th_ref[i0], th_ref[i0 + 1]]
        cw = [cw_ref[i0], cw_ref[i0 + 1]]
        for tc in range(nch):
            ls = slice(tc * LANES, (tc + 1) * LANES)
            for sub in range(2):
                w = [None, None]
                for hh in range(PEER_HEADS):
                    r0 = hh * N_KEYS + sub * 64
                    s2 = st_ref[2048 + r0:2048 + r0 + 64, ls]
                    e2 = e2_ref[r0:r0 + 64, ls]
                    for half in range(2):
                        sel = jnp.where(s2 >= th[half][hh:hh + 1, ls], e2, 0.0) * cw[half][hh:hh + 1, ls]
                        w[half] = sel if w[half] is None else w[half] + sel
                for half in range(2):
                    r1 = half * N_KEYS + sub * 64
                    hw_ref[b % 2, r1:r1 + 64, ls] = (jax.nn.gelu(a[r1:r1 + 64, ls]) * w[half]).astype(BF)

    a = pre_act(0)
    for b in range(nblk):
        a_next = pre_act(b + 1) if b + 1 < nblk else None
        gate(b, a)
        if b > 0:
            acc_ref[...] += _dot(vt_ref[b - 1], hw_ref[(b - 1) % 2])
        a = a_next
    acc_ref[...] += _dot(vt_ref[nblk - 1], hw_ref[(nblk - 1) % 2])

    @pl.when(e == pl.num_programs(1) - 1)
    def _finish():
        y = x_ref[...] + acc_ref[...].T
        if final:
            y = _rms(y, gf_ref[...])
        o_ref[...] = y


def _peer(x, g, wqt, kpad, u, vt, gf, *, tt, final):
    T, D = x.shape
    ne = u.shape[0] // PEER_EB
    const = lambda shape: pl.BlockSpec(shape, lambda i, e: (0,) * len(shape),
                                       pipeline_mode=pl.Buffered(1))
    return pl.pallas_call(
        functools.partial(_peer_body, tt=tt, final=final),
        grid=(T // tt, ne),
        in_specs=[pl.BlockSpec((tt, D), lambda i, e: (i, 0)),
                  const((1, D)), const((D, D)), const((3 * 1024, D)),
                  pl.BlockSpec((PEER_EB, D), lambda i, e: (e, 0)),
                  pl.BlockSpec((PEER_EB // 256, D, 256), lambda i, e: (e, 0, 0)),
                  const((1, D))],
        out_specs=pl.BlockSpec((tt, D), lambda i, e: (i, 0)),
        out_shape=jax.ShapeDtypeStruct((T, D), F32),
        scratch_shapes=[pltpu.VMEM((D, tt), BF),
                        pltpu.VMEM((3 * 1024, tt), F32),
                        pltpu.VMEM((1024, tt), F32),
                        pltpu.VMEM((N_KEYS, PEER_HEADS, tt), F32),
                        pltpu.VMEM((N_KEYS, PEER_HEADS, tt), F32),
                        pltpu.VMEM((2, 256, tt), BF),
                        pltpu.VMEM((D, tt), F32)],
        compiler_params=pltpu.CompilerParams(
            dimension_semantics=("arbitrary", "arbitrary"), vmem_limit_bytes=VMEM_LIMIT),
        name="peer",
    )(x, g, wqt, kpad, u, vt, gf)


def _pack_in_weights(w_in):
    aq, ak, av, cq, ckv, kpe, zc, dq, dk, dv = jnp.split(
        w_in, np.cumsum([256, 256, 256, 256, 128, 32, 512, 256, 256])[:].tolist(), axis=1)
    kpe_rep = jnp.zeros((D_MODEL, 4, 128), w_in.dtype).at[:, :, 64:96].set(kpe[:, None, :])
    return jnp.concatenate(
        [aq, ak, av, cq, ckv, kpe_rep.reshape(D_MODEL, 512), zc, dq, dk, dv], axis=1).astype(BF)


def _pack_mla_weights(w_uq, w_ukv):
    uq = jnp.pad(w_uq.reshape(256, 4, 96), ((0, 0), (0, 0), (0, 32))).reshape(256, 512)
    ukv = w_ukv.reshape(128, 4, 128)
    uk = jnp.pad(ukv[:, :, :64], ((0, 0), (0, 0), (0, 64))).reshape(128, 512)
    uv = ukv[:, :, 64:].reshape(128, 256)
    return uq.astype(BF), uk.astype(BF), uv.astype(BF)


def _pack_sub_keys(sub_keys):
    eye = jnp.eye(PEER_HEADS, dtype=sub_keys.dtype)
    def block(c, key_major):
        k = sub_keys[c]
        sel = jnp.zeros((2,), sub_keys.dtype).at[c].set(1.0)
        if key_major:
            m = jnp.einsum('nd,hg,c->nhgcd', k, eye, sel)
        else:
            m = jnp.einsum('nd,hg,c->hngcd', k, eye, sel)
        return m.reshape(1024, 1024)
    return jnp.concatenate([block(0, True), block(1, True), block(1, False)], axis=0).astype(BF)


def _pack_expert_out(v):
    n, d = v.shape
    return v.reshape(n // 256, 256, d).transpose(0, 2, 1).astype(BF)


def _deinterleave(a, d):
    B, S, W = a.shape
    return a.reshape(B, S // d, d, W).transpose(0, 2, 1, 3).reshape(B * d, S // d, W)


def _interleave(a, d, B):
    N, L, W = a.shape
    return a.reshape(B, d, L, W).transpose(0, 2, 1, 3).reshape(B, L * d, W)


def kernel(x, w_in, w_out, norm_mix, norm_ffn, diff_lambda, diff_head_norm, mla_q_norm, mla_kv_norm, mla_w_uq, mla_w_ukv, mla_out_norm, sgu_v_norm, sgu_w, sgu_b, sgu_out_norm, dil_out_norm, peer_w_q, peer_sub_keys, peer_u, peer_v, final_norm):
    B, S, D = x.shape
    depth = w_in.shape[0]
    T = B * S
    tm = min(512, S)
    ta = min(512, S)
    tt = min(512, T)
    row = lambda a: a.reshape(1, -1)

    tabs = (_rope_tables(S, 8, 32, 0) + _rope_tables(S, 32, 128, 64) + _rope_tables(S, 16, 64, 0))

    for l in range(depth):
        lam_init = 0.8 - 0.6 * math.exp(-0.3 * l)
        w_cat = _pack_in_weights(w_in[l])
        wuq, wuk, wuv = _pack_mla_weights(mla_w_uq[l], mla_w_ukv[l])
        sgub = jnp.repeat(sgu_b[l].T, 64, axis=1)
        qa, ka, va, qb, kb, vb, oc, qd, kd, vd = _inproj(
            x, row(norm_mix[l]), w_cat, tabs, row(mla_q_norm[l]), row(mla_kv_norm[l]),
            wuq, wuk, wuv, row(sgu_v_norm[l]), sgu_w[l], sgub, row(sgu_out_norm[l]), tm=tm)

        oa = _attn_a(diff_lambda[l], row(diff_head_norm[l]), qa, ka, va, t=ta, lam_init=lam_init)
        ob = _attn_b(row(mla_out_norm[l]), qb, kb, vb, t=ta)

        branches = []
        for dil in (1, 4, 16):
            if dil == 1:
                o, lse = _swa(qd, kd, vd, tq=min(512, S))
            else:
                qq, kk, vv = (_deinterleave(a, dil) for a in (qd, kd, vd))
                o, lse = _swa(qq, kk, vv, tq=min(512, S // dil))
                o, lse = _interleave(o, dil, B), _interleave(lse, dil, B)
            branches += [o.reshape(T, GROUP), lse.reshape(T, GROUP)]

        x2 = _outproj(x.reshape(T, D), oa.reshape(T, GROUP), ob.reshape(T, GROUP),
                      oc.reshape(T, GROUP), branches, row(dil_out_norm[l]),
                      w_out[l].astype(BF), tm=min(512, T))

        x2 = _peer(x2, row(norm_ffn[l]), peer_w_q[l].T.astype(BF), _pack_sub_keys(peer_sub_keys[l]),
                   peer_u[l].astype(BF), _pack_expert_out(peer_v[l]), row(final_norm),
                   tt=tt, final=(l == depth - 1))
        x = x2.reshape(B, S, D)
    return x
```

```python
import functools
import math

import numpy as np
import jax
import jax.numpy as jnp
from jax import lax
from jax.experimental import pallas as pl
from jax.experimental.pallas import tpu as pltpu

D_MODEL = 1024
GROUP = 256
ROPE_THETA = 500000.0
NEG = -1e30
EPS = 1e-6
LANES = 128
LOG2E = math.log2(math.e)

N_KEYS = 128
PEER_HEADS = 8
PEER_TOP = 16

BF = jnp.bfloat16
F32 = jnp.float32

C_AQ, C_AK, C_AV, C_CQ, C_CKV, C_KPE, C_ZC, C_DQ, C_DK, C_DV, C_END = (
    0, 256, 512, 768, 1024, 1152, 1664, 2176, 2432, 2688, 2944)

VMEM_LIMIT = 56 * 1024 * 1024


def _rms(x, g):
    return x * lax.rsqrt(jnp.mean(x * x, axis=-1, keepdims=True) + EPS) * g


def _nt_dot(a, b):
    return lax.dot_general(a, b, (((1,), (1,)), ((), ())), preferred_element_type=F32)


def _dot(a, b):
    return jnp.dot(a, b, preferred_element_type=F32)


def _rope_tables(seq, n_rot, period, offset):
    half = n_rot // 2
    pos = jnp.arange(seq, dtype=F32)
    inv = ROPE_THETA ** (-jnp.arange(half, dtype=F32) * (2.0 / n_rot))
    ang = pos[:, None] * inv[None, :]
    cos, sin = jnp.cos(ang), jnp.sin(ang)
    g = np.arange(LANES) % period - offset
    rot = (g >= 0) & (g < n_rot)
    idx = np.where(rot, g % half, 0)
    sign = np.where(g < half, -1.0, 1.0).astype(np.float32)
    cos_t = jnp.where(rot[None, :], cos[:, idx], 1.0)
    sin_t = jnp.where(rot[None, :], sin[:, idx] * sign[None, :], 0.0)
    return cos_t.astype(F32), sin_t.astype(F32)


def _rope_apply(x, cos_t, sin_t, n_rot, period, offset):
    half = n_rot // 2
    lane = lax.broadcasted_iota(jnp.int32, (1, LANES), 1)
    first = (lane % period - offset) < half
    outs = []
    for c in range(x.shape[1] // LANES):
        xc = x[:, c * LANES:(c + 1) * LANES]
        fwd = pltpu.roll(xc, LANES - half, 1)
        bwd = pltpu.roll(xc, half, 1)
        outs.append(xc * cos_t + jnp.where(first, fwd, bwd) * sin_t)
    return outs[0] if len(outs) == 1 else jnp.concatenate(outs, axis=1)


def _inproj_body(x_ref, gmix_ref, w_ref, ca_ref, sa_ref, cb_ref, sb_ref, cd_ref, sd_ref,
                 gq_ref, gkv_ref, wuq_ref, wuk_ref, wuv_ref,
                 gsv_ref, sguw_ref, sgub_ref, gso_ref,
                 qa_ref, ka_ref, va_ref, qb_ref, kb_ref, vb_ref, oc_ref,
                 qd_ref, kd_ref, vd_ref, *, tm):
    x = x_ref[0]
    h = _rms(x, gmix_ref[...]).astype(BF)

    def proj(a, b):
        return _dot(h, w_ref[:, a:b])

    ca, sa = ca_ref[...], sa_ref[...]
    aq = _rope_apply(proj(C_AQ, C_AK), ca, sa, 8, 32, 0)
    qa_ref[0] = (aq * (32.0 ** -0.5 * LOG2E)).T.astype(BF)
    ka_ref[0] = _rope_apply(proj(C_AK, C_AV), ca, sa, 8, 32, 0).astype(BF)
    va_ref[0] = proj(C_AV, C_CQ).T.astype(BF)

    cb, sb = cb_ref[...], sb_ref[...]
    cq = _rms(proj(C_CQ, C_CKV), gq_ref[...]).astype(BF)
    qb = _rope_apply(_dot(cq, wuq_ref[...]), cb, sb, 32, 128, 64)
    qb_ref[0] = (qb * (96.0 ** -0.5 * LOG2E)).T.astype(BF)
    ckv = _rms(proj(C_CKV, C_KPE), gkv_ref[...]).astype(BF)
    kpe = _rope_apply(proj(C_KPE, C_ZC), cb, sb, 32, 128, 64)
    kb_ref[0] = (_dot(ckv, wuk_ref[...]) + kpe).astype(BF)
    vb_ref[0] = _dot(ckv, wuv_ref[...]).T.astype(BF)

    zc = jax.nn.gelu(proj(C_ZC, C_DQ))
    u = zc[:, :GROUP]
    vn = _rms(zc[:, GROUP:], gsv_ref[...]).astype(BF)
    r = lax.broadcasted_iota(jnp.int32, (128, 128), 0)
    c = lax.broadcasted_iota(jnp.int32, (128, 128), 1)
    wcat = jnp.concatenate(
        [jnp.where(r >= c, sguw_ref[g], 0.0).astype(BF) for g in range(4)], axis=1)
    lane = lax.broadcasted_iota(jnp.int32, (1, GROUP), 1)
    bias = sgub_ref[...]
    gso = gso_ref[...]
    for ch in range(tm // 128):
        vc = vn[ch * 128:(ch + 1) * 128]
        vst = jnp.concatenate(
            [jnp.where(lane // 64 == g, vc, jnp.zeros_like(vc)) for g in range(4)], axis=0)
        sv = _dot(wcat, vst) + bias
        oc = u[ch * 128:(ch + 1) * 128] * sv
        oc_ref[0, ch * 128:(ch + 1) * 128, :] = _rms(oc, gso).astype(BF)

    cd, sd = cd_ref[...], sd_ref[...]
    dq = _rope_apply(proj(C_DQ, C_DK), cd, sd, 16, 64, 0)
    qd_ref[0] = (dq * 0.125).astype(BF)
    kd_ref[0] = _rope_apply(proj(C_DK, C_DV), cd, sd, 16, 64, 0).astype(BF)
    vd_ref[0] = proj(C_DV, C_END).astype(BF)


def _inproj(x, gmix, w_cat, tabs, gq, gkv, wuq, wuk, wuv, gsv, sguw, sgub, gso, *, tm):
    B, S, D = x.shape
    ns = S // tm
    full = lambda shape: pl.BlockSpec(shape, lambda s, b: (0,) * len(shape))
    tab = pl.BlockSpec((tm, LANES), lambda s, b: (s, 0))
    seq = lambda w: pl.BlockSpec((1, tm, w), lambda s, b: (b, s, 0))
    seq_t = lambda w: pl.BlockSpec((1, w, tm), lambda s, b: (b, 0, s))
    out_w = [256, 256, 256, 512, 512, 256, 256, 256, 256, 256]
    transposed = [True, False, True, True, False, True, False, False, False, False]
    return pl.pallas_call(
        functools.partial(_inproj_body, tm=tm),
        grid=(ns, B),
        in_specs=[seq(D), full((1, D)), full((D, C_END))] + [tab] * 6 + [
            full((1, 256)), full((1, 128)), full((256, 512)), full((128, 512)), full((128, 256)),
            full((1, 256)), full((4, 128, 128)), full((128, 256)), full((1, 256))],
        out_specs=[seq_t(w) if tr else seq(w) for w, tr in zip(out_w, transposed)],
        out_shape=[jax.ShapeDtypeStruct((B, w, S) if tr else (B, S, w), BF)
                   for w, tr in zip(out_w, transposed)],
        compiler_params=pltpu.CompilerParams(
            dimension_semantics=("arbitrary", "arbitrary"), vmem_limit_bytes=VMEM_LIMIT),
        name="inproj",
    )(x, gmix, w_cat, *tabs, gq, gkv, wuq, wuk, wuv, gsv, sguw, sgub, gso)


def _head_expand(cols, lane, width):
    out = cols[-1]
    for h in range(len(cols) - 2, -1, -1):
        out = jnp.where(lane // width == h, cols[h], out)
    return out


def _vstack(v, lane):
    return jnp.concatenate(
        [jnp.where(lane // 64 == h, v, jnp.zeros_like(v)) for h in range(4)], axis=0)


QB = 512


def _online_softmax_t(st, m_ref, l_ref, j, cs):
    m_prev = m_ref[j:j + 1, cs]
    m_new = jnp.maximum(m_prev, jnp.max(st, axis=0, keepdims=True))
    alpha = jnp.exp2(m_prev - m_new)
    et = jnp.exp2(st - m_new)
    l_ref[j:j + 1, cs] = alpha * l_ref[j:j + 1, cs] + jnp.sum(et, axis=0, keepdims=True)
    m_ref[j:j + 1, cs] = m_new
    return et.astype(BF), alpha


def _attn_a_body(lam_ref, gh_ref, q_ref, k_ref, v_ref, o_ref,
                 qs_ref, m_ref, l_ref, acc_ref, *, t, lam_init):
    qi = pl.program_id(1)
    ki = pl.program_id(2)

    @pl.when(ki == 0)
    def _init():
        m_ref[...] = jnp.full(m_ref.shape, NEG, F32)
        l_ref[...] = jnp.zeros(l_ref.shape, F32)
        acc_ref[...] = jnp.zeros(acc_ref.shape, F32)
        qt = q_ref[0]
        row = lax.broadcasted_iota(jnp.int32, (GROUP, 1), 0)
        for j in range(8):
            qs_ref[j] = jnp.where(row // 32 == j, qt, jnp.zeros_like(qt))

    def step(masked):
        k = k_ref[0]
        jobs = [(mp, h, c) for mp in range(2) for h in range(4) for c in range(t // QB)]

        def qk(mp, h, c):
            return _dot(k, qs_ref[2 * h + mp, :, c * QB:(c + 1) * QB])

        def pv(mp, h, c, pt, alpha):
            hs, cs = slice(64 * h, 64 * (h + 1)), slice(c * QB, (c + 1) * QB)
            acc_ref[mp, hs, cs] = acc_ref[mp, hs, cs] * alpha + _dot(v_ref[0, hs, :], pt)

        st_next = qk(*jobs[0])
        pending = None
        for n, (mp, h, c) in enumerate(jobs):
            st = st_next
            if n + 1 < len(jobs):
                st_next = qk(*jobs[n + 1])
            if masked:
                keep = (lax.broadcasted_iota(jnp.int32, (t, QB), 0)
                        <= lax.broadcasted_iota(jnp.int32, (t, QB), 1) + c * QB)
                st = jnp.where(keep, st, NEG)
            pt, alpha = _online_softmax_t(st, m_ref, l_ref, 2 * h + mp, slice(c * QB, (c + 1) * QB))
            if pending is not None:
                pv(*pending)
            pending = (mp, h, c, pt, alpha)
        pv(*pending)

    @pl.when(ki < qi)
    def _off():
        step(False)

    @pl.when(ki == qi)
    def _diag():
        step(True)
        lf = lam_ref[...]
        lam = (jnp.exp(jnp.sum(lf[0:1] * lf[1:2], axis=-1, keepdims=True))
               - jnp.exp(jnp.sum(lf[2:3] * lf[3:4], axis=-1, keepdims=True)) + lam_init)
        rows = []
        for h in range(4):
            hs = slice(64 * h, 64 * (h + 1))
            oh = (acc_ref[0, hs, :] * (1.0 / l_ref[2 * h:2 * h + 1, :])
                  - lam * (acc_ref[1, hs, :] * (1.0 / l_ref[2 * h + 1:2 * h + 2, :])))
            ms = jnp.mean(oh * oh, axis=0, keepdims=True)
            rows.append(oh * lax.rsqrt(ms + EPS))
        ot = jnp.concatenate(rows, axis=0)
        o_ref[0] = (ot.T * gh_ref[...] * (1.0 - lam_init)).astype(BF)


def _attn_a(lam_p, gh, q, k, v, *, t, lam_init):
    B, S, _ = k.shape
    n = S // t
    return pl.pallas_call(
        functools.partial(_attn_a_body, t=t, lam_init=lam_init),
        grid=(B, n, n),
        in_specs=[pl.BlockSpec((4, 32), lambda b, i, j: (0, 0)),
                  pl.BlockSpec((1, GROUP), lambda b, i, j: (0, 0)),
                  pl.BlockSpec((1, GROUP, t), lambda b, i, j: (b, 0, i)),
                  pl.BlockSpec((1, t, GROUP), lambda b, i, j: (b, jnp.minimum(i, j), 0)),
                  pl.BlockSpec((1, GROUP, t), lambda b, i, j: (b, 0, jnp.minimum(i, j)))],
        out_specs=pl.BlockSpec((1, t, GROUP), lambda b, i, j: (b, i, 0)),
        out_shape=jax.ShapeDtypeStruct((B, S, GROUP), BF),
        scratch_shapes=[pltpu.VMEM((8, GROUP, t), BF),
                        pltpu.VMEM((8, t), F32),
                        pltpu.VMEM((8, t), F32),
                        pltpu.VMEM((2, GROUP, t), F32)],
        compiler_params=pltpu.CompilerParams(
            dimension_semantics=("arbitrary", "arbitrary", "arbitrary"),
            vmem_limit_bytes=VMEM_LIMIT),
        name="attn_diff",
    )(lam_p, gh, q, k, v)


def _attn_b_body(g_ref, q_ref, k_ref, v_ref, o_ref, m_ref, l_ref, acc_ref, *, t):
    qi = pl.program_id(1)
    ki = pl.program_id(2)

    @pl.when(ki == 0)
    def _init():
        m_ref[...] = jnp.full(m_ref.shape, NEG, F32)
        l_ref[...] = jnp.zeros(l_ref.shape, F32)
        acc_ref[...] = jnp.zeros(acc_ref.shape, F32)

    def step(masked):
        jobs = [(h, c) for h in range(4) for c in range(t // QB)]

        def qk(h, c):
            return _dot(k_ref[0, :, h * 128:(h + 1) * 128],
                        q_ref[0, h * 128:(h + 1) * 128, c * QB:(c + 1) * QB])

        def pv(h, c, pt, alpha):
            hs, cs = slice(64 * h, 64 * (h + 1)), slice(c * QB, (c + 1) * QB)
            acc_ref[hs, cs] = acc_ref[hs, cs] * alpha + _dot(v_ref[0, hs, :], pt)

        st_next = qk(*jobs[0])
        pending = None
        for n, (h, c) in enumerate(jobs):
            st = st_next
            if n + 1 < len(jobs):
                st_next = qk(*jobs[n + 1])
            if masked:
                keep = (lax.broadcasted_iota(jnp.int32, (t, QB), 0)
                        <= lax.broadcasted_iota(jnp.int32, (t, QB), 1) + c * QB)
                st = jnp.where(keep, st, NEG)
            pt, alpha = _online_softmax_t(st, m_ref, l_ref, h, slice(c * QB, (c + 1) * QB))
            if pending is not None:
                pv(*pending)
            pending = (h, c, pt, alpha)
        pv(*pending)

    @pl.when(ki < qi)
    def _off():
        step(False)

    @pl.when(ki == qi)
    def _diag():
        step(True)
        rows = [acc_ref[64 * h:64 * (h + 1), :] * (1.0 / l_ref[h:h + 1, :]) for h in range(4)]
        ot = jnp.concatenate(rows, axis=0)
        o_ref[0] = _rms(ot.T, g_ref[...]).astype(BF)


def _attn_b(g, q, k, v, *, t):
    B, S, _ = k.shape
    n = S // t
    return pl.pallas_call(
        functools.partial(_attn_b_body, t=t),
        grid=(B, n, n),
        in_specs=[pl.BlockSpec((1, GROUP), lambda b, i, j: (0, 0)),
                  pl.BlockSpec((1, 512, t), lambda b, i, j: (b, 0, i)),
                  pl.BlockSpec((1, t, 512), lambda b, i, j: (b, jnp.minimum(i, j), 0)),
                  pl.BlockSpec((1, GROUP, t), lambda b, i, j: (b, 0, jnp.minimum(i, j)))],
        out_specs=pl.BlockSpec((1, t, GROUP), lambda b, i, j: (b, i, 0)),
        out_shape=jax.ShapeDtypeStruct((B, S, GROUP), BF),
        scratch_shapes=[pltpu.VMEM((4, t), F32),
                        pltpu.VMEM((4, t), F32),
                        pltpu.VMEM((GROUP, t), F32)],
        compiler_params=pltpu.CompilerParams(
            dimension_semantics=("arbitrary", "arbitrary", "arbitrary"),
            vmem_limit_bytes=VMEM_LIMIT),
        name="attn_mla",
    )(g, q, k, v)


def _swa_body(q_ref, kp_ref, vp_ref, kc_ref, vc_ref, o_ref, lse_ref, *, tq):
    i = pl.program_id(1)
    lane = lax.broadcasted_iota(jnp.int32, (1, GROUP), 1)
    q = q_ref[0]
    kw = jnp.concatenate([kp_ref[0], kc_ref[0]], axis=0)
    vw = jnp.concatenate([vp_ref[0], vc_ref[0]], axis=0)
    a = lax.broadcasted_iota(jnp.int32, (128, 256), 0)
    c = lax.broadcasted_iota(jnp.int32, (128, 256), 1)
    d = c - a
    band = jnp.where(d >= 0, d, 1000) <= 128
    cmin = jnp.where(i > 0, 0, 128)
    band0 = jnp.where(c >= cmin, jnp.where(d >= 0, d, 1000), 1000) <= 128
    for sb in range(tq // 128):
        qs = q[sb * 128:(sb + 1) * 128]
        kwin = kw[sb * 128:sb * 128 + 256]
        vst = _vstack(vw[sb * 128:sb * 128 + 256], lane)
        ok = band0 if sb == 0 else band
        es, ils, lses = [], [], []
        for h in range(4):
            s = _nt_dot(jnp.where(lane // 64 == h, qs, jnp.zeros_like(qs)), kwin)
            s = jnp.where(ok, s, NEG)
            m = jnp.max(s, axis=-1, keepdims=True)
            e = jnp.exp(s - m)
            den = jnp.sum(e, axis=-1, keepdims=True)
            es.append(e.astype(BF))
            ils.append(1.0 / den)
            lses.append(m + jnp.log(den))
        pv = _dot(jnp.concatenate(es, axis=1), vst)
        o_ref[0, sb * 128:(sb + 1) * 128, :] = pv * _head_expand(ils, lane, 64)
        lse_ref[0, sb * 128:(sb + 1) * 128, :] = _head_expand(lses, lane, 64)


def _swa(q, k, v, *, tq):
    N, L, _ = q.shape
    nb = tq // 128
    cur = pl.BlockSpec((1, tq, GROUP), lambda n, i: (n, i, 0))
    prev = pl.BlockSpec((1, 128, GROUP), lambda n, i: (n, jnp.maximum(i * nb - 1, 0), 0))
    return pl.pallas_call(
        functools.partial(_swa_body, tq=tq),
        grid=(N, L // tq),
        in_specs=[cur, prev, prev, cur, cur],
        out_specs=[cur, cur],
        out_shape=[jax.ShapeDtypeStruct((N, L, GROUP), F32)] * 2,
        compiler_params=pltpu.CompilerParams(
            dimension_semantics=("arbitrary", "arbitrary"), vmem_limit_bytes=VMEM_LIMIT),
        name="swa",
    )(q, k, v, k, v)


def _outproj_body(x_ref, oa_ref, ob_ref, oc_ref, o1_ref, l1_ref, o2_ref, l2_ref, o3_ref, l3_ref,
                  gd_ref, w_ref, y_ref):
    l1, l2, l3 = l1_ref[...], l2_ref[...], l3_ref[...]
    mx = jnp.maximum(jnp.maximum(l1, l2), l3)
    w1, w2, w3 = jnp.exp(l1 - mx), jnp.exp(l2 - mx), jnp.exp(l3 - mx)
    od = (w1 * o1_ref[...] + w2 * o2_ref[...] + w3 * o3_ref[...]) / (w1 + w2 + w3)
    od = _rms(od, gd_ref[...]).astype(BF)
    y = _dot(oa_ref[...], w_ref[0:256, :])
    y += _dot(ob_ref[...], w_ref[256:512, :])
    y += _dot(oc_ref[...], w_ref[512:768, :])
    y += _dot(od, w_ref[768:1024, :])
    y_ref[...] = x_ref[...] + y


def _outproj(x, oa, ob, oc, branches, gd, w_out, *, tm):
    T, D = x.shape
    row = lambda w: pl.BlockSpec((tm, w), lambda i: (i, 0))
    full = lambda shape: pl.BlockSpec(shape, lambda i: (0,) * len(shape))
    return pl.pallas_call(
        _outproj_body,
        grid=(T // tm,),
        in_specs=[row(D)] + [row(GROUP)] * 9 + [full((1, GROUP)), full((D, D))],
        out_specs=row(D),
        out_shape=jax.ShapeDtypeStruct((T, D), F32),
        compiler_params=pltpu.CompilerParams(
            dimension_semantics=("arbitrary",), vmem_limit_bytes=VMEM_LIMIT),
        name="outproj",
    )(x, oa, ob, oc, *branches, gd, w_out)


PEER_EB = 1024
N_CAND = PEER_TOP + 1


def _peer_body(x_ref, g_ref, wqt_ref, kpad_ref, u_ref, vt_ref, gf_ref, o_ref,
               ht_ref, st_ref, e2_ref, th_ref, cw_ref, at_ref, hw_ref, acc_ref,
               *, tt, final):
    e = pl.program_id(1)
    nch = tt // LANES

    @pl.when(e == 0)
    def _prologue():
        h = _rms(x_ref[...], g_ref[...])
        ht_ref[...] = h.T.astype(BF)
        qt = _dot(wqt_ref[...], ht_ref[...]).astype(BF)
        st_ref[...] = _dot(kpad_ref[...], qt)
        acc_ref[...] = jnp.zeros(acc_ref.shape, F32)
        at_ref[1] = jnp.zeros(at_ref.shape[1:], F32)
        hw_ref[0] = jnp.zeros(hw_ref.shape[1:], BF)

        def top_vals(load, n_rows, n_out):
            vals = []
            prev = None
            for _ in range(n_out):
                accs = [None] * 4
                for r in range(n_rows):
                    v = load(r)
                    if prev is not None:
                        v = jnp.where(v < prev, v, NEG)
                    a = accs[r % 4]
                    accs[r % 4] = v if a is None else jnp.maximum(a, v)
                prev = jnp.maximum(jnp.maximum(accs[0], accs[1]), jnp.maximum(accs[2], accs[3]))
                vals.append(prev)
            return vals

        def token_chunk(tc, carry):
            ls = pl.ds(pl.multiple_of(tc * LANES, LANES), LANES)
            v1 = top_vals(lambda r: st_ref[r * 8:(r + 1) * 8, ls], N_KEYS, N_CAND)
            v2 = top_vals(lambda r: st_ref[1024 + r * 8:1024 + (r + 1) * 8, ls], N_KEYS, N_CAND)
            cands = [v1[a] + v2[b] for a in range(N_CAND) for b in range(N_CAND)
                     if (a + 1) * (b + 1) <= N_CAND]
            top = top_vals(lambda r: cands[r], len(cands), N_CAND)
            z = jnp.ones_like(top[0])
            for r in range(1, PEER_TOP):
                z = z + jnp.exp(top[r] - top[0])
            tau = 0.5 * (top[PEER_TOP - 1] + top[PEER_TOP])
            iz = 1.0 / z
            for r in range(N_KEYS):
                s1 = st_ref[r * 8:(r + 1) * 8, ls]
                th_ref[r, :, ls] = tau - s1
                cw_ref[r, :, ls] = jnp.exp(s1 - v1[0]) * iz
            for hh in range(PEER_HEADS):
                rows = slice(2048 + hh * 128, 2048 + (hh + 1) * 128)
                e2_ref[hh * 128:(hh + 1) * 128, ls] = jnp.exp(st_ref[rows, ls] - v2[0][hh:hh + 1, :])
            return carry

        lax.fori_loop(0, nch, token_chunk, 0)

    ne = pl.num_programs(1) - 2
    blk = jnp.clip(e - 1, 0, ne - 1)

    def gate(par, b, tc):
        if True:
            i0 = blk * (PEER_EB // N_KEYS) + 2 * b
            th = [th_ref[i0], th_ref[i0 + 1]]
            cw = [cw_ref[i0], cw_ref[i0 + 1]]
            if True:
                ls = slice(tc * LANES, (tc + 1) * LANES)
                for sub in range(2):
                    w = [None, None]
                    for hh in range(PEER_HEADS):
                        r0 = hh * N_KEYS + sub * 64
                        s2 = st_ref[2048 + r0:2048 + r0 + 64, ls]
                        e2 = e2_ref[r0:r0 + 64, ls]
                        for half in range(2):
                            sel = jnp.where(s2 >= th[half][hh:hh + 1, ls], e2, 0.0) * cw[half][hh:hh + 1, ls]
                            w[half] = sel if w[half] is None else w[half] + sel
                    for half in range(2):
                        r1 = b * 256 + half * N_KEYS + sub * 64
                        act = jax.nn.gelu(at_ref[1 - par, r1:r1 + 64, ls])
                        hw_ref[1 - par, r1:r1 + 64, ls] = (act * w[half]).astype(BF)

    def stage(par):
        for b in range(PEER_EB // 256):
            rs = slice(b * 256, (b + 1) * 256)
            for lt in range(tt // 256):
                cs = slice(lt * 256, (lt + 1) * 256)
                acc_ref[rs, cs] += _dot(vt_ref[rs, :], hw_ref[par, :, cs])
                gate(par, b, 2 * lt)
                at_ref[par, rs, cs] = _dot(u_ref[rs, :], ht_ref[:, cs])
                gate(par, b, 2 * lt + 1)

    @pl.when(e % 2 == 0)
    def _even():
        stage(0)

    @pl.when(e % 2 == 1)
    def _odd():
        stage(1)

    @pl.when(e == pl.num_programs(1) - 1)
    def _finish():
        y = x_ref[...] + acc_ref[...].T
        if final:
            y = _rms(y, gf_ref[...])
        o_ref[...] = y


def _peer(x, g, wqt, kpad, u, vt, gf, *, tt, final):
    T, D = x.shape
    ne = u.shape[0] // PEER_EB
    const = lambda shape: pl.BlockSpec(shape, lambda i, e: (0,) * len(shape),
                                       pipeline_mode=pl.Buffered(1))
    return pl.pallas_call(
        functools.partial(_peer_body, tt=tt, final=final),
        grid=(T // tt, ne + 2),
        in_specs=[pl.BlockSpec((tt, D), lambda i, e: (i, 0)),
                  const((1, D)), const((D, D)), const((3 * 1024, D)),
                  pl.BlockSpec((PEER_EB, D), lambda i, e: (jnp.minimum(e, ne - 1), 0)),
                  pl.BlockSpec((D, PEER_EB), lambda i, e: (0, jnp.clip(e - 2, 0, ne - 1))),
                  const((1, D))],
        out_specs=pl.BlockSpec((tt, D), lambda i, e: (i, 0)),
        out_shape=jax.ShapeDtypeStruct((T, D), F32),
        scratch_shapes=[pltpu.VMEM((D, tt), BF),
                        pltpu.VMEM((3 * 1024, tt), F32),
                        pltpu.VMEM((1024, tt), F32),
                        pltpu.VMEM((N_KEYS, PEER_HEADS, tt), F32),
                        pltpu.VMEM((N_KEYS, PEER_HEADS, tt), F32),
                        pltpu.VMEM((2, PEER_EB, tt), F32),
                        pltpu.VMEM((2, PEER_EB, tt), BF),
                        pltpu.VMEM((D, tt), F32)],
        compiler_params=pltpu.CompilerParams(
            dimension_semantics=("arbitrary", "arbitrary"), vmem_limit_bytes=VMEM_LIMIT),
        name="peer",
    )(x, g, wqt, kpad, u, vt, gf)


def _pack_in_weights(w_in):
    aq, ak, av, cq, ckv, kpe, zc, dq, dk, dv = jnp.split(
        w_in, np.cumsum([256, 256, 256, 256, 128, 32, 512, 256, 256])[:].tolist(), axis=1)
    kpe_rep = jnp.zeros((D_MODEL, 4, 128), w_in.dtype).at[:, :, 64:96].set(kpe[:, None, :])
    return jnp.concatenate(
        [aq, ak, av, cq, ckv, kpe_rep.reshape(D_MODEL, 512), zc, dq, dk, dv], axis=1).astype(BF)


def _pack_mla_weights(w_uq, w_ukv):
    uq = jnp.pad(w_uq.reshape(256, 4, 96), ((0, 0), (0, 0), (0, 32))).reshape(256, 512)
    ukv = w_ukv.reshape(128, 4, 128)
    uk = jnp.pad(ukv[:, :, :64], ((0, 0), (0, 0), (0, 64))).reshape(128, 512)
    uv = ukv[:, :, 64:].reshape(128, 256)
    return uq.astype(BF), uk.astype(BF), uv.astype(BF)


def _pack_sub_keys(sub_keys):
    eye = jnp.eye(PEER_HEADS, dtype=sub_keys.dtype)
    def block(c, key_major):
        k = sub_keys[c]
        sel = jnp.zeros((2,), sub_keys.dtype).at[c].set(1.0)
        if key_major:
            m = jnp.einsum('nd,hg,c->nhgcd', k, eye, sel)
        else:
            m = jnp.einsum('nd,hg,c->hngcd', k, eye, sel)
        return m.reshape(1024, 1024)
    return jnp.concatenate([block(0, True), block(1, True), block(1, False)], axis=0).astype(BF)


def _deinterleave(a, d):
    B, S, W = a.shape
    return a.reshape(B, S // d, d, W).transpose(0, 2, 1, 3).reshape(B * d, S // d, W)


def _interleave(a, d, B):
    N, L, W = a.shape
    return a.reshape(B, d, L, W).transpose(0, 2, 1, 3).reshape(B, L * d, W)


def kernel(x, w_in, w_out, norm_mix, norm_ffn, diff_lambda, diff_head_norm, mla_q_norm, mla_kv_norm, mla_w_uq, mla_w_ukv, mla_out_norm, sgu_v_norm, sgu_w, sgu_b, sgu_out_norm, dil_out_norm, peer_w_q, peer_sub_keys, peer_u, peer_v, final_norm):
    B, S, D = x.shape
    depth = w_in.shape[0]
    T = B * S
    tm = min(512, S)
    ta = min(512, S)
    tt = min(512, T)
    row = lambda a: a.reshape(1, -1)

    tabs = (_rope_tables(S, 8, 32, 0) + _rope_tables(S, 32, 128, 64) + _rope_tables(S, 16, 64, 0))

    for l in range(depth):
        lam_init = 0.8 - 0.6 * math.exp(-0.3 * l)
        w_cat = _pack_in_weights(w_in[l])
        wuq, wuk, wuv = _pack_mla_weights(mla_w_uq[l], mla_w_ukv[l])
        sgub = jnp.repeat(sgu_b[l].T, 64, axis=1)
        qa, ka, va, qb, kb, vb, oc, qd, kd, vd = _inproj(
            x, row(norm_mix[l]), w_cat, tabs, row(mla_q_norm[l]), row(mla_kv_norm[l]),
            wuq, wuk, wuv, row(sgu_v_norm[l]), sgu_w[l], sgub, row(sgu_out_norm[l]), tm=tm)

        oa = _attn_a(diff_lambda[l], row(diff_head_norm[l]), qa, ka, va, t=ta, lam_init=lam_init)
        ob = _attn_b(row(mla_out_norm[l]), qb, kb, vb, t=ta)

        branches = []
        for dil in (1, 4, 16):
            if dil == 1:
                o, lse = _swa(qd, kd, vd, tq=min(512, S))
            else:
                qq, kk, vv = (_deinterleave(a, dil) for a in (qd, kd, vd))
                o, lse = _swa(qq, kk, vv, tq=min(512, S // dil))
                o, lse = _interleave(o, dil, B), _interleave(lse, dil, B)
            branches += [o.reshape(T, GROUP), lse.reshape(T, GROUP)]

        x2 = _outproj(x.reshape(T, D), oa.reshape(T, GROUP), ob.reshape(T, GROUP),
                      oc.reshape(T, GROUP), branches, row(dil_out_norm[l]),
                      w_out[l].astype(BF), tm=min(512, T))

        x2 = _peer(x2, row(norm_ffn[l]), peer_w_q[l].T.astype(BF), _pack_sub_keys(peer_sub_keys[l]),
                   peer_u[l].astype(BF), peer_v[l].T.astype(BF), row(final_norm),
                   tt=tt, final=(l == depth - 1))
        x = x2.reshape(B, S, D)
    return x
```

```python
import functools
import math

import numpy as np
import jax
import jax.numpy as jnp
from jax import lax
from jax.experimental import pallas as pl
from jax.experimental.pallas import tpu as pltpu

D_MODEL = 1024
GROUP = 256
ROPE_THETA = 500000.0
NEG = -1e30
EPS = 1e-6
LANES = 128
LOG2E = math.log2(math.e)

N_KEYS = 128
PEER_HEADS = 8
PEER_TOP = 16

BF = jnp.bfloat16
F32 = jnp.float32

C_AQ, C_AK, C_AV, C_CQ, C_CKV, C_KPE, C_ZC, C_DQ, C_DK, C_DV, C_END = (
    0, 256, 512, 768, 1024, 1152, 1664, 2176, 2432, 2688, 2944)

VMEM_LIMIT = 56 * 1024 * 1024


def _rms(x, g):
    return x * lax.rsqrt(jnp.mean(x * x, axis=-1, keepdims=True) + EPS) * g


def _nt_dot(a, b):
    return lax.dot_general(a, b, (((1,), (1,)), ((), ())), preferred_element_type=F32)


def _dot(a, b):
    return jnp.dot(a, b, preferred_element_type=F32)


def _rope_tables(seq, n_rot, period, offset):
    half = n_rot // 2
    pos = jnp.arange(seq, dtype=F32)
    inv = ROPE_THETA ** (-jnp.arange(half, dtype=F32) * (2.0 / n_rot))
    ang = pos[:, None] * inv[None, :]
    cos, sin = jnp.cos(ang), jnp.sin(ang)
    g = np.arange(LANES) % period - offset
    rot = (g >= 0) & (g < n_rot)
    idx = np.where(rot, g % half, 0)
    sign = np.where(g < half, -1.0, 1.0).astype(np.float32)
    cos_t = jnp.where(rot[None, :], cos[:, idx], 1.0)
    sin_t = jnp.where(rot[None, :], sin[:, idx] * sign[None, :], 0.0)
    return cos_t.astype(F32), sin_t.astype(F32)


def _rope_apply(x, cos_t, sin_t, n_rot, period, offset):
    half = n_rot // 2
    lane = lax.broadcasted_iota(jnp.int32, (1, LANES), 1)
    first = (lane % period - offset) < half
    outs = []
    for c in range(x.shape[1] // LANES):
        xc = x[:, c * LANES:(c + 1) * LANES]
        fwd = pltpu.roll(xc, LANES - half, 1)
        bwd = pltpu.roll(xc, half, 1)
        outs.append(xc * cos_t + jnp.where(first, fwd, bwd) * sin_t)
    return outs[0] if len(outs) == 1 else jnp.concatenate(outs, axis=1)


def _inproj_body(x_ref, gmix_ref, w_ref, ca_ref, sa_ref, cb_ref, sb_ref, cd_ref, sd_ref,
                 gq_ref, gkv_ref, wuq_ref, wuk_ref, wuv_ref,
                 gsv_ref, sguw_ref, sgub_ref, gso_ref,
                 qa_ref, ka_ref, va_ref, qb_ref, kb_ref, vb_ref, oc_ref,
                 qd_ref, kd_ref, vd_ref, *, tm):
    x = x_ref[0]
    h = _rms(x, gmix_ref[...]).astype(BF)

    def proj(a, b):
        return _dot(h, w_ref[:, a:b])

    ca, sa = ca_ref[...], sa_ref[...]
    aq = _rope_apply(proj(C_AQ, C_AK), ca, sa, 8, 32, 0)
    qa_ref[0] = (aq * (32.0 ** -0.5 * LOG2E)).T.astype(BF)
    ka_ref[0] = _rope_apply(proj(C_AK, C_AV), ca, sa, 8, 32, 0).astype(BF)
    va_ref[0] = proj(C_AV, C_CQ).T.astype(BF)

    cb, sb = cb_ref[...], sb_ref[...]
    cq = _rms(proj(C_CQ, C_CKV), gq_ref[...]).astype(BF)
    qb = _rope_apply(_dot(cq, wuq_ref[...]), cb, sb, 32, 128, 64)
    qb_ref[0] = (qb * (96.0 ** -0.5 * LOG2E)).T.astype(BF)
    ckv = _rms(proj(C_CKV, C_KPE), gkv_ref[...]).astype(BF)
    kpe = _rope_apply(proj(C_KPE, C_ZC), cb, sb, 32, 128, 64)
    kb_ref[0] = (_dot(ckv, wuk_ref[...]) + kpe).astype(BF)
    vb_ref[0] = _dot(ckv, wuv_ref[...]).T.astype(BF)

    zc = jax.nn.gelu(proj(C_ZC, C_DQ))
    u = zc[:, :GROUP]
    vn = _rms(zc[:, GROUP:], gsv_ref[...]).astype(BF)
    r = lax.broadcasted_iota(jnp.int32, (128, 128), 0)
    c = lax.broadcasted_iota(jnp.int32, (128, 128), 1)
    wcat = jnp.concatenate(
        [jnp.where(r >= c, sguw_ref[g], 0.0).astype(BF) for g in range(4)], axis=1)
    lane = lax.broadcasted_iota(jnp.int32, (1, GROUP), 1)
    bias = sgub_ref[...]
    gso = gso_ref[...]
    for ch in range(tm // 128):
        vc = vn[ch * 128:(ch + 1) * 128]
        vst = jnp.concatenate(
            [jnp.where(lane // 64 == g, vc, jnp.zeros_like(vc)) for g in range(4)], axis=0)
        sv = _dot(wcat, vst) + bias
        oc = u[ch * 128:(ch + 1) * 128] * sv
        oc_ref[0, ch * 128:(ch + 1) * 128, :] = _rms(oc, gso).astype(BF)

    cd, sd = cd_ref[...], sd_ref[...]
    dq = _rope_apply(proj(C_DQ, C_DK), cd, sd, 16, 64, 0)
    qd_ref[0] = (dq * 0.125).astype(BF)
    kd_ref[0] = _rope_apply(proj(C_DK, C_DV), cd, sd, 16, 64, 0).astype(BF)
    vd_ref[0] = proj(C_DV, C_END).astype(BF)


def _inproj(x, gmix, w_cat, tabs, gq, gkv, wuq, wuk, wuv, gsv, sguw, sgub, gso, *, tm):
    B, S, D = x.shape
    ns = S // tm
    full = lambda shape: pl.BlockSpec(shape, lambda s, b: (0,) * len(shape))
    tab = pl.BlockSpec((tm, LANES), lambda s, b: (s, 0))
    seq = lambda w: pl.BlockSpec((1, tm, w), lambda s, b: (b, s, 0))
    seq_t = lambda w: pl.BlockSpec((1, w, tm), lambda s, b: (b, 0, s))
    out_w = [256, 256, 256, 512, 512, 256, 256, 256, 256, 256]
    transposed = [True, False, True, True, False, True, False, False, False, False]
    return pl.pallas_call(
        functools.partial(_inproj_body, tm=tm),
        grid=(ns, B),
        in_specs=[seq(D), full((1, D)), full((D, C_END))] + [tab] * 6 + [
            full((1, 256)), full((1, 128)), full((256, 512)), full((128, 512)), full((128, 256)),
            full((1, 256)), full((4, 128, 128)), full((128, 256)), full((1, 256))],
        out_specs=[seq_t(w) if tr else seq(w) for w, tr in zip(out_w, transposed)],
        out_shape=[jax.ShapeDtypeStruct((B, w, S) if tr else (B, S, w), BF)
                   for w, tr in zip(out_w, transposed)],
        compiler_params=pltpu.CompilerParams(
            dimension_semantics=("arbitrary", "arbitrary"), vmem_limit_bytes=VMEM_LIMIT),
        name="inproj",
    )(x, gmix, w_cat, *tabs, gq, gkv, wuq, wuk, wuv, gsv, sguw, sgub, gso)


def _head_expand(cols, lane, width):
    out = cols[-1]
    for h in range(len(cols) - 2, -1, -1):
        out = jnp.where(lane // width == h, cols[h], out)
    return out


def _vstack(v, lane):
    return jnp.concatenate(
        [jnp.where(lane // 64 == h, v, jnp.zeros_like(v)) for h in range(4)], axis=0)


QB = 512


def _online_softmax_t(st, m_ref, l_ref, j, cs):
    m_prev = m_ref[j:j + 1, cs]
    m_new = jnp.maximum(m_prev, jnp.max(st, axis=0, keepdims=True))
    alpha = jnp.exp2(m_prev - m_new)
    et = jnp.exp2(st - m_new)
    l_ref[j:j + 1, cs] = alpha * l_ref[j:j + 1, cs] + jnp.sum(et, axis=0, keepdims=True)
    m_ref[j:j + 1, cs] = m_new
    return et.astype(BF), alpha


def _causal_pairs(n):
    pairs = [(i, j) for i in range(n) for j in range(i + 1)]
    return (jnp.asarray([p[0] for p in pairs], jnp.int32),
            jnp.asarray([p[1] for p in pairs], jnp.int32))


def _attn_a_body(qtab_ref, ktab_ref, lam_ref, gh_ref, q_ref, k_ref, v_ref, o_ref,
                 qs_ref, m_ref, l_ref, acc_ref, *, t, lam_init):
    qi = qtab_ref[pl.program_id(1)]
    ki = ktab_ref[pl.program_id(1)]

    @pl.when(ki == 0)
    def _init():
        m_ref[...] = jnp.full(m_ref.shape, NEG, F32)
        l_ref[...] = jnp.zeros(l_ref.shape, F32)
        acc_ref[...] = jnp.zeros(acc_ref.shape, F32)
        qt = q_ref[0]
        row = lax.broadcasted_iota(jnp.int32, (GROUP, 1), 0)
        for j in range(8):
            qs_ref[j] = jnp.where(row // 32 == j, qt, jnp.zeros_like(qt))

    def step(masked):
        k = k_ref[0]
        jobs = [(mp, h, c) for mp in range(2) for h in range(4) for c in range(t // QB)]

        def qk(mp, h, c):
            return _dot(k, qs_ref[2 * h + mp, :, c * QB:(c + 1) * QB])

        def pv(mp, h, c, pt, alpha):
            hs, cs = slice(64 * h, 64 * (h + 1)), slice(c * QB, (c + 1) * QB)
            acc_ref[mp, hs, cs] = acc_ref[mp, hs, cs] * alpha + _dot(v_ref[0, hs, :], pt)

        st_next = qk(*jobs[0])
        pending = None
        for n, (mp, h, c) in enumerate(jobs):
            st = st_next
            if n + 1 < len(jobs):
                st_next = qk(*jobs[n + 1])
            if masked:
                keep = (lax.broadcasted_iota(jnp.int32, (t, QB), 0)
                        <= lax.broadcasted_iota(jnp.int32, (t, QB), 1) + c * QB)
                st = jnp.where(keep, st, NEG)
            pt, alpha = _online_softmax_t(st, m_ref, l_ref, 2 * h + mp, slice(c * QB, (c + 1) * QB))
            if pending is not None:
                pv(*pending)
            pending = (mp, h, c, pt, alpha)
        pv(*pending)

    @pl.when(ki < qi)
    def _off():
        step(False)

    @pl.when(ki == qi)
    def _diag():
        step(True)
        lf = lam_ref[...]
        lam = (jnp.exp(jnp.sum(lf[0:1] * lf[1:2], axis=-1, keepdims=True))
               - jnp.exp(jnp.sum(lf[2:3] * lf[3:4], axis=-1, keepdims=True)) + lam_init)
        rows = []
        for h in range(4):
            hs = slice(64 * h, 64 * (h + 1))
            oh = (acc_ref[0, hs, :] * (1.0 / l_ref[2 * h:2 * h + 1, :])
                  - lam * (acc_ref[1, hs, :] * (1.0 / l_ref[2 * h + 1:2 * h + 2, :])))
            ms = jnp.mean(oh * oh, axis=0, keepdims=True)
            rows.append(oh * lax.rsqrt(ms + EPS))
        ot = jnp.concatenate(rows, axis=0)
        o_ref[0] = (ot.T * gh_ref[...] * (1.0 - lam_init)).astype(BF)


def _attn_a(lam_p, gh, q, k, v, *, t, lam_init):
    B, S, _ = k.shape
    n = S // t
    qtab, ktab = _causal_pairs(n)
    grid_spec = pltpu.PrefetchScalarGridSpec(
        num_scalar_prefetch=2,
        grid=(B, qtab.shape[0]),
        in_specs=[pl.BlockSpec((4, 32), lambda b, p, qt, kt: (0, 0)),
                  pl.BlockSpec((1, GROUP), lambda b, p, qt, kt: (0, 0)),
                  pl.BlockSpec((1, GROUP, t), lambda b, p, qt, kt: (b, 0, qt[p])),
                  pl.BlockSpec((1, t, GROUP), lambda b, p, qt, kt: (b, kt[p], 0)),
                  pl.BlockSpec((1, GROUP, t), lambda b, p, qt, kt: (b, 0, kt[p]))],
        out_specs=pl.BlockSpec((1, t, GROUP), lambda b, p, qt, kt: (b, qt[p], 0)),
        scratch_shapes=[pltpu.VMEM((8, GROUP, t), BF),
                        pltpu.VMEM((8, t), F32),
                        pltpu.VMEM((8, t), F32),
                        pltpu.VMEM((2, GROUP, t), F32)])
    return pl.pallas_call(
        functools.partial(_attn_a_body, t=t, lam_init=lam_init),
        grid_spec=grid_spec,
        out_shape=jax.ShapeDtypeStruct((B, S, GROUP), BF),
        compiler_params=pltpu.CompilerParams(
            dimension_semantics=("arbitrary", "arbitrary"), vmem_limit_bytes=VMEM_LIMIT),
        name="attn_diff",
    )(qtab, ktab, lam_p, gh, q, k, v)


def _attn_b_body(qtab_ref, ktab_ref, g_ref, q_ref, k_ref, v_ref, o_ref, m_ref, l_ref, acc_ref, *, t):
    qi = qtab_ref[pl.program_id(1)]
    ki = ktab_ref[pl.program_id(1)]

    @pl.when(ki == 0)
    def _init():
        m_ref[...] = jnp.full(m_ref.shape, NEG, F32)
        l_ref[...] = jnp.zeros(l_ref.shape, F32)
        acc_ref[...] = jnp.zeros(acc_ref.shape, F32)

    def step(masked):
        jobs = [(h, c) for h in range(4) for c in range(t // QB)]

        def qk(h, c):
            return _dot(k_ref[0, :, h * 128:(h + 1) * 128],
                        q_ref[0, h * 128:(h + 1) * 128, c * QB:(c + 1) * QB])

        def pv(h, c, pt, alpha):
            hs, cs = slice(64 * h, 64 * (h + 1)), slice(c * QB, (c + 1) * QB)
            acc_ref[hs, cs] = acc_ref[hs, cs] * alpha + _dot(v_ref[0, hs, :], pt)

        st_next = qk(*jobs[0])
        pending = None
        for n, (h, c) in enumerate(jobs):
            st = st_next
            if n + 1 < len(jobs):
                st_next = qk(*jobs[n + 1])
            if masked:
                keep = (lax.broadcasted_iota(jnp.int32, (t, QB), 0)
                        <= lax.broadcasted_iota(jnp.int32, (t, QB), 1) + c * QB)
                st = jnp.where(keep, st, NEG)
            pt, alpha = _online_softmax_t(st, m_ref, l_ref, h, slice(c * QB, (c + 1) * QB))
            if pending is not None:
                pv(*pending)
            pending = (h, c, pt, alpha)
        pv(*pending)

    @pl.when(ki < qi)
    def _off():
        step(False)

    @pl.when(ki == qi)
    def _diag():
        step(True)
        rows = [acc_ref[64 * h:64 * (h + 1), :] * (1.0 / l_ref[h:h + 1, :]) for h in range(4)]
        ot = jnp.concatenate(rows, axis=0)
        o_ref[0] = _rms(ot.T, g_ref[...]).astype(BF)


def _attn_b(g, q, k, v, *, t):
    B, S, _ = k.shape
    n = S // t
    qtab, ktab = _causal_pairs(n)
    grid_spec = pltpu.PrefetchScalarGridSpec(
        num_scalar_prefetch=2,
        grid=(B, qtab.shape[0]),
        in_specs=[pl.BlockSpec((1, GROUP), lambda b, p, qt, kt: (0, 0)),
                  pl.BlockSpec((1, 512, t), lambda b, p, qt, kt: (b, 0, qt[p])),
                  pl.BlockSpec((1, t, 512), lambda b, p, qt, kt: (b, kt[p], 0)),
                  pl.BlockSpec((1, GROUP, t), lambda b, p, qt, kt: (b, 0, kt[p]))],
        out_specs=pl.BlockSpec((1, t, GROUP), lambda b, p, qt, kt: (b, qt[p], 0)),
        scratch_shapes=[pltpu.VMEM((4, t), F32),
                        pltpu.VMEM((4, t), F32),
                        pltpu.VMEM((GROUP, t), F32)])
    return pl.pallas_call(
        functools.partial(_attn_b_body, t=t),
        grid_spec=grid_spec,
        out_shape=jax.ShapeDtypeStruct((B, S, GROUP), BF),
        compiler_params=pltpu.CompilerParams(
            dimension_semantics=("arbitrary", "arbitrary"), vmem_limit_bytes=VMEM_LIMIT),
        name="attn_mla",
    )(qtab, ktab, g, q, k, v)


def _swa_body(q_ref, kp_ref, vp_ref, kc_ref, vc_ref, o_ref, lse_ref, *, tq):
    i = pl.program_id(1)
    lane = lax.broadcasted_iota(jnp.int32, (1, GROUP), 1)
    q = q_ref[0]
    kw = jnp.concatenate([kp_ref[0], kc_ref[0]], axis=0)
    vw = jnp.concatenate([vp_ref[0], vc_ref[0]], axis=0)
    a = lax.broadcasted_iota(jnp.int32, (128, 256), 0)
    c = lax.broadcasted_iota(jnp.int32, (128, 256), 1)
    d = c - a
    band = jnp.where(d >= 0, d, 1000) <= 128
    cmin = jnp.where(i > 0, 0, 128)
    band0 = jnp.where(c >= cmin, jnp.where(d >= 0, d, 1000), 1000) <= 128
    for sb in range(tq // 128):
        qs = q[sb * 128:(sb + 1) * 128]
        kwin = kw[sb * 128:sb * 128 + 256]
        vst = _vstack(vw[sb * 128:sb * 128 + 256], lane)
        ok = band0 if sb == 0 else band
        es, ils, lses = [], [], []
        for h in range(4):
            s = _nt_dot(jnp.where(lane // 64 == h, qs, jnp.zeros_like(qs)), kwin)
            s = jnp.where(ok, s, NEG)
            m = jnp.max(s, axis=-1, keepdims=True)
            e = jnp.exp(s - m)
            den = jnp.sum(e, axis=-1, keepdims=True)
            es.append(e.astype(BF))
            ils.append(1.0 / den)
            lses.append(m + jnp.log(den))
        pv = _dot(jnp.concatenate(es, axis=1), vst)
        o_ref[0, sb * 128:(sb + 1) * 128, :] = pv * _head_expand(ils, lane, 64)
        lse_ref[0, sb * 128:(sb + 1) * 128, :] = _head_expand(lses, lane, 64)


def _swa(q, k, v, *, tq):
    N, L, _ = q.shape
    nb = tq // 128
    cur = pl.BlockSpec((1, tq, GROUP), lambda n, i: (n, i, 0))
    prev = pl.BlockSpec((1, 128, GROUP), lambda n, i: (n, jnp.maximum(i * nb - 1, 0), 0))
    return pl.pallas_call(
        functools.partial(_swa_body, tq=tq),
        grid=(N, L // tq),
        in_specs=[cur, prev, prev, cur, cur],
        out_specs=[cur, cur],
        out_shape=[jax.ShapeDtypeStruct((N, L, GROUP), F32)] * 2,
        compiler_params=pltpu.CompilerParams(
            dimension_semantics=("arbitrary", "arbitrary"), vmem_limit_bytes=VMEM_LIMIT),
        name="swa",
    )(q, k, v, k, v)


def _outproj_body(x_ref, oa_ref, ob_ref, oc_ref, o1_ref, l1_ref, o2_ref, l2_ref, o3_ref, l3_ref,
                  gd_ref, w_ref, y_ref):
    l1, l2, l3 = l1_ref[...], l2_ref[...], l3_ref[...]
    mx = jnp.maximum(jnp.maximum(l1, l2), l3)
    w1, w2, w3 = jnp.exp(l1 - mx), jnp.exp(l2 - mx), jnp.exp(l3 - mx)
    od = (w1 * o1_ref[...] + w2 * o2_ref[...] + w3 * o3_ref[...]) / (w1 + w2 + w3)
    od = _rms(od, gd_ref[...]).astype(BF)
    y = _dot(oa_ref[...], w_ref[0:256, :])
    y += _dot(ob_ref[...], w_ref[256:512, :])
    y += _dot(oc_ref[...], w_ref[512:768, :])
    y += _dot(od, w_ref[768:1024, :])
    y_ref[...] = x_ref[...] + y


def _outproj(x, oa, ob, oc, branches, gd, w_out, *, tm):
    T, D = x.shape
    row = lambda w: pl.BlockSpec((tm, w), lambda i: (i, 0))
    full = lambda shape: pl.BlockSpec(shape, lambda i: (0,) * len(shape))
    return pl.pallas_call(
        _outproj_body,
        grid=(T // tm,),
        in_specs=[row(D)] + [row(GROUP)] * 9 + [full((1, GROUP)), full((D, D))],
        out_specs=row(D),
        out_shape=jax.ShapeDtypeStruct((T, D), F32),
        compiler_params=pltpu.CompilerParams(
            dimension_semantics=("arbitrary",), vmem_limit_bytes=VMEM_LIMIT),
        name="outproj",
    )(x, oa, ob, oc, *branches, gd, w_out)


PEER_EB = 1024
N_CAND = PEER_TOP + 1


def _peer_body(x_ref, g_ref, wqt_ref, sk_ref, u_ref, vt_ref, gf_ref, o_ref,
               ht_ref, st_ref, e2_ref, th_ref, cw_ref, hw_ref, acc_ref,
               *, tt, final):
    e = pl.program_id(1)
    nch = tt // LANES

    @pl.when(e == 0)
    def _prologue():
        h = _rms(x_ref[...], g_ref[...])
        ht_ref[...] = h.T.astype(BF)
        qt = _dot(wqt_ref[...], ht_ref[...]).astype(BF)
        for hh in range(PEER_HEADS):
            for c in range(2):
                r0 = hh * N_KEYS + c * 64
                sc = _dot(sk_ref[c], qt[r0:r0 + 64, :])
                for tc in range(nch):
                    blk = sc[:, tc * LANES:(tc + 1) * LANES]
                    st_ref[tc, pl.ds(c * 1024 + hh, N_KEYS, stride=PEER_HEADS), :] = blk
                    if c == 1:
                        st_ref[tc, 2048 + hh * N_KEYS:2048 + (hh + 1) * N_KEYS, :] = blk
        acc_ref[...] = jnp.zeros(acc_ref.shape, F32)

        def top_vals(load, n_rows, n_out):
            top = []
            for r in range(n_rows):
                v = load(r)
                for q in range(len(top)):
                    top[q], v = jnp.maximum(top[q], v), jnp.minimum(top[q], v)
                if len(top) < n_out:
                    top.append(v)
            return top

        def token_chunk(tc, carry):
            ls = pl.ds(pl.multiple_of(tc * LANES, LANES), LANES)
            v1 = top_vals(lambda r: st_ref[tc, r * 8:(r + 1) * 8, :], N_KEYS, N_CAND)
            v2 = top_vals(lambda r: st_ref[tc, 1024 + r * 8:1024 + (r + 1) * 8, :], N_KEYS, N_CAND)
            cands = [v1[a] + v2[b] for a in range(N_CAND) for b in range(N_CAND)
                     if (a + 1) * (b + 1) <= N_CAND]
            top = top_vals(lambda r: cands[r], len(cands), N_CAND)
            z = jnp.ones_like(top[0])
            for r in range(1, PEER_TOP):
                z = z + jnp.exp(top[r] - top[0])
            tau = 0.5 * (top[PEER_TOP - 1] + top[PEER_TOP])
            iz = 1.0 / z
            for r in range(N_KEYS):
                s1 = st_ref[tc, r * 8:(r + 1) * 8, :]
                th_ref[r, :, ls] = tau - s1
                cw_ref[r, :, ls] = jnp.exp(s1 - v1[0]) * iz
            for hh in range(PEER_HEADS):
                rows = slice(2048 + hh * 128, 2048 + (hh + 1) * 128)
                e2_ref[hh * 128:(hh + 1) * 128, ls] = jnp.exp(st_ref[tc, rows, :] - v2[0][hh:hh + 1, :])
            return carry

        lax.fori_loop(0, nch, token_chunk, 0)

    nblk = PEER_EB // 256

    def pre_act(b):
        return _dot(u_ref[b * 256:(b + 1) * 256, :], ht_ref[...])

    def gate(b, a):
        i0 = e * (PEER_EB // N_KEYS) + 2 * b
        th = [th_ref[i0], th_ref[i0 + 1]]
        cw = [cw_ref[i0], cw_ref[i0 + 1]]
        for tc in range(nch):
            ls = slice(tc * LANES, (tc + 1) * LANES)
            for sub in range(2):
                w = [None, None]
                for hh in range(PEER_HEADS):
                    r0 = hh * N_KEYS + sub * 64
                    s2 = st_ref[tc, 2048 + r0:2048 + r0 + 64, :]
                    e2 = e2_ref[r0:r0 + 64, ls]
                    for half in range(2):
                        sel = jnp.where(s2 >= th[half][hh:hh + 1, ls], e2, 0.0) * cw[half][hh:hh + 1, ls]
                        w[half] = sel if w[half] is None else w[half] + sel
                for half in range(2):
                    r1 = half * N_KEYS + sub * 64
                    hw_ref[b % 2, r1:r1 + 64, ls] = (jax.nn.gelu(a[r1:r1 + 64, ls]) * w[half]).astype(BF)

    a = pre_act(0)
    for b in range(nblk):
        a_next = pre_act(b + 1) if b + 1 < nblk else None
        gate(b, a)
        if b > 0:
            acc_ref[...] += _dot(vt_ref[b - 1], hw_ref[(b - 1) % 2])
        a = a_next
    acc_ref[...] += _dot(vt_ref[nblk - 1], hw_ref[(nblk - 1) % 2])

    @pl.when(e == pl.num_programs(1) - 1)
    def _finish():
        y = x_ref[...] + acc_ref[...].T
        if final:
            y = _rms(y, gf_ref[...])
        o_ref[...] = y


def _peer(x, g, wqt, sk, u, vt, gf, *, tt, final):
    T, D = x.shape
    ne = u.shape[0] // PEER_EB
    const = lambda shape: pl.BlockSpec(shape, lambda i, e: (0,) * len(shape),
                                       pipeline_mode=pl.Buffered(1))
    return pl.pallas_call(
        functools.partial(_peer_body, tt=tt, final=final),
        grid=(T // tt, ne),
        in_specs=[pl.BlockSpec((tt, D), lambda i, e: (i, 0)),
                  const((1, D)), const((D, D)), const((2, N_KEYS, 64)),
                  pl.BlockSpec((PEER_EB, D), lambda i, e: (e, 0)),
                  pl.BlockSpec((PEER_EB // 256, D, 256), lambda i, e: (e, 0, 0)),
                  const((1, D))],
        out_specs=pl.BlockSpec((tt, D), lambda i, e: (i, 0)),
        out_shape=jax.ShapeDtypeStruct((T, D), F32),
        scratch_shapes=[pltpu.VMEM((D, tt), BF),
                        pltpu.VMEM((tt // LANES, 3 * 1024, LANES), F32),
                        pltpu.VMEM((1024, tt), F32),
                        pltpu.VMEM((N_KEYS, PEER_HEADS, tt), F32),
                        pltpu.VMEM((N_KEYS, PEER_HEADS, tt), F32),
                        pltpu.VMEM((2, 256, tt), BF),
                        pltpu.VMEM((D, tt), F32)],
        compiler_params=pltpu.CompilerParams(
            dimension_semantics=("arbitrary", "arbitrary"), vmem_limit_bytes=VMEM_LIMIT),
        name="peer",
    )(x, g, wqt, sk, u, vt, gf)


def _pack_in_weights(w_in):
    aq, ak, av, cq, ckv, kpe, zc, dq, dk, dv = jnp.split(
        w_in, np.cumsum([256, 256, 256, 256, 128, 32, 512, 256, 256])[:].tolist(), axis=1)
    kpe_rep = jnp.zeros((D_MODEL, 4, 128), w_in.dtype).at[:, :, 64:96].set(kpe[:, None, :])
    return jnp.concatenate(
        [aq, ak, av, cq, ckv, kpe_rep.reshape(D_MODEL, 512), zc, dq, dk, dv], axis=1).astype(BF)


def _pack_mla_weights(w_uq, w_ukv):
    uq = jnp.pad(w_uq.reshape(256, 4, 96), ((0, 0), (0, 0), (0, 32))).reshape(256, 512)
    ukv = w_ukv.reshape(128, 4, 128)
    uk = jnp.pad(ukv[:, :, :64], ((0, 0), (0, 0), (0, 64))).reshape(128, 512)
    uv = ukv[:, :, 64:].reshape(128, 256)
    return uq.astype(BF), uk.astype(BF), uv.astype(BF)


def _pack_expert_out(v):
    n, d = v.shape
    return v.reshape(n // 256, 256, d).transpose(0, 2, 1).astype(BF)


def _deinterleave(a, d):
    B, S, W = a.shape
    return a.reshape(B, S // d, d, W).transpose(0, 2, 1, 3).reshape(B * d, S // d, W)


def _interleave(a, d, B):
    N, L, W = a.shape
    return a.reshape(B, d, L, W).transpose(0, 2, 1, 3).reshape(B, L * d, W)


def kernel(x, w_in, w_out, norm_mix, norm_ffn, diff_lambda, diff_head_norm, mla_q_norm, mla_kv_norm, mla_w_uq, mla_w_ukv, mla_out_norm, sgu_v_norm, sgu_w, sgu_b, sgu_out_norm, dil_out_norm, peer_w_q, peer_sub_keys, peer_u, peer_v, final_norm):
    B, S, D = x.shape
    depth = w_in.shape[0]
    T = B * S
    tm = min(512, S)
    ta = min(512, S)
    tt = min(512, T)
    row = lambda a: a.reshape(1, -1)

    tabs = (_rope_tables(S, 8, 32, 0) + _rope_tables(S, 32, 128, 64) + _rope_tables(S, 16, 64, 0))

    for l in range(depth):
        lam_init = 0.8 - 0.6 * math.exp(-0.3 * l)
        w_cat = _pack_in_weights(w_in[l])
        wuq, wuk, wuv = _pack_mla_weights(mla_w_uq[l], mla_w_ukv[l])
        sgub = jnp.repeat(sgu_b[l].T, 64, axis=1)
        qa, ka, va, qb, kb, vb, oc, qd, kd, vd = _inproj(
            x, row(norm_mix[l]), w_cat, tabs, row(mla_q_norm[l]), row(mla_kv_norm[l]),
            wuq, wuk, wuv, row(sgu_v_norm[l]), sgu_w[l], sgub, row(sgu_out_norm[l]), tm=tm)

        oa = _attn_a(diff_lambda[l], row(diff_head_norm[l]), qa, ka, va, t=ta, lam_init=lam_init)
        ob = _attn_b(row(mla_out_norm[l]), qb, kb, vb, t=ta)

        branches = []
        for dil in (1, 4, 16):
            if dil == 1:
                o, lse = _swa(qd, kd, vd, tq=min(512, S))
            else:
                qq, kk, vv = (_deinterleave(a, dil) for a in (qd, kd, vd))
                o, lse = _swa(qq, kk, vv, tq=min(512, S // dil))
                o, lse = _interleave(o, dil, B), _interleave(lse, dil, B)
            branches += [o.reshape(T, GROUP), lse.reshape(T, GROUP)]

        x2 = _outproj(x.reshape(T, D), oa.reshape(T, GROUP), ob.reshape(T, GROUP),
                      oc.reshape(T, GROUP), branches, row(dil_out_norm[l]),
                      w_out[l].astype(BF), tm=min(512, T))

        x2 = _peer(x2, row(norm_ffn[l]), peer_w_q[l].T.astype(BF), peer_sub_keys[l].astype(BF),
                   peer_u[l].astype(BF), _pack_expert_out(peer_v[l]), row(final_norm),
                   tt=tt, final=(l == depth - 1))
        x = x2.reshape(B, S, D)
    return x
```

```python
import functools
import math

import numpy as np
import jax
import jax.numpy as jnp
from jax import lax
from jax.experimental import pallas as pl
from jax.experimental.pallas import tpu as pltpu

D_MODEL = 1024
GROUP = 256
ROPE_THETA = 500000.0
NEG = -1e30
EPS = 1e-6
LANES = 128
LOG2E = math.log2(math.e)

N_KEYS = 128
PEER_HEADS = 8
PEER_TOP = 16

BF = jnp.bfloat16
F32 = jnp.float32

C_AQ, C_AK, C_AV, C_CQ, C_CKV, C_KPE, C_ZC, C_DQ, C_DK, C_DV, C_END = (
    0, 256, 512, 768, 1024, 1152, 1664, 2176, 2432, 2688, 2944)

VMEM_LIMIT = 56 * 1024 * 1024


def _rms(x, g):
    return x * lax.rsqrt(jnp.mean(x * x, axis=-1, keepdims=True) + EPS) * g


def _nt_dot(a, b):
    return lax.dot_general(a, b, (((1,), (1,)), ((), ())), preferred_element_type=F32)


def _dot(a, b):
    return jnp.dot(a, b, preferred_element_type=F32)


def _rope_tables(seq, n_rot, period, offset):
    half = n_rot // 2
    pos = jnp.arange(seq, dtype=F32)
    inv = ROPE_THETA ** (-jnp.arange(half, dtype=F32) * (2.0 / n_rot))
    ang = pos[:, None] * inv[None, :]
    cos, sin = jnp.cos(ang), jnp.sin(ang)
    g = np.arange(LANES) % period - offset
    rot = (g >= 0) & (g < n_rot)
    idx = np.where(rot, g % half, 0)
    sign = np.where(g < half, -1.0, 1.0).astype(np.float32)
    cos_t = jnp.where(rot[None, :], cos[:, idx], 1.0)
    sin_t = jnp.where(rot[None, :], sin[:, idx] * sign[None, :], 0.0)
    return cos_t.astype(F32), sin_t.astype(F32)


def _rope_apply(x, cos_t, sin_t, n_rot, period, offset):
    half = n_rot // 2
    lane = lax.broadcasted_iota(jnp.int32, (1, LANES), 1)
    first = (lane % period - offset) < half
    outs = []
    for c in range(x.shape[1] // LANES):
        xc = x[:, c * LANES:(c + 1) * LANES]
        fwd = pltpu.roll(xc, LANES - half, 1)
        bwd = pltpu.roll(xc, half, 1)
        outs.append(xc * cos_t + jnp.where(first, fwd, bwd) * sin_t)
    return outs[0] if len(outs) == 1 else jnp.concatenate(outs, axis=1)


def _inproj_body(x_ref, gmix_ref, w_ref, ca_ref, sa_ref, cb_ref, sb_ref, cd_ref, sd_ref,
                 gq_ref, gkv_ref, wuq_ref, wuk_ref, wuv_ref,
                 gsv_ref, sguw_ref, sgub_ref, gso_ref,
                 qa_ref, ka_ref, va_ref, qb_ref, kb_ref, vb_ref, oc_ref,
                 qd_ref, kd_ref, vd_ref, *, tm):
    x = x_ref[0]
    h = _rms(x, gmix_ref[...]).astype(BF)

    def proj(a, b):
        return _dot(h, w_ref[:, a:b])

    ca, sa = ca_ref[...], sa_ref[...]
    aq = _rope_apply(proj(C_AQ, C_AK), ca, sa, 8, 32, 0)
    qa_ref[0] = (aq * (32.0 ** -0.5 * LOG2E)).T.astype(BF)
    ka_ref[0] = _rope_apply(proj(C_AK, C_AV), ca, sa, 8, 32, 0).astype(BF)
    va_ref[0] = proj(C_AV, C_CQ).T.astype(BF)

    cb, sb = cb_ref[...], sb_ref[...]
    cq = _rms(proj(C_CQ, C_CKV), gq_ref[...]).astype(BF)
    qb = _rope_apply(_dot(cq, wuq_ref[...]), cb, sb, 32, 128, 64)
    qb_ref[0] = (qb * (96.0 ** -0.5 * LOG2E)).T.astype(BF)
    ckv = _rms(proj(C_CKV, C_KPE), gkv_ref[...]).astype(BF)
    kpe = _rope_apply(proj(C_KPE, C_ZC), cb, sb, 32, 128, 64)
    kb_ref[0] = (_dot(ckv, wuk_ref[...]) + kpe).astype(BF)
    vb_ref[0] = _dot(ckv, wuv_ref[...]).T.astype(BF)

    zc = jax.nn.gelu(proj(C_ZC, C_DQ))
    u = zc[:, :GROUP]
    vn = _rms(zc[:, GROUP:], gsv_ref[...]).astype(BF)
    r = lax.broadcasted_iota(jnp.int32, (128, 128), 0)
    c = lax.broadcasted_iota(jnp.int32, (128, 128), 1)
    wcat = jnp.concatenate(
        [jnp.where(r >= c, sguw_ref[g], 0.0).astype(BF) for g in range(4)], axis=1)
    lane = lax.broadcasted_iota(jnp.int32, (1, GROUP), 1)
    bias = sgub_ref[...]
    gso = gso_ref[...]
    for ch in range(tm // 128):
        vc = vn[ch * 128:(ch + 1) * 128]
        vst = jnp.concatenate(
            [jnp.where(lane // 64 == g, vc, jnp.zeros_like(vc)) for g in range(4)], axis=0)
        sv = _dot(wcat, vst) + bias
        oc = u[ch * 128:(ch + 1) * 128] * sv
        oc_ref[0, ch * 128:(ch + 1) * 128, :] = _rms(oc, gso).astype(BF)

    cd, sd = cd_ref[...], sd_ref[...]
    dq = _rope_apply(proj(C_DQ, C_DK), cd, sd, 16, 64, 0)
    qd_ref[0] = (dq * 0.125).astype(BF)
    kd_ref[0] = _rope_apply(proj(C_DK, C_DV), cd, sd, 16, 64, 0).astype(BF)
    vd_ref[0] = proj(C_DV, C_END).astype(BF)


def _inproj(x, gmix, w_cat, tabs, gq, gkv, wuq, wuk, wuv, gsv, sguw, sgub, gso, *, tm):
    B, S, D = x.shape
    ns = S // tm
    full = lambda shape: pl.BlockSpec(shape, lambda s, b: (0,) * len(shape))
    tab = pl.BlockSpec((tm, LANES), lambda s, b: (s, 0))
    seq = lambda w: pl.BlockSpec((1, tm, w), lambda s, b: (b, s, 0))
    seq_t = lambda w: pl.BlockSpec((1, w, tm), lambda s, b: (b, 0, s))
    out_w = [256, 256, 256, 512, 512, 256, 256, 256, 256, 256]
    transposed = [True, False, True, True, False, True, False, False, False, False]
    return pl.pallas_call(
        functools.partial(_inproj_body, tm=tm),
        grid=(ns, B),
        in_specs=[seq(D), full((1, D)), full((D, C_END))] + [tab] * 6 + [
            full((1, 256)), full((1, 128)), full((256, 512)), full((128, 512)), full((128, 256)),
            full((1, 256)), full((4, 128, 128)), full((128, 256)), full((1, 256))],
        out_specs=[seq_t(w) if tr else seq(w) for w, tr in zip(out_w, transposed)],
        out_shape=[jax.ShapeDtypeStruct((B, w, S) if tr else (B, S, w), BF)
                   for w, tr in zip(out_w, transposed)],
        compiler_params=pltpu.CompilerParams(
            dimension_semantics=("arbitrary", "arbitrary"), vmem_limit_bytes=VMEM_LIMIT),
        name="inproj",
    )(x, gmix, w_cat, *tabs, gq, gkv, wuq, wuk, wuv, gsv, sguw, sgub, gso)


def _head_expand(cols, lane, width):
    out = cols[-1]
    for h in range(len(cols) - 2, -1, -1):
        out = jnp.where(lane // width == h, cols[h], out)
    return out


def _vstack(v, lane):
    return jnp.concatenate(
        [jnp.where(lane // 64 == h, v, jnp.zeros_like(v)) for h in range(4)], axis=0)


QB = 512


def _online_softmax_t(st, m_ref, l_ref, j, cs):
    m_prev = m_ref[j:j + 1, cs]
    m_new = jnp.maximum(m_prev, jnp.max(st, axis=0, keepdims=True))
    alpha = jnp.exp2(m_prev - m_new)
    et = jnp.exp2(st - m_new)
    l_ref[j:j + 1, cs] = alpha * l_ref[j:j + 1, cs] + jnp.sum(et, axis=0, keepdims=True)
    m_ref[j:j + 1, cs] = m_new
    return et.astype(BF), alpha


def _causal_pairs(n):
    pairs = [(i, j) for i in range(n) for j in range(i + 1)]
    return (jnp.asarray([p[0] for p in pairs], jnp.int32),
            jnp.asarray([p[1] for p in pairs], jnp.int32))


def _attn_a_body(qtab_ref, ktab_ref, lam_ref, gh_ref, q_ref, k_ref, v_ref, o_ref,
                 qs_ref, m_ref, l_ref, acc_ref, *, t, lam_init):
    qi = qtab_ref[pl.program_id(1)]
    ki = ktab_ref[pl.program_id(1)]

    @pl.when(ki == 0)
    def _init():
        m_ref[...] = jnp.full(m_ref.shape, NEG, F32)
        l_ref[...] = jnp.zeros(l_ref.shape, F32)
        acc_ref[...] = jnp.zeros(acc_ref.shape, F32)
        qt = q_ref[0]
        row = lax.broadcasted_iota(jnp.int32, (GROUP, 1), 0)
        for j in range(8):
            qs_ref[j] = jnp.where(row // 32 == j, qt, jnp.zeros_like(qt))

    def step(masked):
        k = k_ref[0]
        jobs = [(mp, h, c) for mp in range(2) for h in range(4) for c in range(t // QB)]

        def qk(mp, h, c):
            return _dot(k, qs_ref[2 * h + mp, :, c * QB:(c + 1) * QB])

        def pv(mp, h, c, pt, alpha):
            hs, cs = slice(64 * h, 64 * (h + 1)), slice(c * QB, (c + 1) * QB)
            acc_ref[mp, hs, cs] = acc_ref[mp, hs, cs] * alpha + _dot(v_ref[0, hs, :], pt)

        st_next = qk(*jobs[0])
        pending = None
        for n, (mp, h, c) in enumerate(jobs):
            st = st_next
            if n + 1 < len(jobs):
                st_next = qk(*jobs[n + 1])
            if masked:
                keep = (lax.broadcasted_iota(jnp.int32, (t, QB), 0)
                        <= lax.broadcasted_iota(jnp.int32, (t, QB), 1) + c * QB)
                st = jnp.where(keep, st, NEG)
            pt, alpha = _online_softmax_t(st, m_ref, l_ref, 2 * h + mp, slice(c * QB, (c + 1) * QB))
            if pending is not None:
                pv(*pending)
            pending = (mp, h, c, pt, alpha)
        pv(*pending)

    @pl.when(ki < qi)
    def _off():
        step(False)

    @pl.when(ki == qi)
    def _diag():
        step(True)
        lf = lam_ref[...]
        lam = (jnp.exp(jnp.sum(lf[0:1] * lf[1:2], axis=-1, keepdims=True))
               - jnp.exp(jnp.sum(lf[2:3] * lf[3:4], axis=-1, keepdims=True)) + lam_init)
        rows = []
        for h in range(4):
            hs = slice(64 * h, 64 * (h + 1))
            oh = (acc_ref[0, hs, :] * (1.0 / l_ref[2 * h:2 * h + 1, :])
                  - lam * (acc_ref[1, hs, :] * (1.0 / l_ref[2 * h + 1:2 * h + 2, :])))
            ms = jnp.mean(oh * oh, axis=0, keepdims=True)
            rows.append(oh * lax.rsqrt(ms + EPS))
        ot = jnp.concatenate(rows, axis=0)
        o_ref[0] = (ot.T * gh_ref[...] * (1.0 - lam_init)).astype(BF)


def _attn_a(lam_p, gh, q, k, v, *, t, lam_init):
    B, S, _ = k.shape
    n = S // t
    qtab, ktab = _causal_pairs(n)
    grid_spec = pltpu.PrefetchScalarGridSpec(
        num_scalar_prefetch=2,
        grid=(B, qtab.shape[0]),
        in_specs=[pl.BlockSpec((4, 32), lambda b, p, qt, kt: (0, 0)),
                  pl.BlockSpec((1, GROUP), lambda b, p, qt, kt: (0, 0)),
                  pl.BlockSpec((1, GROUP, t), lambda b, p, qt, kt: (b, 0, qt[p])),
                  pl.BlockSpec((1, t, GROUP), lambda b, p, qt, kt: (b, kt[p], 0)),
                  pl.BlockSpec((1, GROUP, t), lambda b, p, qt, kt: (b, 0, kt[p]))],
        out_specs=pl.BlockSpec((1, t, GROUP), lambda b, p, qt, kt: (b, qt[p], 0)),
        scratch_shapes=[pltpu.VMEM((8, GROUP, t), BF),
                        pltpu.VMEM((8, t), F32),
                        pltpu.VMEM((8, t), F32),
                        pltpu.VMEM((2, GROUP, t), F32)])
    return pl.pallas_call(
        functools.partial(_attn_a_body, t=t, lam_init=lam_init),
        grid_spec=grid_spec,
        out_shape=jax.ShapeDtypeStruct((B, S, GROUP), BF),
        compiler_params=pltpu.CompilerParams(
            dimension_semantics=("arbitrary", "arbitrary"), vmem_limit_bytes=VMEM_LIMIT),
        name="attn_diff",
    )(qtab, ktab, lam_p, gh, q, k, v)


def _attn_b_body(qtab_ref, ktab_ref, g_ref, q_ref, k_ref, v_ref, o_ref, m_ref, l_ref, acc_ref, *, t):
    qi = qtab_ref[pl.program_id(1)]
    ki = ktab_ref[pl.program_id(1)]

    @pl.when(ki == 0)
    def _init():
        m_ref[...] = jnp.full(m_ref.shape, NEG, F32)
        l_ref[...] = jnp.zeros(l_ref.shape, F32)
        acc_ref[...] = jnp.zeros(acc_ref.shape, F32)

    def step(masked):
        jobs = [(h, c) for h in range(4) for c in range(t // QB)]

        def qk(h, c):
            return _dot(k_ref[0, :, h * 128:(h + 1) * 128],
                        q_ref[0, h * 128:(h + 1) * 128, c * QB:(c + 1) * QB])

        def pv(h, c, pt, alpha):
            hs, cs = slice(64 * h, 64 * (h + 1)), slice(c * QB, (c + 1) * QB)
            acc_ref[hs, cs] = acc_ref[hs, cs] * alpha + _dot(v_ref[0, hs, :], pt)

        st_next = qk(*jobs[0])
        pending = None
        for n, (h, c) in enumerate(jobs):
            st = st_next
            if n + 1 < len(jobs):
                st_next = qk(*jobs[n + 1])
            if masked:
                keep = (lax.broadcasted_iota(jnp.int32, (t, QB), 0)
                        <= lax.broadcasted_iota(jnp.int32, (t, QB), 1) + c * QB)
                st = jnp.where(keep, st, NEG)
            pt, alpha = _online_softmax_t(st, m_ref, l_ref, h, slice(c * QB, (c + 1) * QB))
            if pending is not None:
                pv(*pending)
            pending = (h, c, pt, alpha)
        pv(*pending)

    @pl.when(ki < qi)
    def _off():
        step(False)

    @pl.when(ki == qi)
    def _diag():
        step(True)
        rows = [acc_ref[64 * h:64 * (h + 1), :] * (1.0 / l_ref[h:h + 1, :]) for h in range(4)]
        ot = jnp.concatenate(rows, axis=0)
        o_ref[0] = _rms(ot.T, g_ref[...]).astype(BF)


def _attn_b(g, q, k, v, *, t):
    B, S, _ = k.shape
    n = S // t
    qtab, ktab = _causal_pairs(n)
    grid_spec = pltpu.PrefetchScalarGridSpec(
        num_scalar_prefetch=2,
        grid=(B, qtab.shape[0]),
        in_specs=[pl.BlockSpec((1, GROUP), lambda b, p, qt, kt: (0, 0)),
                  pl.BlockSpec((1, 512, t), lambda b, p, qt, kt: (b, 0, qt[p])),
                  pl.BlockSpec((1, t, 512), lambda b, p, qt, kt: (b, kt[p], 0)),
                  pl.BlockSpec((1, GROUP, t), lambda b, p, qt, kt: (b, 0, kt[p]))],
        out_specs=pl.BlockSpec((1, t, GROUP), lambda b, p, qt, kt: (b, qt[p], 0)),
        scratch_shapes=[pltpu.VMEM((4, t), F32),
                        pltpu.VMEM((4, t), F32),
                        pltpu.VMEM((GROUP, t), F32)])
    return pl.pallas_call(
        functools.partial(_attn_b_body, t=t),
        grid_spec=grid_spec,
        out_shape=jax.ShapeDtypeStruct((B, S, GROUP), BF),
        compiler_params=pltpu.CompilerParams(
            dimension_semantics=("arbitrary", "arbitrary"), vmem_limit_bytes=VMEM_LIMIT),
        name="attn_mla",
    )(qtab, ktab, g, q, k, v)


def _swa_body(q_ref, kp_ref, vp_ref, kc_ref, vc_ref, o_ref, lse_ref, *, tq):
    i = pl.program_id(1)
    lane = lax.broadcasted_iota(jnp.int32, (1, GROUP), 1)
    q = q_ref[0]
    kw = jnp.concatenate([kp_ref[0], kc_ref[0]], axis=0)
    vw = jnp.concatenate([vp_ref[0], vc_ref[0]], axis=0)
    a = lax.broadcasted_iota(jnp.int32, (128, 256), 0)
    c = lax.broadcasted_iota(jnp.int32, (128, 256), 1)
    d = c - a
    band = jnp.where(d >= 0, d, 1000) <= 128
    cmin = jnp.where(i > 0, 0, 128)
    band0 = jnp.where(c >= cmin, jnp.where(d >= 0, d, 1000), 1000) <= 128
    for sb in range(tq // 128):
        qs = q[sb * 128:(sb + 1) * 128]
        kwin = kw[sb * 128:sb * 128 + 256]
        vst = _vstack(vw[sb * 128:sb * 128 + 256], lane)
        ok = band0 if sb == 0 else band
        es, ils, lses = [], [], []
        for h in range(4):
            s = _nt_dot(jnp.where(lane // 64 == h, qs, jnp.zeros_like(qs)), kwin)
            s = jnp.where(ok, s, NEG)
            m = jnp.max(s, axis=-1, keepdims=True)
            e = jnp.exp(s - m)
            den = jnp.sum(e, axis=-1, keepdims=True)
            es.append(e.astype(BF))
            ils.append(1.0 / den)
            lses.append(m + jnp.log(den))
        pv = _dot(jnp.concatenate(es, axis=1), vst)
        o_ref[0, sb * 128:(sb + 1) * 128, :] = pv * _head_expand(ils, lane, 64)
        lse_ref[0, sb * 128:(sb + 1) * 128, :] = _head_expand(lses, lane, 64)


def _swa(q, k, v, *, tq):
    N, L, _ = q.shape
    nb = tq // 128
    cur = pl.BlockSpec((1, tq, GROUP), lambda n, i: (n, i, 0))
    prev = pl.BlockSpec((1, 128, GROUP), lambda n, i: (n, jnp.maximum(i * nb - 1, 0), 0))
    return pl.pallas_call(
        functools.partial(_swa_body, tq=tq),
        grid=(N, L // tq),
        in_specs=[cur, prev, prev, cur, cur],
        out_specs=[cur, cur],
        out_shape=[jax.ShapeDtypeStruct((N, L, GROUP), F32)] * 2,
        compiler_params=pltpu.CompilerParams(
            dimension_semantics=("arbitrary", "arbitrary"), vmem_limit_bytes=VMEM_LIMIT),
        name="swa",
    )(q, k, v, k, v)


def _outproj_body(x_ref, oa_ref, ob_ref, oc_ref, o1_ref, l1_ref, o2_ref, l2_ref, o3_ref, l3_ref,
                  gd_ref, w_ref, y_ref):
    l1, l2, l3 = l1_ref[...], l2_ref[...], l3_ref[...]
    mx = jnp.maximum(jnp.maximum(l1, l2), l3)
    w1, w2, w3 = jnp.exp(l1 - mx), jnp.exp(l2 - mx), jnp.exp(l3 - mx)
    od = (w1 * o1_ref[...] + w2 * o2_ref[...] + w3 * o3_ref[...]) / (w1 + w2 + w3)
    od = _rms(od, gd_ref[...]).astype(BF)
    y = _dot(oa_ref[...], w_ref[0:256, :])
    y += _dot(ob_ref[...], w_ref[256:512, :])
    y += _dot(oc_ref[...], w_ref[512:768, :])
    y += _dot(od, w_ref[768:1024, :])
    y_ref[...] = x_ref[...] + y


def _outproj(x, oa, ob, oc, branches, gd, w_out, *, tm):
    T, D = x.shape
    row = lambda w: pl.BlockSpec((tm, w), lambda i: (i, 0))
    full = lambda shape: pl.BlockSpec(shape, lambda i: (0,) * len(shape))
    return pl.pallas_call(
        _outproj_body,
        grid=(T // tm,),
        in_specs=[row(D)] + [row(GROUP)] * 9 + [full((1, GROUP)), full((D, D))],
        out_specs=row(D),
        out_shape=jax.ShapeDtypeStruct((T, D), F32),
        compiler_params=pltpu.CompilerParams(
            dimension_semantics=("arbitrary",), vmem_limit_bytes=VMEM_LIMIT),
        name="outproj",
    )(x, oa, ob, oc, *branches, gd, w_out)


PEER_EB = 2048
N_CAND = PEER_TOP + 1


def _peer_body(x_ref, g_ref, wqt_ref, sk_ref, u_ref, vt_ref, gf_ref, o_ref,
               ht_ref, st_ref, e2_ref, th_ref, cw_ref, hw_ref, acc_ref,
               *, tt, final):
    e = pl.program_id(1)
    nch = tt // LANES

    @pl.when(e == 0)
    def _prologue():
        h = _rms(x_ref[...], g_ref[...])
        ht_ref[...] = h.T.astype(BF)
        qt = _dot(wqt_ref[...], ht_ref[...]).astype(BF)
        for hh in range(PEER_HEADS):
            for c in range(2):
                r0 = hh * N_KEYS + c * 64
                sc = _dot(sk_ref[c], qt[r0:r0 + 64, :])
                for tc in range(nch):
                    blk = sc[:, tc * LANES:(tc + 1) * LANES]
                    st_ref.at[tc][pl.ds(c * 1024 + hh, N_KEYS, stride=PEER_HEADS), :] = blk
                    if c == 1:
                        st_ref[tc, 2048 + hh * N_KEYS:2048 + (hh + 1) * N_KEYS, :] = blk
        acc_ref[...] = jnp.zeros(acc_ref.shape, F32)

        def top_vals(load, n_rows, n_out):
            top = []
            for r in range(n_rows):
                v = load(r)
                for q in range(len(top)):
                    top[q], v = jnp.maximum(top[q], v), jnp.minimum(top[q], v)
                if len(top) < n_out:
                    top.append(v)
            return top

        def token_chunk(tc, carry):
            ls = pl.ds(pl.multiple_of(tc * LANES, LANES), LANES)
            v1 = top_vals(lambda r: st_ref[tc, r * 8:(r + 1) * 8, :], N_KEYS, N_CAND)
            v2 = top_vals(lambda r: st_ref[tc, 1024 + r * 8:1024 + (r + 1) * 8, :], N_KEYS, N_CAND)
            cands = [v1[a] + v2[b] for a in range(N_CAND) for b in range(N_CAND)
                     if (a + 1) * (b + 1) <= N_CAND]
            top = top_vals(lambda r: cands[r], len(cands), N_CAND)
            z = jnp.ones_like(top[0])
            for r in range(1, PEER_TOP):
                z = z + jnp.exp(top[r] - top[0])
            tau = 0.5 * (top[PEER_TOP - 1] + top[PEER_TOP])
            iz = 1.0 / z
            for r in range(N_KEYS):
                s1 = st_ref[tc, r * 8:(r + 1) * 8, :]
                th_ref[r, :, ls] = tau - s1
                cw_ref[r, :, ls] = jnp.exp(s1 - v1[0]) * iz
            for hh in range(PEER_HEADS):
                rows = slice(2048 + hh * 128, 2048 + (hh + 1) * 128)
                e2_ref[tc, hh * 128:(hh + 1) * 128, :] = jnp.exp(st_ref[tc, rows, :] - v2[0][hh:hh + 1, :])
            return carry

        lax.fori_loop(0, nch, token_chunk, 0)

    nblk = PEER_EB // 256

    def pre_act(b):
        return _dot(u_ref[b * 256:(b + 1) * 256, :], ht_ref[...])

    def gate(b, a):
        i0 = e * (PEER_EB // N_KEYS) + 2 * b
        th = [th_ref[i0], th_ref[i0 + 1]]
        cw = [cw_ref[i0], cw_ref[i0 + 1]]
        for tc in range(nch):
            ls = slice(tc * LANES, (tc + 1) * LANES)
            for sub in range(2):
                w = [None, None]
                for hh in range(PEER_HEADS):
                    r0 = hh * N_KEYS + sub * 64
                    s2 = st_ref[tc, 2048 + r0:2048 + r0 + 64, :]
                    e2 = e2_ref[tc, r0:r0 + 64, :]
                    for half in range(2):
                        sel = jnp.where(s2 >= th[half][hh:hh + 1, ls], e2, 0.0) * cw[half][hh:hh + 1, ls]
                        w[half] = sel if w[half] is None else w[half] + sel
                for half in range(2):
                    r1 = half * N_KEYS + sub * 64
                    hw_ref[b % 2, r1:r1 + 64, ls] = (jax.nn.gelu(a[r1:r1 + 64, ls]) * w[half]).astype(BF)

    a = pre_act(0)
    for b in range(nblk):
        a_next = pre_act(b + 1) if b + 1 < nblk else None
        gate(b, a)
        if b > 0:
            acc_ref[...] += _dot(vt_ref[b - 1], hw_ref[(b - 1) % 2])
        a = a_next
    acc_ref[...] += _dot(vt_ref[nblk - 1], hw_ref[(nblk - 1) % 2])

    @pl.when(e == pl.num_programs(1) - 1)
    def _finish():
        y = x_ref[...] + acc_ref[...].T
        if final:
            y = _rms(y, gf_ref[...])
        o_ref[...] = y


def _peer(x, g, wqt, sk, u, vt, gf, *, tt, final):
    T, D = x.shape
    ne = u.shape[0] // PEER_EB
    const = lambda shape: pl.BlockSpec(shape, lambda i, e: (0,) * len(shape),
                                       pipeline_mode=pl.Buffered(1))
    return pl.pallas_call(
        functools.partial(_peer_body, tt=tt, final=final),
        grid=(T // tt, ne),
        in_specs=[pl.BlockSpec((tt, D), lambda i, e: (i, 0)),
                  const((1, D)), const((D, D)), const((2, N_KEYS, 64)),
                  pl.BlockSpec((PEER_EB, D), lambda i, e: (e, 0)),
                  pl.BlockSpec((PEER_EB // 256, D, 256), lambda i, e: (e, 0, 0)),
                  const((1, D))],
        out_specs=pl.BlockSpec((tt, D), lambda i, e: (i, 0)),
        out_shape=jax.ShapeDtypeStruct((T, D), F32),
        scratch_shapes=[pltpu.VMEM((D, tt), BF),
                        pltpu.VMEM((tt // LANES, 3 * 1024, LANES), F32),
                        pltpu.VMEM((tt // LANES, 1024, LANES), F32),
                        pltpu.VMEM((N_KEYS, PEER_HEADS, tt), F32),
                        pltpu.VMEM((N_KEYS, PEER_HEADS, tt), F32),
                        pltpu.VMEM((2, 256, tt), BF),
                        pltpu.VMEM((D, tt), F32)],
        compiler_params=pltpu.CompilerParams(
            dimension_semantics=("arbitrary", "arbitrary"), vmem_limit_bytes=VMEM_LIMIT),
        name="peer",
    )(x, g, wqt, sk, u, vt, gf)


def _pack_in_weights(w_in):
    aq, ak, av, cq, ckv, kpe, zc, dq, dk, dv = jnp.split(
        w_in, np.cumsum([256, 256, 256, 256, 128, 32, 512, 256, 256])[:].tolist(), axis=1)
    kpe_rep = jnp.zeros((D_MODEL, 4, 128), w_in.dtype).at[:, :, 64:96].set(kpe[:, None, :])
    return jnp.concatenate(
        [aq, ak, av, cq, ckv, kpe_rep.reshape(D_MODEL, 512), zc, dq, dk, dv], axis=1).astype(BF)


def _pack_mla_weights(w_uq, w_ukv):
    uq = jnp.pad(w_uq.reshape(256, 4, 96), ((0, 0), (0, 0), (0, 32))).reshape(256, 512)
    ukv = w_ukv.reshape(128, 4, 128)
    uk = jnp.pad(ukv[:, :, :64], ((0, 0), (0, 0), (0, 64))).reshape(128, 512)
    uv = ukv[:, :, 64:].reshape(128, 256)
    return uq.astype(BF), uk.astype(BF), uv.astype(BF)


def _pack_expert_out(v):
    n, d = v.shape
    return v.reshape(n // 256, 256, d).transpose(0, 2, 1).astype(BF)


def _deinterleave(a, d):
    B, S, W = a.shape
    return a.reshape(B, S // d, d, W).transpose(0, 2, 1, 3).reshape(B * d, S // d, W)


def _interleave(a, d, B):
    N, L, W = a.shape
    return a.reshape(B, d, L, W).transpose(0, 2, 1, 3).reshape(B, L * d, W)


def kernel(x, w_in, w_out, norm_mix, norm_ffn, diff_lambda, diff_head_norm, mla_q_norm, mla_kv_norm, mla_w_uq, mla_w_ukv, mla_out_norm, sgu_v_norm, sgu_w, sgu_b, sgu_out_norm, dil_out_norm, peer_w_q, peer_sub_keys, peer_u, peer_v, final_norm):
    B, S, D = x.shape
    depth = w_in.shape[0]
    T = B * S
    tm = min(512, S)
    ta = min(512, S)
    tt = min(512, T)
    row = lambda a: a.reshape(1, -1)

    tabs = (_rope_tables(S, 8, 32, 0) + _rope_tables(S, 32, 128, 64) + _rope_tables(S, 16, 64, 0))

    for l in range(depth):
        lam_init = 0.8 - 0.6 * math.exp(-0.3 * l)
        w_cat = _pack_in_weights(w_in[l])
        wuq, wuk, wuv = _pack_mla_weights(mla_w_uq[l], mla_w_ukv[l])
        sgub = jnp.repeat(sgu_b[l].T, 64, axis=1)
        qa, ka, va, qb, kb, vb, oc, qd, kd, vd = _inproj(
            x, row(norm_mix[l]), w_cat, tabs, row(mla_q_norm[l]), row(mla_kv_norm[l]),
            wuq, wuk, wuv, row(sgu_v_norm[l]), sgu_w[l], sgub, row(sgu_out_norm[l]), tm=tm)

        oa = _attn_a(diff_lambda[l], row(diff_head_norm[l]), qa, ka, va, t=ta, lam_init=lam_init)
        ob = _attn_b(row(mla_out_norm[l]), qb, kb, vb, t=ta)

        branches = []
        for dil in (1, 4, 16):
            if dil == 1:
                o, lse = _swa(qd, kd, vd, tq=min(512, S))
            else:
                qq, kk, vv = (_deinterleave(a, dil) for a in (qd, kd, vd))
                o, lse = _swa(qq, kk, vv, tq=min(512, S // dil))
                o, lse = _interleave(o, dil, B), _interleave(lse, dil, B)
            branches += [o.reshape(T, GROUP), lse.reshape(T, GROUP)]

        x2 = _outproj(x.reshape(T, D), oa.reshape(T, GROUP), ob.reshape(T, GROUP),
                      oc.reshape(T, GROUP), branches, row(dil_out_norm[l]),
                      w_out[l].astype(BF), tm=min(512, T))

        x2 = _peer(x2, row(norm_ffn[l]), peer_w_q[l].T.astype(BF), peer_sub_keys[l].astype(BF),
                   peer_u[l].astype(BF), _pack_expert_out(peer_v[l]), row(final_norm),
                   tt=tt, final=(l == depth - 1))
        x = x2.reshape(B, S, D)
    return x
```

```python
import functools
import math

import numpy as np
import jax
import jax.numpy as jnp
from jax import lax
from jax.experimental import pallas as pl
from jax.experimental.pallas import tpu as pltpu

D_MODEL = 1024
GROUP = 256
ROPE_THETA = 500000.0
NEG = -1e30
EPS = 1e-6
LANES = 128
LOG2E = math.log2(math.e)

N_KEYS = 128
PEER_HEADS = 8
PEER_TOP = 16

BF = jnp.bfloat16
F32 = jnp.float32

C_AQ, C_AK, C_AV, C_CQ, C_CKV, C_KPE, C_ZC, C_DQ, C_DK, C_DV, C_END = (
    0, 256, 512, 768, 1024, 1152, 1664, 2176, 2432, 2688, 2944)

VMEM_LIMIT = 56 * 1024 * 1024


def _rms(x, g):
    return x * lax.rsqrt(jnp.mean(x * x, axis=-1, keepdims=True) + EPS) * g


def _nt_dot(a, b):
    return lax.dot_general(a, b, (((1,), (1,)), ((), ())), preferred_element_type=F32)


def _dot(a, b):
    return jnp.dot(a, b, preferred_element_type=F32)


def _rope_tables(seq, n_rot, period, offset):
    half = n_rot // 2
    pos = jnp.arange(seq, dtype=F32)
    inv = ROPE_THETA ** (-jnp.arange(half, dtype=F32) * (2.0 / n_rot))
    ang = pos[:, None] * inv[None, :]
    cos, sin = jnp.cos(ang), jnp.sin(ang)
    g = np.arange(LANES) % period - offset
    rot = (g >= 0) & (g < n_rot)
    idx = np.where(rot, g % half, 0)
    sign = np.where(g < half, -1.0, 1.0).astype(np.float32)
    cos_t = jnp.where(rot[None, :], cos[:, idx], 1.0)
    sin_t = jnp.where(rot[None, :], sin[:, idx] * sign[None, :], 0.0)
    return cos_t.astype(F32), sin_t.astype(F32)


def _rope_apply(x, cos_t, sin_t, n_rot, period, offset):
    half = n_rot // 2
    lane = lax.broadcasted_iota(jnp.int32, (1, LANES), 1)
    first = (lane % period - offset) < half
    outs = []
    for c in range(x.shape[1] // LANES):
        xc = x[:, c * LANES:(c + 1) * LANES]
        fwd = pltpu.roll(xc, LANES - half, 1)
        bwd = pltpu.roll(xc, half, 1)
        outs.append(xc * cos_t + jnp.where(first, fwd, bwd) * sin_t)
    return outs[0] if len(outs) == 1 else jnp.concatenate(outs, axis=1)


def _inproj_body(x_ref, gmix_ref, w_ref, ca_ref, sa_ref, cb_ref, sb_ref, cd_ref, sd_ref,
                 gq_ref, gkv_ref, wuq_ref, wuk_ref, wuv_ref,
                 gsv_ref, sguw_ref, sgub_ref, gso_ref,
                 qa_ref, ka_ref, va_ref, qb_ref, kb_ref, vb_ref, oc_ref,
                 qd_ref, kd_ref, vd_ref, *, tm):
    x = x_ref[0]
    h = _rms(x, gmix_ref[...]).astype(BF)

    def proj(a, b):
        return _dot(h, w_ref[:, a:b])

    ca, sa = ca_ref[...], sa_ref[...]
    aq = _rope_apply(proj(C_AQ, C_AK), ca, sa, 8, 32, 0)
    qa_ref[0] = (aq * (32.0 ** -0.5 * LOG2E)).T.astype(BF)
    ka_ref[0] = _rope_apply(proj(C_AK, C_AV), ca, sa, 8, 32, 0).astype(BF)
    va_ref[0] = proj(C_AV, C_CQ).T.astype(BF)

    cb, sb = cb_ref[...], sb_ref[...]
    cq = _rms(proj(C_CQ, C_CKV), gq_ref[...]).astype(BF)
    qb = _rope_apply(_dot(cq, wuq_ref[...]), cb, sb, 32, 128, 64)
    qb_ref[0] = (qb * (96.0 ** -0.5 * LOG2E)).T.astype(BF)
    ckv = _rms(proj(C_CKV, C_KPE), gkv_ref[...]).astype(BF)
    kpe = _rope_apply(proj(C_KPE, C_ZC), cb, sb, 32, 128, 64)
    kb_ref[0] = (_dot(ckv, wuk_ref[...]) + kpe).astype(BF)
    vb_ref[0] = _dot(ckv, wuv_ref[...]).T.astype(BF)

    zc = jax.nn.gelu(proj(C_ZC, C_DQ))
    u = zc[:, :GROUP]
    vn = _rms(zc[:, GROUP:], gsv_ref[...]).astype(BF)
    r = lax.broadcasted_iota(jnp.int32, (128, 128), 0)
    c = lax.broadcasted_iota(jnp.int32, (128, 128), 1)
    wcat = jnp.concatenate(
        [jnp.where(r >= c, sguw_ref[g], 0.0).astype(BF) for g in range(4)], axis=1)
    lane = lax.broadcasted_iota(jnp.int32, (1, GROUP), 1)
    bias = sgub_ref[...]
    gso = gso_ref[...]
    for ch in range(tm // 128):
        vc = vn[ch * 128:(ch + 1) * 128]
        vst = jnp.concatenate(
            [jnp.where(lane // 64 == g, vc, jnp.zeros_like(vc)) for g in range(4)], axis=0)
        sv = _dot(wcat, vst) + bias
        oc = u[ch * 128:(ch + 1) * 128] * sv
        oc_ref[0, ch * 128:(ch + 1) * 128, :] = _rms(oc, gso).astype(BF)

    cd, sd = cd_ref[...], sd_ref[...]
    dq = _rope_apply(proj(C_DQ, C_DK), cd, sd, 16, 64, 0)
    dk = _rope_apply(proj(C_DK, C_DV), cd, sd, 16, 64, 0)
    dv = proj(C_DV, C_END)
    for hf in range(2):
        qd_ref[0, hf] = dq[:, hf * LANES:(hf + 1) * LANES] * 0.125
        kd_ref[0, hf] = dk[:, hf * LANES:(hf + 1) * LANES]
        vd_ref[0, hf] = dv[:, hf * LANES:(hf + 1) * LANES]


def _inproj(x, gmix, w_cat, tabs, gq, gkv, wuq, wuk, wuv, gsv, sguw, sgub, gso, *, tm):
    B, S, D = x.shape
    ns = S // tm
    full = lambda shape: pl.BlockSpec(shape, lambda s, b: (0,) * len(shape))
    tab = pl.BlockSpec((tm, LANES), lambda s, b: (s, 0))
    seq = lambda w: pl.BlockSpec((1, tm, w), lambda s, b: (b, s, 0))
    seq_t = lambda w: pl.BlockSpec((1, w, tm), lambda s, b: (b, 0, s))
    halves = pl.BlockSpec((1, 2, tm, LANES), lambda s, b: (b, 0, s, 0))
    out_w = [256, 256, 256, 512, 512, 256, 256]
    transposed = [True, False, True, True, False, True, False]
    return pl.pallas_call(
        functools.partial(_inproj_body, tm=tm),
        grid=(ns, B),
        in_specs=[seq(D), full((1, D)), full((D, C_END))] + [tab] * 6 + [
            full((1, 256)), full((1, 128)), full((256, 512)), full((128, 512)), full((128, 256)),
            full((1, 256)), full((4, 128, 128)), full((128, 256)), full((1, 256))],
        out_specs=[seq_t(w) if tr else seq(w) for w, tr in zip(out_w, transposed)] + [halves] * 3,
        out_shape=[jax.ShapeDtypeStruct((B, w, S) if tr else (B, S, w), BF)
                   for w, tr in zip(out_w, transposed)]
        + [jax.ShapeDtypeStruct((B, 2, S, LANES), F32)] * 3,
        compiler_params=pltpu.CompilerParams(
            dimension_semantics=("arbitrary", "arbitrary"), vmem_limit_bytes=VMEM_LIMIT),
        name="inproj",
    )(x, gmix, w_cat, *tabs, gq, gkv, wuq, wuk, wuv, gsv, sguw, sgub, gso)


def _head_expand(cols, lane, width):
    out = cols[-1]
    for h in range(len(cols) - 2, -1, -1):
        out = jnp.where(lane // width == h, cols[h], out)
    return out


def _vstack(v, lane):
    return jnp.concatenate(
        [jnp.where(lane // 64 == h, v, jnp.zeros_like(v)) for h in range(4)], axis=0)


QB = 512


def _online_softmax_t(st, m_ref, l_ref, j, cs):
    m_prev = m_ref[j:j + 1, cs]
    m_new = jnp.maximum(m_prev, jnp.max(st, axis=0, keepdims=True))
    alpha = jnp.exp2(m_prev - m_new)
    et = jnp.exp2(st - m_new)
    l_ref[j:j + 1, cs] = alpha * l_ref[j:j + 1, cs] + jnp.sum(et, axis=0, keepdims=True)
    m_ref[j:j + 1, cs] = m_new
    return et.astype(BF), alpha


def _causal_pairs(n):
    pairs = [(i, j) for i in range(n) for j in range(i + 1)]
    return (jnp.asarray([p[0] for p in pairs], jnp.int32),
            jnp.asarray([p[1] for p in pairs], jnp.int32))


def _attn_a_body(qtab_ref, ktab_ref, lam_ref, gh_ref, q_ref, k_ref, v_ref, o_ref,
                 qs_ref, m_ref, l_ref, acc_ref, *, t, lam_init):
    qi = qtab_ref[pl.program_id(1)]
    ki = ktab_ref[pl.program_id(1)]

    @pl.when(ki == 0)
    def _init():
        m_ref[...] = jnp.full(m_ref.shape, NEG, F32)
        l_ref[...] = jnp.zeros(l_ref.shape, F32)
        acc_ref[...] = jnp.zeros(acc_ref.shape, F32)
        qt = q_ref[0]
        row = lax.broadcasted_iota(jnp.int32, (GROUP, 1), 0)
        for j in range(8):
            qs_ref[j] = jnp.where(row // 32 == j, qt, jnp.zeros_like(qt))

    def step(masked):
        k = k_ref[0]
        jobs = [(mp, h, c) for mp in range(2) for h in range(4) for c in range(t // QB)]

        def qk(mp, h, c):
            return _dot(k, qs_ref[2 * h + mp, :, c * QB:(c + 1) * QB])

        def pv(mp, h, c, pt, alpha):
            hs, cs = slice(64 * h, 64 * (h + 1)), slice(c * QB, (c + 1) * QB)
            acc_ref[mp, hs, cs] = acc_ref[mp, hs, cs] * alpha + _dot(v_ref[0, hs, :], pt)

        st_next = qk(*jobs[0])
        pending = None
        for n, (mp, h, c) in enumerate(jobs):
            st = st_next
            if n + 1 < len(jobs):
                st_next = qk(*jobs[n + 1])
            if masked:
                keep = (lax.broadcasted_iota(jnp.int32, (t, QB), 0)
                        <= lax.broadcasted_iota(jnp.int32, (t, QB), 1) + c * QB)
                st = jnp.where(keep, st, NEG)
            pt, alpha = _online_softmax_t(st, m_ref, l_ref, 2 * h + mp, slice(c * QB, (c + 1) * QB))
            if pending is not None:
                pv(*pending)
            pending = (mp, h, c, pt, alpha)
        pv(*pending)

    @pl.when(ki < qi)
    def _off():
        step(False)

    @pl.when(ki == qi)
    def _diag():
        step(True)
        lf = lam_ref[...]
        lam = (jnp.exp(jnp.sum(lf[0:1] * lf[1:2], axis=-1, keepdims=True))
               - jnp.exp(jnp.sum(lf[2:3] * lf[3:4], axis=-1, keepdims=True)) + lam_init)
        rows = []
        for h in range(4):
            hs = slice(64 * h, 64 * (h + 1))
            oh = (acc_ref[0, hs, :] * (1.0 / l_ref[2 * h:2 * h + 1, :])
                  - lam * (acc_ref[1, hs, :] * (1.0 / l_ref[2 * h + 1:2 * h + 2, :])))
            ms = jnp.mean(oh * oh, axis=0, keepdims=True)
            rows.append(oh * lax.rsqrt(ms + EPS))
        ot = jnp.concatenate(rows, axis=0)
        o_ref[0] = (ot.T * gh_ref[...] * (1.0 - lam_init)).astype(BF)


def _attn_a(lam_p, gh, q, k, v, *, t, lam_init):
    B, S, _ = k.shape
    n = S // t
    qtab, ktab = _causal_pairs(n)
    grid_spec = pltpu.PrefetchScalarGridSpec(
        num_scalar_prefetch=2,
        grid=(B, qtab.shape[0]),
        in_specs=[pl.BlockSpec((4, 32), lambda b, p, qt, kt: (0, 0)),
                  pl.BlockSpec((1, GROUP), lambda b, p, qt, kt: (0, 0)),
                  pl.BlockSpec((1, GROUP, t), lambda b, p, qt, kt: (b, 0, qt[p])),
                  pl.BlockSpec((1, t, GROUP), lambda b, p, qt, kt: (b, kt[p], 0)),
                  pl.BlockSpec((1, GROUP, t), lambda b, p, qt, kt: (b, 0, kt[p]))],
        out_specs=pl.BlockSpec((1, t, GROUP), lambda b, p, qt, kt: (b, qt[p], 0)),
        scratch_shapes=[pltpu.VMEM((8, GROUP, t), BF),
                        pltpu.VMEM((8, t), F32),
                        pltpu.VMEM((8, t), F32),
                        pltpu.VMEM((2, GROUP, t), F32)])
    return pl.pallas_call(
        functools.partial(_attn_a_body, t=t, lam_init=lam_init),
        grid_spec=grid_spec,
        out_shape=jax.ShapeDtypeStruct((B, S, GROUP), BF),
        compiler_params=pltpu.CompilerParams(
            dimension_semantics=("arbitrary", "arbitrary"), vmem_limit_bytes=VMEM_LIMIT),
        name="attn_diff",
    )(qtab, ktab, lam_p, gh, q, k, v)


def _attn_b_body(qtab_ref, ktab_ref, g_ref, q_ref, k_ref, v_ref, o_ref, m_ref, l_ref, acc_ref, *, t):
    qi = qtab_ref[pl.program_id(1)]
    ki = ktab_ref[pl.program_id(1)]

    @pl.when(ki == 0)
    def _init():
        m_ref[...] = jnp.full(m_ref.shape, NEG, F32)
        l_ref[...] = jnp.zeros(l_ref.shape, F32)
        acc_ref[...] = jnp.zeros(acc_ref.shape, F32)

    def step(masked):
        jobs = [(h, c) for h in range(4) for c in range(t // QB)]

        def qk(h, c):
            return _dot(k_ref[0, :, h * 128:(h + 1) * 128],
                        q_ref[0, h * 128:(h + 1) * 128, c * QB:(c + 1) * QB])

        def pv(h, c, pt, alpha):
            hs, cs = slice(64 * h, 64 * (h + 1)), slice(c * QB, (c + 1) * QB)
            acc_ref[hs, cs] = acc_ref[hs, cs] * alpha + _dot(v_ref[0, hs, :], pt)

        st_next = qk(*jobs[0])
        pending = None
        for n, (h, c) in enumerate(jobs):
            st = st_next
            if n + 1 < len(jobs):
                st_next = qk(*jobs[n + 1])
            if masked:
                keep = (lax.broadcasted_iota(jnp.int32, (t, QB), 0)
                        <= lax.broadcasted_iota(jnp.int32, (t, QB), 1) + c * QB)
                st = jnp.where(keep, st, NEG)
            pt, alpha = _online_softmax_t(st, m_ref, l_ref, h, slice(c * QB, (c + 1) * QB))
            if pending is not None:
                pv(*pending)
            pending = (h, c, pt, alpha)
        pv(*pending)

    @pl.when(ki < qi)
    def _off():
        step(False)

    @pl.when(ki == qi)
    def _diag():
        step(True)
        rows = [acc_ref[64 * h:64 * (h + 1), :] * (1.0 / l_ref[h:h + 1, :]) for h in range(4)]
        ot = jnp.concatenate(rows, axis=0)
        o_ref[0] = _rms(ot.T, g_ref[...]).astype(BF)


def _attn_b(g, q, k, v, *, t):
    B, S, _ = k.shape
    n = S // t
    qtab, ktab = _causal_pairs(n)
    grid_spec = pltpu.PrefetchScalarGridSpec(
        num_scalar_prefetch=2,
        grid=(B, qtab.shape[0]),
        in_specs=[pl.BlockSpec((1, GROUP), lambda b, p, qt, kt: (0, 0)),
                  pl.BlockSpec((1, 512, t), lambda b, p, qt, kt: (b, 0, qt[p])),
                  pl.BlockSpec((1, t, 512), lambda b, p, qt, kt: (b, kt[p], 0)),
                  pl.BlockSpec((1, GROUP, t), lambda b, p, qt, kt: (b, 0, kt[p]))],
        out_specs=pl.BlockSpec((1, t, GROUP), lambda b, p, qt, kt: (b, qt[p], 0)),
        scratch_shapes=[pltpu.VMEM((4, t), F32),
                        pltpu.VMEM((4, t), F32),
                        pltpu.VMEM((GROUP, t), F32)])
    return pl.pallas_call(
        functools.partial(_attn_b_body, t=t),
        grid_spec=grid_spec,
        out_shape=jax.ShapeDtypeStruct((B, S, GROUP), BF),
        compiler_params=pltpu.CompilerParams(
            dimension_semantics=("arbitrary", "arbitrary"), vmem_limit_bytes=VMEM_LIMIT),
        name="attn_mla",
    )(qtab, ktab, g, q, k, v)


DIL_TILE = 2048
DILATIONS = (1, 4, 16)


def _dil_body(g_ref, q_ref, kp_ref, kc_ref, vp_ref, vc_ref, o_ref, ob_ref, lb_ref, *, td):
    i = pl.program_id(1)
    lane = lax.broadcasted_iota(jnp.int32, (1, GROUP), 1)
    a = lax.broadcasted_iota(jnp.int32, (128, 256), 0)
    c = lax.broadcasted_iota(jnp.int32, (128, 256), 1)
    dist = jnp.where(c - a >= 0, c - a, 1000)

    def rows(ref, start, d):
        idx = pl.ds(start, 128) if d == 1 else pl.ds(start, 128, stride=d)
        return jnp.concatenate([ref.at[0, 0][idx, :], ref.at[0, 1][idx, :]], axis=1).astype(BF)

    for bi, d in enumerate(DILATIONS):
        def block(n, carry, bi=bi, d=d):
            if d == 1:
                r, blk = 0, n
            elif td // d == 128:
                r, blk = n, 0
            else:
                r, blk = n % d, n // d
            q0 = r + d * 128 * blk
            if d == 1:
                q0 = pl.multiple_of(q0, 128)
            q = rows(q_ref, q0, d)
            k_hi, v_hi = rows(kc_ref, q0, d), rows(vc_ref, q0, d)
            lo_prev = td - d * 128 + r
            if isinstance(blk, int):
                first = True
                k_lo, v_lo = rows(kp_ref, lo_prev, d), rows(vp_ref, lo_prev, d)
            else:
                first = blk == 0
                lo_cur = jnp.maximum(q0 - d * 128, 0)
                if d == 1:
                    lo_cur = pl.multiple_of(lo_cur, 128)
                k_lo = jnp.where(first, rows(kp_ref, lo_prev, d), rows(kc_ref, lo_cur, d))
                v_lo = jnp.where(first, rows(vp_ref, lo_prev, d), rows(vc_ref, lo_cur, d))
            kwin = jnp.concatenate([k_lo, k_hi], axis=0)
            vst = _vstack(jnp.concatenate([v_lo, v_hi], axis=0), lane)
            cmin = jnp.where(jnp.logical_and(first, i == 0), 128, 0)
            ok = jnp.where(c >= cmin, dist, 1000) <= 128
            es, ils, lses = [], [], []
            for h in range(4):
                s = _nt_dot(jnp.where(lane // 64 == h, q, jnp.zeros_like(q)), kwin)
                s = jnp.where(ok, s, NEG)
                m = jnp.max(s, axis=-1, keepdims=True)
                ex = jnp.exp(s - m)
                den = jnp.sum(ex, axis=-1, keepdims=True)
                es.append(ex.astype(BF))
                ils.append(1.0 / den)
                lses.append(m + jnp.log(den))
            o = _dot(jnp.concatenate(es, axis=1), vst) * _head_expand(ils, lane, 64)
            lse = _head_expand(lses, lane, 64)
            idx = pl.ds(q0, 128) if d == 1 else pl.ds(q0, 128, stride=d)
            for hf in range(2):
                ob_ref.at[bi, hf][idx, :] = o[:, hf * LANES:(hf + 1) * LANES]
                lb_ref.at[bi, hf][idx, :] = lse[:, hf * LANES:(hf + 1) * LANES]
            return carry

        lax.fori_loop(0, td // 128, block, 0)

    for ch in range(td // 256):
        rs = slice(ch * 256, (ch + 1) * 256)
        halves = []
        for hf in range(2):
            ls = [lb_ref[bi, hf, rs, :] for bi in range(3)]
            mx = jnp.maximum(jnp.maximum(ls[0], ls[1]), ls[2])
            ws = [jnp.exp(l - mx) for l in ls]
            num = ws[0] * ob_ref[0, hf, rs, :] + ws[1] * ob_ref[1, hf, rs, :] + ws[2] * ob_ref[2, hf, rs, :]
            halves.append(num / (ws[0] + ws[1] + ws[2]))
        o_ref[0, rs, :] = _rms(jnp.concatenate(halves, axis=1), g_ref[...]).astype(BF)


def _dil(g, q, k, v, *, td):
    B, _, S, _ = q.shape
    cur = pl.BlockSpec((1, 2, td, LANES), lambda b, i: (b, 0, i, 0))
    prev = pl.BlockSpec((1, 2, td, LANES), lambda b, i: (b, 0, jnp.maximum(i - 1, 0), 0))
    return pl.pallas_call(
        functools.partial(_dil_body, td=td),
        grid=(B, S // td),
        in_specs=[pl.BlockSpec((1, GROUP), lambda b, i: (0, 0)), cur, prev, cur, prev, cur],
        out_specs=pl.BlockSpec((1, td, GROUP), lambda b, i: (b, i, 0)),
        out_shape=jax.ShapeDtypeStruct((B, S, GROUP), BF),
        scratch_shapes=[pltpu.VMEM((3, 2, td, LANES), F32),
                        pltpu.VMEM((3, 2, td, LANES), F32)],
        compiler_params=pltpu.CompilerParams(
            dimension_semantics=("arbitrary", "arbitrary"), vmem_limit_bytes=VMEM_LIMIT),
        name="dilated",
    )(g, q, k, k, v, v)


def _outproj_body(x_ref, oa_ref, ob_ref, oc_ref, od_ref, w_ref, y_ref):
    y = _dot(oa_ref[...], w_ref[0:256, :])
    y += _dot(ob_ref[...], w_ref[256:512, :])
    y += _dot(oc_ref[...], w_ref[512:768, :])
    y += _dot(od_ref[...], w_ref[768:1024, :])
    y_ref[...] = x_ref[...] + y


def _outproj(x, oa, ob, oc, od, w_out, *, tm):
    T, D = x.shape
    row = lambda w: pl.BlockSpec((tm, w), lambda i: (i, 0))
    full = lambda shape: pl.BlockSpec(shape, lambda i: (0,) * len(shape))
    return pl.pallas_call(
        _outproj_body,
        grid=(T // tm,),
        in_specs=[row(D)] + [row(GROUP)] * 4 + [full((D, D))],
        out_specs=row(D),
        out_shape=jax.ShapeDtypeStruct((T, D), F32),
        compiler_params=pltpu.CompilerParams(
            dimension_semantics=("arbitrary",), vmem_limit_bytes=VMEM_LIMIT),
        name="outproj",
    )(x, oa, ob, oc, od, w_out)


PEER_EB = 1024
N_CAND = PEER_TOP + 1


def _peer_body(x_ref, g_ref, wqt_ref, sk_ref, u_ref, vt_ref, gf_ref, o_ref,
               ht_ref, st_ref, e2_ref, th_ref, cw_ref, hw_ref, acc_ref,
               *, tt, final):
    e = pl.program_id(1)
    nch = tt // LANES

    @pl.when(e == 0)
    def _prologue():
        h = _rms(x_ref[...], g_ref[...])
        ht_ref[...] = h.T.astype(BF)
        qt = _dot(wqt_ref[...], ht_ref[...]).astype(BF)
        for hh in range(PEER_HEADS):
            for c in range(2):
                r0 = hh * N_KEYS + c * 64
                sc = _dot(sk_ref[c], qt[r0:r0 + 64, :])
                for tc in range(nch):
                    blk = sc[:, tc * LANES:(tc + 1) * LANES]
                    st_ref.at[tc][pl.ds(c * 1024 + hh, N_KEYS, stride=PEER_HEADS), :] = blk
                    if c == 1:
                        st_ref[tc, 2048 + hh * N_KEYS:2048 + (hh + 1) * N_KEYS, :] = blk
        acc_ref[...] = jnp.zeros(acc_ref.shape, F32)

        def top_vals(load, n_rows, n_out):
            top = []
            for r in range(n_rows):
                v = load(r)
                for q in range(len(top)):
                    top[q], v = jnp.maximum(top[q], v), jnp.minimum(top[q], v)
                if len(top) < n_out:
                    top.append(v)
            return top

        def token_chunk(tc, carry):
            ls = pl.ds(pl.multiple_of(tc * LANES, LANES), LANES)
            v1 = top_vals(lambda r: st_ref[tc, r * 8:(r + 1) * 8, :], N_KEYS, N_CAND)
            v2 = top_vals(lambda r: st_ref[tc, 1024 + r * 8:1024 + (r + 1) * 8, :], N_KEYS, N_CAND)
            cands = [v1[a] + v2[b] for a in range(N_CAND) for b in range(N_CAND)
                     if (a + 1) * (b + 1) <= N_CAND]
            top = top_vals(lambda r: cands[r], len(cands), N_CAND)
            z = jnp.ones_like(top[0])
            for r in range(1, PEER_TOP):
                z = z + jnp.exp(top[r] - top[0])
            tau = 0.5 * (top[PEER_TOP - 1] + top[PEER_TOP])
            iz = 1.0 / z
            for r in range(N_KEYS):
                s1 = st_ref[tc, r * 8:(r + 1) * 8, :]
                th_ref[r, :, ls] = tau - s1
                cw_ref[r, :, ls] = jnp.exp(s1 - v1[0]) * iz
            for hh in range(PEER_HEADS):
                rows = slice(2048 + hh * 128, 2048 + (hh + 1) * 128)
                e2_ref[hh * 128:(hh + 1) * 128, ls] = jnp.exp(st_ref[tc, rows, :] - v2[0][hh:hh + 1, :])
            return carry

        lax.fori_loop(0, nch, token_chunk, 0)

    nblk = PEER_EB // 256

    def pre_act(b):
        return _dot(u_ref[b * 256:(b + 1) * 256, :], ht_ref[...])

    def gate(b, a):
        i0 = e * (PEER_EB // N_KEYS) + 2 * b
        th = [th_ref[i0], th_ref[i0 + 1]]
        cw = [cw_ref[i0], cw_ref[i0 + 1]]
        for tc in range(nch):
            ls = slice(tc * LANES, (tc + 1) * LANES)
            for sub in range(2):
                w = [None, None]
                for hh in range(PEER_HEADS):
                    r0 = hh * N_KEYS + sub * 64
                    s2 = st_ref[tc, 2048 + r0:2048 + r0 + 64, :]
                    e2 = e2_ref[r0:r0 + 64, ls]
                    for half in range(2):
                        sel = jnp.where(s2 >= th[half][hh:hh + 1, ls], e2, 0.0) * cw[half][hh:hh + 1, ls]
                        w[half] = sel if w[half] is None else w[half] + sel
                for half in range(2):
                    r1 = half * N_KEYS + sub * 64
                    hw_ref[b % 2, r1:r1 + 64, ls] = (jax.nn.gelu(a[r1:r1 + 64, ls]) * w[half]).astype(BF)

    a = pre_act(0)
    for b in range(nblk):
        a_next = pre_act(b + 1) if b + 1 < nblk else None
        gate(b, a)
        if b > 0:
            acc_ref[...] += _dot(vt_ref[b - 1], hw_ref[(b - 1) % 2])
        a = a_next
    acc_ref[...] += _dot(vt_ref[nblk - 1], hw_ref[(nblk - 1) % 2])

    @pl.when(e == pl.num_programs(1) - 1)
    def _finish():
        y = x_ref[...] + acc_ref[...].T
        if final:
            y = _rms(y, gf_ref[...])
        o_ref[...] = y


def _peer(x, g, wqt, sk, u, vt, gf, *, tt, final):
    T, D = x.shape
    ne = u.shape[0] // PEER_EB
    const = lambda shape: pl.BlockSpec(shape, lambda i, e: (0,) * len(shape),
                                       pipeline_mode=pl.Buffered(1))
    return pl.pallas_call(
        functools.partial(_peer_body, tt=tt, final=final),
        grid=(T // tt, ne),
        in_specs=[pl.BlockSpec((tt, D), lambda i, e: (i, 0)),
                  const((1, D)), const((D, D)), const((2, N_KEYS, 64)),
                  pl.BlockSpec((PEER_EB, D), lambda i, e: (e, 0)),
                  pl.BlockSpec((PEER_EB // 256, D, 256), lambda i, e: (e, 0, 0)),
                  const((1, D))],
        out_specs=pl.BlockSpec((tt, D), lambda i, e: (i, 0)),
        out_shape=jax.ShapeDtypeStruct((T, D), F32),
        scratch_shapes=[pltpu.VMEM((D, tt), BF),
                        pltpu.VMEM((tt // LANES, 3 * 1024, LANES), F32),
                        pltpu.VMEM((1024, tt), F32),
                        pltpu.VMEM((N_KEYS, PEER_HEADS, tt), F32),
                        pltpu.VMEM((N_KEYS, PEER_HEADS, tt), F32),
                        pltpu.VMEM((2, 256, tt), BF),
                        pltpu.VMEM((D, tt), F32)],
        compiler_params=pltpu.CompilerParams(
            dimension_semantics=("arbitrary", "arbitrary"), vmem_limit_bytes=VMEM_LIMIT),
        name="peer",
    )(x, g, wqt, sk, u, vt, gf)


def _pack_in_weights(w_in):
    aq, ak, av, cq, ckv, kpe, zc, dq, dk, dv = jnp.split(
        w_in, np.cumsum([256, 256, 256, 256, 128, 32, 512, 256, 256])[:].tolist(), axis=1)
    kpe_rep = jnp.zeros((D_MODEL, 4, 128), w_in.dtype).at[:, :, 64:96].set(kpe[:, None, :])
    return jnp.concatenate(
        [aq, ak, av, cq, ckv, kpe_rep.reshape(D_MODEL, 512), zc, dq, dk, dv], axis=1).astype(BF)


def _pack_mla_weights(w_uq, w_ukv):
    uq = jnp.pad(w_uq.reshape(256, 4, 96), ((0, 0), (0, 0), (0, 32))).reshape(256, 512)
    ukv = w_ukv.reshape(128, 4, 128)
    uk = jnp.pad(ukv[:, :, :64], ((0, 0), (0, 0), (0, 64))).reshape(128, 512)
    uv = ukv[:, :, 64:].reshape(128, 256)
    return uq.astype(BF), uk.astype(BF), uv.astype(BF)


def _pack_expert_out(v):
    n, d = v.shape
    return v.reshape(n // 256, 256, d).transpose(0, 2, 1).astype(BF)


def kernel(x, w_in, w_out, norm_mix, norm_ffn, diff_lambda, diff_head_norm, mla_q_norm, mla_kv_norm, mla_w_uq, mla_w_ukv, mla_out_norm, sgu_v_norm, sgu_w, sgu_b, sgu_out_norm, dil_out_norm, peer_w_q, peer_sub_keys, peer_u, peer_v, final_norm):
    B, S, D = x.shape
    depth = w_in.shape[0]
    T = B * S
    tm = min(512, S)
    ta = min(512, S)
    tt = min(512, T)
    row = lambda a: a.reshape(1, -1)

    tabs = (_rope_tables(S, 8, 32, 0) + _rope_tables(S, 32, 128, 64) + _rope_tables(S, 16, 64, 0))

    for l in range(depth):
        lam_init = 0.8 - 0.6 * math.exp(-0.3 * l)
        w_cat = _pack_in_weights(w_in[l])
        wuq, wuk, wuv = _pack_mla_weights(mla_w_uq[l], mla_w_ukv[l])
        sgub = jnp.repeat(sgu_b[l].T, 64, axis=1)
        qa, ka, va, qb, kb, vb, oc, qd, kd, vd = _inproj(
            x, row(norm_mix[l]), w_cat, tabs, row(mla_q_norm[l]), row(mla_kv_norm[l]),
            wuq, wuk, wuv, row(sgu_v_norm[l]), sgu_w[l], sgub, row(sgu_out_norm[l]), tm=tm)

        oa = _attn_a(diff_lambda[l], row(diff_head_norm[l]), qa, ka, va, t=ta, lam_init=lam_init)
        ob = _attn_b(row(mla_out_norm[l]), qb, kb, vb, t=ta)

        od = _dil(row(dil_out_norm[l]), qd, kd, vd, td=DIL_TILE)

        x2 = _outproj(x.reshape(T, D), oa.reshape(T, GROUP), ob.reshape(T, GROUP),
                      oc.reshape(T, GROUP), od.reshape(T, GROUP),
                      w_out[l].astype(BF), tm=min(512, T))

        x2 = _peer(x2, row(norm_ffn[l]), peer_w_q[l].T.astype(BF), peer_sub_keys[l].astype(BF),
                   peer_u[l].astype(BF), _pack_expert_out(peer_v[l]), row(final_norm),
                   tt=tt, final=(l == depth - 1))
        x = x2.reshape(B, S, D)
    return x
```

```python
import functools
import math

import numpy as np
import jax
import jax.numpy as jnp
from jax import lax
from jax.experimental import pallas as pl
from jax.experimental.pallas import tpu as pltpu

D_MODEL = 1024
GROUP = 256
ROPE_THETA = 500000.0
NEG = -1e30
EPS = 1e-6
LANES = 128
LOG2E = math.log2(math.e)

N_KEYS = 128
PEER_HEADS = 8
PEER_TOP = 16

BF = jnp.bfloat16
F32 = jnp.float32

C_AQ, C_AK, C_AV, C_CQ, C_CKV, C_KPE, C_ZC, C_DQ, C_DK, C_DV, C_END = (
    0, 256, 512, 768, 1024, 1152, 1664, 2176, 2432, 2688, 2944)

VMEM_LIMIT = 56 * 1024 * 1024


def _rms(x, g):
    return x * lax.rsqrt(jnp.mean(x * x, axis=-1, keepdims=True) + EPS) * g


def _nt_dot(a, b):
    return lax.dot_general(a, b, (((1,), (1,)), ((), ())), preferred_element_type=F32)


def _dot(a, b):
    return jnp.dot(a, b, preferred_element_type=F32)


def _rope_tables(seq, n_rot, period, offset):
    half = n_rot // 2
    pos = jnp.arange(seq, dtype=F32)
    inv = ROPE_THETA ** (-jnp.arange(half, dtype=F32) * (2.0 / n_rot))
    ang = pos[:, None] * inv[None, :]
    cos, sin = jnp.cos(ang), jnp.sin(ang)
    g = np.arange(LANES) % period - offset
    rot = (g >= 0) & (g < n_rot)
    idx = np.where(rot, g % half, 0)
    sign = np.where(g < half, -1.0, 1.0).astype(np.float32)
    cos_t = jnp.where(rot[None, :], cos[:, idx], 1.0)
    sin_t = jnp.where(rot[None, :], sin[:, idx] * sign[None, :], 0.0)
    return cos_t.astype(F32), sin_t.astype(F32)


def _rope_apply(x, cos_t, sin_t, n_rot, period, offset):
    half = n_rot // 2
    lane = lax.broadcasted_iota(jnp.int32, (1, LANES), 1)
    first = (lane % period - offset) < half
    outs = []
    for c in range(x.shape[1] // LANES):
        xc = x[:, c * LANES:(c + 1) * LANES]
        fwd = pltpu.roll(xc, LANES - half, 1)
        bwd = pltpu.roll(xc, half, 1)
        outs.append(xc * cos_t + jnp.where(first, fwd, bwd) * sin_t)
    return outs[0] if len(outs) == 1 else jnp.concatenate(outs, axis=1)


VROWS = 80


def _value_rows(vt):
    one = (lax.broadcasted_iota(jnp.int32, (VROWS - 64, vt.shape[1]), 0) == 0).astype(F32)
    parts = []
    for h in range(4):
        parts += [vt[64 * h:64 * (h + 1)], one]
    return jnp.concatenate(parts, axis=0).astype(BF)


def _inproj_body(x_ref, gmix_ref, w_ref, ca_ref, sa_ref, cb_ref, sb_ref, cd_ref, sd_ref,
                 gq_ref, gkv_ref, wuq_ref, wuk_ref, wuv_ref,
                 gsv_ref, sguw_ref, sgub_ref, gso_ref,
                 qa_ref, ka_ref, va_ref, qb_ref, kb_ref, vb_ref, oc_ref,
                 qd_ref, kd_ref, vd_ref, *, tm):
    x = x_ref[0]
    h = _rms(x, gmix_ref[...]).astype(BF)

    def proj(a, b):
        return _dot(h, w_ref[:, a:b])

    ca, sa = ca_ref[...], sa_ref[...]
    aq = _rope_apply(proj(C_AQ, C_AK), ca, sa, 8, 32, 0)
    qa_ref[0] = (aq * (32.0 ** -0.5 * LOG2E)).T.astype(BF)
    ka_ref[0] = _rope_apply(proj(C_AK, C_AV), ca, sa, 8, 32, 0).astype(BF)
    va_ref[0] = _value_rows(proj(C_AV, C_CQ).T)

    cb, sb = cb_ref[...], sb_ref[...]
    cq = _rms(proj(C_CQ, C_CKV), gq_ref[...]).astype(BF)
    qb = _rope_apply(_dot(cq, wuq_ref[...]), cb, sb, 32, 128, 64)
    qb_ref[0] = (qb * (96.0 ** -0.5 * LOG2E)).T.astype(BF)
    ckv = _rms(proj(C_CKV, C_KPE), gkv_ref[...]).astype(BF)
    kpe = _rope_apply(proj(C_KPE, C_ZC), cb, sb, 32, 128, 64)
    kb_ref[0] = (_dot(ckv, wuk_ref[...]) + kpe).astype(BF)
    vb_ref[0] = _value_rows(_dot(ckv, wuv_ref[...]).T)

    zc = jax.nn.gelu(proj(C_ZC, C_DQ))
    u = zc[:, :GROUP]
    vn = _rms(zc[:, GROUP:], gsv_ref[...]).astype(BF)
    r = lax.broadcasted_iota(jnp.int32, (128, 128), 0)
    c = lax.broadcasted_iota(jnp.int32, (128, 128), 1)
    wcat = jnp.concatenate(
        [jnp.where(r >= c, sguw_ref[g], 0.0).astype(BF) for g in range(4)], axis=1)
    lane = lax.broadcasted_iota(jnp.int32, (1, GROUP), 1)
    bias = sgub_ref[...]
    gso = gso_ref[...]
    for ch in range(tm // 128):
        vc = vn[ch * 128:(ch + 1) * 128]
        vst = jnp.concatenate(
            [jnp.where(lane // 64 == g, vc, jnp.zeros_like(vc)) for g in range(4)], axis=0)
        sv = _dot(wcat, vst) + bias
        oc = u[ch * 128:(ch + 1) * 128] * sv
        oc_ref[0, ch * 128:(ch + 1) * 128, :] = _rms(oc, gso).astype(BF)

    cd, sd = cd_ref[...], sd_ref[...]
    dq = _rope_apply(proj(C_DQ, C_DK), cd, sd, 16, 64, 0)
    dk = _rope_apply(proj(C_DK, C_DV), cd, sd, 16, 64, 0)
    dv = proj(C_DV, C_END)
    for hf in range(2):
        qd_ref[0, hf] = dq[:, hf * LANES:(hf + 1) * LANES] * 0.125
        kd_ref[0, hf] = dk[:, hf * LANES:(hf + 1) * LANES]
        vd_ref[0, hf] = dv[:, hf * LANES:(hf + 1) * LANES]


def _inproj(x, gmix, w_cat, tabs, gq, gkv, wuq, wuk, wuv, gsv, sguw, sgub, gso, *, tm):
    B, S, D = x.shape
    ns = S // tm
    full = lambda shape: pl.BlockSpec(shape, lambda s, b: (0,) * len(shape))
    tab = pl.BlockSpec((tm, LANES), lambda s, b: (s, 0))
    seq = lambda w: pl.BlockSpec((1, tm, w), lambda s, b: (b, s, 0))
    seq_t = lambda w: pl.BlockSpec((1, w, tm), lambda s, b: (b, 0, s))
    halves = pl.BlockSpec((1, 2, tm, LANES), lambda s, b: (b, 0, s, 0))
    out_w = [256, 256, 4 * VROWS, 512, 512, 4 * VROWS, 256]
    transposed = [True, False, True, True, False, True, False]
    return pl.pallas_call(
        functools.partial(_inproj_body, tm=tm),
        grid=(ns, B),
        in_specs=[seq(D), full((1, D)), full((D, C_END))] + [tab] * 6 + [
            full((1, 256)), full((1, 128)), full((256, 512)), full((128, 512)), full((128, 256)),
            full((1, 256)), full((4, 128, 128)), full((128, 256)), full((1, 256))],
        out_specs=[seq_t(w) if tr else seq(w) for w, tr in zip(out_w, transposed)] + [halves] * 3,
        out_shape=[jax.ShapeDtypeStruct((B, w, S) if tr else (B, S, w), BF)
                   for w, tr in zip(out_w, transposed)]
        + [jax.ShapeDtypeStruct((B, 2, S, LANES), F32)] * 3,
        compiler_params=pltpu.CompilerParams(
            dimension_semantics=("arbitrary", "arbitrary"), vmem_limit_bytes=VMEM_LIMIT),
        name="inproj",
    )(x, gmix, w_cat, *tabs, gq, gkv, wuq, wuk, wuv, gsv, sguw, sgub, gso)


def _head_expand(cols, lane, width):
    out = cols[-1]
    for h in range(len(cols) - 2, -1, -1):
        out = jnp.where(lane // width == h, cols[h], out)
    return out


def _vstack(v, lane):
    return jnp.concatenate(
        [jnp.where(lane // 64 == h, v, jnp.zeros_like(v)) for h in range(4)], axis=0)


QB = 512


def _online_softmax_t(st, m_ref, j, cs):
    m_prev = m_ref[j:j + 1, cs]
    m_new = jnp.maximum(m_prev, jnp.max(st, axis=0, keepdims=True))
    alpha = jnp.exp2(m_prev - m_new)
    et = jnp.exp2(st - m_new)
    m_ref[j:j + 1, cs] = m_new
    return et.astype(BF), alpha


def _causal_pairs(n):
    pairs = [(i, j) for i in range(n) for j in range(i + 1)]
    return (jnp.asarray([p[0] for p in pairs], jnp.int32),
            jnp.asarray([p[1] for p in pairs], jnp.int32))


def _attn_a_body(qtab_ref, ktab_ref, lam_ref, gh_ref, q_ref, k_ref, v_ref, o_ref,
                 qs_ref, m_ref, acc_ref, *, t, lam_init):
    qi = qtab_ref[pl.program_id(1)]
    ki = ktab_ref[pl.program_id(1)]

    @pl.when(ki == 0)
    def _init():
        m_ref[...] = jnp.full(m_ref.shape, NEG, F32)
        acc_ref[...] = jnp.zeros(acc_ref.shape, F32)
        qt = q_ref[0]
        row = lax.broadcasted_iota(jnp.int32, (GROUP, 1), 0)
        for j in range(8):
            qs_ref[j] = jnp.where(row // 32 == j, qt, jnp.zeros_like(qt))

    def step(masked):
        k = k_ref[0]
        jobs = [(mp, h, c) for mp in range(2) for h in range(4) for c in range(t // QB)]

        def qk(mp, h, c):
            return _dot(k, qs_ref[2 * h + mp, :, c * QB:(c + 1) * QB])

        def pv(mp, h, c, pt, alpha):
            hs, cs = slice(VROWS * h, VROWS * (h + 1)), slice(c * QB, (c + 1) * QB)
            acc_ref[mp, hs, cs] = acc_ref[mp, hs, cs] * alpha + _dot(v_ref[0, hs, :], pt)

        st_next = qk(*jobs[0])
        pending = None
        for n, (mp, h, c) in enumerate(jobs):
            st = st_next
            if n + 1 < len(jobs):
                st_next = qk(*jobs[n + 1])
            if masked:
                keep = (lax.broadcasted_iota(jnp.int32, (t, QB), 0)
                        <= lax.broadcasted_iota(jnp.int32, (t, QB), 1) + c * QB)
                st = jnp.where(keep, st, NEG)
            pt, alpha = _online_softmax_t(st, m_ref, 2 * h + mp,slice(c * QB, (c + 1) * QB))
            if pending is not None:
                pv(*pending)
            pending = (mp, h, c, pt, alpha)
        pv(*pending)

    @pl.when(ki < qi)
    def _off():
        step(False)

    @pl.when(ki == qi)
    def _diag():
        step(True)
        lf = lam_ref[...]
        lam = (jnp.exp(jnp.sum(lf[0:1] * lf[1:2], axis=-1, keepdims=True))
               - jnp.exp(jnp.sum(lf[2:3] * lf[3:4], axis=-1, keepdims=True)) + lam_init)
        rows = []
        for h in range(4):
            hs, ls = slice(VROWS * h, VROWS * h + 64), slice(VROWS * h + 64, VROWS * h + 65)
            oh = (acc_ref[0, hs, :] * (1.0 / acc_ref[0, ls, :])
                  - lam * (acc_ref[1, hs, :] * (1.0 / acc_ref[1, ls, :])))
            ms = jnp.mean(oh * oh, axis=0, keepdims=True)
            rows.append(oh * lax.rsqrt(ms + EPS))
        ot = jnp.concatenate(rows, axis=0)
        o_ref[0] = (ot.T * gh_ref[...] * (1.0 - lam_init)).astype(BF)


def _attn_a(lam_p, gh, q, k, v, *, t, lam_init):
    B, S, _ = k.shape
    n = S // t
    qtab, ktab = _causal_pairs(n)
    grid_spec = pltpu.PrefetchScalarGridSpec(
        num_scalar_prefetch=2,
        grid=(B, qtab.shape[0]),
        in_specs=[pl.BlockSpec((4, 32), lambda b, p, qt, kt: (0, 0)),
                  pl.BlockSpec((1, GROUP), lambda b, p, qt, kt: (0, 0)),
                  pl.BlockSpec((1, GROUP, t), lambda b, p, qt, kt: (b, 0, qt[p])),
                  pl.BlockSpec((1, t, GROUP), lambda b, p, qt, kt: (b, kt[p], 0)),
                  pl.BlockSpec((1, 4 * VROWS, t), lambda b, p, qt, kt: (b, 0, kt[p]))],
        out_specs=pl.BlockSpec((1, t, GROUP), lambda b, p, qt, kt: (b, qt[p], 0)),
        scratch_shapes=[pltpu.VMEM((8, GROUP, t), BF),
                        pltpu.VMEM((8, t), F32),
                        pltpu.VMEM((2, 4 * VROWS, t), F32)])
    return pl.pallas_call(
        functools.partial(_attn_a_body, t=t, lam_init=lam_init),
        grid_spec=grid_spec,
        out_shape=jax.ShapeDtypeStruct((B, S, GROUP), BF),
        compiler_params=pltpu.CompilerParams(
            dimension_semantics=("arbitrary", "arbitrary"), vmem_limit_bytes=VMEM_LIMIT),
        name="attn_diff",
    )(qtab, ktab, lam_p, gh, q, k, v)


def _attn_b_body(qtab_ref, ktab_ref, g_ref, q_ref, k_ref, v_ref, o_ref, m_ref, acc_ref, *, t):
    qi = qtab_ref[pl.program_id(1)]
    ki = ktab_ref[pl.program_id(1)]

    @pl.when(ki == 0)
    def _init():
        m_ref[...] = jnp.full(m_ref.shape, NEG, F32)
        acc_ref[...] = jnp.zeros(acc_ref.shape, F32)

    def step(masked):
        jobs = [(h, c) for h in range(4) for c in range(t // QB)]

        def qk(h, c):
            return _dot(k_ref[0, :, h * 128:(h + 1) * 128],
                        q_ref[0, h * 128:(h + 1) * 128, c * QB:(c + 1) * QB])

        def pv(h, c, pt, alpha):
            hs, cs = slice(VROWS * h, VROWS * (h + 1)), slice(c * QB, (c + 1) * QB)
            acc_ref[hs, cs] = acc_ref[hs, cs] * alpha + _dot(v_ref[0, hs, :], pt)

        st_next = qk(*jobs[0])
        pending = None
        for n, (h, c) in enumerate(jobs):
            st = st_next
            if n + 1 < len(jobs):
                st_next = qk(*jobs[n + 1])
            if masked:
                keep = (lax.broadcasted_iota(jnp.int32, (t, QB), 0)
                        <= lax.broadcasted_iota(jnp.int32, (t, QB), 1) + c * QB)
                st = jnp.where(keep, st, NEG)
            pt, alpha = _online_softmax_t(st, m_ref, h,slice(c * QB, (c + 1) * QB))
            if pending is not None:
                pv(*pending)
            pending = (h, c, pt, alpha)
        pv(*pending)

    @pl.when(ki < qi)
    def _off():
        step(False)

    @pl.when(ki == qi)
    def _diag():
        step(True)
        rows = [acc_ref[VROWS * h:VROWS * h + 64, :] * (1.0 / acc_ref[VROWS * h + 64:VROWS * h + 65, :])
                for h in range(4)]
        ot = jnp.concatenate(rows, axis=0)
        o_ref[0] = _rms(ot.T, g_ref[...]).astype(BF)


def _attn_b(g, q, k, v, *, t):
    B, S, _ = k.shape
    n = S // t
    qtab, ktab = _causal_pairs(n)
    grid_spec = pltpu.PrefetchScalarGridSpec(
        num_scalar_prefetch=2,
        grid=(B, qtab.shape[0]),
        in_specs=[pl.BlockSpec((1, GROUP), lambda b, p, qt, kt: (0, 0)),
                  pl.BlockSpec((1, 512, t), lambda b, p, qt, kt: (b, 0, qt[p])),
                  pl.BlockSpec((1, t, 512), lambda b, p, qt, kt: (b, kt[p], 0)),
                  pl.BlockSpec((1, 4 * VROWS, t), lambda b, p, qt, kt: (b, 0, kt[p]))],
        out_specs=pl.BlockSpec((1, t, GROUP), lambda b, p, qt, kt: (b, qt[p], 0)),
        scratch_shapes=[pltpu.VMEM((4, t), F32),
                        pltpu.VMEM((4 * VROWS, t), F32)])
    return pl.pallas_call(
        functools.partial(_attn_b_body, t=t),
        grid_spec=grid_spec,
        out_shape=jax.ShapeDtypeStruct((B, S, GROUP), BF),
        compiler_params=pltpu.CompilerParams(
            dimension_semantics=("arbitrary", "arbitrary"), vmem_limit_bytes=VMEM_LIMIT),
        name="attn_mla",
    )(qtab, ktab, g, q, k, v)


DIL_TILE = 2048
DILATIONS = (1, 4, 16)


def _dil_body(g_ref, q_ref, kp_ref, kc_ref, vp_ref, vc_ref, o_ref, ob_ref, lb_ref, *, td):
    i = pl.program_id(1)
    lane = lax.broadcasted_iota(jnp.int32, (1, GROUP), 1)
    a = lax.broadcasted_iota(jnp.int32, (128, 256), 0)
    c = lax.broadcasted_iota(jnp.int32, (128, 256), 1)
    dist = jnp.where(c - a >= 0, c - a, 1000)

    def rows(ref, start, d):
        idx = pl.ds(start, 128) if d == 1 else pl.ds(start, 128, stride=d)
        return jnp.concatenate([ref.at[0, 0][idx, :], ref.at[0, 1][idx, :]], axis=1).astype(BF)

    for bi, d in enumerate(DILATIONS):
        def block(n, carry, bi=bi, d=d):
            if d == 1:
                r, blk = 0, n
            elif td // d == 128:
                r, blk = n, 0
            else:
                r, blk = n % d, n // d
            q0 = r + d * 128 * blk
            if d == 1:
                q0 = pl.multiple_of(q0, 128)
            q = rows(q_ref, q0, d)
            k_hi, v_hi = rows(kc_ref, q0, d), rows(vc_ref, q0, d)
            lo_prev = td - d * 128 + r
            if isinstance(blk, int):
                first = True
                k_lo, v_lo = rows(kp_ref, lo_prev, d), rows(vp_ref, lo_prev, d)
            else:
                first = blk == 0
                lo_cur = jnp.maximum(q0 - d * 128, 0)
                if d == 1:
                    lo_cur = pl.multiple_of(lo_cur, 128)
                k_lo = jnp.where(first, rows(kp_ref, lo_prev, d), rows(kc_ref, lo_cur, d))
                v_lo = jnp.where(first, rows(vp_ref, lo_prev, d), rows(vc_ref, lo_cur, d))
            kwin = jnp.concatenate([k_lo, k_hi], axis=0)
            vst = _vstack(jnp.concatenate([v_lo, v_hi], axis=0), lane)
            cmin = jnp.where(jnp.logical_and(first, i == 0), 128, 0)
            ok = jnp.where(c >= cmin, dist, 1000) <= 128
            es, ils, lses = [], [], []
            for h in range(4):
                s = _nt_dot(jnp.where(lane // 64 == h, q, jnp.zeros_like(q)), kwin)
                s = jnp.where(ok, s, NEG)
                m = jnp.max(s, axis=-1, keepdims=True)
                ex = jnp.exp(s - m)
                den = jnp.sum(ex, axis=-1, keepdims=True)
                es.append(ex.astype(BF))
                ils.append(1.0 / den)
                lses.append(m + jnp.log(den))
            o = _dot(jnp.concatenate(es, axis=1), vst) * _head_expand(ils, lane, 64)
            lse = _head_expand(lses, lane, 64)
            idx = pl.ds(q0, 128) if d == 1 else pl.ds(q0, 128, stride=d)
            for hf in range(2):
                ob_ref.at[bi, hf][idx, :] = o[:, hf * LANES:(hf + 1) * LANES]
                lb_ref.at[bi, hf][idx, :] = lse[:, hf * LANES:(hf + 1) * LANES]
            return carry

        lax.fori_loop(0, td // 128, block, 0, unroll=2)

    for ch in range(td // 256):
        rs = slice(ch * 256, (ch + 1) * 256)
        halves = []
        for hf in range(2):
            ls = [lb_ref[bi, hf, rs, :] for bi in range(3)]
            mx = jnp.maximum(jnp.maximum(ls[0], ls[1]), ls[2])
            ws = [jnp.exp(l - mx) for l in ls]
            num = ws[0] * ob_ref[0, hf, rs, :] + ws[1] * ob_ref[1, hf, rs, :] + ws[2] * ob_ref[2, hf, rs, :]
            halves.append(num / (ws[0] + ws[1] + ws[2]))
        o_ref[0, rs, :] = _rms(jnp.concatenate(halves, axis=1), g_ref[...]).astype(BF)


def _dil(g, q, k, v, *, td):
    B, _, S, _ = q.shape
    cur = pl.BlockSpec((1, 2, td, LANES), lambda b, i: (b, 0, i, 0))
    prev = pl.BlockSpec((1, 2, td, LANES), lambda b, i: (b, 0, jnp.maximum(i - 1, 0), 0))
    return pl.pallas_call(
        functools.partial(_dil_body, td=td),
        grid=(B, S // td),
        in_specs=[pl.BlockSpec((1, GROUP), lambda b, i: (0, 0)), cur, prev, cur, prev, cur],
        out_specs=pl.BlockSpec((1, td, GROUP), lambda b, i: (b, i, 0)),
        out_shape=jax.ShapeDtypeStruct((B, S, GROUP), BF),
        scratch_shapes=[pltpu.VMEM((3, 2, td, LANES), F32),
                        pltpu.VMEM((3, 2, td, LANES), F32)],
        compiler_params=pltpu.CompilerParams(
            dimension_semantics=("arbitrary", "arbitrary"), vmem_limit_bytes=VMEM_LIMIT),
        name="dilated",
    )(g, q, k, k, v, v)


def _outproj_body(x_ref, oa_ref, ob_ref, oc_ref, od_ref, w_ref, y_ref):
    y = _dot(oa_ref[...], w_ref[0:256, :])
    y += _dot(ob_ref[...], w_ref[256:512, :])
    y += _dot(oc_ref[...], w_ref[512:768, :])
    y += _dot(od_ref[...], w_ref[768:1024, :])
    y_ref[...] = x_ref[...] + y


def _outproj(x, oa, ob, oc, od, w_out, *, tm):
    T, D = x.shape
    row = lambda w: pl.BlockSpec((tm, w), lambda i: (i, 0))
    full = lambda shape: pl.BlockSpec(shape, lambda i: (0,) * len(shape))
    return pl.pallas_call(
        _outproj_body,
        grid=(T // tm,),
        in_specs=[row(D)] + [row(GROUP)] * 4 + [full((D, D))],
        out_specs=row(D),
        out_shape=jax.ShapeDtypeStruct((T, D), F32),
        compiler_params=pltpu.CompilerParams(
            dimension_semantics=("arbitrary",), vmem_limit_bytes=VMEM_LIMIT),
        name="outproj",
    )(x, oa, ob, oc, od, w_out)


PEER_EB = 1024
GELU_C1 = math.sqrt(2.0 / math.pi)
GELU_C2 = 0.044715 * GELU_C1
N_CAND = PEER_TOP + 1


def _peer_body(x_ref, g_ref, wqt_ref, sk_ref, u_ref, vt_ref, gf_ref, o_ref,
               ht_ref, st_ref, e2_ref, th_ref, cw_ref, hw_ref, acc_ref,
               *, tt, final):
    e = pl.program_id(1)
    nch = tt // LANES

    @pl.when(e == 0)
    def _prologue():
        h = _rms(x_ref[...], g_ref[...])
        ht_ref[...] = h.T.astype(BF)
        qt = _dot(wqt_ref[...], ht_ref[...]).astype(BF)
        for hh in range(PEER_HEADS):
            for c in range(2):
                r0 = hh * N_KEYS + c * 64
                sc = _dot(sk_ref[c], qt[r0:r0 + 64, :])
                for tc in range(nch):
                    blk = sc[:, tc * LANES:(tc + 1) * LANES]
                    st_ref.at[tc][pl.ds(c * 1024 + hh, N_KEYS, stride=PEER_HEADS), :] = blk
                    if c == 1:
                        st_ref[tc, 2048 + hh * N_KEYS:2048 + (hh + 1) * N_KEYS, :] = blk
        acc_ref[...] = jnp.zeros(acc_ref.shape, F32)

        def top_vals(load, n_rows, n_out):
            top = []
            for r in range(n_rows):
                v = load(r)
                for q in range(len(top)):
                    top[q], v = jnp.maximum(top[q], v), jnp.minimum(top[q], v)
                if len(top) < n_out:
                    top.append(v)
            return top

        def token_chunk(tc, carry):
            ls = pl.ds(pl.multiple_of(tc * LANES, LANES), LANES)
            v1 = top_vals(lambda r: st_ref[tc, r * 8:(r + 1) * 8, :], N_KEYS, N_CAND)
            v2 = top_vals(lambda r: st_ref[tc, 1024 + r * 8:1024 + (r + 1) * 8, :], N_KEYS, N_CAND)
            cands = [v1[a] + v2[b] for a in range(N_CAND) for b in range(N_CAND)
                     if (a + 1) * (b + 1) <= N_CAND]
            top = top_vals(lambda r: cands[r], len(cands), N_CAND)
            z = jnp.ones_like(top[0])
            for r in range(1, PEER_TOP):
                z = z + jnp.exp(top[r] - top[0])
            tau = 0.5 * (top[PEER_TOP - 1] + top[PEER_TOP])
            iz = 1.0 / z
            for r in range(N_KEYS):
                s1 = st_ref[tc, r * 8:(r + 1) * 8, :]
                th_ref[r, :, ls] = tau - s1
                cw_ref[r, :, ls] = jnp.exp(s1 - v1[0]) * (0.5 * iz)
            for hh in range(PEER_HEADS):
                rows = slice(2048 + hh * 128, 2048 + (hh + 1) * 128)
                e2_ref[hh * 128:(hh + 1) * 128, ls] = jnp.exp(st_ref[tc, rows, :] - v2[0][hh:hh + 1, :])
            return carry

        lax.fori_loop(0, nch, token_chunk, 0)

    nblk = PEER_EB // 256

    def pre_act(b):
        return _dot(u_ref[b * 256:(b + 1) * 256, :], ht_ref[...])

    def gate(b, a):
        i0 = e * (PEER_EB // N_KEYS) + 2 * b
        th = [th_ref[i0], th_ref[i0 + 1]]
        cw = [cw_ref[i0], cw_ref[i0 + 1]]
        for tc in range(nch):
            ls = slice(tc * LANES, (tc + 1) * LANES)
            for sub in range(2):
                w = [None, None]
                for hh in range(PEER_HEADS):
                    r0 = hh * N_KEYS + sub * 64
                    s2 = st_ref[tc, 2048 + r0:2048 + r0 + 64, :]
                    e2 = e2_ref[r0:r0 + 64, ls]
                    for half in range(2):
                        sel = jnp.where(s2 >= th[half][hh:hh + 1, ls], e2, 0.0) * cw[half][hh:hh + 1, ls]
                        w[half] = sel if w[half] is None else w[half] + sel
                for half in range(2):
                    r1 = half * N_KEYS + sub * 64
                    x = a[r1:r1 + 64, ls]
                    xw = x * w[half]
                    t = jnp.tanh(x * (GELU_C1 + GELU_C2 * (x * x)))
                    hw_ref[b % 2, r1:r1 + 64, ls] = (xw + xw * t).astype(BF)

    a = pre_act(0)
    for b in range(nblk):
        a_next = pre_act(b + 1) if b + 1 < nblk else None
        gate(b, a)
        if b > 0:
            acc_ref[...] += _dot(vt_ref[b - 1], hw_ref[(b - 1) % 2])
        a = a_next
    acc_ref[...] += _dot(vt_ref[nblk - 1], hw_ref[(nblk - 1) % 2])

    @pl.when(e == pl.num_programs(1) - 1)
    def _finish():
        y = x_ref[...] + acc_ref[...].T
        if final:
            y = _rms(y, gf_ref[...])
        o_ref[...] = y


def _peer(x, g, wqt, sk, u, vt, gf, *, tt, final):
    T, D = x.shape
    ne = u.shape[0] // PEER_EB
    const = lambda shape: pl.BlockSpec(shape, lambda i, e: (0,) * len(shape),
                                       pipeline_mode=pl.Buffered(1))
    return pl.pallas_call(
        functools.partial(_peer_body, tt=tt, final=final),
        grid=(T // tt, ne),
        in_specs=[pl.BlockSpec((tt, D), lambda i, e: (i, 0)),
                  const((1, D)), const((D, D)), const((2, N_KEYS, 64)),
                  pl.BlockSpec((PEER_EB, D), lambda i, e: (e, 0)),
                  pl.BlockSpec((PEER_EB // 256, D, 256), lambda i, e: (e, 0, 0)),
                  const((1, D))],
        out_specs=pl.BlockSpec((tt, D), lambda i, e: (i, 0)),
        out_shape=jax.ShapeDtypeStruct((T, D), F32),
        scratch_shapes=[pltpu.VMEM((D, tt), BF),
                        pltpu.VMEM((tt // LANES, 3 * 1024, LANES), F32),
                        pltpu.VMEM((1024, tt), F32),
                        pltpu.VMEM((N_KEYS, PEER_HEADS, tt), F32),
                        pltpu.VMEM((N_KEYS, PEER_HEADS, tt), F32),
                        pltpu.VMEM((2, 256, tt), BF),
                        pltpu.VMEM((D, tt), F32)],
        compiler_params=pltpu.CompilerParams(
            dimension_semantics=("arbitrary", "arbitrary"), vmem_limit_bytes=VMEM_LIMIT),
        name="peer",
    )(x, g, wqt, sk, u, vt, gf)


def _pack_in_weights(w_in):
    aq, ak, av, cq, ckv, kpe, zc, dq, dk, dv = jnp.split(
        w_in, np.cumsum([256, 256, 256, 256, 128, 32, 512, 256, 256])[:].tolist(), axis=1)
    kpe_rep = jnp.zeros((D_MODEL, 4, 128), w_in.dtype).at[:, :, 64:96].set(kpe[:, None, :])
    return jnp.concatenate(
        [aq, ak, av, cq, ckv, kpe_rep.reshape(D_MODEL, 512), zc, dq, dk, dv], axis=1).astype(BF)


def _pack_mla_weights(w_uq, w_ukv):
    uq = jnp.pad(w_uq.reshape(256, 4, 96), ((0, 0), (0, 0), (0, 32))).reshape(256, 512)
    ukv = w_ukv.reshape(128, 4, 128)
    uk = jnp.pad(ukv[:, :, :64], ((0, 0), (0, 0), (0, 64))).reshape(128, 512)
    uv = ukv[:, :, 64:].reshape(128, 256)
    return uq.astype(BF), uk.astype(BF), uv.astype(BF)


def _pack_expert_out(v):
    n, d = v.shape
    return v.reshape(n // 256, 256, d).transpose(0, 2, 1).astype(BF)


def kernel(x, w_in, w_out, norm_mix, norm_ffn, diff_lambda, diff_head_norm, mla_q_norm, mla_kv_norm, mla_w_uq, mla_w_ukv, mla_out_norm, sgu_v_norm, sgu_w, sgu_b, sgu_out_norm, dil_out_norm, peer_w_q, peer_sub_keys, peer_u, peer_v, final_norm):
    B, S, D = x.shape
    depth = w_in.shape[0]
    T = B * S
    tm = min(512, S)
    ta = min(512, S)
    tt = min(512, T)
    row = lambda a: a.reshape(1, -1)

    tabs = (_rope_tables(S, 8, 32, 0) + _rope_tables(S, 32, 128, 64) + _rope_tables(S, 16, 64, 0))

    for l in range(depth):
        lam_init = 0.8 - 0.6 * math.exp(-0.3 * l)
        w_cat = _pack_in_weights(w_in[l])
        wuq, wuk, wuv = _pack_mla_weights(mla_w_uq[l], mla_w_ukv[l])
        sgub = jnp.repeat(sgu_b[l].T, 64, axis=1)
        qa, ka, va, qb, kb, vb, oc, qd, kd, vd = _inproj(
            x, row(norm_mix[l]), w_cat, tabs, row(mla_q_norm[l]), row(mla_kv_norm[l]),
            wuq, wuk, wuv, row(sgu_v_norm[l]), sgu_w[l], sgub, row(sgu_out_norm[l]), tm=tm)

        oa = _attn_a(diff_lambda[l], row(diff_head_norm[l]), qa, ka, va, t=ta, lam_init=lam_init)
        ob = _attn_b(row(mla_out_norm[l]), qb, kb, vb, t=ta)

        od = _dil(row(dil_out_norm[l]), qd, kd, vd, td=DIL_TILE)

        x2 = _outproj(x.reshape(T, D), oa.reshape(T, GROUP), ob.reshape(T, GROUP),
                      oc.reshape(T, GROUP), od.reshape(T, GROUP),
                      w_out[l].astype(BF), tm=min(512, T))

        x2 = _peer(x2, row(norm_ffn[l]), peer_w_q[l].T.astype(BF), peer_sub_keys[l].astype(BF),
                   peer_u[l].astype(BF), _pack_expert_out(peer_v[l]), row(final_norm),
                   tt=tt, final=(l == depth - 1))
        x = x2.reshape(B, S, D)
    return x
```

```python
import functools
import math

import numpy as np
import jax
import jax.numpy as jnp
from jax import lax
from jax.experimental import pallas as pl
from jax.experimental.pallas import tpu as pltpu

D_MODEL = 1024
GROUP = 256
ROPE_THETA = 500000.0
NEG = -1e30
EPS = 1e-6
LANES = 128
LOG2E = math.log2(math.e)

N_KEYS = 128
PEER_HEADS = 8
PEER_TOP = 16

BF = jnp.bfloat16
F32 = jnp.float32

C_AQ, C_AK, C_AV, C_CQ, C_CKV, C_KPE, C_ZC, C_DQ, C_DK, C_DV, C_END = (
    0, 256, 512, 768, 1024, 1152, 1664, 2176, 2432, 2688, 2944)

VMEM_LIMIT = 56 * 1024 * 1024


def _rms(x, g):
    return x * lax.rsqrt(jnp.mean(x * x, axis=-1, keepdims=True) + EPS) * g


def _nt_dot(a, b):
    return lax.dot_general(a, b, (((1,), (1,)), ((), ())), preferred_element_type=F32)


def _dot(a, b):
    return jnp.dot(a, b, preferred_element_type=F32)


def _rope_tables(seq, n_rot, period, offset):
    half = n_rot // 2
    pos = jnp.arange(seq, dtype=F32)
    inv = ROPE_THETA ** (-jnp.arange(half, dtype=F32) * (2.0 / n_rot))
    ang = pos[:, None] * inv[None, :]
    cos, sin = jnp.cos(ang), jnp.sin(ang)
    g = np.arange(LANES) % period - offset
    rot = (g >= 0) & (g < n_rot)
    idx = np.where(rot, g % half, 0)
    sign = np.where(g < half, -1.0, 1.0).astype(np.float32)
    cos_t = jnp.where(rot[None, :], cos[:, idx], 1.0)
    sin_t = jnp.where(rot[None, :], sin[:, idx] * sign[None, :], 0.0)
    return cos_t.astype(F32), sin_t.astype(F32)


def _rope_apply(x, cos_t, sin_t, n_rot, period, offset):
    half = n_rot // 2
    lane = lax.broadcasted_iota(jnp.int32, (1, LANES), 1)
    first = (lane % period - offset) < half
    outs = []
    for c in range(x.shape[1] // LANES):
        xc = x[:, c * LANES:(c + 1) * LANES]
        fwd = pltpu.roll(xc, LANES - half, 1)
        bwd = pltpu.roll(xc, half, 1)
        outs.append(xc * cos_t + jnp.where(first, fwd, bwd) * sin_t)
    return outs[0] if len(outs) == 1 else jnp.concatenate(outs, axis=1)


VROWS = 80


def _value_rows(vt):
    one = (lax.broadcasted_iota(jnp.int32, (VROWS - 64, vt.shape[1]), 0) == 0).astype(F32)
    parts = []
    for h in range(4):
        parts += [vt[64 * h:64 * (h + 1)], one]
    return jnp.concatenate(parts, axis=0).astype(BF)


def _inproj_body(x_ref, gmix_ref, w_ref, ca_ref, sa_ref, cb_ref, sb_ref, cd_ref, sd_ref,
                 gq_ref, gkv_ref, wuq_ref, wuk_ref, wuv_ref,
                 gsv_ref, sguw_ref, sgub_ref, gso_ref,
                 qa_ref, ka_ref, va_ref, qb_ref, kb_ref, vb_ref, oc_ref,
                 qd_ref, kd_ref, vd_ref, *, tm):
    x = x_ref[0]
    h = _rms(x, gmix_ref[...]).astype(BF)

    def proj(a, b):
        return _dot(h, w_ref[:, a:b])

    ca, sa = ca_ref[...], sa_ref[...]
    aq = _rope_apply(proj(C_AQ, C_AK), ca, sa, 8, 32, 0)
    qa_ref[0] = (aq * (32.0 ** -0.5 * LOG2E)).T.astype(BF)
    ka_ref[0] = _rope_apply(proj(C_AK, C_AV), ca, sa, 8, 32, 0).astype(BF)
    va_ref[0] = _value_rows(proj(C_AV, C_CQ).T)

    cb, sb = cb_ref[...], sb_ref[...]
    cq = _rms(proj(C_CQ, C_CKV), gq_ref[...]).astype(BF)
    qb = _rope_apply(_dot(cq, wuq_ref[...]), cb, sb, 32, 128, 64)
    qb_ref[0] = (qb * (96.0 ** -0.5 * LOG2E)).T.astype(BF)
    ckv = _rms(proj(C_CKV, C_KPE), gkv_ref[...]).astype(BF)
    kpe = _rope_apply(proj(C_KPE, C_ZC), cb, sb, 32, 128, 64)
    kb_ref[0] = (_dot(ckv, wuk_ref[...]) + kpe).astype(BF)
    vb_ref[0] = _value_rows(_dot(ckv, wuv_ref[...]).T)

    zc = jax.nn.gelu(proj(C_ZC, C_DQ))
    u = zc[:, :GROUP]
    vn = _rms(zc[:, GROUP:], gsv_ref[...]).astype(BF)
    r = lax.broadcasted_iota(jnp.int32, (128, 128), 0)
    c = lax.broadcasted_iota(jnp.int32, (128, 128), 1)
    wcat = jnp.concatenate(
        [jnp.where(r >= c, sguw_ref[g], 0.0).astype(BF) for g in range(4)], axis=1)
    lane = lax.broadcasted_iota(jnp.int32, (1, GROUP), 1)
    bias = sgub_ref[...]
    gso = gso_ref[...]
    for ch in range(tm // 128):
        vc = vn[ch * 128:(ch + 1) * 128]
        vst = jnp.concatenate(
            [jnp.where(lane // 64 == g, vc, jnp.zeros_like(vc)) for g in range(4)], axis=0)
        sv = _dot(wcat, vst) + bias
        oc = u[ch * 128:(ch + 1) * 128] * sv
        oc_ref[0, ch * 128:(ch + 1) * 128, :] = _rms(oc, gso).astype(BF)

    cd, sd = cd_ref[...], sd_ref[...]
    dq = _rope_apply(proj(C_DQ, C_DK), cd, sd, 16, 64, 0)
    dk = _rope_apply(proj(C_DK, C_DV), cd, sd, 16, 64, 0)
    dv = proj(C_DV, C_END)
    for hf in range(2):
        qd_ref[0, hf] = dq[:, hf * LANES:(hf + 1) * LANES] * 0.125
        kd_ref[0, hf] = dk[:, hf * LANES:(hf + 1) * LANES]
        vd_ref[0, hf] = dv[:, hf * LANES:(hf + 1) * LANES]


def _inproj(x, gmix, w_cat, tabs, gq, gkv, wuq, wuk, wuv, gsv, sguw, sgub, gso, *, tm):
    B, S, D = x.shape
    ns = S // tm
    full = lambda shape: pl.BlockSpec(shape, lambda s, b: (0,) * len(shape))
    tab = pl.BlockSpec((tm, LANES), lambda s, b: (s, 0))
    seq = lambda w: pl.BlockSpec((1, tm, w), lambda s, b: (b, s, 0))
    seq_t = lambda w: pl.BlockSpec((1, w, tm), lambda s, b: (b, 0, s))
    halves = pl.BlockSpec((1, 2, tm, LANES), lambda s, b: (b, 0, s, 0))
    out_w = [256, 256, 4 * VROWS, 512, 512, 4 * VROWS, 256]
    transposed = [True, False, True, True, False, True, False]
    return pl.pallas_call(
        functools.partial(_inproj_body, tm=tm),
        grid=(ns, B),
        in_specs=[seq(D), full((1, D)), full((D, C_END))] + [tab] * 6 + [
            full((1, 256)), full((1, 128)), full((256, 512)), full((128, 512)), full((128, 256)),
            full((1, 256)), full((4, 128, 128)), full((128, 256)), full((1, 256))],
        out_specs=[seq_t(w) if tr else seq(w) for w, tr in zip(out_w, transposed)] + [halves] * 3,
        out_shape=[jax.ShapeDtypeStruct((B, w, S) if tr else (B, S, w), BF)
                   for w, tr in zip(out_w, transposed)]
        + [jax.ShapeDtypeStruct((B, 2, S, LANES), F32)] * 3,
        compiler_params=pltpu.CompilerParams(
            dimension_semantics=("arbitrary", "arbitrary"), vmem_limit_bytes=VMEM_LIMIT),
        name="inproj",
    )(x, gmix, w_cat, *tabs, gq, gkv, wuq, wuk, wuv, gsv, sguw, sgub, gso)


def _head_expand(cols, lane, width):
    out = cols[-1]
    for h in range(len(cols) - 2, -1, -1):
        out = jnp.where(lane // width == h, cols[h], out)
    return out


def _vstack(v, lane):
    return jnp.concatenate(
        [jnp.where(lane // 64 == h, v, jnp.zeros_like(v)) for h in range(4)], axis=0)


QB = 512


def _online_softmax_t(st, m_ref, j, cs):
    m_prev = m_ref[j:j + 1, cs]
    m_new = jnp.maximum(m_prev, jnp.max(st, axis=0, keepdims=True))
    alpha = jnp.exp2(m_prev - m_new)
    et = jnp.exp2(st - m_new)
    m_ref[j:j + 1, cs] = m_new
    return et.astype(BF), alpha


def _causal_pairs(n):
    pairs = [(i, j) for i in range(n) for j in range(i + 1)]
    return (jnp.asarray([p[0] for p in pairs], jnp.int32),
            jnp.asarray([p[1] for p in pairs], jnp.int32))


def _attn_a_body(qtab_ref, ktab_ref, lam_ref, gh_ref, q_ref, k_ref, v_ref, o_ref,
                 qs_ref, m_ref, acc_ref, *, t, lam_init):
    qi = qtab_ref[pl.program_id(1)]
    ki = ktab_ref[pl.program_id(1)]

    @pl.when(ki == 0)
    def _init():
        m_ref[...] = jnp.full(m_ref.shape, NEG, F32)
        acc_ref[...] = jnp.zeros(acc_ref.shape, F32)
        qt = q_ref[0]
        row = lax.broadcasted_iota(jnp.int32, (GROUP, 1), 0)
        for j in range(8):
            qs_ref[j] = jnp.where(row // 32 == j, qt, jnp.zeros_like(qt))

    def step(masked):
        k = k_ref[0]
        jobs = [(mp, h, c) for mp in range(2) for h in range(4) for c in range(t // QB)]

        def qk(mp, h, c):
            return _dot(k, qs_ref[2 * h + mp, :, c * QB:(c + 1) * QB])

        def pv(mp, h, c, pt, alpha):
            hs, cs = slice(VROWS * h, VROWS * (h + 1)), slice(c * QB, (c + 1) * QB)
            acc_ref[mp, hs, cs] = acc_ref[mp, hs, cs] * alpha + _dot(v_ref[0, hs, :], pt)

        st_next = qk(*jobs[0])
        pending = None
        for n, (mp, h, c) in enumerate(jobs):
            st = st_next
            if n + 1 < len(jobs):
                st_next = qk(*jobs[n + 1])
            if masked:
                keep = (lax.broadcasted_iota(jnp.int32, (t, QB), 0)
                        <= lax.broadcasted_iota(jnp.int32, (t, QB), 1) + c * QB)
                st = jnp.where(keep, st, NEG)
            pt, alpha = _online_softmax_t(st, m_ref, 2 * h + mp,slice(c * QB, (c + 1) * QB))
            if pending is not None:
                pv(*pending)
            pending = (mp, h, c, pt, alpha)
        pv(*pending)

    @pl.when(ki < qi)
    def _off():
        step(False)

    @pl.when(ki == qi)
    def _diag():
        step(True)
        lf = lam_ref[...]
        lam = (jnp.exp(jnp.sum(lf[0:1] * lf[1:2], axis=-1, keepdims=True))
               - jnp.exp(jnp.sum(lf[2:3] * lf[3:4], axis=-1, keepdims=True)) + lam_init)
        rows = []
        for h in range(4):
            hs, ls = slice(VROWS * h, VROWS * h + 64), slice(VROWS * h + 64, VROWS * h + 65)
            oh = (acc_ref[0, hs, :] * (1.0 / acc_ref[0, ls, :])
                  - lam * (acc_ref[1, hs, :] * (1.0 / acc_ref[1, ls, :])))
            ms = jnp.mean(oh * oh, axis=0, keepdims=True)
            rows.append(oh * lax.rsqrt(ms + EPS))
        ot = jnp.concatenate(rows, axis=0)
        o_ref[0] = (ot.T * gh_ref[...] * (1.0 - lam_init)).astype(BF)


def _attn_a(lam_p, gh, q, k, v, *, t, lam_init):
    B, S, _ = k.shape
    n = S // t
    qtab, ktab = _causal_pairs(n)
    grid_spec = pltpu.PrefetchScalarGridSpec(
        num_scalar_prefetch=2,
        grid=(B, qtab.shape[0]),
        in_specs=[pl.BlockSpec((4, 32), lambda b, p, qt, kt: (0, 0)),
                  pl.BlockSpec((1, GROUP), lambda b, p, qt, kt: (0, 0)),
                  pl.BlockSpec((1, GROUP, t), lambda b, p, qt, kt: (b, 0, qt[p])),
                  pl.BlockSpec((1, t, GROUP), lambda b, p, qt, kt: (b, kt[p], 0)),
                  pl.BlockSpec((1, 4 * VROWS, t), lambda b, p, qt, kt: (b, 0, kt[p]))],
        out_specs=pl.BlockSpec((1, t, GROUP), lambda b, p, qt, kt: (b, qt[p], 0)),
        scratch_shapes=[pltpu.VMEM((8, GROUP, t), BF),
                        pltpu.VMEM((8, t), F32),
                        pltpu.VMEM((2, 4 * VROWS, t), F32)])
    return pl.pallas_call(
        functools.partial(_attn_a_body, t=t, lam_init=lam_init),
        grid_spec=grid_spec,
        out_shape=jax.ShapeDtypeStruct((B, S, GROUP), BF),
        compiler_params=pltpu.CompilerParams(
            dimension_semantics=("arbitrary", "arbitrary"), vmem_limit_bytes=VMEM_LIMIT),
        name="attn_diff",
    )(qtab, ktab, lam_p, gh, q, k, v)


def _attn_b_body(qtab_ref, ktab_ref, g_ref, q_ref, k_ref, v_ref, o_ref, m_ref, acc_ref, *, t):
    qi = qtab_ref[pl.program_id(1)]
    ki = ktab_ref[pl.program_id(1)]

    @pl.when(ki == 0)
    def _init():
        m_ref[...] = jnp.full(m_ref.shape, NEG, F32)
        acc_ref[...] = jnp.zeros(acc_ref.shape, F32)

    def step(masked):
        jobs = [(h, c) for h in range(4) for c in range(t // QB)]

        def qk(h, c):
            return _dot(k_ref[0, :, h * 128:(h + 1) * 128],
                        q_ref[0, h * 128:(h + 1) * 128, c * QB:(c + 1) * QB])

        def pv(h, c, pt, alpha):
            hs, cs = slice(VROWS * h, VROWS * (h + 1)), slice(c * QB, (c + 1) * QB)
            acc_ref[hs, cs] = acc_ref[hs, cs] * alpha + _dot(v_ref[0, hs, :], pt)

        st_next = qk(*jobs[0])
        pending = None
        for n, (h, c) in enumerate(jobs):
            st = st_next
            if n + 1 < len(jobs):
                st_next = qk(*jobs[n + 1])
            if masked:
                keep = (lax.broadcasted_iota(jnp.int32, (t, QB), 0)
                        <= lax.broadcasted_iota(jnp.int32, (t, QB), 1) + c * QB)
                st = jnp.where(keep, st, NEG)
            pt, alpha = _online_softmax_t(st, m_ref, h,slice(c * QB, (c + 1) * QB))
            if pending is not None:
                pv(*pending)
            pending = (h, c, pt, alpha)
        pv(*pending)

    @pl.when(ki < qi)
    def _off():
        step(False)

    @pl.when(ki == qi)
    def _diag():
        step(True)
        rows = [acc_ref[VROWS * h:VROWS * h + 64, :] * (1.0 / acc_ref[VROWS * h + 64:VROWS * h + 65, :])
                for h in range(4)]
        ot = jnp.concatenate(rows, axis=0)
        o_ref[0] = _rms(ot.T, g_ref[...]).astype(BF)


def _attn_b(g, q, k, v, *, t):
    B, S, _ = k.shape
    n = S // t
    qtab, ktab = _causal_pairs(n)
    grid_spec = pltpu.PrefetchScalarGridSpec(
        num_scalar_prefetch=2,
        grid=(B, qtab.shape[0]),
        in_specs=[pl.BlockSpec((1, GROUP), lambda b, p, qt, kt: (0, 0)),
                  pl.BlockSpec((1, 512, t), lambda b, p, qt, kt: (b, 0, qt[p])),
                  pl.BlockSpec((1, t, 512), lambda b, p, qt, kt: (b, kt[p], 0)),
                  pl.BlockSpec((1, 4 * VROWS, t), lambda b, p, qt, kt: (b, 0, kt[p]))],
        out_specs=pl.BlockSpec((1, t, GROUP), lambda b, p, qt, kt: (b, qt[p], 0)),
        scratch_shapes=[pltpu.VMEM((4, t), F32),
                        pltpu.VMEM((4 * VROWS, t), F32)])
    return pl.pallas_call(
        functools.partial(_attn_b_body, t=t),
        grid_spec=grid_spec,
        out_shape=jax.ShapeDtypeStruct((B, S, GROUP), BF),
        compiler_params=pltpu.CompilerParams(
            dimension_semantics=("arbitrary", "arbitrary"), vmem_limit_bytes=VMEM_LIMIT),
        name="attn_mla",
    )(qtab, ktab, g, q, k, v)


DIL_TILE = 2048
DILATIONS = (1, 4, 16)


def _dil_body(g_ref, q_ref, kp_ref, kc_ref, vp_ref, vc_ref, o_ref, ob_ref, lb_ref, *, td):
    i = pl.program_id(1)
    lane = lax.broadcasted_iota(jnp.int32, (1, GROUP), 1)
    a = lax.broadcasted_iota(jnp.int32, (128, 256), 0)
    c = lax.broadcasted_iota(jnp.int32, (128, 256), 1)
    dist = jnp.where(c - a >= 0, c - a, 1000)

    def rows(ref, start, d):
        idx = pl.ds(start, 128) if d == 1 else pl.ds(start, 128, stride=d)
        return jnp.concatenate([ref.at[0, 0][idx, :], ref.at[0, 1][idx, :]], axis=1).astype(BF)

    for bi, d in enumerate(DILATIONS):
        def block(n, carry, bi=bi, d=d):
            if d == 1:
                r, blk = 0, n
            elif td // d == 128:
                r, blk = n, 0
            else:
                r, blk = n % d, n // d
            q0 = r + d * 128 * blk
            if d == 1:
                q0 = pl.multiple_of(q0, 128)
            q = rows(q_ref, q0, d)
            k_hi, v_hi = rows(kc_ref, q0, d), rows(vc_ref, q0, d)
            lo_prev = td - d * 128 + r
            if isinstance(blk, int):
                first = True
                k_lo, v_lo = rows(kp_ref, lo_prev, d), rows(vp_ref, lo_prev, d)
            else:
                first = blk == 0
                lo_cur = jnp.maximum(q0 - d * 128, 0)
                if d == 1:
                    lo_cur = pl.multiple_of(lo_cur, 128)
                k_lo = jnp.where(first, rows(kp_ref, lo_prev, d), rows(kc_ref, lo_cur, d))
                v_lo = jnp.where(first, rows(vp_ref, lo_prev, d), rows(vc_ref, lo_cur, d))
            kwin = jnp.concatenate([k_lo, k_hi], axis=0)
            vst = _vstack(jnp.concatenate([v_lo, v_hi], axis=0), lane)
            cmin = jnp.where(jnp.logical_and(first, i == 0), 128, 0)
            ok = jnp.where(c >= cmin, dist, 1000) <= 128
            es, ils, lses = [], [], []
            for h in range(4):
                s = _nt_dot(jnp.where(lane // 64 == h, q, jnp.zeros_like(q)), kwin)
                s = jnp.where(ok, s, NEG)
                m = jnp.max(s, axis=-1, keepdims=True)
                ex = jnp.exp(s - m)
                den = jnp.sum(ex, axis=-1, keepdims=True)
                es.append(ex.astype(BF))
                ils.append(1.0 / den)
                lses.append(m + jnp.log(den))
            o = _dot(jnp.concatenate(es, axis=1), vst) * _head_expand(ils, lane, 64)
            lse = _head_expand(lses, lane, 64)
            idx = pl.ds(q0, 128) if d == 1 else pl.ds(q0, 128, stride=d)
            for hf in range(2):
                ob_ref.at[bi, hf][idx, :] = o[:, hf * LANES:(hf + 1) * LANES]
                lb_ref.at[bi, hf][idx, :] = lse[:, hf * LANES:(hf + 1) * LANES]
            return carry

        lax.fori_loop(0, td // 128, block, 0, unroll=4)

    for ch in range(td // 256):
        rs = slice(ch * 256, (ch + 1) * 256)
        halves = []
        for hf in range(2):
            ls = [lb_ref[bi, hf, rs, :] for bi in range(3)]
            mx = jnp.maximum(jnp.maximum(ls[0], ls[1]), ls[2])
            ws = [jnp.exp(l - mx) for l in ls]
            num = ws[0] * ob_ref[0, hf, rs, :] + ws[1] * ob_ref[1, hf, rs, :] + ws[2] * ob_ref[2, hf, rs, :]
            halves.append(num / (ws[0] + ws[1] + ws[2]))
        o_ref[0, rs, :] = _rms(jnp.concatenate(halves, axis=1), g_ref[...]).astype(BF)


def _dil(g, q, k, v, *, td):
    B, _, S, _ = q.shape
    cur = pl.BlockSpec((1, 2, td, LANES), lambda b, i: (b, 0, i, 0))
    prev = pl.BlockSpec((1, 2, td, LANES), lambda b, i: (b, 0, jnp.maximum(i - 1, 0), 0))
    return pl.pallas_call(
        functools.partial(_dil_body, td=td),
        grid=(B, S // td),
        in_specs=[pl.BlockSpec((1, GROUP), lambda b, i: (0, 0)), cur, prev, cur, prev, cur],
        out_specs=pl.BlockSpec((1, td, GROUP), lambda b, i: (b, i, 0)),
        out_shape=jax.ShapeDtypeStruct((B, S, GROUP), BF),
        scratch_shapes=[pltpu.VMEM((3, 2, td, LANES), F32),
                        pltpu.VMEM((3, 2, td, LANES), F32)],
        compiler_params=pltpu.CompilerParams(
            dimension_semantics=("arbitrary", "arbitrary"), vmem_limit_bytes=VMEM_LIMIT),
        name="dilated",
    )(g, q, k, k, v, v)


def _outproj_body(x_ref, oa_ref, ob_ref, oc_ref, od_ref, w_ref, y_ref):
    y = _dot(oa_ref[...], w_ref[0:256, :])
    y += _dot(ob_ref[...], w_ref[256:512, :])
    y += _dot(oc_ref[...], w_ref[512:768, :])
    y += _dot(od_ref[...], w_ref[768:1024, :])
    y_ref[...] = x_ref[...] + y


def _outproj(x, oa, ob, oc, od, w_out, *, tm):
    T, D = x.shape
    row = lambda w: pl.BlockSpec((tm, w), lambda i: (i, 0))
    full = lambda shape: pl.BlockSpec(shape, lambda i: (0,) * len(shape))
    return pl.pallas_call(
        _outproj_body,
        grid=(T // tm,),
        in_specs=[row(D)] + [row(GROUP)] * 4 + [full((D, D))],
        out_specs=row(D),
        out_shape=jax.ShapeDtypeStruct((T, D), F32),
        compiler_params=pltpu.CompilerParams(
            dimension_semantics=("arbitrary",), vmem_limit_bytes=VMEM_LIMIT),
        name="outproj",
    )(x, oa, ob, oc, od, w_out)


PEER_EB = 2048
GELU_C1 = math.sqrt(2.0 / math.pi)
GELU_C2 = 0.044715 * GELU_C1
N_CAND = PEER_TOP + 1


def _peer_body(x_ref, g_ref, wqt_ref, sk_ref, u_ref, vt_ref, gf_ref, o_ref,
               ht_ref, st_ref, e2_ref, th_ref, cw_ref, hw_ref, acc_ref,
               *, tt, final):
    e = pl.program_id(1)
    nch = tt // LANES

    @pl.when(e == 0)
    def _prologue():
        h = _rms(x_ref[...], g_ref[...])
        ht_ref[...] = h.T.astype(BF)
        qt = _dot(wqt_ref[...], ht_ref[...]).astype(BF)
        for hh in range(PEER_HEADS):
            for c in range(2):
                r0 = hh * N_KEYS + c * 64
                sc = _dot(sk_ref[c], qt[r0:r0 + 64, :])
                for tc in range(nch):
                    blk = sc[:, tc * LANES:(tc + 1) * LANES]
                    st_ref.at[tc][pl.ds(c * 1024 + hh, N_KEYS, stride=PEER_HEADS), :] = blk
                    if c == 1:
                        st_ref[tc, 2048 + hh * N_KEYS:2048 + (hh + 1) * N_KEYS, :] = blk
        acc_ref[...] = jnp.zeros(acc_ref.shape, F32)

        def top_vals(load, n_rows, n_out):
            top = []
            for r in range(n_rows):
                v = load(r)
                for q in range(len(top)):
                    top[q], v = jnp.maximum(top[q], v), jnp.minimum(top[q], v)
                if len(top) < n_out:
                    top.append(v)
            return top

        def token_chunk(tc, carry):
            ls = pl.ds(pl.multiple_of(tc * LANES, LANES), LANES)
            v1 = top_vals(lambda r: st_ref[tc, r * 8:(r + 1) * 8, :], N_KEYS, N_CAND)
            v2 = top_vals(lambda r: st_ref[tc, 1024 + r * 8:1024 + (r + 1) * 8, :], N_KEYS, N_CAND)
            cands = [v1[a] + v2[b] for a in range(N_CAND) for b in range(N_CAND)
                     if (a + 1) * (b + 1) <= N_CAND]
            top = top_vals(lambda r: cands[r], len(cands), N_CAND)
            z = jnp.ones_like(top[0])
            for r in range(1, PEER_TOP):
                z = z + jnp.exp(top[r] - top[0])
            tau = 0.5 * (top[PEER_TOP - 1] + top[PEER_TOP])
            iz = 1.0 / z
            for r in range(N_KEYS):
                s1 = st_ref[tc, r * 8:(r + 1) * 8, :]
                th_ref[r, :, ls] = tau - s1
                cw_ref[r, :, ls] = jnp.exp(s1 - v1[0]) * (0.5 * iz)
            for hh in range(PEER_HEADS):
                rows = slice(2048 + hh * 128, 2048 + (hh + 1) * 128)
                e2_ref[hh * 128:(hh + 1) * 128, ls] = jnp.exp(st_ref[tc, rows, :] - v2[0][hh:hh + 1, :])
            return carry

        lax.fori_loop(0, nch, token_chunk, 0)

    nblk = PEER_EB // 256

    def pre_act(b):
        return _dot(u_ref[b * 256:(b + 1) * 256, :], ht_ref[...])

    def gate(b, a):
        i0 = e * (PEER_EB // N_KEYS) + 2 * b
        th = [th_ref[i0], th_ref[i0 + 1]]
        cw = [cw_ref[i0], cw_ref[i0 + 1]]
        for tc in range(nch):
            ls = slice(tc * LANES, (tc + 1) * LANES)
            for sub in range(2):
                w = [None, None]
                for hh in range(PEER_HEADS):
                    r0 = hh * N_KEYS + sub * 64
                    s2 = st_ref[tc, 2048 + r0:2048 + r0 + 64, :]
                    e2 = e2_ref[r0:r0 + 64, ls]
                    for half in range(2):
                        sel = jnp.where(s2 >= th[half][hh:hh + 1, ls], e2, 0.0) * cw[half][hh:hh + 1, ls]
                        w[half] = sel if w[half] is None else w[half] + sel
                for half in range(2):
                    r1 = half * N_KEYS + sub * 64
                    x = a[r1:r1 + 64, ls]
                    xw = x * w[half]
                    t = jnp.tanh(x * (GELU_C1 + GELU_C2 * (x * x)))
                    hw_ref[b % 2, r1:r1 + 64, ls] = (xw + xw * t).astype(BF)

    a = pre_act(0)
    for b in range(nblk):
        a_next = pre_act(b + 1) if b + 1 < nblk else None
        gate(b, a)
        if b > 0:
            acc_ref[...] += _dot(vt_ref[b - 1], hw_ref[(b - 1) % 2])
        a = a_next
    acc_ref[...] += _dot(vt_ref[nblk - 1], hw_ref[(nblk - 1) % 2])

    @pl.when(e == pl.num_programs(1) - 1)
    def _finish():
        y = x_ref[...] + acc_ref[...].T
        if final:
            y = _rms(y, gf_ref[...])
        o_ref[...] = y


def _peer(x, g, wqt, sk, u, vt, gf, *, tt, final):
    T, D = x.shape
    ne = u.shape[0] // PEER_EB
    const = lambda shape: pl.BlockSpec(shape, lambda i, e: (0,) * len(shape),
                                       pipeline_mode=pl.Buffered(1))
    return pl.pallas_call(
        functools.partial(_peer_body, tt=tt, final=final),
        grid=(T // tt, ne),
        in_specs=[pl.BlockSpec((tt, D), lambda i, e: (i, 0)),
                  const((1, D)), const((D, D)), const((2, N_KEYS, 64)),
                  pl.BlockSpec((PEER_EB, D), lambda i, e: (e, 0)),
                  pl.BlockSpec((PEER_EB // 256, D, 256), lambda i, e: (e, 0, 0)),
                  const((1, D))],
        out_specs=pl.BlockSpec((tt, D), lambda i, e: (i, 0)),
        out_shape=jax.ShapeDtypeStruct((T, D), F32),
        scratch_shapes=[pltpu.VMEM((D, tt), BF),
                        pltpu.VMEM((tt // LANES, 3 * 1024, LANES), F32),
                        pltpu.VMEM((1024, tt), F32),
                        pltpu.VMEM((N_KEYS, PEER_HEADS, tt), F32),
                        pltpu.VMEM((N_KEYS, PEER_HEADS, tt), F32),
                        pltpu.VMEM((2, 256, tt), BF),
                        pltpu.VMEM((D, tt), F32)],
        compiler_params=pltpu.CompilerParams(
            dimension_semantics=("arbitrary", "arbitrary"), vmem_limit_bytes=VMEM_LIMIT),
        name="peer",
    )(x, g, wqt, sk, u, vt, gf)


def _pack_in_weights(w_in):
    aq, ak, av, cq, ckv, kpe, zc, dq, dk, dv = jnp.split(
        w_in, np.cumsum([256, 256, 256, 256, 128, 32, 512, 256, 256])[:].tolist(), axis=1)
    kpe_rep = jnp.zeros((D_MODEL, 4, 128), w_in.dtype).at[:, :, 64:96].set(kpe[:, None, :])
    return jnp.concatenate(
        [aq, ak, av, cq, ckv, kpe_rep.reshape(D_MODEL, 512), zc, dq, dk, dv], axis=1).astype(BF)


def _pack_mla_weights(w_uq, w_ukv):
    uq = jnp.pad(w_uq.reshape(256, 4, 96), ((0, 0), (0, 0), (0, 32))).reshape(256, 512)
    ukv = w_ukv.reshape(128, 4, 128)
    uk = jnp.pad(ukv[:, :, :64], ((0, 0), (0, 0), (0, 64))).reshape(128, 512)
    uv = ukv[:, :, 64:].reshape(128, 256)
    return uq.astype(BF), uk.astype(BF), uv.astype(BF)


def _pack_expert_out(v):
    n, d = v.shape
    return v.reshape(n // 256, 256, d).transpose(0, 2, 1).astype(BF)


def kernel(x, w_in, w_out, norm_mix, norm_ffn, diff_lambda, diff_head_norm, mla_q_norm, mla_kv_norm, mla_w_uq, mla_w_ukv, mla_out_norm, sgu_v_norm, sgu_w, sgu_b, sgu_out_norm, dil_out_norm, peer_w_q, peer_sub_keys, peer_u, peer_v, final_norm):
    B, S, D = x.shape
    depth = w_in.shape[0]
    T = B * S
    tm = min(512, S)
    ta = min(512, S)
    tt = min(512, T)
    row = lambda a: a.reshape(1, -1)

    tabs = (_rope_tables(S, 8, 32, 0) + _rope_tables(S, 32, 128, 64) + _rope_tables(S, 16, 64, 0))

    for l in range(depth):
        lam_init = 0.8 - 0.6 * math.exp(-0.3 * l)
        w_cat = _pack_in_weights(w_in[l])
        wuq, wuk, wuv = _pack_mla_weights(mla_w_uq[l], mla_w_ukv[l])
        sgub = jnp.repeat(sgu_b[l].T, 64, axis=1)
        qa, ka, va, qb, kb, vb, oc, qd, kd, vd = _inproj(
            x, row(norm_mix[l]), w_cat, tabs, row(mla_q_norm[l]), row(mla_kv_norm[l]),
            wuq, wuk, wuv, row(sgu_v_norm[l]), sgu_w[l], sgub, row(sgu_out_norm[l]), tm=tm)

        oa = _attn_a(diff_lambda[l], row(diff_head_norm[l]), qa, ka, va, t=ta, lam_init=lam_init)
        ob = _attn_b(row(mla_out_norm[l]), qb, kb, vb, t=ta)

        od = _dil(row(dil_out_norm[l]), qd, kd, vd, td=DIL_TILE)

        x2 = _outproj(x.reshape(T, D), oa.reshape(T, GROUP), ob.reshape(T, GROUP),
                      oc.reshape(T, GROUP), od.reshape(T, GROUP),
                      w_out[l].astype(BF), tm=min(512, T))

        x2 = _peer(x2, row(norm_ffn[l]), peer_w_q[l].T.astype(BF), peer_sub_keys[l].astype(BF),
                   peer_u[l].astype(BF), _pack_expert_out(peer_v[l]), row(final_norm),
                   tt=tt, final=(l == depth - 1))
        x = x2.reshape(B, S, D)
    return x
```

```python
import functools
import math

import numpy as np
import jax
import jax.numpy as jnp
from jax import lax
from jax.experimental import pallas as pl
from jax.experimental.pallas import tpu as pltpu

D_MODEL = 1024
GROUP = 256
ROPE_THETA = 500000.0
NEG = -1e30
EPS = 1e-6
LANES = 128
LOG2E = math.log2(math.e)

N_KEYS = 128
PEER_HEADS = 8
PEER_TOP = 16

BF = jnp.bfloat16
F32 = jnp.float32

C_AQ, C_AK, C_AV, C_CQ, C_CKV, C_KPE, C_ZC, C_DQ, C_DK, C_DV, C_END = (
    0, 256, 512, 768, 1024, 1152, 1664, 2176, 2432, 2688, 2944)

VMEM_LIMIT = 56 * 1024 * 1024


def _rms(x, g):
    return x * lax.rsqrt(jnp.mean(x * x, axis=-1, keepdims=True) + EPS) * g


def _nt_dot(a, b):
    return lax.dot_general(a, b, (((1,), (1,)), ((), ())), preferred_element_type=F32)


def _dot(a, b):
    return jnp.dot(a, b, preferred_element_type=F32)


def _rope_tables(seq, n_rot, period, offset):
    half = n_rot // 2
    pos = jnp.arange(seq, dtype=F32)
    inv = ROPE_THETA ** (-jnp.arange(half, dtype=F32) * (2.0 / n_rot))
    ang = pos[:, None] * inv[None, :]
    cos, sin = jnp.cos(ang), jnp.sin(ang)
    g = np.arange(LANES) % period - offset
    rot = (g >= 0) & (g < n_rot)
    idx = np.where(rot, g % half, 0)
    sign = np.where(g < half, -1.0, 1.0).astype(np.float32)
    cos_t = jnp.where(rot[None, :], cos[:, idx], 1.0)
    sin_t = jnp.where(rot[None, :], sin[:, idx] * sign[None, :], 0.0)
    return cos_t.astype(F32), sin_t.astype(F32)


def _rope_apply(x, cos_t, sin_t, n_rot, period, offset):
    half = n_rot // 2
    lane = lax.broadcasted_iota(jnp.int32, (1, LANES), 1)
    first = (lane % period - offset) < half
    outs = []
    for c in range(x.shape[1] // LANES):
        xc = x[:, c * LANES:(c + 1) * LANES]
        fwd = pltpu.roll(xc, LANES - half, 1)
        bwd = pltpu.roll(xc, half, 1)
        outs.append(xc * cos_t + jnp.where(first, fwd, bwd) * sin_t)
    return outs[0] if len(outs) == 1 else jnp.concatenate(outs, axis=1)


VROWS = 80


def _value_rows(vt):
    one = (lax.broadcasted_iota(jnp.int32, (VROWS - 64, vt.shape[1]), 0) == 0).astype(F32)
    parts = []
    for h in range(4):
        parts += [vt[64 * h:64 * (h + 1)], one]
    return jnp.concatenate(parts, axis=0).astype(BF)


def _inproj_body(x_ref, gmix_ref, w_ref, ca_ref, sa_ref, cb_ref, sb_ref, cd_ref, sd_ref,
                 gq_ref, gkv_ref, wuq_ref, wuk_ref, wuv_ref,
                 gsv_ref, sguw_ref, sgub_ref, gso_ref,
                 qa_ref, ka_ref, va_ref, qb_ref, kb_ref, vb_ref, oc_ref,
                 qd_ref, kd_ref, vd_ref, *, tm):
    x = x_ref[0]
    h = _rms(x, gmix_ref[...]).astype(BF)

    def proj(a, b):
        return _dot(h, w_ref[:, a:b])

    ca, sa = ca_ref[...], sa_ref[...]
    aq = _rope_apply(proj(C_AQ, C_AK), ca, sa, 8, 32, 0)
    qa_ref[0] = (aq * (32.0 ** -0.5 * LOG2E)).T.astype(BF)
    ka_ref[0] = _rope_apply(proj(C_AK, C_AV), ca, sa, 8, 32, 0).astype(BF)
    va_ref[0] = _value_rows(proj(C_AV, C_CQ).T)

    cb, sb = cb_ref[...], sb_ref[...]
    cq = _rms(proj(C_CQ, C_CKV), gq_ref[...]).astype(BF)
    qb = _rope_apply(_dot(cq, wuq_ref[...]), cb, sb, 32, 128, 64)
    qb_ref[0] = (qb * (96.0 ** -0.5 * LOG2E)).T.astype(BF)
    ckv = _rms(proj(C_CKV, C_KPE), gkv_ref[...]).astype(BF)
    kpe = _rope_apply(proj(C_KPE, C_ZC), cb, sb, 32, 128, 64)
    kb_ref[0] = (_dot(ckv, wuk_ref[...]) + kpe).astype(BF)
    vb_ref[0] = _value_rows(_dot(ckv, wuv_ref[...]).T)

    zc = jax.nn.gelu(proj(C_ZC, C_DQ))
    u = zc[:, :GROUP]
    vn = _rms(zc[:, GROUP:], gsv_ref[...]).astype(BF)
    r = lax.broadcasted_iota(jnp.int32, (128, 128), 0)
    c = lax.broadcasted_iota(jnp.int32, (128, 128), 1)
    wcat = jnp.concatenate(
        [jnp.where(r >= c, sguw_ref[g], 0.0).astype(BF) for g in range(4)], axis=1)
    lane = lax.broadcasted_iota(jnp.int32, (1, GROUP), 1)
    bias = sgub_ref[...]
    gso = gso_ref[...]
    for ch in range(tm // 128):
        vc = vn[ch * 128:(ch + 1) * 128]
        vst = jnp.concatenate(
            [jnp.where(lane // 64 == g, vc, jnp.zeros_like(vc)) for g in range(4)], axis=0)
        sv = _dot(wcat, vst) + bias
        oc = u[ch * 128:(ch + 1) * 128] * sv
        oc_ref[0, ch * 128:(ch + 1) * 128, :] = _rms(oc, gso).astype(BF)

    cd, sd = cd_ref[...], sd_ref[...]
    dq = _rope_apply(proj(C_DQ, C_DK), cd, sd, 16, 64, 0)
    dk = _rope_apply(proj(C_DK, C_DV), cd, sd, 16, 64, 0)
    dv = proj(C_DV, C_END)
    for hf in range(2):
        qd_ref[0, hf] = dq[:, hf * LANES:(hf + 1) * LANES] * 0.125
        kd_ref[0, hf] = dk[:, hf * LANES:(hf + 1) * LANES]
        vd_ref[0, hf] = dv[:, hf * LANES:(hf + 1) * LANES]


def _inproj(x, gmix, w_cat, tabs, gq, gkv, wuq, wuk, wuv, gsv, sguw, sgub, gso, *, tm):
    B, S, D = x.shape
    ns = S // tm
    full = lambda shape: pl.BlockSpec(shape, lambda s, b: (0,) * len(shape))
    tab = pl.BlockSpec((tm, LANES), lambda s, b: (s, 0))
    seq = lambda w: pl.BlockSpec((1, tm, w), lambda s, b: (b, s, 0))
    seq_t = lambda w: pl.BlockSpec((1, w, tm), lambda s, b: (b, 0, s))
    halves = pl.BlockSpec((1, 2, tm, LANES), lambda s, b: (b, 0, s, 0))
    out_w = [256, 256, 4 * VROWS, 512, 512, 4 * VROWS, 256]
    transposed = [True, False, True, True, False, True, False]
    return pl.pallas_call(
        functools.partial(_inproj_body, tm=tm),
        grid=(ns, B),
        in_specs=[seq(D), full((1, D)), full((D, C_END))] + [tab] * 6 + [
            full((1, 256)), full((1, 128)), full((256, 512)), full((128, 512)), full((128, 256)),
            full((1, 256)), full((4, 128, 128)), full((128, 256)), full((1, 256))],
        out_specs=[seq_t(w) if tr else seq(w) for w, tr in zip(out_w, transposed)] + [halves] * 3,
        out_shape=[jax.ShapeDtypeStruct((B, w, S) if tr else (B, S, w), BF)
                   for w, tr in zip(out_w, transposed)]
        + [jax.ShapeDtypeStruct((B, 2, S, LANES), F32)] * 3,
        compiler_params=pltpu.CompilerParams(
            dimension_semantics=("arbitrary", "arbitrary"), vmem_limit_bytes=VMEM_LIMIT),
        name="inproj",
    )(x, gmix, w_cat, *tabs, gq, gkv, wuq, wuk, wuv, gsv, sguw, sgub, gso)


def _head_expand(cols, lane, width):
    out = cols[-1]
    for h in range(len(cols) - 2, -1, -1):
        out = jnp.where(lane // width == h, cols[h], out)
    return out


def _vstack(v, lane):
    return jnp.concatenate(
        [jnp.where(lane // 64 == h, v, jnp.zeros_like(v)) for h in range(4)], axis=0)


ATT_TILE = 1024
QB = 512


def _online_softmax_t(st, m_ref, j, cs):
    m_prev = m_ref[j:j + 1, cs]
    m_new = jnp.maximum(m_prev, jnp.max(st, axis=0, keepdims=True))
    alpha = jnp.exp2(m_prev - m_new)
    et = jnp.exp2(st - m_new)
    m_ref[j:j + 1, cs] = m_new
    return et.astype(BF), alpha


def _attn_jobs(ns, diag, heads):
    return [(*hd, qc, kc) for kc in range(ns) for qc in range(ns) for hd in heads
            if not (diag and kc > qc)]


def _causal_keep():
    return (lax.broadcasted_iota(jnp.int32, (QB, QB), 0)
            <= lax.broadcasted_iota(jnp.int32, (QB, QB), 1))


def _causal_pairs(n):
    pairs = [(i, j) for i in range(n) for j in range(i + 1)]
    return (jnp.asarray([p[0] for p in pairs], jnp.int32),
            jnp.asarray([p[1] for p in pairs], jnp.int32))


def _attn_a_body(qtab_ref, ktab_ref, lam_ref, gh_ref, q_ref, k_ref, v_ref, o_ref,
                 qs_ref, m_ref, acc_ref, *, t, lam_init):
    qi = qtab_ref[pl.program_id(1)]
    ki = ktab_ref[pl.program_id(1)]

    @pl.when(ki == 0)
    def _init():
        m_ref[...] = jnp.full(m_ref.shape, NEG, F32)
        acc_ref[...] = jnp.zeros(acc_ref.shape, F32)
        qt = q_ref[0]
        row = lax.broadcasted_iota(jnp.int32, (GROUP, 1), 0)
        for j in range(8):
            qs_ref[j] = jnp.where(row // 32 == j, qt, jnp.zeros_like(qt))

    def step(diag):
        jobs = _attn_jobs(t // QB, diag, [(mp, h) for mp in range(2) for h in range(4)])

        def qk(mp, h, qc, kc):
            return _dot(k_ref[0, kc * QB:(kc + 1) * QB, :],
                        qs_ref[2 * h + mp, :, qc * QB:(qc + 1) * QB])

        def pv(mp, h, qc, kc, pt, alpha):
            hs, cs = slice(VROWS * h, VROWS * (h + 1)), slice(qc * QB, (qc + 1) * QB)
            acc_ref[mp, hs, cs] = (acc_ref[mp, hs, cs] * alpha
                                   + _dot(v_ref[0, hs, kc * QB:(kc + 1) * QB], pt))

        st_next = qk(*jobs[0])
        pending = None
        for n, (mp, h, qc, kc) in enumerate(jobs):
            st = st_next
            if n + 1 < len(jobs):
                st_next = qk(*jobs[n + 1])
            if diag and kc == qc:
                st = jnp.where(_causal_keep(), st, NEG)
            pt, alpha = _online_softmax_t(st, m_ref, 2 * h + mp, slice(qc * QB, (qc + 1) * QB))
            if pending is not None:
                pv(*pending)
            pending = (mp, h, qc, kc, pt, alpha)
        pv(*pending)

    @pl.when(ki < qi)
    def _off():
        step(False)

    @pl.when(ki == qi)
    def _diag():
        step(True)
        lf = lam_ref[...]
        lam = (jnp.exp(jnp.sum(lf[0:1] * lf[1:2], axis=-1, keepdims=True))
               - jnp.exp(jnp.sum(lf[2:3] * lf[3:4], axis=-1, keepdims=True)) + lam_init)
        rows = []
        for h in range(4):
            hs, ls = slice(VROWS * h, VROWS * h + 64), slice(VROWS * h + 64, VROWS * h + 65)
            oh = (acc_ref[0, hs, :] * (1.0 / acc_ref[0, ls, :])
                  - lam * (acc_ref[1, hs, :] * (1.0 / acc_ref[1, ls, :])))
            ms = jnp.mean(oh * oh, axis=0, keepdims=True)
            rows.append(oh * lax.rsqrt(ms + EPS))
        ot = jnp.concatenate(rows, axis=0)
        o_ref[0] = (ot.T * gh_ref[...] * (1.0 - lam_init)).astype(BF)


def _attn_a(lam_p, gh, q, k, v, *, t, lam_init):
    B, S, _ = k.shape
    n = S // t
    qtab, ktab = _causal_pairs(n)
    grid_spec = pltpu.PrefetchScalarGridSpec(
        num_scalar_prefetch=2,
        grid=(B, qtab.shape[0]),
        in_specs=[pl.BlockSpec((4, 32), lambda b, p, qt, kt: (0, 0)),
                  pl.BlockSpec((1, GROUP), lambda b, p, qt, kt: (0, 0)),
                  pl.BlockSpec((1, GROUP, t), lambda b, p, qt, kt: (b, 0, qt[p])),
                  pl.BlockSpec((1, t, GROUP), lambda b, p, qt, kt: (b, kt[p], 0)),
                  pl.BlockSpec((1, 4 * VROWS, t), lambda b, p, qt, kt: (b, 0, kt[p]))],
        out_specs=pl.BlockSpec((1, t, GROUP), lambda b, p, qt, kt: (b, qt[p], 0)),
        scratch_shapes=[pltpu.VMEM((8, GROUP, t), BF),
                        pltpu.VMEM((8, t), F32),
                        pltpu.VMEM((2, 4 * VROWS, t), F32)])
    return pl.pallas_call(
        functools.partial(_attn_a_body, t=t, lam_init=lam_init),
        grid_spec=grid_spec,
        out_shape=jax.ShapeDtypeStruct((B, S, GROUP), BF),
        compiler_params=pltpu.CompilerParams(
            dimension_semantics=("arbitrary", "arbitrary"), vmem_limit_bytes=VMEM_LIMIT),
        name="attn_diff",
    )(qtab, ktab, lam_p, gh, q, k, v)


def _attn_b_body(qtab_ref, ktab_ref, g_ref, q_ref, k_ref, v_ref, o_ref, m_ref, acc_ref, *, t):
    qi = qtab_ref[pl.program_id(1)]
    ki = ktab_ref[pl.program_id(1)]

    @pl.when(ki == 0)
    def _init():
        m_ref[...] = jnp.full(m_ref.shape, NEG, F32)
        acc_ref[...] = jnp.zeros(acc_ref.shape, F32)

    def step(diag):
        jobs = _attn_jobs(t // QB, diag, [(h,) for h in range(4)])

        def qk(h, qc, kc):
            return _dot(k_ref[0, kc * QB:(kc + 1) * QB, h * 128:(h + 1) * 128],
                        q_ref[0, h * 128:(h + 1) * 128, qc * QB:(qc + 1) * QB])

        def pv(h, qc, kc, pt, alpha):
            hs, cs = slice(VROWS * h, VROWS * (h + 1)), slice(qc * QB, (qc + 1) * QB)
            acc_ref[hs, cs] = acc_ref[hs, cs] * alpha + _dot(v_ref[0, hs, kc * QB:(kc + 1) * QB], pt)

        st_next = qk(*jobs[0])
        pending = None
        for n, (h, qc, kc) in enumerate(jobs):
            st = st_next
            if n + 1 < len(jobs):
                st_next = qk(*jobs[n + 1])
            if diag and kc == qc:
                st = jnp.where(_causal_keep(), st, NEG)
            pt, alpha = _online_softmax_t(st, m_ref, h, slice(qc * QB, (qc + 1) * QB))
            if pending is not None:
                pv(*pending)
            pending = (h, qc, kc, pt, alpha)
        pv(*pending)

    @pl.when(ki < qi)
    def _off():
        step(False)

    @pl.when(ki == qi)
    def _diag():
        step(True)
        rows = [acc_ref[VROWS * h:VROWS * h + 64, :] * (1.0 / acc_ref[VROWS * h + 64:VROWS * h + 65, :])
                for h in range(4)]
        ot = jnp.concatenate(rows, axis=0)
        o_ref[0] = _rms(ot.T, g_ref[...]).astype(BF)


def _attn_b(g, q, k, v, *, t):
    B, S, _ = k.shape
    n = S // t
    qtab, ktab = _causal_pairs(n)
    grid_spec = pltpu.PrefetchScalarGridSpec(
        num_scalar_prefetch=2,
        grid=(B, qtab.shape[0]),
        in_specs=[pl.BlockSpec((1, GROUP), lambda b, p, qt, kt: (0, 0)),
                  pl.BlockSpec((1, 512, t), lambda b, p, qt, kt: (b, 0, qt[p])),
                  pl.BlockSpec((1, t, 512), lambda b, p, qt, kt: (b, kt[p], 0)),
                  pl.BlockSpec((1, 4 * VROWS, t), lambda b, p, qt, kt: (b, 0, kt[p]))],
        out_specs=pl.BlockSpec((1, t, GROUP), lambda b, p, qt, kt: (b, qt[p], 0)),
        scratch_shapes=[pltpu.VMEM((4, t), F32),
                        pltpu.VMEM((4 * VROWS, t), F32)])
    return pl.pallas_call(
        functools.partial(_attn_b_body, t=t),
        grid_spec=grid_spec,
        out_shape=jax.ShapeDtypeStruct((B, S, GROUP), BF),
        compiler_params=pltpu.CompilerParams(
            dimension_semantics=("arbitrary", "arbitrary"), vmem_limit_bytes=VMEM_LIMIT),
        name="attn_mla",
    )(qtab, ktab, g, q, k, v)


DIL_TILE = 2048
DILATIONS = (1, 4, 16)


def _dil_body(g_ref, q_ref, kp_ref, kc_ref, vp_ref, vc_ref, o_ref, ob_ref, lb_ref, *, td):
    i = pl.program_id(1)
    lane = lax.broadcasted_iota(jnp.int32, (1, GROUP), 1)
    a = lax.broadcasted_iota(jnp.int32, (128, 256), 0)
    c = lax.broadcasted_iota(jnp.int32, (128, 256), 1)
    dist = jnp.where(c - a >= 0, c - a, 1000)

    def rows(ref, start, d):
        idx = pl.ds(start, 128) if d == 1 else pl.ds(start, 128, stride=d)
        return jnp.concatenate([ref.at[0, 0][idx, :], ref.at[0, 1][idx, :]], axis=1).astype(BF)

    for bi, d in enumerate(DILATIONS):
        def block(n, carry, bi=bi, d=d):
            if d == 1:
                r, blk = 0, n
            elif td // d == 128:
                r, blk = n, 0
            else:
                r, blk = n % d, n // d
            q0 = r + d * 128 * blk
            if d == 1:
                q0 = pl.multiple_of(q0, 128)
            q = rows(q_ref, q0, d)
            k_hi, v_hi = rows(kc_ref, q0, d), rows(vc_ref, q0, d)
            lo_prev = td - d * 128 + r
            if isinstance(blk, int):
                first = True
                k_lo, v_lo = rows(kp_ref, lo_prev, d), rows(vp_ref, lo_prev, d)
            else:
                first = blk == 0
                lo_cur = jnp.maximum(q0 - d * 128, 0)
                if d == 1:
                    lo_cur = pl.multiple_of(lo_cur, 128)
                k_lo = jnp.where(first, rows(kp_ref, lo_prev, d), rows(kc_ref, lo_cur, d))
                v_lo = jnp.where(first, rows(vp_ref, lo_prev, d), rows(vc_ref, lo_cur, d))
            kwin = jnp.concatenate([k_lo, k_hi], axis=0)
            vst = _vstack(jnp.concatenate([v_lo, v_hi], axis=0), lane)
            cmin = jnp.where(jnp.logical_and(first, i == 0), 128, 0)
            ok = jnp.where(c >= cmin, dist, 1000) <= 128
            es, ils, lses = [], [], []
            for h in range(4):
                s = _nt_dot(jnp.where(lane // 64 == h, q, jnp.zeros_like(q)), kwin)
                s = jnp.where(ok, s, NEG)
                m = jnp.max(s, axis=-1, keepdims=True)
                ex = jnp.exp(s - m)
                den = jnp.sum(ex, axis=-1, keepdims=True)
                es.append(ex.astype(BF))
                ils.append(1.0 / den)
                lses.append(m + jnp.log(den))
            o = _dot(jnp.concatenate(es, axis=1), vst) * _head_expand(ils, lane, 64)
            lse = _head_expand(lses, lane, 64)
            idx = pl.ds(q0, 128) if d == 1 else pl.ds(q0, 128, stride=d)
            for hf in range(2):
                ob_ref.at[bi, hf][idx, :] = o[:, hf * LANES:(hf + 1) * LANES]
                lb_ref.at[bi, hf][idx, :] = lse[:, hf * LANES:(hf + 1) * LANES]
            return carry

        lax.fori_loop(0, td // 128, block, 0, unroll=4)

    for ch in range(td // 256):
        rs = slice(ch * 256, (ch + 1) * 256)
        halves = []
        for hf in range(2):
            ls = [lb_ref[bi, hf, rs, :] for bi in range(3)]
            mx = jnp.maximum(jnp.maximum(ls[0], ls[1]), ls[2])
            ws = [jnp.exp(l - mx) for l in ls]
            num = ws[0] * ob_ref[0, hf, rs, :] + ws[1] * ob_ref[1, hf, rs, :] + ws[2] * ob_ref[2, hf, rs, :]
            halves.append(num / (ws[0] + ws[1] + ws[2]))
        o_ref[0, rs, :] = _rms(jnp.concatenate(halves, axis=1), g_ref[...]).astype(BF)


def _dil(g, q, k, v, *, td):
    B, _, S, _ = q.shape
    cur = pl.BlockSpec((1, 2, td, LANES), lambda b, i: (b, 0, i, 0))
    prev = pl.BlockSpec((1, 2, td, LANES), lambda b, i: (b, 0, jnp.maximum(i - 1, 0), 0))
    return pl.pallas_call(
        functools.partial(_dil_body, td=td),
        grid=(B, S // td),
        in_specs=[pl.BlockSpec((1, GROUP), lambda b, i: (0, 0)), cur, prev, cur, prev, cur],
        out_specs=pl.BlockSpec((1, td, GROUP), lambda b, i: (b, i, 0)),
        out_shape=jax.ShapeDtypeStruct((B, S, GROUP), BF),
        scratch_shapes=[pltpu.VMEM((3, 2, td, LANES), F32),
                        pltpu.VMEM((3, 2, td, LANES), F32)],
        compiler_params=pltpu.CompilerParams(
            dimension_semantics=("arbitrary", "arbitrary"), vmem_limit_bytes=VMEM_LIMIT),
        name="dilated",
    )(g, q, k, k, v, v)


def _outproj_body(x_ref, oa_ref, ob_ref, oc_ref, od_ref, w_ref, y_ref):
    y = _dot(oa_ref[...], w_ref[0:256, :])
    y += _dot(ob_ref[...], w_ref[256:512, :])
    y += _dot(oc_ref[...], w_ref[512:768, :])
    y += _dot(od_ref[...], w_ref[768:1024, :])
    y_ref[...] = x_ref[...] + y


def _outproj(x, oa, ob, oc, od, w_out, *, tm):
    T, D = x.shape
    row = lambda w: pl.BlockSpec((tm, w), lambda i: (i, 0))
    full = lambda shape: pl.BlockSpec(shape, lambda i: (0,) * len(shape))
    return pl.pallas_call(
        _outproj_body,
        grid=(T // tm,),
        in_specs=[row(D)] + [row(GROUP)] * 4 + [full((D, D))],
        out_specs=row(D),
        out_shape=jax.ShapeDtypeStruct((T, D), F32),
        compiler_params=pltpu.CompilerParams(
            dimension_semantics=("arbitrary",), vmem_limit_bytes=VMEM_LIMIT),
        name="outproj",
    )(x, oa, ob, oc, od, w_out)


PEER_EB = 2048
GELU_C1 = math.sqrt(2.0 / math.pi)
GELU_C2 = 0.044715 * GELU_C1
N_CAND = PEER_TOP + 1


def _peer_body(x_ref, g_ref, wqt_ref, sk_ref, u_ref, vt_ref, gf_ref, o_ref,
               ht_ref, st_ref, e2_ref, th_ref, cw_ref, hw_ref, acc_ref,
               *, tt, final):
    e = pl.program_id(1)
    nch = tt // LANES

    @pl.when(e == 0)
    def _prologue():
        h = _rms(x_ref[...], g_ref[...])
        ht_ref[...] = h.T.astype(BF)
        qt = _dot(wqt_ref[...], ht_ref[...]).astype(BF)
        for hh in range(PEER_HEADS):
            for c in range(2):
                r0 = hh * N_KEYS + c * 64
                sc = _dot(sk_ref[c], qt[r0:r0 + 64, :])
                for tc in range(nch):
                    blk = sc[:, tc * LANES:(tc + 1) * LANES]
                    st_ref.at[tc][pl.ds(c * 1024 + hh, N_KEYS, stride=PEER_HEADS), :] = blk
                    if c == 1:
                        st_ref[tc, 2048 + hh * N_KEYS:2048 + (hh + 1) * N_KEYS, :] = blk
        acc_ref[...] = jnp.zeros(acc_ref.shape, F32)

        def top_vals(load, n_rows, n_out):
            top = []
            for r in range(n_rows):
                v = load(r)
                for q in range(len(top)):
                    top[q], v = jnp.maximum(top[q], v), jnp.minimum(top[q], v)
                if len(top) < n_out:
                    top.append(v)
            return top

        def token_chunk(tc, carry):
            ls = pl.ds(pl.multiple_of(tc * LANES, LANES), LANES)
            v1 = top_vals(lambda r: st_ref[tc, r * 8:(r + 1) * 8, :], N_KEYS, N_CAND)
            v2 = top_vals(lambda r: st_ref[tc, 1024 + r * 8:1024 + (r + 1) * 8, :], N_KEYS, N_CAND)
            cands = [v1[a] + v2[b] for a in range(N_CAND) for b in range(N_CAND)
                     if (a + 1) * (b + 1) <= N_CAND]
            top = top_vals(lambda r: cands[r], len(cands), N_CAND)
            z = jnp.ones_like(top[0])
            for r in range(1, PEER_TOP):
                z = z + jnp.exp(top[r] - top[0])
            tau = 0.5 * (top[PEER_TOP - 1] + top[PEER_TOP])
            iz = 1.0 / z
            for r in range(N_KEYS):
                s1 = st_ref[tc, r * 8:(r + 1) * 8, :]
                th_ref[r, :, ls] = tau - s1
                cw_ref[r, :, ls] = jnp.exp(s1 - v1[0]) * (0.5 * iz)
            for hh in range(PEER_HEADS):
                rows = slice(2048 + hh * 128, 2048 + (hh + 1) * 128)
                e2_ref[hh * 128:(hh + 1) * 128, ls] = jnp.exp(st_ref[tc, rows, :] - v2[0][hh:hh + 1, :])
            return carry

        lax.fori_loop(0, nch, token_chunk, 0)

    nblk = PEER_EB // 256

    def pre_act(b):
        return _dot(u_ref[b * 256:(b + 1) * 256, :], ht_ref[...])

    def gate(b, a):
        i0 = e * (PEER_EB // N_KEYS) + 2 * b
        th = [th_ref[i0], th_ref[i0 + 1]]
        cw = [cw_ref[i0], cw_ref[i0 + 1]]
        for tc in range(nch):
            ls = slice(tc * LANES, (tc + 1) * LANES)
            for sub in range(2):
                w = [None, None]
                for hh in range(PEER_HEADS):
                    r0 = hh * N_KEYS + sub * 64
                    s2 = st_ref[tc, 2048 + r0:2048 + r0 + 64, :]
                    e2 = e2_ref[r0:r0 + 64, ls]
                    for half in range(2):
                        sel = jnp.where(s2 >= th[half][hh:hh + 1, ls], e2, 0.0) * cw[half][hh:hh + 1, ls]
                        w[half] = sel if w[half] is None else w[half] + sel
                for half in range(2):
                    r1 = half * N_KEYS + sub * 64
                    x = a[r1:r1 + 64, ls]
                    xw = x * w[half]
                    t = jnp.tanh(x * (GELU_C1 + GELU_C2 * (x * x)))
                    hw_ref[b % 2, r1:r1 + 64, ls] = (xw + xw * t).astype(BF)

    a = pre_act(0)
    for b in range(nblk):
        a_next = pre_act(b + 1) if b + 1 < nblk else None
        gate(b, a)
        if b > 0:
            acc_ref[...] += _dot(vt_ref[b - 1], hw_ref[(b - 1) % 2])
        a = a_next
    acc_ref[...] += _dot(vt_ref[nblk - 1], hw_ref[(nblk - 1) % 2])

    @pl.when(e == pl.num_programs(1) - 1)
    def _finish():
        y = x_ref[...] + acc_ref[...].T
        if final:
            y = _rms(y, gf_ref[...])
        o_ref[...] = y


def _peer(x, g, wqt, sk, u, vt, gf, *, tt, final):
    T, D = x.shape
    ne = u.shape[0] // PEER_EB
    const = lambda shape: pl.BlockSpec(shape, lambda i, e: (0,) * len(shape),
                                       pipeline_mode=pl.Buffered(1))
    return pl.pallas_call(
        functools.partial(_peer_body, tt=tt, final=final),
        grid=(T // tt, ne),
        in_specs=[pl.BlockSpec((tt, D), lambda i, e: (i, 0)),
                  const((1, D)), const((D, D)), const((2, N_KEYS, 64)),
                  pl.BlockSpec((PEER_EB, D), lambda i, e: (e, 0)),
                  pl.BlockSpec((PEER_EB // 256, D, 256), lambda i, e: (e, 0, 0)),
                  const((1, D))],
        out_specs=pl.BlockSpec((tt, D), lambda i, e: (i, 0)),
        out_shape=jax.ShapeDtypeStruct((T, D), F32),
        scratch_shapes=[pltpu.VMEM((D, tt), BF),
                        pltpu.VMEM((tt // LANES, 3 * 1024, LANES), F32),
                        pltpu.VMEM((1024, tt), F32),
                        pltpu.VMEM((N_KEYS, PEER_HEADS, tt), F32),
                        pltpu.VMEM((N_KEYS, PEER_HEADS, tt), F32),
                        pltpu.VMEM((2, 256, tt), BF),
                        pltpu.VMEM((D, tt), F32)],
        compiler_params=pltpu.CompilerParams(
            dimension_semantics=("arbitrary", "arbitrary"), vmem_limit_bytes=VMEM_LIMIT),
        name="peer",
    )(x, g, wqt, sk, u, vt, gf)


def _pack_in_weights(w_in):
    aq, ak, av, cq, ckv, kpe, zc, dq, dk, dv = jnp.split(
        w_in, np.cumsum([256, 256, 256, 256, 128, 32, 512, 256, 256])[:].tolist(), axis=1)
    kpe_rep = jnp.zeros((D_MODEL, 4, 128), w_in.dtype).at[:, :, 64:96].set(kpe[:, None, :])
    return jnp.concatenate(
        [aq, ak, av, cq, ckv, kpe_rep.reshape(D_MODEL, 512), zc, dq, dk, dv], axis=1).astype(BF)


def _pack_mla_weights(w_uq, w_ukv):
    uq = jnp.pad(w_uq.reshape(256, 4, 96), ((0, 0), (0, 0), (0, 32))).reshape(256, 512)
    ukv = w_ukv.reshape(128, 4, 128)
    uk = jnp.pad(ukv[:, :, :64], ((0, 0), (0, 0), (0, 64))).reshape(128, 512)
    uv = ukv[:, :, 64:].reshape(128, 256)
    return uq.astype(BF), uk.astype(BF), uv.astype(BF)


def _pack_expert_out(v):
    n, d = v.shape
    return v.reshape(n // 256, 256, d).transpose(0, 2, 1).astype(BF)


def kernel(x, w_in, w_out, norm_mix, norm_ffn, diff_lambda, diff_head_norm, mla_q_norm, mla_kv_norm, mla_w_uq, mla_w_ukv, mla_out_norm, sgu_v_norm, sgu_w, sgu_b, sgu_out_norm, dil_out_norm, peer_w_q, peer_sub_keys, peer_u, peer_v, final_norm):
    B, S, D = x.shape
    depth = w_in.shape[0]
    T = B * S
    tm = min(512, S)
    ta = min(ATT_TILE, S)
    tt = min(512, T)
    row = lambda a: a.reshape(1, -1)

    tabs = (_rope_tables(S, 8, 32, 0) + _rope_tables(S, 32, 128, 64) + _rope_tables(S, 16, 64, 0))

    for l in range(depth):
        lam_init = 0.8 - 0.6 * math.exp(-0.3 * l)
        w_cat = _pack_in_weights(w_in[l])
        wuq, wuk, wuv = _pack_mla_weights(mla_w_uq[l], mla_w_ukv[l])
        sgub = jnp.repeat(sgu_b[l].T, 64, axis=1)
        qa, ka, va, qb, kb, vb, oc, qd, kd, vd = _inproj(
            x, row(norm_mix[l]), w_cat, tabs, row(mla_q_norm[l]), row(mla_kv_norm[l]),
            wuq, wuk, wuv, row(sgu_v_norm[l]), sgu_w[l], sgub, row(sgu_out_norm[l]), tm=tm)

        oa = _attn_a(diff_lambda[l], row(diff_head_norm[l]), qa, ka, va, t=ta, lam_init=lam_init)
        ob = _attn_b(row(mla_out_norm[l]), qb, kb, vb, t=ta)

        od = _dil(row(dil_out_norm[l]), qd, kd, vd, td=DIL_TILE)

        x2 = _outproj(x.reshape(T, D), oa.reshape(T, GROUP), ob.reshape(T, GROUP),
                      oc.reshape(T, GROUP), od.reshape(T, GROUP),
                      w_out[l].astype(BF), tm=min(512, T))

        x2 = _peer(x2, row(norm_ffn[l]), peer_w_q[l].T.astype(BF), peer_sub_keys[l].astype(BF),
                   peer_u[l].astype(BF), _pack_expert_out(peer_v[l]), row(final_norm),
                   tt=tt, final=(l == depth - 1))
        x = x2.reshape(B, S, D)
    return x
```

```python
import functools
import math

import numpy as np
import jax
import jax.numpy as jnp
from jax import lax
from jax.experimental import pallas as pl
from jax.experimental.pallas import tpu as pltpu

D_MODEL = 1024
GROUP = 256
ROPE_THETA = 500000.0
NEG = -1e30
EPS = 1e-6
LANES = 128
LOG2E = math.log2(math.e)

N_KEYS = 128
PEER_HEADS = 8
PEER_TOP = 16

BF = jnp.bfloat16
F32 = jnp.float32

C_AQ, C_AK, C_AV, C_CQ, C_CKV, C_KPE, C_ZC, C_DQ, C_DK, C_DV, C_END = (
    0, 256, 512, 768, 1024, 1152, 1664, 2176, 2432, 2688, 2944)

VMEM_LIMIT = 56 * 1024 * 1024


def _rms(x, g):
    return x * lax.rsqrt(jnp.mean(x * x, axis=-1, keepdims=True) + EPS) * g


def _nt_dot(a, b):
    return lax.dot_general(a, b, (((1,), (1,)), ((), ())), preferred_element_type=F32)


def _dot(a, b):
    return jnp.dot(a, b, preferred_element_type=F32)


def _rope_tables(seq, n_rot, period, offset):
    half = n_rot // 2
    pos = jnp.arange(seq, dtype=F32)
    inv = ROPE_THETA ** (-jnp.arange(half, dtype=F32) * (2.0 / n_rot))
    ang = pos[:, None] * inv[None, :]
    cos, sin = jnp.cos(ang), jnp.sin(ang)
    g = np.arange(LANES) % period - offset
    rot = (g >= 0) & (g < n_rot)
    idx = np.where(rot, g % half, 0)
    sign = np.where(g < half, -1.0, 1.0).astype(np.float32)
    cos_t = jnp.where(rot[None, :], cos[:, idx], 1.0)
    sin_t = jnp.where(rot[None, :], sin[:, idx] * sign[None, :], 0.0)
    return cos_t.astype(F32), sin_t.astype(F32)


def _rope_apply(x, cos_t, sin_t, n_rot, period, offset):
    half = n_rot // 2
    lane = lax.broadcasted_iota(jnp.int32, (1, LANES), 1)
    first = (lane % period - offset) < half
    outs = []
    for c in range(x.shape[1] // LANES):
        xc = x[:, c * LANES:(c + 1) * LANES]
        fwd = pltpu.roll(xc, LANES - half, 1)
        bwd = pltpu.roll(xc, half, 1)
        outs.append(xc * cos_t + jnp.where(first, fwd, bwd) * sin_t)
    return outs[0] if len(outs) == 1 else jnp.concatenate(outs, axis=1)


VROWS = 80


def _value_rows(vt):
    one = (lax.broadcasted_iota(jnp.int32, (VROWS - 64, vt.shape[1]), 0) == 0).astype(F32)
    parts = []
    for h in range(4):
        parts += [vt[64 * h:64 * (h + 1)], one]
    return jnp.concatenate(parts, axis=0).astype(BF)


def _inproj_body(x_ref, gmix_ref, w_ref, ca_ref, sa_ref, cb_ref, sb_ref, cd_ref, sd_ref,
                 gq_ref, gkv_ref, wuq_ref, wuk_ref, wuv_ref,
                 gsv_ref, sguw_ref, sgub_ref, gso_ref,
                 qa_ref, ka_ref, va_ref, qb_ref, kb_ref, vb_ref, oc_ref,
                 qd_ref, kd_ref, vd_ref, *, tm):
    x = x_ref[0]
    h = _rms(x, gmix_ref[...]).astype(BF)

    def proj(a, b):
        return _dot(h, w_ref[:, a:b])

    ca, sa = ca_ref[...], sa_ref[...]
    aq = _rope_apply(proj(C_AQ, C_AK), ca, sa, 8, 32, 0)
    qa_ref[0] = (aq * (32.0 ** -0.5 * LOG2E)).T.astype(BF)
    ka_ref[0] = _rope_apply(proj(C_AK, C_AV), ca, sa, 8, 32, 0).astype(BF)
    va_ref[0] = _value_rows(proj(C_AV, C_CQ).T)

    cb, sb = cb_ref[...], sb_ref[...]
    cq = _rms(proj(C_CQ, C_CKV), gq_ref[...]).astype(BF)
    qb = _rope_apply(_dot(cq, wuq_ref[...]), cb, sb, 32, 128, 64)
    qb_ref[0] = (qb * (96.0 ** -0.5 * LOG2E)).T.astype(BF)
    ckv = _rms(proj(C_CKV, C_KPE), gkv_ref[...]).astype(BF)
    kpe = _rope_apply(proj(C_KPE, C_ZC), cb, sb, 32, 128, 64)
    kb_ref[0] = (_dot(ckv, wuk_ref[...]) + kpe).astype(BF)
    vb_ref[0] = _value_rows(_dot(ckv, wuv_ref[...]).T)

    zc = jax.nn.gelu(proj(C_ZC, C_DQ))
    u = zc[:, :GROUP]
    vn = _rms(zc[:, GROUP:], gsv_ref[...]).astype(BF)
    r = lax.broadcasted_iota(jnp.int32, (128, 128), 0)
    c = lax.broadcasted_iota(jnp.int32, (128, 128), 1)
    wcat = jnp.concatenate(
        [jnp.where(r >= c, sguw_ref[g], 0.0).astype(BF) for g in range(4)], axis=1)
    lane = lax.broadcasted_iota(jnp.int32, (1, GROUP), 1)
    bias = sgub_ref[...]
    gso = gso_ref[...]
    for ch in range(tm // 128):
        vc = vn[ch * 128:(ch + 1) * 128]
        vst = jnp.concatenate(
            [jnp.where(lane // 64 == g, vc, jnp.zeros_like(vc)) for g in range(4)], axis=0)
        sv = _dot(wcat, vst) + bias
        oc = u[ch * 128:(ch + 1) * 128] * sv
        oc_ref[0, ch * 128:(ch + 1) * 128, :] = _rms(oc, gso).astype(BF)

    cd, sd = cd_ref[...], sd_ref[...]
    dq = _rope_apply(proj(C_DQ, C_DK), cd, sd, 16, 64, 0)
    dk = _rope_apply(proj(C_DK, C_DV), cd, sd, 16, 64, 0)
    dv = proj(C_DV, C_END)
    for hf in range(2):
        qd_ref[0, hf] = dq[:, hf * LANES:(hf + 1) * LANES] * 0.125
        kd_ref[0, hf] = dk[:, hf * LANES:(hf + 1) * LANES]
        vd_ref[0, hf] = dv[:, hf * LANES:(hf + 1) * LANES]


def _inproj(x, gmix, w_cat, tabs, gq, gkv, wuq, wuk, wuv, gsv, sguw, sgub, gso, *, tm):
    B, S, D = x.shape
    ns = S // tm
    full = lambda shape: pl.BlockSpec(shape, lambda s, b: (0,) * len(shape))
    tab = pl.BlockSpec((tm, LANES), lambda s, b: (s, 0))
    seq = lambda w: pl.BlockSpec((1, tm, w), lambda s, b: (b, s, 0))
    seq_t = lambda w: pl.BlockSpec((1, w, tm), lambda s, b: (b, 0, s))
    halves = pl.BlockSpec((1, 2, tm, LANES), lambda s, b: (b, 0, s, 0))
    out_w = [256, 256, 4 * VROWS, 512, 512, 4 * VROWS, 256]
    transposed = [True, False, True, True, False, True, False]
    return pl.pallas_call(
        functools.partial(_inproj_body, tm=tm),
        grid=(ns, B),
        in_specs=[seq(D), full((1, D)), full((D, C_END))] + [tab] * 6 + [
            full((1, 256)), full((1, 128)), full((256, 512)), full((128, 512)), full((128, 256)),
            full((1, 256)), full((4, 128, 128)), full((128, 256)), full((1, 256))],
        out_specs=[seq_t(w) if tr else seq(w) for w, tr in zip(out_w, transposed)] + [halves] * 3,
        out_shape=[jax.ShapeDtypeStruct((B, w, S) if tr else (B, S, w), BF)
                   for w, tr in zip(out_w, transposed)]
        + [jax.ShapeDtypeStruct((B, 2, S, LANES), F32)] * 3,
        compiler_params=pltpu.CompilerParams(
            dimension_semantics=("arbitrary", "arbitrary"), vmem_limit_bytes=VMEM_LIMIT),
        name="inproj",
    )(x, gmix, w_cat, *tabs, gq, gkv, wuq, wuk, wuv, gsv, sguw, sgub, gso)


def _head_expand(cols, lane, width):
    out = cols[-1]
    for h in range(len(cols) - 2, -1, -1):
        out = jnp.where(lane // width == h, cols[h], out)
    return out


def _vstack(v, lane):
    return jnp.concatenate(
        [jnp.where(lane // 64 == h, v, jnp.zeros_like(v)) for h in range(4)], axis=0)


DIFF_TILE = 512
MLA_TILE = 1024
QB = 512


def _online_softmax_t(st, m_ref, j, cs):
    m_prev = m_ref[j:j + 1, cs]
    m_new = jnp.maximum(m_prev, jnp.max(st, axis=0, keepdims=True))
    alpha = jnp.exp2(m_prev - m_new)
    et = jnp.exp2(st - m_new)
    m_ref[j:j + 1, cs] = m_new
    return et.astype(BF), alpha


def _attn_jobs(ns, diag, heads):
    return [(*hd, qc, kc) for kc in range(ns) for qc in range(ns) for hd in heads
            if not (diag and kc > qc)]


def _causal_keep():
    return (lax.broadcasted_iota(jnp.int32, (QB, QB), 0)
            <= lax.broadcasted_iota(jnp.int32, (QB, QB), 1))


def _causal_pairs(n):
    pairs = [(i, j) for i in range(n) for j in range(i + 1)]
    return (jnp.asarray([p[0] for p in pairs], jnp.int32),
            jnp.asarray([p[1] for p in pairs], jnp.int32))


def _attn_a_body(qtab_ref, ktab_ref, lam_ref, gh_ref, q_ref, k_ref, v_ref, o_ref,
                 qs_ref, m_ref, acc_ref, *, t, lam_init):
    qi = qtab_ref[pl.program_id(1)]
    ki = ktab_ref[pl.program_id(1)]

    @pl.when(ki == 0)
    def _init():
        m_ref[...] = jnp.full(m_ref.shape, NEG, F32)
        acc_ref[...] = jnp.zeros(acc_ref.shape, F32)
        qt = q_ref[0]
        row = lax.broadcasted_iota(jnp.int32, (GROUP, 1), 0)
        for j in range(8):
            qs_ref[j] = jnp.where(row // 32 == j, qt, jnp.zeros_like(qt))

    def step(diag):
        jobs = _attn_jobs(t // QB, diag, [(mp, h) for mp in range(2) for h in range(4)])

        def qk(mp, h, qc, kc):
            return _dot(k_ref[0, kc * QB:(kc + 1) * QB, :],
                        qs_ref[2 * h + mp, :, qc * QB:(qc + 1) * QB])

        def pv(mp, h, qc, kc, pt, alpha):
            hs, cs = slice(VROWS * h, VROWS * (h + 1)), slice(qc * QB, (qc + 1) * QB)
            acc_ref[mp, hs, cs] = (acc_ref[mp, hs, cs] * alpha
                                   + _dot(v_ref[0, hs, kc * QB:(kc + 1) * QB], pt))

        st_next = qk(*jobs[0])
        pending = None
        for n, (mp, h, qc, kc) in enumerate(jobs):
            st = st_next
            if n + 1 < len(jobs):
                st_next = qk(*jobs[n + 1])
            if diag and kc == qc:
                st = jnp.where(_causal_keep(), st, NEG)
            pt, alpha = _online_softmax_t(st, m_ref, 2 * h + mp, slice(qc * QB, (qc + 1) * QB))
            if pending is not None:
                pv(*pending)
            pending = (mp, h, qc, kc, pt, alpha)
        pv(*pending)

    @pl.when(ki < qi)
    def _off():
        step(False)

    @pl.when(ki == qi)
    def _diag():
        step(True)
        lf = lam_ref[...]
        lam = (jnp.exp(jnp.sum(lf[0:1] * lf[1:2], axis=-1, keepdims=True))
               - jnp.exp(jnp.sum(lf[2:3] * lf[3:4], axis=-1, keepdims=True)) + lam_init)
        rows = []
        for h in range(4):
            hs, ls = slice(VROWS * h, VROWS * h + 64), slice(VROWS * h + 64, VROWS * h + 65)
            oh = (acc_ref[0, hs, :] * (1.0 / acc_ref[0, ls, :])
                  - lam * (acc_ref[1, hs, :] * (1.0 / acc_ref[1, ls, :])))
            ms = jnp.mean(oh * oh, axis=0, keepdims=True)
            rows.append(oh * lax.rsqrt(ms + EPS))
        ot = jnp.concatenate(rows, axis=0)
        o_ref[0] = (ot.T * gh_ref[...] * (1.0 - lam_init)).astype(BF)


def _attn_a(lam_p, gh, q, k, v, *, t, lam_init):
    B, S, _ = k.shape
    n = S // t
    qtab, ktab = _causal_pairs(n)
    grid_spec = pltpu.PrefetchScalarGridSpec(
        num_scalar_prefetch=2,
        grid=(B, qtab.shape[0]),
        in_specs=[pl.BlockSpec((4, 32), lambda b, p, qt, kt: (0, 0)),
                  pl.BlockSpec((1, GROUP), lambda b, p, qt, kt: (0, 0)),
                  pl.BlockSpec((1, GROUP, t), lambda b, p, qt, kt: (b, 0, qt[p])),
                  pl.BlockSpec((1, t, GROUP), lambda b, p, qt, kt: (b, kt[p], 0)),
                  pl.BlockSpec((1, 4 * VROWS, t), lambda b, p, qt, kt: (b, 0, kt[p]))],
        out_specs=pl.BlockSpec((1, t, GROUP), lambda b, p, qt, kt: (b, qt[p], 0)),
        scratch_shapes=[pltpu.VMEM((8, GROUP, t), BF),
                        pltpu.VMEM((8, t), F32),
                        pltpu.VMEM((2, 4 * VROWS, t), F32)])
    return pl.pallas_call(
        functools.partial(_attn_a_body, t=t, lam_init=lam_init),
        grid_spec=grid_spec,
        out_shape=jax.ShapeDtypeStruct((B, S, GROUP), BF),
        compiler_params=pltpu.CompilerParams(
            dimension_semantics=("arbitrary", "arbitrary"), vmem_limit_bytes=VMEM_LIMIT),
        name="attn_diff",
    )(qtab, ktab, lam_p, gh, q, k, v)


def _attn_b_body(qtab_ref, ktab_ref, g_ref, q_ref, k_ref, v_ref, o_ref, m_ref, acc_ref, *, t):
    qi = qtab_ref[pl.program_id(1)]
    ki = ktab_ref[pl.program_id(1)]

    @pl.when(ki == 0)
    def _init():
        m_ref[...] = jnp.full(m_ref.shape, NEG, F32)
        acc_ref[...] = jnp.zeros(acc_ref.shape, F32)

    def step(diag):
        jobs = _attn_jobs(t // QB, diag, [(h,) for h in range(4)])

        def qk(h, qc, kc):
            return _dot(k_ref[0, kc * QB:(kc + 1) * QB, h * 128:(h + 1) * 128],
                        q_ref[0, h * 128:(h + 1) * 128, qc * QB:(qc + 1) * QB])

        def pv(h, qc, kc, pt, alpha):
            hs, cs = slice(VROWS * h, VROWS * (h + 1)), slice(qc * QB, (qc + 1) * QB)
            acc_ref[hs, cs] = acc_ref[hs, cs] * alpha + _dot(v_ref[0, hs, kc * QB:(kc + 1) * QB], pt)

        st_next = qk(*jobs[0])
        pending = None
        for n, (h, qc, kc) in enumerate(jobs):
            st = st_next
            if n + 1 < len(jobs):
                st_next = qk(*jobs[n + 1])
            if diag and kc == qc:
                st = jnp.where(_causal_keep(), st, NEG)
            pt, alpha = _online_softmax_t(st, m_ref, h, slice(qc * QB, (qc + 1) * QB))
            if pending is not None:
                pv(*pending)
            pending = (h, qc, kc, pt, alpha)
        pv(*pending)

    @pl.when(ki < qi)
    def _off():
        step(False)

    @pl.when(ki == qi)
    def _diag():
        step(True)
        rows = [acc_ref[VROWS * h:VROWS * h + 64, :] * (1.0 / acc_ref[VROWS * h + 64:VROWS * h + 65, :])
                for h in range(4)]
        ot = jnp.concatenate(rows, axis=0)
        o_ref[0] = _rms(ot.T, g_ref[...]).astype(BF)


def _attn_b(g, q, k, v, *, t):
    B, S, _ = k.shape
    n = S // t
    qtab, ktab = _causal_pairs(n)
    grid_spec = pltpu.PrefetchScalarGridSpec(
        num_scalar_prefetch=2,
        grid=(B, qtab.shape[0]),
        in_specs=[pl.BlockSpec((1, GROUP), lambda b, p, qt, kt: (0, 0)),
                  pl.BlockSpec((1, 512, t), lambda b, p, qt, kt: (b, 0, qt[p])),
                  pl.BlockSpec((1, t, 512), lambda b, p, qt, kt: (b, kt[p], 0)),
                  pl.BlockSpec((1, 4 * VROWS, t), lambda b, p, qt, kt: (b, 0, kt[p]))],
        out_specs=pl.BlockSpec((1, t, GROUP), lambda b, p, qt, kt: (b, qt[p], 0)),
        scratch_shapes=[pltpu.VMEM((4, t), F32),
                        pltpu.VMEM((4 * VROWS, t), F32)])
    return pl.pallas_call(
        functools.partial(_attn_b_body, t=t),
        grid_spec=grid_spec,
        out_shape=jax.ShapeDtypeStruct((B, S, GROUP), BF),
        compiler_params=pltpu.CompilerParams(
            dimension_semantics=("arbitrary", "arbitrary"), vmem_limit_bytes=VMEM_LIMIT),
        name="attn_mla",
    )(qtab, ktab, g, q, k, v)


DIL_TILE = 2048
DILATIONS = (1, 4, 16)


def _dil_body(g_ref, q_ref, kp_ref, kc_ref, vp_ref, vc_ref, o_ref, ob_ref, lb_ref, *, td):
    i = pl.program_id(1)
    lane = lax.broadcasted_iota(jnp.int32, (1, GROUP), 1)
    a = lax.broadcasted_iota(jnp.int32, (128, 256), 0)
    c = lax.broadcasted_iota(jnp.int32, (128, 256), 1)
    dist = jnp.where(c - a >= 0, c - a, 1000)

    def rows(ref, start, d):
        idx = pl.ds(start, 128) if d == 1 else pl.ds(start, 128, stride=d)
        return jnp.concatenate([ref.at[0, 0][idx, :], ref.at[0, 1][idx, :]], axis=1).astype(BF)

    for bi, d in enumerate(DILATIONS):
        def block(n, carry, bi=bi, d=d):
            if d == 1:
                r, blk = 0, n
            elif td // d == 128:
                r, blk = n, 0
            else:
                r, blk = n % d, n // d
            q0 = r + d * 128 * blk
            if d == 1:
                q0 = pl.multiple_of(q0, 128)
            q = rows(q_ref, q0, d)
            k_hi, v_hi = rows(kc_ref, q0, d), rows(vc_ref, q0, d)
            lo_prev = td - d * 128 + r
            if isinstance(blk, int):
                first = True
                k_lo, v_lo = rows(kp_ref, lo_prev, d), rows(vp_ref, lo_prev, d)
            else:
                first = blk == 0
                lo_cur = jnp.maximum(q0 - d * 128, 0)
                if d == 1:
                    lo_cur = pl.multiple_of(lo_cur, 128)
                k_lo = jnp.where(first, rows(kp_ref, lo_prev, d), rows(kc_ref, lo_cur, d))
                v_lo = jnp.where(first, rows(vp_ref, lo_prev, d), rows(vc_ref, lo_cur, d))
            kwin = jnp.concatenate([k_lo, k_hi], axis=0)
            vst = _vstack(jnp.concatenate([v_lo, v_hi], axis=0), lane)
            cmin = jnp.where(jnp.logical_and(first, i == 0), 128, 0)
            ok = jnp.where(c >= cmin, dist, 1000) <= 128
            es, ils, lses = [], [], []
            for h in range(4):
                s = _nt_dot(jnp.where(lane // 64 == h, q, jnp.zeros_like(q)), kwin)
                s = jnp.where(ok, s, NEG)
                m = jnp.max(s, axis=-1, keepdims=True)
                ex = jnp.exp(s - m)
                den = jnp.sum(ex, axis=-1, keepdims=True)
                es.append(ex.astype(BF))
                ils.append(1.0 / den)
                lses.append(m + jnp.log(den))
            o = _dot(jnp.concatenate(es, axis=1), vst) * _head_expand(ils, lane, 64)
            lse = _head_expand(lses, lane, 64)
            idx = pl.ds(q0, 128) if d == 1 else pl.ds(q0, 128, stride=d)
            for hf in range(2):
                ob_ref.at[bi, hf][idx, :] = o[:, hf * LANES:(hf + 1) * LANES]
                lb_ref.at[bi, hf][idx, :] = lse[:, hf * LANES:(hf + 1) * LANES]
            return carry

        lax.fori_loop(0, td // 128, block, 0, unroll=4)

    for ch in range(td // 256):
        rs = slice(ch * 256, (ch + 1) * 256)
        halves = []
        for hf in range(2):
            ls = [lb_ref[bi, hf, rs, :] for bi in range(3)]
            mx = jnp.maximum(jnp.maximum(ls[0], ls[1]), ls[2])
            ws = [jnp.exp(l - mx) for l in ls]
            num = ws[0] * ob_ref[0, hf, rs, :] + ws[1] * ob_ref[1, hf, rs, :] + ws[2] * ob_ref[2, hf, rs, :]
            halves.append(num / (ws[0] + ws[1] + ws[2]))
        o_ref[0, rs, :] = _rms(jnp.concatenate(halves, axis=1), g_ref[...]).astype(BF)


def _dil(g, q, k, v, *, td):
    B, _, S, _ = q.shape
    cur = pl.BlockSpec((1, 2, td, LANES), lambda b, i: (b, 0, i, 0))
    prev = pl.BlockSpec((1, 2, td, LANES), lambda b, i: (b, 0, jnp.maximum(i - 1, 0), 0))
    return pl.pallas_call(
        functools.partial(_dil_body, td=td),
        grid=(B, S // td),
        in_specs=[pl.BlockSpec((1, GROUP), lambda b, i: (0, 0)), cur, prev, cur, prev, cur],
        out_specs=pl.BlockSpec((1, td, GROUP), lambda b, i: (b, i, 0)),
        out_shape=jax.ShapeDtypeStruct((B, S, GROUP), BF),
        scratch_shapes=[pltpu.VMEM((3, 2, td, LANES), F32),
                        pltpu.VMEM((3, 2, td, LANES), F32)],
        compiler_params=pltpu.CompilerParams(
            dimension_semantics=("arbitrary", "arbitrary"), vmem_limit_bytes=VMEM_LIMIT),
        name="dilated",
    )(g, q, k, k, v, v)


def _outproj_body(x_ref, oa_ref, ob_ref, oc_ref, od_ref, w_ref, y_ref):
    y = _dot(oa_ref[...], w_ref[0:256, :])
    y += _dot(ob_ref[...], w_ref[256:512, :])
    y += _dot(oc_ref[...], w_ref[512:768, :])
    y += _dot(od_ref[...], w_ref[768:1024, :])
    y_ref[...] = x_ref[...] + y


def _outproj(x, oa, ob, oc, od, w_out, *, tm):
    T, D = x.shape
    row = lambda w: pl.BlockSpec((tm, w), lambda i: (i, 0))
    full = lambda shape: pl.BlockSpec(shape, lambda i: (0,) * len(shape))
    return pl.pallas_call(
        _outproj_body,
        grid=(T // tm,),
        in_specs=[row(D)] + [row(GROUP)] * 4 + [full((D, D))],
        out_specs=row(D),
        out_shape=jax.ShapeDtypeStruct((T, D), F32),
        compiler_params=pltpu.CompilerParams(
            dimension_semantics=("arbitrary",), vmem_limit_bytes=VMEM_LIMIT),
        name="outproj",
    )(x, oa, ob, oc, od, w_out)


PEER_EB = 2048
GELU_C1 = math.sqrt(2.0 / math.pi)
GELU_C2 = 0.044715 * GELU_C1
N_CAND = PEER_TOP + 1


def _peer_body(x_ref, g_ref, wqt_ref, sk_ref, u_ref, vt_ref, gf_ref, o_ref,
               ht_ref, st_ref, e2_ref, th_ref, cw_ref, hw_ref, acc_ref,
               *, tt, final):
    e = pl.program_id(1)
    nch = tt // LANES

    @pl.when(e == 0)
    def _prologue():
        h = _rms(x_ref[...], g_ref[...])
        ht_ref[...] = h.T.astype(BF)
        qt = _dot(wqt_ref[...], ht_ref[...]).astype(BF)
        for hh in range(PEER_HEADS):
            for c in range(2):
                r0 = hh * N_KEYS + c * 64
                sc = _dot(sk_ref[c], qt[r0:r0 + 64, :])
                for tc in range(nch):
                    blk = sc[:, tc * LANES:(tc + 1) * LANES]
                    st_ref.at[tc][pl.ds(c * 1024 + hh, N_KEYS, stride=PEER_HEADS), :] = blk
                    if c == 1:
                        st_ref[tc, 2048 + hh * N_KEYS:2048 + (hh + 1) * N_KEYS, :] = blk
        acc_ref[...] = jnp.zeros(acc_ref.shape, F32)

        def top_vals(load, n_rows, n_out):
            top = []
            for r in range(n_rows):
                v = load(r)
                for q in range(len(top)):
                    top[q], v = jnp.maximum(top[q], v), jnp.minimum(top[q], v)
                if len(top) < n_out:
                    top.append(v)
            return top

        def token_chunk(tc, carry):
            ls = pl.ds(pl.multiple_of(tc * LANES, LANES), LANES)
            v1 = top_vals(lambda r: st_ref[tc, r * 8:(r + 1) * 8, :], N_KEYS, N_CAND)
            v2 = top_vals(lambda r: st_ref[tc, 1024 + r * 8:1024 + (r + 1) * 8, :], N_KEYS, N_CAND)
            cands = [v1[a] + v2[b] for a in range(N_CAND) for b in range(N_CAND)
                     if (a + 1) * (b + 1) <= N_CAND]
            top = top_vals(lambda r: cands[r], len(cands), N_CAND)
            z = jnp.ones_like(top[0])
            for r in range(1, PEER_TOP):
                z = z + jnp.exp(top[r] - top[0])
            tau = 0.5 * (top[PEER_TOP - 1] + top[PEER_TOP])
            iz = 1.0 / z
            for r in range(N_KEYS):
                s1 = st_ref[tc, r * 8:(r + 1) * 8, :]
                th_ref[r, :, ls] = tau - s1
                cw_ref[r, :, ls] = jnp.exp(s1 - v1[0]) * (0.5 * iz)
            for hh in range(PEER_HEADS):
                rows = slice(2048 + hh * 128, 2048 + (hh + 1) * 128)
                e2_ref[hh * 128:(hh + 1) * 128, ls] = jnp.exp(st_ref[tc, rows, :] - v2[0][hh:hh + 1, :])
            return carry

        lax.fori_loop(0, nch, token_chunk, 0)

    nblk = PEER_EB // 256

    def pre_act(b):
        return _dot(u_ref[b * 256:(b + 1) * 256, :], ht_ref[...])

    def gate(b, a):
        i0 = e * (PEER_EB // N_KEYS) + 2 * b
        th = [th_ref[i0], th_ref[i0 + 1]]
        cw = [cw_ref[i0], cw_ref[i0 + 1]]
        for tc in range(nch):
            ls = slice(tc * LANES, (tc + 1) * LANES)
            for sub in range(2):
                w = [None, None]
                for hh in range(PEER_HEADS):
                    r0 = hh * N_KEYS + sub * 64
                    s2 = st_ref[tc, 2048 + r0:2048 + r0 + 64, :]
                    e2 = e2_ref[r0:r0 + 64, ls]
                    for half in range(2):
                        sel = jnp.where(s2 >= th[half][hh:hh + 1, ls], e2, 0.0) * cw[half][hh:hh + 1, ls]
                        w[half] = sel if w[half] is None else w[half] + sel
                for half in range(2):
                    r1 = half * N_KEYS + sub * 64
                    x = a[r1:r1 + 64, ls]
                    xw = x * w[half]
                    t = jnp.tanh(x * (GELU_C1 + GELU_C2 * (x * x)))
                    hw_ref[b % 2, r1:r1 + 64, ls] = (xw + xw * t).astype(BF)

    a = pre_act(0)
    for b in range(nblk):
        a_next = pre_act(b + 1) if b + 1 < nblk else None
        gate(b, a)
        if b > 0:
            acc_ref[...] += _dot(vt_ref[b - 1], hw_ref[(b - 1) % 2])
        a = a_next
    acc_ref[...] += _dot(vt_ref[nblk - 1], hw_ref[(nblk - 1) % 2])

    @pl.when(e == pl.num_programs(1) - 1)
    def _finish():
        y = x_ref[...] + acc_ref[...].T
        if final:
            y = _rms(y, gf_ref[...])
        o_ref[...] = y


def _peer(x, g, wqt, sk, u, vt, gf, *, tt, final):
    T, D = x.shape
    ne = u.shape[0] // PEER_EB
    const = lambda shape: pl.BlockSpec(shape, lambda i, e: (0,) * len(shape),
                                       pipeline_mode=pl.Buffered(1))
    return pl.pallas_call(
        functools.partial(_peer_body, tt=tt, final=final),
        grid=(T // tt, ne),
        in_specs=[pl.BlockSpec((tt, D), lambda i, e: (i, 0)),
                  const((1, D)), const((D, D)), const((2, N_KEYS, 64)),
                  pl.BlockSpec((PEER_EB, D), lambda i, e: (e, 0)),
                  pl.BlockSpec((PEER_EB // 256, D, 256), lambda i, e: (e, 0, 0)),
                  const((1, D))],
        out_specs=pl.BlockSpec((tt, D), lambda i, e: (i, 0)),
        out_shape=jax.ShapeDtypeStruct((T, D), F32),
        scratch_shapes=[pltpu.VMEM((D, tt), BF),
                        pltpu.VMEM((tt // LANES, 3 * 1024, LANES), F32),
                        pltpu.VMEM((1024, tt), F32),
                        pltpu.VMEM((N_KEYS, PEER_HEADS, tt), F32),
                        pltpu.VMEM((N_KEYS, PEER_HEADS, tt), F32),
                        pltpu.VMEM((2, 256, tt), BF),
                        pltpu.VMEM((D, tt), F32)],
        compiler_params=pltpu.CompilerParams(
            dimension_semantics=("arbitrary", "arbitrary"), vmem_limit_bytes=VMEM_LIMIT),
        name="peer",
    )(x, g, wqt, sk, u, vt, gf)


def _pack_in_weights(w_in):
    aq, ak, av, cq, ckv, kpe, zc, dq, dk, dv = jnp.split(
        w_in, np.cumsum([256, 256, 256, 256, 128, 32, 512, 256, 256])[:].tolist(), axis=1)
    kpe_rep = jnp.zeros((D_MODEL, 4, 128), w_in.dtype).at[:, :, 64:96].set(kpe[:, None, :])
    return jnp.concatenate(
        [aq, ak, av, cq, ckv, kpe_rep.reshape(D_MODEL, 512), zc, dq, dk, dv], axis=1).astype(BF)


def _pack_mla_weights(w_uq, w_ukv):
    uq = jnp.pad(w_uq.reshape(256, 4, 96), ((0, 0), (0, 0), (0, 32))).reshape(256, 512)
    ukv = w_ukv.reshape(128, 4, 128)
    uk = jnp.pad(ukv[:, :, :64], ((0, 0), (0, 0), (0, 64))).reshape(128, 512)
    uv = ukv[:, :, 64:].reshape(128, 256)
    return uq.astype(BF), uk.astype(BF), uv.astype(BF)


def _pack_expert_out(v):
    n, d = v.shape
    return v.reshape(n // 256, 256, d).transpose(0, 2, 1).astype(BF)


def kernel(x, w_in, w_out, norm_mix, norm_ffn, diff_lambda, diff_head_norm, mla_q_norm, mla_kv_norm, mla_w_uq, mla_w_ukv, mla_out_norm, sgu_v_norm, sgu_w, sgu_b, sgu_out_norm, dil_out_norm, peer_w_q, peer_sub_keys, peer_u, peer_v, final_norm):
    B, S, D = x.shape
    depth = w_in.shape[0]
    T = B * S
    tm = min(512, S)
    ta, tb = min(DIFF_TILE, S), min(MLA_TILE, S)
    tt = min(512, T)
    row = lambda a: a.reshape(1, -1)

    tabs = (_rope_tables(S, 8, 32, 0) + _rope_tables(S, 32, 128, 64) + _rope_tables(S, 16, 64, 0))

    for l in range(depth):
        lam_init = 0.8 - 0.6 * math.exp(-0.3 * l)
        w_cat = _pack_in_weights(w_in[l])
        wuq, wuk, wuv = _pack_mla_weights(mla_w_uq[l], mla_w_ukv[l])
        sgub = jnp.repeat(sgu_b[l].T, 64, axis=1)
        qa, ka, va, qb, kb, vb, oc, qd, kd, vd = _inproj(
            x, row(norm_mix[l]), w_cat, tabs, row(mla_q_norm[l]), row(mla_kv_norm[l]),
            wuq, wuk, wuv, row(sgu_v_norm[l]), sgu_w[l], sgub, row(sgu_out_norm[l]), tm=tm)

        oa = _attn_a(diff_lambda[l], row(diff_head_norm[l]), qa, ka, va, t=ta, lam_init=lam_init)
        ob = _attn_b(row(mla_out_norm[l]), qb, kb, vb, t=tb)

        od = _dil(row(dil_out_norm[l]), qd, kd, vd, td=DIL_TILE)

        x2 = _outproj(x.reshape(T, D), oa.reshape(T, GROUP), ob.reshape(T, GROUP),
                      oc.reshape(T, GROUP), od.reshape(T, GROUP),
                      w_out[l].astype(BF), tm=min(512, T))

        x2 = _peer(x2, row(norm_ffn[l]), peer_w_q[l].T.astype(BF), peer_sub_keys[l].astype(BF),
                   peer_u[l].astype(BF), _pack_expert_out(peer_v[l]), row(final_norm),
                   tt=tt, final=(l == depth - 1))
        x = x2.reshape(B, S, D)
    return x
```

```python
import functools
import math

import numpy as np
import jax
import jax.numpy as jnp
from jax import lax
from jax.experimental import pallas as pl
from jax.experimental.pallas import tpu as pltpu

D_MODEL = 1024
GROUP = 256
ROPE_THETA = 500000.0
NEG = -1e30
EPS = 1e-6
LANES = 128
LOG2E = math.log2(math.e)

N_KEYS = 128
PEER_HEADS = 8
PEER_TOP = 16

BF = jnp.bfloat16
F32 = jnp.float32

C_AQ, C_AK, C_AV, C_CQ, C_CKV, C_KPE, C_ZC, C_DQ, C_DK, C_DV, C_END = (
    0, 256, 512, 768, 1024, 1152, 1664, 2176, 2432, 2688, 2944)

VMEM_LIMIT = 56 * 1024 * 1024


def _rms(x, g):
    return x * lax.rsqrt(jnp.mean(x * x, axis=-1, keepdims=True) + EPS) * g


def _nt_dot(a, b):
    return lax.dot_general(a, b, (((1,), (1,)), ((), ())), preferred_element_type=F32)


def _dot(a, b):
    return jnp.dot(a, b, preferred_element_type=F32)


def _rope_tables(seq, n_rot, period, offset):
    half = n_rot // 2
    pos = jnp.arange(seq, dtype=F32)
    inv = ROPE_THETA ** (-jnp.arange(half, dtype=F32) * (2.0 / n_rot))
    ang = pos[:, None] * inv[None, :]
    cos, sin = jnp.cos(ang), jnp.sin(ang)
    g = np.arange(LANES) % period - offset
    rot = (g >= 0) & (g < n_rot)
    idx = np.where(rot, g % half, 0)
    sign = np.where(g < half, -1.0, 1.0).astype(np.float32)
    cos_t = jnp.where(rot[None, :], cos[:, idx], 1.0)
    sin_t = jnp.where(rot[None, :], sin[:, idx] * sign[None, :], 0.0)
    return cos_t.astype(F32), sin_t.astype(F32)


def _rope_apply(x, cos_t, sin_t, n_rot, period, offset):
    half = n_rot // 2
    lane = lax.broadcasted_iota(jnp.int32, (1, LANES), 1)
    first = (lane % period - offset) < half
    outs = []
    for c in range(x.shape[1] // LANES):
        xc = x[:, c * LANES:(c + 1) * LANES]
        fwd = pltpu.roll(xc, LANES - half, 1)
        bwd = pltpu.roll(xc, half, 1)
        outs.append(xc * cos_t + jnp.where(first, fwd, bwd) * sin_t)
    return outs[0] if len(outs) == 1 else jnp.concatenate(outs, axis=1)


VROWS = 80


def _value_rows(vt):
    one = (lax.broadcasted_iota(jnp.int32, (VROWS - 64, vt.shape[1]), 0) == 0).astype(F32)
    parts = []
    for h in range(4):
        parts += [vt[64 * h:64 * (h + 1)], one]
    return jnp.concatenate(parts, axis=0).astype(BF)


def _inproj_body(x_ref, gmix_ref, w_ref, ca_ref, sa_ref, cb_ref, sb_ref, cd_ref, sd_ref,
                 gq_ref, gkv_ref, wuq_ref, wuk_ref, wuv_ref,
                 gsv_ref, sguw_ref, sgub_ref, gso_ref,
                 qa_ref, ka_ref, va_ref, qb_ref, kb_ref, vb_ref, oc_ref,
                 qd_ref, kd_ref, vd_ref, *, tm):
    x = x_ref[0]
    h = _rms(x, gmix_ref[...]).astype(BF)

    def proj(a, b):
        return _dot(h, w_ref[:, a:b])

    ca, sa = ca_ref[...], sa_ref[...]
    aq = _rope_apply(proj(C_AQ, C_AK), ca, sa, 8, 32, 0)
    qa_ref[0] = (aq * (32.0 ** -0.5 * LOG2E)).T.astype(BF)
    ka_ref[0] = _rope_apply(proj(C_AK, C_AV), ca, sa, 8, 32, 0).astype(BF)
    va_ref[0] = _value_rows(proj(C_AV, C_CQ).T)

    cb, sb = cb_ref[...], sb_ref[...]
    cq = _rms(proj(C_CQ, C_CKV), gq_ref[...]).astype(BF)
    qb = _rope_apply(_dot(cq, wuq_ref[...]), cb, sb, 32, 128, 64)
    qb_ref[0] = (qb * (96.0 ** -0.5 * LOG2E)).T.astype(BF)
    ckv = _rms(proj(C_CKV, C_KPE), gkv_ref[...]).astype(BF)
    kpe = _rope_apply(proj(C_KPE, C_ZC), cb, sb, 32, 128, 64)
    kb_ref[0] = (_dot(ckv, wuk_ref[...]) + kpe).astype(BF)
    vb_ref[0] = _value_rows(_dot(ckv, wuv_ref[...]).T)

    zc = jax.nn.gelu(proj(C_ZC, C_DQ))
    u = zc[:, :GROUP]
    vn = _rms(zc[:, GROUP:], gsv_ref[...]).astype(BF)
    r = lax.broadcasted_iota(jnp.int32, (128, 128), 0)
    c = lax.broadcasted_iota(jnp.int32, (128, 128), 1)
    wcat = jnp.concatenate(
        [jnp.where(r >= c, sguw_ref[g], 0.0).astype(BF) for g in range(4)], axis=1)
    lane = lax.broadcasted_iota(jnp.int32, (1, GROUP), 1)
    bias = sgub_ref[...]
    gso = gso_ref[...]
    for ch in range(tm // 128):
        vc = vn[ch * 128:(ch + 1) * 128]
        vst = jnp.concatenate(
            [jnp.where(lane // 64 == g, vc, jnp.zeros_like(vc)) for g in range(4)], axis=0)
        sv = _dot(wcat, vst) + bias
        oc = u[ch * 128:(ch + 1) * 128] * sv
        oc_ref[0, ch * 128:(ch + 1) * 128, :] = _rms(oc, gso).astype(BF)

    cd, sd = cd_ref[...], sd_ref[...]
    dq = _rope_apply(proj(C_DQ, C_DK), cd, sd, 16, 64, 0)
    dk = _rope_apply(proj(C_DK, C_DV), cd, sd, 16, 64, 0)
    dv = proj(C_DV, C_END)
    for hf in range(2):
        qd_ref[0, hf] = dq[:, hf * LANES:(hf + 1) * LANES] * 0.125
        kd_ref[0, hf] = dk[:, hf * LANES:(hf + 1) * LANES]
        vd_ref[0, hf] = dv[:, hf * LANES:(hf + 1) * LANES]


def _inproj(x, gmix, w_cat, tabs, gq, gkv, wuq, wuk, wuv, gsv, sguw, sgub, gso, *, tm):
    B, S, D = x.shape
    ns = S // tm
    full = lambda shape: pl.BlockSpec(shape, lambda s, b: (0,) * len(shape))
    tab = pl.BlockSpec((tm, LANES), lambda s, b: (s, 0))
    seq = lambda w: pl.BlockSpec((1, tm, w), lambda s, b: (b, s, 0))
    seq_t = lambda w: pl.BlockSpec((1, w, tm), lambda s, b: (b, 0, s))
    halves = pl.BlockSpec((1, 2, tm, LANES), lambda s, b: (b, 0, s, 0))
    out_w = [256, 256, 4 * VROWS, 512, 512, 4 * VROWS, 256]
    transposed = [True, False, True, True, False, True, False]
    return pl.pallas_call(
        functools.partial(_inproj_body, tm=tm),
        grid=(ns, B),
        in_specs=[seq(D), full((1, D)), full((D, C_END))] + [tab] * 6 + [
            full((1, 256)), full((1, 128)), full((256, 512)), full((128, 512)), full((128, 256)),
            full((1, 256)), full((4, 128, 128)), full((128, 256)), full((1, 256))],
        out_specs=[seq_t(w) if tr else seq(w) for w, tr in zip(out_w, transposed)] + [halves] * 3,
        out_shape=[jax.ShapeDtypeStruct((B, w, S) if tr else (B, S, w), BF)
                   for w, tr in zip(out_w, transposed)]
        + [jax.ShapeDtypeStruct((B, 2, S, LANES), F32)] * 3,
        compiler_params=pltpu.CompilerParams(
            dimension_semantics=("arbitrary", "arbitrary"), vmem_limit_bytes=VMEM_LIMIT),
        name="inproj",
    )(x, gmix, w_cat, *tabs, gq, gkv, wuq, wuk, wuv, gsv, sguw, sgub, gso)


def _head_expand(cols, lane, width):
    out = cols[-1]
    for h in range(len(cols) - 2, -1, -1):
        out = jnp.where(lane // width == h, cols[h], out)
    return out


def _vstack(v, lane):
    return jnp.concatenate(
        [jnp.where(lane // 64 == h, v, jnp.zeros_like(v)) for h in range(4)], axis=0)


DIFF_TILE = 1024
MLA_TILE = 1024
QB = 512


def _online_softmax_t(st, m_ref, j, cs):
    m_prev = m_ref[j:j + 1, cs]
    m_new = jnp.maximum(m_prev, jnp.max(st, axis=0, keepdims=True))
    alpha = jnp.exp2(m_prev - m_new)
    et = jnp.exp2(st - m_new)
    m_ref[j:j + 1, cs] = m_new
    return et.astype(BF), alpha


def _attn_jobs(ns, diag, heads):
    return [(*hd, qc, kc) for kc in range(ns) for qc in range(ns) for hd in heads
            if not (diag and kc > qc)]


def _causal_keep():
    return (lax.broadcasted_iota(jnp.int32, (QB, QB), 0)
            <= lax.broadcasted_iota(jnp.int32, (QB, QB), 1))


def _causal_pairs(n):
    pairs = [(i, j) for i in range(n) for j in range(i + 1)]
    return (jnp.asarray([p[0] for p in pairs], jnp.int32),
            jnp.asarray([p[1] for p in pairs], jnp.int32))


def _attn_a_body(qtab_ref, ktab_ref, lam_ref, gh_ref, q_ref, k_ref, v_ref, o_ref,
                 qs_ref, m_ref, acc_ref, *, t, lam_init):
    qi = qtab_ref[pl.program_id(1)]
    ki = ktab_ref[pl.program_id(1)]

    @pl.when(ki == 0)
    def _init():
        m_ref[...] = jnp.full(m_ref.shape, NEG, F32)
        acc_ref[...] = jnp.zeros(acc_ref.shape, F32)
        qt = q_ref[0]
        row = lax.broadcasted_iota(jnp.int32, (GROUP, 1), 0)
        for j in range(8):
            qs_ref[j] = jnp.where(row // 32 == j, qt, jnp.zeros_like(qt))

    def step(diag):
        jobs = _attn_jobs(t // QB, diag, [(mp, h) for mp in range(2) for h in range(4)])
        kblk = [k_ref[0, kc * QB:(kc + 1) * QB, :] for kc in range(t // QB)]
        keep = _causal_keep() if diag else None

        def qk(mp, h, qc, kc):
            return _dot(kblk[kc], qs_ref[2 * h + mp, :, qc * QB:(qc + 1) * QB])

        def pv(mp, h, qc, kc, pt, alpha):
            hs, cs = slice(VROWS * h, VROWS * (h + 1)), slice(qc * QB, (qc + 1) * QB)
            acc_ref[mp, hs, cs] = (acc_ref[mp, hs, cs] * alpha
                                   + _dot(v_ref[0, hs, kc * QB:(kc + 1) * QB], pt))

        st_next = qk(*jobs[0])
        pending = None
        for n, (mp, h, qc, kc) in enumerate(jobs):
            st = st_next
            if n + 1 < len(jobs):
                st_next = qk(*jobs[n + 1])
            if diag and kc == qc:
                st = jnp.where(keep, st, NEG)
            pt, alpha = _online_softmax_t(st, m_ref, 2 * h + mp, slice(qc * QB, (qc + 1) * QB))
            if pending is not None:
                pv(*pending)
            pending = (mp, h, qc, kc, pt, alpha)
        pv(*pending)

    @pl.when(ki < qi)
    def _off():
        step(False)

    @pl.when(ki == qi)
    def _diag():
        step(True)
        lf = lam_ref[...]
        lam = (jnp.exp(jnp.sum(lf[0:1] * lf[1:2], axis=-1, keepdims=True))
               - jnp.exp(jnp.sum(lf[2:3] * lf[3:4], axis=-1, keepdims=True)) + lam_init)
        rows = []
        for h in range(4):
            hs, ls = slice(VROWS * h, VROWS * h + 64), slice(VROWS * h + 64, VROWS * h + 65)
            oh = (acc_ref[0, hs, :] * (1.0 / acc_ref[0, ls, :])
                  - lam * (acc_ref[1, hs, :] * (1.0 / acc_ref[1, ls, :])))
            ms = jnp.mean(oh * oh, axis=0, keepdims=True)
            rows.append(oh * lax.rsqrt(ms + EPS))
        ot = jnp.concatenate(rows, axis=0)
        o_ref[0] = (ot.T * gh_ref[...] * (1.0 - lam_init)).astype(BF)


def _attn_a(lam_p, gh, q, k, v, *, t, lam_init):
    B, S, _ = k.shape
    n = S // t
    qtab, ktab = _causal_pairs(n)
    grid_spec = pltpu.PrefetchScalarGridSpec(
        num_scalar_prefetch=2,
        grid=(B, qtab.shape[0]),
        in_specs=[pl.BlockSpec((4, 32), lambda b, p, qt, kt: (0, 0)),
                  pl.BlockSpec((1, GROUP), lambda b, p, qt, kt: (0, 0)),
                  pl.BlockSpec((1, GROUP, t), lambda b, p, qt, kt: (b, 0, qt[p])),
                  pl.BlockSpec((1, t, GROUP), lambda b, p, qt, kt: (b, kt[p], 0)),
                  pl.BlockSpec((1, 4 * VROWS, t), lambda b, p, qt, kt: (b, 0, kt[p]))],
        out_specs=pl.BlockSpec((1, t, GROUP), lambda b, p, qt, kt: (b, qt[p], 0)),
        scratch_shapes=[pltpu.VMEM((8, GROUP, t), BF),
                        pltpu.VMEM((8, t), F32),
                        pltpu.VMEM((2, 4 * VROWS, t), F32)])
    return pl.pallas_call(
        functools.partial(_attn_a_body, t=t, lam_init=lam_init),
        grid_spec=grid_spec,
        out_shape=jax.ShapeDtypeStruct((B, S, GROUP), BF),
        compiler_params=pltpu.CompilerParams(
            dimension_semantics=("arbitrary", "arbitrary"), vmem_limit_bytes=VMEM_LIMIT),
        name="attn_diff",
    )(qtab, ktab, lam_p, gh, q, k, v)


def _attn_b_body(qtab_ref, ktab_ref, g_ref, q_ref, k_ref, v_ref, o_ref, m_ref, acc_ref, *, t):
    qi = qtab_ref[pl.program_id(1)]
    ki = ktab_ref[pl.program_id(1)]

    @pl.when(ki == 0)
    def _init():
        m_ref[...] = jnp.full(m_ref.shape, NEG, F32)
        acc_ref[...] = jnp.zeros(acc_ref.shape, F32)

    def step(diag):
        jobs = _attn_jobs(t // QB, diag, [(h,) for h in range(4)])
        keep = _causal_keep() if diag else None

        def qk(h, qc, kc):
            return _dot(k_ref[0, kc * QB:(kc + 1) * QB, h * 128:(h + 1) * 128],
                        q_ref[0, h * 128:(h + 1) * 128, qc * QB:(qc + 1) * QB])

        def pv(h, qc, kc, pt, alpha):
            hs, cs = slice(VROWS * h, VROWS * (h + 1)), slice(qc * QB, (qc + 1) * QB)
            acc_ref[hs, cs] = acc_ref[hs, cs] * alpha + _dot(v_ref[0, hs, kc * QB:(kc + 1) * QB], pt)

        st_next = qk(*jobs[0])
        pending = None
        for n, (h, qc, kc) in enumerate(jobs):
            st = st_next
            if n + 1 < len(jobs):
                st_next = qk(*jobs[n + 1])
            if diag and kc == qc:
                st = jnp.where(keep, st, NEG)
            pt, alpha = _online_softmax_t(st, m_ref, h, slice(qc * QB, (qc + 1) * QB))
            if pending is not None:
                pv(*pending)
            pending = (h, qc, kc, pt, alpha)
        pv(*pending)

    @pl.when(ki < qi)
    def _off():
        step(False)

    @pl.when(ki == qi)
    def _diag():
        step(True)
        rows = [acc_ref[VROWS * h:VROWS * h + 64, :] * (1.0 / acc_ref[VROWS * h + 64:VROWS * h + 65, :])
                for h in range(4)]
        ot = jnp.concatenate(rows, axis=0)
        o_ref[0] = _rms(ot.T, g_ref[...]).astype(BF)


def _attn_b(g, q, k, v, *, t):
    B, S, _ = k.shape
    n = S // t
    qtab, ktab = _causal_pairs(n)
    grid_spec = pltpu.PrefetchScalarGridSpec(
        num_scalar_prefetch=2,
        grid=(B, qtab.shape[0]),
        in_specs=[pl.BlockSpec((1, GROUP), lambda b, p, qt, kt: (0, 0)),
                  pl.BlockSpec((1, 512, t), lambda b, p, qt, kt: (b, 0, qt[p])),
                  pl.BlockSpec((1, t, 512), lambda b, p, qt, kt: (b, kt[p], 0)),
                  pl.BlockSpec((1, 4 * VROWS, t), lambda b, p, qt, kt: (b, 0, kt[p]))],
        out_specs=pl.BlockSpec((1, t, GROUP), lambda b, p, qt, kt: (b, qt[p], 0)),
        scratch_shapes=[pltpu.VMEM((4, t), F32),
                        pltpu.VMEM((4 * VROWS, t), F32)])
    return pl.pallas_call(
        functools.partial(_attn_b_body, t=t),
        grid_spec=grid_spec,
        out_shape=jax.ShapeDtypeStruct((B, S, GROUP), BF),
        compiler_params=pltpu.CompilerParams(
            dimension_semantics=("arbitrary", "arbitrary"), vmem_limit_bytes=VMEM_LIMIT),
        name="attn_mla",
    )(qtab, ktab, g, q, k, v)


DIL_TILE = 2048
DILATIONS = (1, 4, 16)


def _dil_body(g_ref, q_ref, kp_ref, kc_ref, vp_ref, vc_ref, o_ref, ob_ref, lb_ref, *, td):
    i = pl.program_id(1)
    lane = lax.broadcasted_iota(jnp.int32, (1, GROUP), 1)
    a = lax.broadcasted_iota(jnp.int32, (128, 256), 0)
    c = lax.broadcasted_iota(jnp.int32, (128, 256), 1)
    dist = jnp.where(c - a >= 0, c - a, 1000)

    def rows(ref, start, d):
        idx = pl.ds(start, 128) if d == 1 else pl.ds(start, 128, stride=d)
        return jnp.concatenate([ref.at[0, 0][idx, :], ref.at[0, 1][idx, :]], axis=1).astype(BF)

    for bi, d in enumerate(DILATIONS):
        def block(n, carry, bi=bi, d=d):
            if d == 1:
                r, blk = 0, n
            elif td // d == 128:
                r, blk = n, 0
            else:
                r, blk = n % d, n // d
            q0 = r + d * 128 * blk
            if d == 1:
                q0 = pl.multiple_of(q0, 128)
            q = rows(q_ref, q0, d)
            k_hi, v_hi = rows(kc_ref, q0, d), rows(vc_ref, q0, d)
            lo_prev = td - d * 128 + r
            if isinstance(blk, int):
                first = True
                k_lo, v_lo = rows(kp_ref, lo_prev, d), rows(vp_ref, lo_prev, d)
            else:
                first = blk == 0
                lo_cur = jnp.maximum(q0 - d * 128, 0)
                if d == 1:
                    lo_cur = pl.multiple_of(lo_cur, 128)
                k_lo = jnp.where(first, rows(kp_ref, lo_prev, d), rows(kc_ref, lo_cur, d))
                v_lo = jnp.where(first, rows(vp_ref, lo_prev, d), rows(vc_ref, lo_cur, d))
            kwin = jnp.concatenate([k_lo, k_hi], axis=0)
            vst = _vstack(jnp.concatenate([v_lo, v_hi], axis=0), lane)
            cmin = jnp.where(jnp.logical_and(first, i == 0), 128, 0)
            ok = jnp.where(c >= cmin, dist, 1000) <= 128
            es, ils, lses = [], [], []
            for h in range(4):
                s = _nt_dot(jnp.where(lane // 64 == h, q, jnp.zeros_like(q)), kwin)
                s = jnp.where(ok, s, NEG)
                m = jnp.max(s, axis=-1, keepdims=True)
                ex = jnp.exp(s - m)
                den = jnp.sum(ex, axis=-1, keepdims=True)
                es.append(ex.astype(BF))
                ils.append(1.0 / den)
                lses.append(m + jnp.log(den))
            o = _dot(jnp.concatenate(es, axis=1), vst) * _head_expand(ils, lane, 64)
            lse = _head_expand(lses, lane, 64)
            idx = pl.ds(q0, 128) if d == 1 else pl.ds(q0, 128, stride=d)
            for hf in range(2):
                ob_ref.at[bi, hf][idx, :] = o[:, hf * LANES:(hf + 1) * LANES]
                lb_ref.at[bi, hf][idx, :] = lse[:, hf * LANES:(hf + 1) * LANES]
            return carry

        lax.fori_loop(0, td // 128, block, 0, unroll=4)

    for ch in range(td // 256):
        rs = slice(ch * 256, (ch + 1) * 256)
        halves = []
        for hf in range(2):
            ls = [lb_ref[bi, hf, rs, :] for bi in range(3)]
            mx = jnp.maximum(jnp.maximum(ls[0], ls[1]), ls[2])
            ws = [jnp.exp(l - mx) for l in ls]
            num = ws[0] * ob_ref[0, hf, rs, :] + ws[1] * ob_ref[1, hf, rs, :] + ws[2] * ob_ref[2, hf, rs, :]
            halves.append(num / (ws[0] + ws[1] + ws[2]))
        o_ref[0, rs, :] = _rms(jnp.concatenate(halves, axis=1), g_ref[...]).astype(BF)


def _dil(g, q, k, v, *, td):
    B, _, S, _ = q.shape
    cur = pl.BlockSpec((1, 2, td, LANES), lambda b, i: (b, 0, i, 0))
    prev = pl.BlockSpec((1, 2, td, LANES), lambda b, i: (b, 0, jnp.maximum(i - 1, 0), 0))
    return pl.pallas_call(
        functools.partial(_dil_body, td=td),
        grid=(B, S // td),
        in_specs=[pl.BlockSpec((1, GROUP), lambda b, i: (0, 0)), cur, prev, cur, prev, cur],
        out_specs=pl.BlockSpec((1, td, GROUP), lambda b, i: (b, i, 0)),
        out_shape=jax.ShapeDtypeStruct((B, S, GROUP), BF),
        scratch_shapes=[pltpu.VMEM((3, 2, td, LANES), F32),
                        pltpu.VMEM((3, 2, td, LANES), F32)],
        compiler_params=pltpu.CompilerParams(
            dimension_semantics=("arbitrary", "arbitrary"), vmem_limit_bytes=VMEM_LIMIT),
        name="dilated",
    )(g, q, k, k, v, v)


def _outproj_body(x_ref, oa_ref, ob_ref, oc_ref, od_ref, w_ref, y_ref):
    y = _dot(oa_ref[...], w_ref[0:256, :])
    y += _dot(ob_ref[...], w_ref[256:512, :])
    y += _dot(oc_ref[...], w_ref[512:768, :])
    y += _dot(od_ref[...], w_ref[768:1024, :])
    y_ref[...] = x_ref[...] + y


def _outproj(x, oa, ob, oc, od, w_out, *, tm):
    T, D = x.shape
    row = lambda w: pl.BlockSpec((tm, w), lambda i: (i, 0))
    full = lambda shape: pl.BlockSpec(shape, lambda i: (0,) * len(shape))
    return pl.pallas_call(
        _outproj_body,
        grid=(T // tm,),
        in_specs=[row(D)] + [row(GROUP)] * 4 + [full((D, D))],
        out_specs=row(D),
        out_shape=jax.ShapeDtypeStruct((T, D), F32),
        compiler_params=pltpu.CompilerParams(
            dimension_semantics=("arbitrary",), vmem_limit_bytes=VMEM_LIMIT),
        name="outproj",
    )(x, oa, ob, oc, od, w_out)


PEER_EB = 2048
GELU_C1 = math.sqrt(2.0 / math.pi)
GELU_C2 = 0.044715 * GELU_C1
N_CAND = PEER_TOP + 1


def _peer_body(x_ref, g_ref, wqt_ref, sk_ref, u_ref, vt_ref, gf_ref, o_ref,
               ht_ref, st_ref, e2_ref, th_ref, cw_ref, hw_ref, acc_ref,
               *, tt, final):
    e = pl.program_id(1)
    nch = tt // LANES

    @pl.when(e == 0)
    def _prologue():
        h = _rms(x_ref[...], g_ref[...])
        ht_ref[...] = h.T.astype(BF)
        qt = _dot(wqt_ref[...], ht_ref[...]).astype(BF)
        for hh in range(PEER_HEADS):
            for c in range(2):
                r0 = hh * N_KEYS + c * 64
                sc = _dot(sk_ref[c], qt[r0:r0 + 64, :])
                for tc in range(nch):
                    blk = sc[:, tc * LANES:(tc + 1) * LANES]
                    st_ref.at[tc][pl.ds(c * 1024 + hh, N_KEYS, stride=PEER_HEADS), :] = blk
                    if c == 1:
                        st_ref[tc, 2048 + hh * N_KEYS:2048 + (hh + 1) * N_KEYS, :] = blk
        acc_ref[...] = jnp.zeros(acc_ref.shape, F32)

        def top_vals(load, n_rows, n_out):
            top = []
            for r in range(n_rows):
                v = load(r)
                for q in range(len(top)):
                    top[q], v = jnp.maximum(top[q], v), jnp.minimum(top[q], v)
                if len(top) < n_out:
                    top.append(v)
            return top

        def token_chunk(tc, carry):
            ls = pl.ds(pl.multiple_of(tc * LANES, LANES), LANES)
            v1 = top_vals(lambda r: st_ref[tc, r * 8:(r + 1) * 8, :], N_KEYS, N_CAND)
            v2 = top_vals(lambda r: st_ref[tc, 1024 + r * 8:1024 + (r + 1) * 8, :], N_KEYS, N_CAND)
            cands = [v1[a] + v2[b] for a in range(N_CAND) for b in range(N_CAND)
                     if (a + 1) * (b + 1) <= N_CAND]
            top = top_vals(lambda r: cands[r], len(cands), N_CAND)
            z = jnp.ones_like(top[0])
            for r in range(1, PEER_TOP):
                z = z + jnp.exp(top[r] - top[0])
            tau = 0.5 * (top[PEER_TOP - 1] + top[PEER_TOP])
            iz = 1.0 / z
            for r in range(N_KEYS):
                s1 = st_ref[tc, r * 8:(r + 1) * 8, :]
                th_ref[r, :, ls] = tau - s1
                cw_ref[r, :, ls] = jnp.exp(s1 - v1[0]) * (0.5 * iz)
            for hh in range(PEER_HEADS):
                rows = slice(2048 + hh * 128, 2048 + (hh + 1) * 128)
                e2_ref[hh * 128:(hh + 1) * 128, ls] = jnp.exp(st_ref[tc, rows, :] - v2[0][hh:hh + 1, :])
            return carry

        lax.fori_loop(0, nch, token_chunk, 0)

    nblk = PEER_EB // 256

    def pre_act(b):
        return _dot(u_ref[b * 256:(b + 1) * 256, :], ht_ref[...])

    def gate(b, a):
        i0 = e * (PEER_EB // N_KEYS) + 2 * b
        th = [th_ref[i0], th_ref[i0 + 1]]
        cw = [cw_ref[i0], cw_ref[i0 + 1]]
        for tc in range(nch):
            ls = slice(tc * LANES, (tc + 1) * LANES)
            for sub in range(2):
                w = [None, None]
                for hh in range(PEER_HEADS):
                    r0 = hh * N_KEYS + sub * 64
                    s2 = st_ref[tc, 2048 + r0:2048 + r0 + 64, :]
                    e2 = e2_ref[r0:r0 + 64, ls]
                    for half in range(2):
                        sel = jnp.where(s2 >= th[half][hh:hh + 1, ls], e2, 0.0) * cw[half][hh:hh + 1, ls]
                        w[half] = sel if w[half] is None else w[half] + sel
                for half in range(2):
                    r1 = half * N_KEYS + sub * 64
                    x = a[r1:r1 + 64, ls]
                    xw = x * w[half]
                    t = jnp.tanh(x * (GELU_C1 + GELU_C2 * (x * x)))
                    hw_ref[b % 2, r1:r1 + 64, ls] = (xw + xw * t).astype(BF)

    a = pre_act(0)
    for b in range(nblk):
        a_next = pre_act(b + 1) if b + 1 < nblk else None
        gate(b, a)
        if b > 0:
            acc_ref[...] += _dot(vt_ref[b - 1], hw_ref[(b - 1) % 2])
        a = a_next
    acc_ref[...] += _dot(vt_ref[nblk - 1], hw_ref[(nblk - 1) % 2])

    @pl.when(e == pl.num_programs(1) - 1)
    def _finish():
        y = x_ref[...] + acc_ref[...].T
        if final:
            y = _rms(y, gf_ref[...])
        o_ref[...] = y


def _peer(x, g, wqt, sk, u, vt, gf, *, tt, final):
    T, D = x.shape
    ne = u.shape[0] // PEER_EB
    const = lambda shape: pl.BlockSpec(shape, lambda i, e: (0,) * len(shape),
                                       pipeline_mode=pl.Buffered(1))
    return pl.pallas_call(
        functools.partial(_peer_body, tt=tt, final=final),
        grid=(T // tt, ne),
        in_specs=[pl.BlockSpec((tt, D), lambda i, e: (i, 0)),
                  const((1, D)), const((D, D)), const((2, N_KEYS, 64)),
                  pl.BlockSpec((PEER_EB, D), lambda i, e: (e, 0)),
                  pl.BlockSpec((PEER_EB // 256, D, 256), lambda i, e: (e, 0, 0)),
                  const((1, D))],
        out_specs=pl.BlockSpec((tt, D), lambda i, e: (i, 0)),
        out_shape=jax.ShapeDtypeStruct((T, D), F32),
        scratch_shapes=[pltpu.VMEM((D, tt), BF),
                        pltpu.VMEM((tt // LANES, 3 * 1024, LANES), F32),
                        pltpu.VMEM((1024, tt), F32),
                        pltpu.VMEM((N_KEYS, PEER_HEADS, tt), F32),
                        pltpu.VMEM((N_KEYS, PEER_HEADS, tt), F32),
                        pltpu.VMEM((2, 256, tt), BF),
                        pltpu.VMEM((D, tt), F32)],
        compiler_params=pltpu.CompilerParams(
            dimension_semantics=("arbitrary", "arbitrary"), vmem_limit_bytes=VMEM_LIMIT),
        name="peer",
    )(x, g, wqt, sk, u, vt, gf)


def _pack_in_weights(w_in):
    aq, ak, av, cq, ckv, kpe, zc, dq, dk, dv = jnp.split(
        w_in, np.cumsum([256, 256, 256, 256, 128, 32, 512, 256, 256])[:].tolist(), axis=1)
    kpe_rep = jnp.zeros((D_MODEL, 4, 128), w_in.dtype).at[:, :, 64:96].set(kpe[:, None, :])
    return jnp.concatenate(
        [aq, ak, av, cq, ckv, kpe_rep.reshape(D_MODEL, 512), zc, dq, dk, dv], axis=1).astype(BF)


def _pack_mla_weights(w_uq, w_ukv):
    uq = jnp.pad(w_uq.reshape(256, 4, 96), ((0, 0), (0, 0), (0, 32))).reshape(256, 512)
    ukv = w_ukv.reshape(128, 4, 128)
    uk = jnp.pad(ukv[:, :, :64], ((0, 0), (0, 0), (0, 64))).reshape(128, 512)
    uv = ukv[:, :, 64:].reshape(128, 256)
    return uq.astype(BF), uk.astype(BF), uv.astype(BF)


def _pack_expert_out(v):
    n, d = v.shape
    return v.reshape(n // 256, 256, d).transpose(0, 2, 1).astype(BF)


def kernel(x, w_in, w_out, norm_mix, norm_ffn, diff_lambda, diff_head_norm, mla_q_norm, mla_kv_norm, mla_w_uq, mla_w_ukv, mla_out_norm, sgu_v_norm, sgu_w, sgu_b, sgu_out_norm, dil_out_norm, peer_w_q, peer_sub_keys, peer_u, peer_v, final_norm):
    B, S, D = x.shape
    depth = w_in.shape[0]
    T = B * S
    tm = min(512, S)
    ta, tb = min(DIFF_TILE, S), min(MLA_TILE, S)
    tt = min(512, T)
    row = lambda a: a.reshape(1, -1)

    tabs = (_rope_tables(S, 8, 32, 0) + _rope_tables(S, 32, 128, 64) + _rope_tables(S, 16, 64, 0))

    for l in range(depth):
        lam_init = 0.8 - 0.6 * math.exp(-0.3 * l)
        w_cat = _pack_in_weights(w_in[l])
        wuq, wuk, wuv = _pack_mla_weights(mla_w_uq[l], mla_w_ukv[l])
        sgub = jnp.repeat(sgu_b[l].T, 64, axis=1)
        qa, ka, va, qb, kb, vb, oc, qd, kd, vd = _inproj(
            x, row(norm_mix[l]), w_cat, tabs, row(mla_q_norm[l]), row(mla_kv_norm[l]),
            wuq, wuk, wuv, row(sgu_v_norm[l]), sgu_w[l], sgub, row(sgu_out_norm[l]), tm=tm)

        oa = _attn_a(diff_lambda[l], row(diff_head_norm[l]), qa, ka, va, t=ta, lam_init=lam_init)
        ob = _attn_b(row(mla_out_norm[l]), qb, kb, vb, t=tb)

        od = _dil(row(dil_out_norm[l]), qd, kd, vd, td=DIL_TILE)

        x2 = _outproj(x.reshape(T, D), oa.reshape(T, GROUP), ob.reshape(T, GROUP),
                      oc.reshape(T, GROUP), od.reshape(T, GROUP),
                      w_out[l].astype(BF), tm=min(512, T))

        x2 = _peer(x2, row(norm_ffn[l]), peer_w_q[l].T.astype(BF), peer_sub_keys[l].astype(BF),
                   peer_u[l].astype(BF), _pack_expert_out(peer_v[l]), row(final_norm),
                   tt=tt, final=(l == depth - 1))
        x = x2.reshape(B, S, D)
    return x
```

```python
import functools
import math

import numpy as np
import jax
import jax.numpy as jnp
from jax import lax
from jax.experimental import pallas as pl
from jax.experimental.pallas import tpu as pltpu

D_MODEL = 1024
GROUP = 256
ROPE_THETA = 500000.0
NEG = -1e30
EPS = 1e-6
LANES = 128
LOG2E = math.log2(math.e)

N_KEYS = 128
PEER_HEADS = 8
PEER_TOP = 16

BF = jnp.bfloat16
F32 = jnp.float32

C_AQ, C_AK, C_AV, C_CQ, C_CKV, C_KPE, C_ZC, C_DQ, C_DK, C_DV, C_END = (
    0, 256, 512, 768, 1024, 1152, 1664, 2176, 2432, 2688, 2944)

VMEM_LIMIT = 56 * 1024 * 1024


def _rms(x, g):
    return x * lax.rsqrt(jnp.mean(x * x, axis=-1, keepdims=True) + EPS) * g


def _nt_dot(a, b):
    return lax.dot_general(a, b, (((1,), (1,)), ((), ())), preferred_element_type=F32)


def _dot(a, b):
    return jnp.dot(a, b, preferred_element_type=F32)


def _rope_tables(seq, n_rot, period, offset):
    half = n_rot // 2
    pos = jnp.arange(seq, dtype=F32)
    inv = ROPE_THETA ** (-jnp.arange(half, dtype=F32) * (2.0 / n_rot))
    ang = pos[:, None] * inv[None, :]
    cos, sin = jnp.cos(ang), jnp.sin(ang)
    g = np.arange(LANES) % period - offset
    rot = (g >= 0) & (g < n_rot)
    idx = np.where(rot, g % half, 0)
    sign = np.where(g < half, -1.0, 1.0).astype(np.float32)
    cos_t = jnp.where(rot[None, :], cos[:, idx], 1.0)
    sin_t = jnp.where(rot[None, :], sin[:, idx] * sign[None, :], 0.0)
    return cos_t.astype(F32), sin_t.astype(F32)


def _rope_apply(x, cos_t, sin_t, n_rot, period, offset):
    half = n_rot // 2
    lane = lax.broadcasted_iota(jnp.int32, (1, LANES), 1)
    first = (lane % period - offset) < half
    outs = []
    for c in range(x.shape[1] // LANES):
        xc = x[:, c * LANES:(c + 1) * LANES]
        fwd = pltpu.roll(xc, LANES - half, 1)
        bwd = pltpu.roll(xc, half, 1)
        outs.append(xc * cos_t + jnp.where(first, fwd, bwd) * sin_t)
    return outs[0] if len(outs) == 1 else jnp.concatenate(outs, axis=1)


VROWS = 80


def _value_rows(vt):
    one = (lax.broadcasted_iota(jnp.int32, (VROWS - 64, vt.shape[1]), 0) == 0).astype(F32)
    parts = []
    for h in range(4):
        parts += [vt[64 * h:64 * (h + 1)], one]
    return jnp.concatenate(parts, axis=0).astype(BF)


def _inproj_body(x_ref, gmix_ref, w_ref, ca_ref, sa_ref, cb_ref, sb_ref, cd_ref, sd_ref,
                 gq_ref, gkv_ref, wuq_ref, wuk_ref, wuv_ref,
                 gsv_ref, sguw_ref, sgub_ref, gso_ref,
                 qa_ref, ka_ref, va_ref, qb_ref, kb_ref, vb_ref, oc_ref,
                 qd_ref, kd_ref, vd_ref, *, tm):
    x = x_ref[0]
    h = _rms(x, gmix_ref[...]).astype(BF)

    def proj(a, b):
        return _dot(h, w_ref[:, a:b])

    ca, sa = ca_ref[...], sa_ref[...]
    aq = _rope_apply(proj(C_AQ, C_AK), ca, sa, 8, 32, 0)
    qa_ref[0] = (aq * (32.0 ** -0.5 * LOG2E)).T.astype(BF)
    ka_ref[0] = _rope_apply(proj(C_AK, C_AV), ca, sa, 8, 32, 0).astype(BF)
    va_ref[0] = _value_rows(proj(C_AV, C_CQ).T)

    cb, sb = cb_ref[...], sb_ref[...]
    cq = _rms(proj(C_CQ, C_CKV), gq_ref[...]).astype(BF)
    qb = _rope_apply(_dot(cq, wuq_ref[...]), cb, sb, 32, 128, 64)
    qb_ref[0] = (qb * (96.0 ** -0.5 * LOG2E)).T.astype(BF)
    ckv = _rms(proj(C_CKV, C_KPE), gkv_ref[...]).astype(BF)
    kpe = _rope_apply(proj(C_KPE, C_ZC), cb, sb, 32, 128, 64)
    kb_ref[0] = (_dot(ckv, wuk_ref[...]) + kpe).astype(BF)
    vb_ref[0] = _value_rows(_dot(ckv, wuv_ref[...]).T)

    zc = jax.nn.gelu(proj(C_ZC, C_DQ))
    u = zc[:, :GROUP]
    vn = _rms(zc[:, GROUP:], gsv_ref[...]).astype(BF)
    r = lax.broadcasted_iota(jnp.int32, (128, 128), 0)
    c = lax.broadcasted_iota(jnp.int32, (128, 128), 1)
    wcat = jnp.concatenate(
        [jnp.where(r >= c, sguw_ref[g], 0.0).astype(BF) for g in range(4)], axis=1)
    lane = lax.broadcasted_iota(jnp.int32, (1, GROUP), 1)
    bias = sgub_ref[...]
    gso = gso_ref[...]
    for ch in range(tm // 128):
        vc = vn[ch * 128:(ch + 1) * 128]
        vst = jnp.concatenate(
            [jnp.where(lane // 64 == g, vc, jnp.zeros_like(vc)) for g in range(4)], axis=0)
        sv = _dot(wcat, vst) + bias
        oc = u[ch * 128:(ch + 1) * 128] * sv
        oc_ref[0, ch * 128:(ch + 1) * 128, :] = _rms(oc, gso).astype(BF)

    cd, sd = cd_ref[...], sd_ref[...]
    dq = _rope_apply(proj(C_DQ, C_DK), cd, sd, 16, 64, 0)
    dk = _rope_apply(proj(C_DK, C_DV), cd, sd, 16, 64, 0)
    dv = proj(C_DV, C_END)
    for hf in range(2):
        qd_ref[0, hf] = dq[:, hf * LANES:(hf + 1) * LANES] * 0.125
        kd_ref[0, hf] = dk[:, hf * LANES:(hf + 1) * LANES]
        vd_ref[0, hf] = dv[:, hf * LANES:(hf + 1) * LANES]


def _inproj(x, gmix, w_cat, tabs, gq, gkv, wuq, wuk, wuv, gsv, sguw, sgub, gso, *, tm):
    B, S, D = x.shape
    ns = S // tm
    full = lambda shape: pl.BlockSpec(shape, lambda s, b: (0,) * len(shape))
    tab = pl.BlockSpec((tm, LANES), lambda s, b: (s, 0))
    seq = lambda w: pl.BlockSpec((1, tm, w), lambda s, b: (b, s, 0))
    seq_t = lambda w: pl.BlockSpec((1, w, tm), lambda s, b: (b, 0, s))
    halves = pl.BlockSpec((1, 2, tm, LANES), lambda s, b: (b, 0, s, 0))
    out_w = [256, 256, 4 * VROWS, 512, 512, 4 * VROWS, 256]
    transposed = [True, False, True, True, False, True, False]
    return pl.pallas_call(
        functools.partial(_inproj_body, tm=tm),
        grid=(ns, B),
        in_specs=[seq(D), full((1, D)), full((D, C_END))] + [tab] * 6 + [
            full((1, 256)), full((1, 128)), full((256, 512)), full((128, 512)), full((128, 256)),
            full((1, 256)), full((4, 128, 128)), full((128, 256)), full((1, 256))],
        out_specs=[seq_t(w) if tr else seq(w) for w, tr in zip(out_w, transposed)] + [halves] * 3,
        out_shape=[jax.ShapeDtypeStruct((B, w, S) if tr else (B, S, w), BF)
                   for w, tr in zip(out_w, transposed)]
        + [jax.ShapeDtypeStruct((B, 2, S, LANES), F32)] * 3,
        compiler_params=pltpu.CompilerParams(
            dimension_semantics=("arbitrary", "arbitrary"), vmem_limit_bytes=VMEM_LIMIT),
        name="inproj",
    )(x, gmix, w_cat, *tabs, gq, gkv, wuq, wuk, wuv, gsv, sguw, sgub, gso)


def _head_expand(cols, lane, width):
    out = cols[-1]
    for h in range(len(cols) - 2, -1, -1):
        out = jnp.where(lane // width == h, cols[h], out)
    return out


def _vstack(v, lane):
    return jnp.concatenate(
        [jnp.where(lane // 64 == h, v, jnp.zeros_like(v)) for h in range(4)], axis=0)


DIFF_TILE = 1024
MLA_TILE = 1024
QB = 512


def _online_softmax_t(st, m_ref, j, cs):
    m_prev = m_ref[j:j + 1, cs]
    m_new = jnp.maximum(m_prev, jnp.max(st, axis=0, keepdims=True))
    alpha = jnp.exp2(m_prev - m_new)
    et = jnp.exp2(st - m_new)
    m_ref[j:j + 1, cs] = m_new
    return et.astype(BF), alpha


def _attn_jobs(ns, diag, heads):
    return [(*hd, qc, kc) for kc in range(ns) for qc in range(ns) for hd in heads
            if not (diag and kc > qc)]


def _causal_keep():
    return (lax.broadcasted_iota(jnp.int32, (QB, QB), 0)
            <= lax.broadcasted_iota(jnp.int32, (QB, QB), 1))


def _causal_pairs(n):
    pairs = [(i, j) for i in range(n) for j in range(i + 1)]
    return (jnp.asarray([p[0] for p in pairs], jnp.int32),
            jnp.asarray([p[1] for p in pairs], jnp.int32))


def _attn_a_body(qtab_ref, ktab_ref, lam_ref, gh_ref, q_ref, k_ref, v_ref, o_ref,
                 qs_ref, m_ref, acc_ref, *, t, lam_init):
    qi = qtab_ref[pl.program_id(1)]
    ki = ktab_ref[pl.program_id(1)]

    @pl.when(ki == 0)
    def _init():
        m_ref[...] = jnp.full(m_ref.shape, NEG, F32)
        acc_ref[...] = jnp.zeros(acc_ref.shape, F32)
        qt = q_ref[0]
        row = lax.broadcasted_iota(jnp.int32, (GROUP, 1), 0)
        for j in range(8):
            qs_ref[j] = jnp.where(row // 32 == j, qt, jnp.zeros_like(qt))

    def step(diag):
        jobs = _attn_jobs(t // QB, diag, [(mp, h) for mp in range(2) for h in range(4)])
        kblk = [k_ref[0, kc * QB:(kc + 1) * QB, :] for kc in range(t // QB)]
        keep = _causal_keep() if diag else None

        def qk(mp, h, qc, kc):
            return _dot(kblk[kc], qs_ref[2 * h + mp, :, qc * QB:(qc + 1) * QB])

        def pv(mp, h, qc, kc, pt, alpha):
            hs, cs = slice(VROWS * h, VROWS * (h + 1)), slice(qc * QB, (qc + 1) * QB)
            acc_ref[mp, hs, cs] = (acc_ref[mp, hs, cs] * alpha
                                   + _dot(v_ref[0, hs, kc * QB:(kc + 1) * QB], pt))

        st_next = qk(*jobs[0])
        pending = None
        for n, (mp, h, qc, kc) in enumerate(jobs):
            st = st_next
            if n + 1 < len(jobs):
                st_next = qk(*jobs[n + 1])
            if diag and kc == qc:
                st = jnp.where(keep, st, NEG)
            pt, alpha = _online_softmax_t(st, m_ref, 2 * h + mp, slice(qc * QB, (qc + 1) * QB))
            if pending is not None:
                pv(*pending)
            pending = (mp, h, qc, kc, pt, alpha)
        pv(*pending)

    @pl.when(ki < qi)
    def _off():
        step(False)

    @pl.when(ki == qi)
    def _diag():
        step(True)
        lf = lam_ref[...]
        lam = (jnp.exp(jnp.sum(lf[0:1] * lf[1:2], axis=-1, keepdims=True))
               - jnp.exp(jnp.sum(lf[2:3] * lf[3:4], axis=-1, keepdims=True)) + lam_init)
        rows = []
        for h in range(4):
            hs, ls = slice(VROWS * h, VROWS * h + 64), slice(VROWS * h + 64, VROWS * h + 65)
            oh = (acc_ref[0, hs, :] * (1.0 / acc_ref[0, ls, :])
                  - lam * (acc_ref[1, hs, :] * (1.0 / acc_ref[1, ls, :])))
            ms = jnp.mean(oh * oh, axis=0, keepdims=True)
            rows.append(oh * lax.rsqrt(ms + EPS))
        ot = jnp.concatenate(rows, axis=0)
        o_ref[0] = (ot.T * gh_ref[...] * (1.0 - lam_init)).astype(BF)


def _attn_a(lam_p, gh, q, k, v, *, t, lam_init):
    B, S, _ = k.shape
    n = S // t
    qtab, ktab = _causal_pairs(n)
    grid_spec = pltpu.PrefetchScalarGridSpec(
        num_scalar_prefetch=2,
        grid=(B, qtab.shape[0]),
        in_specs=[pl.BlockSpec((4, 32), lambda b, p, qt, kt: (0, 0)),
                  pl.BlockSpec((1, GROUP), lambda b, p, qt, kt: (0, 0)),
                  pl.BlockSpec((1, GROUP, t), lambda b, p, qt, kt: (b, 0, qt[p])),
                  pl.BlockSpec((1, t, GROUP), lambda b, p, qt, kt: (b, kt[p], 0)),
                  pl.BlockSpec((1, 4 * VROWS, t), lambda b, p, qt, kt: (b, 0, kt[p]))],
        out_specs=pl.BlockSpec((1, t, GROUP), lambda b, p, qt, kt: (b, qt[p], 0)),
        scratch_shapes=[pltpu.VMEM((8, GROUP, t), BF),
                        pltpu.VMEM((8, t), F32),
                        pltpu.VMEM((2, 4 * VROWS, t), F32)])
    return pl.pallas_call(
        functools.partial(_attn_a_body, t=t, lam_init=lam_init),
        grid_spec=grid_spec,
        out_shape=jax.ShapeDtypeStruct((B, S, GROUP), BF),
        compiler_params=pltpu.CompilerParams(
            dimension_semantics=("arbitrary", "arbitrary"), vmem_limit_bytes=VMEM_LIMIT),
        name="attn_diff",
    )(qtab, ktab, lam_p, gh, q, k, v)


def _attn_b_body(qtab_ref, ktab_ref, g_ref, q_ref, k_ref, v_ref, o_ref, m_ref, acc_ref, *, t):
    qi = qtab_ref[pl.program_id(1)]
    ki = ktab_ref[pl.program_id(1)]

    @pl.when(ki == 0)
    def _init():
        m_ref[...] = jnp.full(m_ref.shape, NEG, F32)
        acc_ref[...] = jnp.zeros(acc_ref.shape, F32)

    def step(diag):
        jobs = _attn_jobs(t // QB, diag, [(h,) for h in range(4)])
        kblk = [k_ref[0, kc * QB:(kc + 1) * QB, :] for kc in range(t // QB)]
        keep = _causal_keep() if diag else None

        def qk(h, qc, kc):
            return _dot(kblk[kc][:, h * 128:(h + 1) * 128],
                        q_ref[0, h * 128:(h + 1) * 128, qc * QB:(qc + 1) * QB])

        def pv(h, qc, kc, pt, alpha):
            hs, cs = slice(VROWS * h, VROWS * (h + 1)), slice(qc * QB, (qc + 1) * QB)
            acc_ref[hs, cs] = acc_ref[hs, cs] * alpha + _dot(v_ref[0, hs, kc * QB:(kc + 1) * QB], pt)

        st_next = qk(*jobs[0])
        pending = None
        for n, (h, qc, kc) in enumerate(jobs):
            st = st_next
            if n + 1 < len(jobs):
                st_next = qk(*jobs[n + 1])
            if diag and kc == qc:
                st = jnp.where(keep, st, NEG)
            pt, alpha = _online_softmax_t(st, m_ref, h, slice(qc * QB, (qc + 1) * QB))
            if pending is not None:
                pv(*pending)
            pending = (h, qc, kc, pt, alpha)
        pv(*pending)

    @pl.when(ki < qi)
    def _off():
        step(False)

    @pl.when(ki == qi)
    def _diag():
        step(True)
        rows = [acc_ref[VROWS * h:VROWS * h + 64, :] * (1.0 / acc_ref[VROWS * h + 64:VROWS * h + 65, :])
                for h in range(4)]
        ot = jnp.concatenate(rows, axis=0)
        o_ref[0] = _rms(ot.T, g_ref[...]).astype(BF)


def _attn_b(g, q, k, v, *, t):
    B, S, _ = k.shape
    n = S // t
    qtab, ktab = _causal_pairs(n)
    grid_spec = pltpu.PrefetchScalarGridSpec(
        num_scalar_prefetch=2,
        grid=(B, qtab.shape[0]),
        in_specs=[pl.BlockSpec((1, GROUP), lambda b, p, qt, kt: (0, 0)),
                  pl.BlockSpec((1, 512, t), lambda b, p, qt, kt: (b, 0, qt[p])),
                  pl.BlockSpec((1, t, 512), lambda b, p, qt, kt: (b, kt[p], 0)),
                  pl.BlockSpec((1, 4 * VROWS, t), lambda b, p, qt, kt: (b, 0, kt[p]))],
        out_specs=pl.BlockSpec((1, t, GROUP), lambda b, p, qt, kt: (b, qt[p], 0)),
        scratch_shapes=[pltpu.VMEM((4, t), F32),
                        pltpu.VMEM((4 * VROWS, t), F32)])
    return pl.pallas_call(
        functools.partial(_attn_b_body, t=t),
        grid_spec=grid_spec,
        out_shape=jax.ShapeDtypeStruct((B, S, GROUP), BF),
        compiler_params=pltpu.CompilerParams(
            dimension_semantics=("arbitrary", "arbitrary"), vmem_limit_bytes=VMEM_LIMIT),
        name="attn_mla",
    )(qtab, ktab, g, q, k, v)


DIL_TILE = 2048
DILATIONS = (1, 4, 16)


def _dil_body(g_ref, q_ref, kp_ref, kc_ref, vp_ref, vc_ref, o_ref, ob_ref, lb_ref, *, td):
    i = pl.program_id(1)
    lane = lax.broadcasted_iota(jnp.int32, (1, GROUP), 1)
    a = lax.broadcasted_iota(jnp.int32, (128, 256), 0)
    c = lax.broadcasted_iota(jnp.int32, (128, 256), 1)
    dist = jnp.where(c - a >= 0, c - a, 1000)

    def rows(ref, start, d):
        idx = pl.ds(start, 128) if d == 1 else pl.ds(start, 128, stride=d)
        return jnp.concatenate([ref.at[0, 0][idx, :], ref.at[0, 1][idx, :]], axis=1).astype(BF)

    for bi, d in enumerate(DILATIONS):
        def block(n, carry, bi=bi, d=d):
            if d == 1:
                r, blk = 0, n
            elif td // d == 128:
                r, blk = n, 0
            else:
                r, blk = n % d, n // d
            q0 = r + d * 128 * blk
            if d == 1:
                q0 = pl.multiple_of(q0, 128)
            q = rows(q_ref, q0, d)
            k_hi, v_hi = rows(kc_ref, q0, d), rows(vc_ref, q0, d)
            lo_prev = td - d * 128 + r
            if isinstance(blk, int):
                first = True
                k_lo, v_lo = rows(kp_ref, lo_prev, d), rows(vp_ref, lo_prev, d)
            else:
                first = blk == 0
                lo_cur = jnp.maximum(q0 - d * 128, 0)
                if d == 1:
                    lo_cur = pl.multiple_of(lo_cur, 128)
                k_lo = jnp.where(first, rows(kp_ref, lo_prev, d), rows(kc_ref, lo_cur, d))
                v_lo = jnp.where(first, rows(vp_ref, lo_prev, d), rows(vc_ref, lo_cur, d))
            kwin = jnp.concatenate([k_lo, k_hi], axis=0)
            vst = _vstack(jnp.concatenate([v_lo, v_hi], axis=0), lane)
            cmin = jnp.where(jnp.logical_and(first, i == 0), 128, 0)
            ok = jnp.where(c >= cmin, dist, 1000) <= 128
            es, ils, lses = [], [], []
            for h in range(4):
                s = _nt_dot(jnp.where(lane // 64 == h, q, jnp.zeros_like(q)), kwin)
                s = jnp.where(ok, s, NEG)
                m = jnp.max(s, axis=-1, keepdims=True)
                ex = jnp.exp(s - m)
                den = jnp.sum(ex, axis=-1, keepdims=True)
                es.append(ex.astype(BF))
                ils.append(1.0 / den)
                lses.append(m + jnp.log(den))
            o = _dot(jnp.concatenate(es, axis=1), vst) * _head_expand(ils, lane, 64)
            lse = _head_expand(lses, lane, 64)
            idx = pl.ds(q0, 128) if d == 1 else pl.ds(q0, 128, stride=d)
            for hf in range(2):
                ob_ref.at[bi, hf][idx, :] = o[:, hf * LANES:(hf + 1) * LANES]
                lb_ref.at[bi, hf][idx, :] = lse[:, hf * LANES:(hf + 1) * LANES]
            return carry

        lax.fori_loop(0, td // 128, block, 0, unroll=4)

    for ch in range(td // 256):
        rs = slice(ch * 256, (ch + 1) * 256)
        halves = []
        for hf in range(2):
            ls = [lb_ref[bi, hf, rs, :] for bi in range(3)]
            mx = jnp.maximum(jnp.maximum(ls[0], ls[1]), ls[2])
            ws = [jnp.exp(l - mx) for l in ls]
            num = ws[0] * ob_ref[0, hf, rs, :] + ws[1] * ob_ref[1, hf, rs, :] + ws[2] * ob_ref[2, hf, rs, :]
            halves.append(num / (ws[0] + ws[1] + ws[2]))
        o_ref[0, rs, :] = _rms(jnp.concatenate(halves, axis=1), g_ref[...]).astype(BF)


def _dil(g, q, k, v, *, td):
    B, _, S, _ = q.shape
    cur = pl.BlockSpec((1, 2, td, LANES), lambda b, i: (b, 0, i, 0))
    prev = pl.BlockSpec((1, 2, td, LANES), lambda b, i: (b, 0, jnp.maximum(i - 1, 0), 0))
    return pl.pallas_call(
        functools.partial(_dil_body, td=td),
        grid=(B, S // td),
        in_specs=[pl.BlockSpec((1, GROUP), lambda b, i: (0, 0)), cur, prev, cur, prev, cur],
        out_specs=pl.BlockSpec((1, td, GROUP), lambda b, i: (b, i, 0)),
        out_shape=jax.ShapeDtypeStruct((B, S, GROUP), BF),
        scratch_shapes=[pltpu.VMEM((3, 2, td, LANES), F32),
                        pltpu.VMEM((3, 2, td, LANES), F32)],
        compiler_params=pltpu.CompilerParams(
            dimension_semantics=("arbitrary", "arbitrary"), vmem_limit_bytes=VMEM_LIMIT),
        name="dilated",
    )(g, q, k, k, v, v)


def _outproj_body(x_ref, oa_ref, ob_ref, oc_ref, od_ref, w_ref, y_ref):
    y = _dot(oa_ref[...], w_ref[0:256, :])
    y += _dot(ob_ref[...], w_ref[256:512, :])
    y += _dot(oc_ref[...], w_ref[512:768, :])
    y += _dot(od_ref[...], w_ref[768:1024, :])
    y_ref[...] = x_ref[...] + y


def _outproj(x, oa, ob, oc, od, w_out, *, tm):
    T, D = x.shape
    row = lambda w: pl.BlockSpec((tm, w), lambda i: (i, 0))
    full = lambda shape: pl.BlockSpec(shape, lambda i: (0,) * len(shape))
    return pl.pallas_call(
        _outproj_body,
        grid=(T // tm,),
        in_specs=[row(D)] + [row(GROUP)] * 4 + [full((D, D))],
        out_specs=row(D),
        out_shape=jax.ShapeDtypeStruct((T, D), F32),
        compiler_params=pltpu.CompilerParams(
            dimension_semantics=("arbitrary",), vmem_limit_bytes=VMEM_LIMIT),
        name="outproj",
    )(x, oa, ob, oc, od, w_out)


PEER_EB = 2048
GELU_C1 = math.sqrt(2.0 / math.pi)
GELU_C2 = 0.044715 * GELU_C1
N_CAND = PEER_TOP + 1


def _peer_body(x_ref, g_ref, wqt_ref, sk_ref, u_ref, vt_ref, gf_ref, o_ref,
               ht_ref, st_ref, e2_ref, th_ref, cw_ref, hw_ref, acc_ref,
               *, tt, final):
    e = pl.program_id(1)
    nch = tt // LANES

    @pl.when(e == 0)
    def _prologue():
        h = _rms(x_ref[...], g_ref[...])
        ht_ref[...] = h.T.astype(BF)
        qt = _dot(wqt_ref[...], ht_ref[...]).astype(BF)
        for hh in range(PEER_HEADS):
            for c in range(2):
                r0 = hh * N_KEYS + c * 64
                sc = _dot(sk_ref[c], qt[r0:r0 + 64, :])
                for tc in range(nch):
                    blk = sc[:, tc * LANES:(tc + 1) * LANES]
                    st_ref.at[tc][pl.ds(c * 1024 + hh, N_KEYS, stride=PEER_HEADS), :] = blk
                    if c == 1:
                        st_ref[tc, 2048 + hh * N_KEYS:2048 + (hh + 1) * N_KEYS, :] = blk
        acc_ref[...] = jnp.zeros(acc_ref.shape, F32)

        def top_vals(load, n_rows, n_out):
            top = []
            for r in range(n_rows):
                v = load(r)
                for q in range(len(top)):
                    top[q], v = jnp.maximum(top[q], v), jnp.minimum(top[q], v)
                if len(top) < n_out:
                    top.append(v)
            return top

        def token_chunk(tc, carry):
            ls = pl.ds(pl.multiple_of(tc * LANES, LANES), LANES)
            v1 = top_vals(lambda r: st_ref[tc, r * 8:(r + 1) * 8, :], N_KEYS, N_CAND)
            v2 = top_vals(lambda r: st_ref[tc, 1024 + r * 8:1024 + (r + 1) * 8, :], N_KEYS, N_CAND)
            cands = [v1[a] + v2[b] for a in range(N_CAND) for b in range(N_CAND)
                     if (a + 1) * (b + 1) <= N_CAND]
            top = top_vals(lambda r: cands[r], len(cands), N_CAND)
            z = jnp.ones_like(top[0])
            for r in range(1, PEER_TOP):
                z = z + jnp.exp(top[r] - top[0])
            tau = 0.5 * (top[PEER_TOP - 1] + top[PEER_TOP])
            iz = 1.0 / z
            for r in range(N_KEYS):
                s1 = st_ref[tc, r * 8:(r + 1) * 8, :]
                th_ref[r, :, ls] = tau - s1
                cw_ref[r, :, ls] = jnp.exp(s1 - v1[0]) * (0.5 * iz)
            for hh in range(PEER_HEADS):
                rows = slice(2048 + hh * 128, 2048 + (hh + 1) * 128)
                e2_ref[hh * 128:(hh + 1) * 128, ls] = jnp.exp(st_ref[tc, rows, :] - v2[0][hh:hh + 1, :])
            return carry

        lax.fori_loop(0, nch, token_chunk, 0)

    nblk = PEER_EB // 256

    def pre_act(b):
        return _dot(u_ref[b * 256:(b + 1) * 256, :], ht_ref[...])

    def gate(b, a):
        i0 = e * (PEER_EB // N_KEYS) + 2 * b
        th = [th_ref[i0], th_ref[i0 + 1]]
        cw = [cw_ref[i0], cw_ref[i0 + 1]]
        for tc in range(nch):
            ls = slice(tc * LANES, (tc + 1) * LANES)
            for sub in range(2):
                w = [None, None]
                for hh in range(PEER_HEADS):
                    r0 = hh * N_KEYS + sub * 64
                    s2 = st_ref[tc, 2048 + r0:2048 + r0 + 64, :]
                    e2 = e2_ref[r0:r0 + 64, ls]
                    for half in range(2):
                        sel = jnp.where(s2 >= th[half][hh:hh + 1, ls], e2, 0.0) * cw[half][hh:hh + 1, ls]
                        w[half] = sel if w[half] is None else w[half] + sel
                for half in range(2):
                    r1 = half * N_KEYS + sub * 64
                    x = a[r1:r1 + 64, ls]
                    xw = x * w[half]
                    t = jnp.tanh(x * (GELU_C1 + GELU_C2 * (x * x)))
                    hw_ref[b % 2, r1:r1 + 64, ls] = (xw + xw * t).astype(BF)

    a = pre_act(0)
    for b in range(nblk):
        a_next = pre_act(b + 1) if b + 1 < nblk else None
        gate(b, a)
        if b > 0:
            acc_ref[...] += _dot(vt_ref[b - 1], hw_ref[(b - 1) % 2])
        a = a_next
    acc_ref[...] += _dot(vt_ref[nblk - 1], hw_ref[(nblk - 1) % 2])

    @pl.when(e == pl.num_programs(1) - 1)
    def _finish():
        y = x_ref[...] + acc_ref[...].T
        if final:
            y = _rms(y, gf_ref[...])
        o_ref[...] = y


def _peer(x, g, wqt, sk, u, vt, gf, *, tt, final):
    T, D = x.shape
    ne = u.shape[0] // PEER_EB
    const = lambda shape: pl.BlockSpec(shape, lambda i, e: (0,) * len(shape),
                                       pipeline_mode=pl.Buffered(1))
    return pl.pallas_call(
        functools.partial(_peer_body, tt=tt, final=final),
        grid=(T // tt, ne),
        in_specs=[pl.BlockSpec((tt, D), lambda i, e: (i, 0)),
                  const((1, D)), const((D, D)), const((2, N_KEYS, 64)),
                  pl.BlockSpec((PEER_EB, D), lambda i, e: (e, 0)),
                  pl.BlockSpec((PEER_EB // 256, D, 256), lambda i, e: (e, 0, 0)),
                  const((1, D))],
        out_specs=pl.BlockSpec((tt, D), lambda i, e: (i, 0)),
        out_shape=jax.ShapeDtypeStruct((T, D), F32),
        scratch_shapes=[pltpu.VMEM((D, tt), BF),
                        pltpu.VMEM((tt // LANES, 3 * 1024, LANES), F32),
                        pltpu.VMEM((1024, tt), F32),
                        pltpu.VMEM((N_KEYS, PEER_HEADS, tt), F32),
                        pltpu.VMEM((N_KEYS, PEER_HEADS, tt), F32),
                        pltpu.VMEM((2, 256, tt), BF),
                        pltpu.VMEM((D, tt), F32)],
        compiler_params=pltpu.CompilerParams(
            dimension_semantics=("arbitrary", "arbitrary"), vmem_limit_bytes=VMEM_LIMIT),
        name="peer",
    )(x, g, wqt, sk, u, vt, gf)


def _pack_in_weights(w_in):
    aq, ak, av, cq, ckv, kpe, zc, dq, dk, dv = jnp.split(
        w_in, np.cumsum([256, 256, 256, 256, 128, 32, 512, 256, 256])[:].tolist(), axis=1)
    kpe_rep = jnp.zeros((D_MODEL, 4, 128), w_in.dtype).at[:, :, 64:96].set(kpe[:, None, :])
    return jnp.concatenate(
        [aq, ak, av, cq, ckv, kpe_rep.reshape(D_MODEL, 512), zc, dq, dk, dv], axis=1).astype(BF)


def _pack_mla_weights(w_uq, w_ukv):
    uq = jnp.pad(w_uq.reshape(256, 4, 96), ((0, 0), (0, 0), (0, 32))).reshape(256, 512)
    ukv = w_ukv.reshape(128, 4, 128)
    uk = jnp.pad(ukv[:, :, :64], ((0, 0), (0, 0), (0, 64))).reshape(128, 512)
    uv = ukv[:, :, 64:].reshape(128, 256)
    return uq.astype(BF), uk.astype(BF), uv.astype(BF)


def _pack_expert_out(v):
    n, d = v.shape
    return v.reshape(n // 256, 256, d).transpose(0, 2, 1).astype(BF)


def kernel(x, w_in, w_out, norm_mix, norm_ffn, diff_lambda, diff_head_norm, mla_q_norm, mla_kv_norm, mla_w_uq, mla_w_ukv, mla_out_norm, sgu_v_norm, sgu_w, sgu_b, sgu_out_norm, dil_out_norm, peer_w_q, peer_sub_keys, peer_u, peer_v, final_norm):
    B, S, D = x.shape
    depth = w_in.shape[0]
    T = B * S
    tm = min(512, S)
    ta, tb = min(DIFF_TILE, S), min(MLA_TILE, S)
    tt = min(512, T)
    row = lambda a: a.reshape(1, -1)

    tabs = (_rope_tables(S, 8, 32, 0) + _rope_tables(S, 32, 128, 64) + _rope_tables(S, 16, 64, 0))

    for l in range(depth):
        lam_init = 0.8 - 0.6 * math.exp(-0.3 * l)
        w_cat = _pack_in_weights(w_in[l])
        wuq, wuk, wuv = _pack_mla_weights(mla_w_uq[l], mla_w_ukv[l])
        sgub = jnp.repeat(sgu_b[l].T, 64, axis=1)
        qa, ka, va, qb, kb, vb, oc, qd, kd, vd = _inproj(
            x, row(norm_mix[l]), w_cat, tabs, row(mla_q_norm[l]), row(mla_kv_norm[l]),
            wuq, wuk, wuv, row(sgu_v_norm[l]), sgu_w[l], sgub, row(sgu_out_norm[l]), tm=tm)

        oa = _attn_a(diff_lambda[l], row(diff_head_norm[l]), qa, ka, va, t=ta, lam_init=lam_init)
        ob = _attn_b(row(mla_out_norm[l]), qb, kb, vb, t=tb)

        od = _dil(row(dil_out_norm[l]), qd, kd, vd, td=DIL_TILE)

        x2 = _outproj(x.reshape(T, D), oa.reshape(T, GROUP), ob.reshape(T, GROUP),
                      oc.reshape(T, GROUP), od.reshape(T, GROUP),
                      w_out[l].astype(BF), tm=min(512, T))

        x2 = _peer(x2, row(norm_ffn[l]), peer_w_q[l].T.astype(BF), peer_sub_keys[l].astype(BF),
                   peer_u[l].astype(BF), _pack_expert_out(peer_v[l]), row(final_norm),
                   tt=tt, final=(l == depth - 1))
        x = x2.reshape(B, S, D)
    return x
```

```python
import functools
import math

import numpy as np
import jax
import jax.numpy as jnp
from jax import lax
from jax.experimental import pallas as pl
from jax.experimental.pallas import tpu as pltpu

D_MODEL = 1024
GROUP = 256
ROPE_THETA = 500000.0
NEG = -1e30
EPS = 1e-6
LANES = 128
LOG2E = math.log2(math.e)

N_KEYS = 128
PEER_HEADS = 8
PEER_TOP = 16

BF = jnp.bfloat16
F32 = jnp.float32

C_AQ, C_AK, C_AV, C_CQ, C_CKV, C_KPE, C_ZC, C_DQ, C_DK, C_DV, C_END = (
    0, 256, 512, 768, 1024, 1152, 1664, 2176, 2432, 2688, 2944)

VMEM_LIMIT = 56 * 1024 * 1024


def _rms(x, g):
    return x * lax.rsqrt(jnp.mean(x * x, axis=-1, keepdims=True) + EPS) * g


def _nt_dot(a, b):
    return lax.dot_general(a, b, (((1,), (1,)), ((), ())), preferred_element_type=F32)


def _dot(a, b):
    return jnp.dot(a, b, preferred_element_type=F32)


def _rope_tables(seq, n_rot, period, offset):
    half = n_rot // 2
    pos = jnp.arange(seq, dtype=F32)
    inv = ROPE_THETA ** (-jnp.arange(half, dtype=F32) * (2.0 / n_rot))
    ang = pos[:, None] * inv[None, :]
    cos, sin = jnp.cos(ang), jnp.sin(ang)
    g = np.arange(LANES) % period - offset
    rot = (g >= 0) & (g < n_rot)
    idx = np.where(rot, g % half, 0)
    sign = np.where(g < half, -1.0, 1.0).astype(np.float32)
    cos_t = jnp.where(rot[None, :], cos[:, idx], 1.0)
    sin_t = jnp.where(rot[None, :], sin[:, idx] * sign[None, :], 0.0)
    return cos_t.astype(F32), sin_t.astype(F32)


def _rope_apply(x, cos_t, sin_t, n_rot, period, offset):
    half = n_rot // 2
    lane = lax.broadcasted_iota(jnp.int32, (1, LANES), 1)
    first = (lane % period - offset) < half
    outs = []
    for c in range(x.shape[1] // LANES):
        xc = x[:, c * LANES:(c + 1) * LANES]
        fwd = pltpu.roll(xc, LANES - half, 1)
        bwd = pltpu.roll(xc, half, 1)
        outs.append(xc * cos_t + jnp.where(first, fwd, bwd) * sin_t)
    return outs[0] if len(outs) == 1 else jnp.concatenate(outs, axis=1)


VROWS = 80


def _value_rows(vt):
    one = (lax.broadcasted_iota(jnp.int32, (VROWS - 64, vt.shape[1]), 0) == 0).astype(F32)
    parts = []
    for h in range(4):
        parts += [vt[64 * h:64 * (h + 1)], one]
    return jnp.concatenate(parts, axis=0).astype(BF)


def _inproj_body(x_ref, gmix_ref, w_ref, ca_ref, sa_ref, cb_ref, sb_ref, cd_ref, sd_ref,
                 gq_ref, gkv_ref, wuq_ref, wuk_ref, wuv_ref,
                 gsv_ref, sguw_ref, sgub_ref, gso_ref,
                 qa_ref, ka_ref, va_ref, qb_ref, kb_ref, vb_ref, oc_ref,
                 qd_ref, kd_ref, vd_ref, *, tm):
    x = x_ref[0]
    h = _rms(x, gmix_ref[...]).astype(BF)

    def proj(a, b):
        return _dot(h, w_ref[:, a:b])

    ca, sa = ca_ref[...], sa_ref[...]
    aq = _rope_apply(proj(C_AQ, C_AK), ca, sa, 8, 32, 0)
    qa_ref[0] = (aq * (32.0 ** -0.5 * LOG2E)).T.astype(BF)
    ka_ref[0] = _rope_apply(proj(C_AK, C_AV), ca, sa, 8, 32, 0).astype(BF)
    va_ref[0] = _value_rows(proj(C_AV, C_CQ).T)

    cb, sb = cb_ref[...], sb_ref[...]
    cq = _rms(proj(C_CQ, C_CKV), gq_ref[...]).astype(BF)
    qb = _rope_apply(_dot(cq, wuq_ref[...]), cb, sb, 32, 128, 64)
    qb_ref[0] = (qb * (96.0 ** -0.5 * LOG2E)).T.astype(BF)
    ckv = _rms(proj(C_CKV, C_KPE), gkv_ref[...]).astype(BF)
    kpe = _rope_apply(proj(C_KPE, C_ZC), cb, sb, 32, 128, 64)
    kb_ref[0] = (_dot(ckv, wuk_ref[...]) + kpe).astype(BF)
    vb_ref[0] = _value_rows(_dot(ckv, wuv_ref[...]).T)

    zc = jax.nn.gelu(proj(C_ZC, C_DQ))
    u = zc[:, :GROUP]
    vn = _rms(zc[:, GROUP:], gsv_ref[...]).astype(BF)
    r = lax.broadcasted_iota(jnp.int32, (128, 128), 0)
    c = lax.broadcasted_iota(jnp.int32, (128, 128), 1)
    wcat = jnp.concatenate(
        [jnp.where(r >= c, sguw_ref[g], 0.0).astype(BF) for g in range(4)], axis=1)
    lane = lax.broadcasted_iota(jnp.int32, (1, GROUP), 1)
    bias = sgub_ref[...]
    gso = gso_ref[...]
    for ch in range(tm // 128):
        vc = vn[ch * 128:(ch + 1) * 128]
        vst = jnp.concatenate(
            [jnp.where(lane // 64 == g, vc, jnp.zeros_like(vc)) for g in range(4)], axis=0)
        sv = _dot(wcat, vst) + bias
        oc = u[ch * 128:(ch + 1) * 128] * sv
        oc_ref[0, ch * 128:(ch + 1) * 128, :] = _rms(oc, gso).astype(BF)

    cd, sd = cd_ref[...], sd_ref[...]
    dq = _rope_apply(proj(C_DQ, C_DK), cd, sd, 16, 64, 0)
    dk = _rope_apply(proj(C_DK, C_DV), cd, sd, 16, 64, 0)
    dv = proj(C_DV, C_END)
    for hf in range(2):
        qd_ref[0, hf] = dq[:, hf * LANES:(hf + 1) * LANES] * 0.125
        kd_ref[0, hf] = dk[:, hf * LANES:(hf + 1) * LANES]
        vd_ref[0, hf] = dv[:, hf * LANES:(hf + 1) * LANES]


def _inproj(x, gmix, w_cat, tabs, gq, gkv, wuq, wuk, wuv, gsv, sguw, sgub, gso, *, tm):
    B, S, D = x.shape
    ns = S // tm
    full = lambda shape: pl.BlockSpec(shape, lambda s, b: (0,) * len(shape))
    tab = pl.BlockSpec((tm, LANES), lambda s, b: (s, 0))
    seq = lambda w: pl.BlockSpec((1, tm, w), lambda s, b: (b, s, 0))
    seq_t = lambda w: pl.BlockSpec((1, w, tm), lambda s, b: (b, 0, s))
    halves = pl.BlockSpec((1, 2, tm, LANES), lambda s, b: (b, 0, s, 0))
    out_w = [256, 256, 4 * VROWS, 512, 512, 4 * VROWS, 256]
    transposed = [True, False, True, True, False, True, False]
    return pl.pallas_call(
        functools.partial(_inproj_body, tm=tm),
        grid=(ns, B),
        in_specs=[seq(D), full((1, D)), full((D, C_END))] + [tab] * 6 + [
            full((1, 256)), full((1, 128)), full((256, 512)), full((128, 512)), full((128, 256)),
            full((1, 256)), full((4, 128, 128)), full((128, 256)), full((1, 256))],
        out_specs=[seq_t(w) if tr else seq(w) for w, tr in zip(out_w, transposed)] + [halves] * 3,
        out_shape=[jax.ShapeDtypeStruct((B, w, S) if tr else (B, S, w), BF)
                   for w, tr in zip(out_w, transposed)]
        + [jax.ShapeDtypeStruct((B, 2, S, LANES), F32)] * 3,
        compiler_params=pltpu.CompilerParams(
            dimension_semantics=("arbitrary", "arbitrary"), vmem_limit_bytes=VMEM_LIMIT),
        name="inproj",
    )(x, gmix, w_cat, *tabs, gq, gkv, wuq, wuk, wuv, gsv, sguw, sgub, gso)


def _head_expand(cols, lane, width):
    out = cols[-1]
    for h in range(len(cols) - 2, -1, -1):
        out = jnp.where(lane // width == h, cols[h], out)
    return out


def _vstack(v, lane):
    return jnp.concatenate(
        [jnp.where(lane // 64 == h, v, jnp.zeros_like(v)) for h in range(4)], axis=0)


DIFF_TILE = 1024
MLA_TILE = 1024
QB = 512


def _online_softmax_t(st, m_ref, j, cs):
    m_prev = m_ref[j:j + 1, cs]
    m_new = jnp.maximum(m_prev, jnp.max(st, axis=0, keepdims=True))
    alpha = jnp.exp2(m_prev - m_new)
    et = jnp.exp2(st - m_new)
    m_ref[j:j + 1, cs] = m_new
    return et.astype(BF), alpha


def _attn_jobs(ns, diag, heads):
    return [(*hd, qc, kc) for kc in range(ns) for qc in range(ns) for hd in heads
            if not (diag and kc > qc)]


def _causal_keep():
    return (lax.broadcasted_iota(jnp.int32, (QB, QB), 0)
            <= lax.broadcasted_iota(jnp.int32, (QB, QB), 1))


def _causal_pairs(n):
    pairs = [(i, j) for i in range(n) for j in range(i + 1)]
    return (jnp.asarray([p[0] for p in pairs], jnp.int32),
            jnp.asarray([p[1] for p in pairs], jnp.int32))


def _attn_a_body(qtab_ref, ktab_ref, lam_ref, gh_ref, q_ref, k_ref, v_ref, o_ref,
                 qs_ref, m_ref, acc_ref, *, t, lam_init):
    qi = qtab_ref[pl.program_id(1)]
    ki = ktab_ref[pl.program_id(1)]

    @pl.when(ki == 0)
    def _init():
        m_ref[...] = jnp.full(m_ref.shape, NEG, F32)
        acc_ref[...] = jnp.zeros(acc_ref.shape, F32)
        qt = q_ref[0]
        row = lax.broadcasted_iota(jnp.int32, (GROUP, 1), 0)
        for j in range(8):
            qs_ref[j] = jnp.where(row // 32 == j, qt, jnp.zeros_like(qt))

    def step(diag):
        jobs = _attn_jobs(t // QB, diag, [(mp, h) for mp in range(2) for h in range(4)])
        kblk = [k_ref[0, kc * QB:(kc + 1) * QB, :] for kc in range(t // QB)]
        keep = _causal_keep() if diag else None

        def qk(mp, h, qc, kc):
            return _dot(kblk[kc], qs_ref[2 * h + mp, :, qc * QB:(qc + 1) * QB])

        def pv(mp, h, qc, kc, pt, alpha):
            hs, cs = slice(VROWS * h, VROWS * (h + 1)), slice(qc * QB, (qc + 1) * QB)
            acc_ref[mp, hs, cs] = (acc_ref[mp, hs, cs] * alpha
                                   + _dot(v_ref[0, hs, kc * QB:(kc + 1) * QB], pt))

        st_next = qk(*jobs[0])
        pending = None
        for n, (mp, h, qc, kc) in enumerate(jobs):
            st = st_next
            if n + 1 < len(jobs):
                st_next = qk(*jobs[n + 1])
            if diag and kc == qc:
                st = jnp.where(keep, st, NEG)
            pt, alpha = _online_softmax_t(st, m_ref, 2 * h + mp, slice(qc * QB, (qc + 1) * QB))
            if pending is not None:
                pv(*pending)
            pending = (mp, h, qc, kc, pt, alpha)
        pv(*pending)

    @pl.when(ki < qi)
    def _off():
        step(False)

    @pl.when(ki == qi)
    def _diag():
        step(True)
        lf = lam_ref[...]
        lam = (jnp.exp(jnp.sum(lf[0:1] * lf[1:2], axis=-1, keepdims=True))
               - jnp.exp(jnp.sum(lf[2:3] * lf[3:4], axis=-1, keepdims=True)) + lam_init)
        rows = []
        for h in range(4):
            hs, ls = slice(VROWS * h, VROWS * h + 64), slice(VROWS * h + 64, VROWS * h + 65)
            oh = (acc_ref[0, hs, :] * (1.0 / acc_ref[0, ls, :])
                  - lam * (acc_ref[1, hs, :] * (1.0 / acc_ref[1, ls, :])))
            ms = jnp.mean(oh * oh, axis=0, keepdims=True)
            rows.append(oh * lax.rsqrt(ms + EPS))
        ot = jnp.concatenate(rows, axis=0)
        o_ref[0] = (ot.T * gh_ref[...] * (1.0 - lam_init)).astype(BF)


def _attn_a(lam_p, gh, q, k, v, *, t, lam_init):
    B, S, _ = k.shape
    n = S // t
    qtab, ktab = _causal_pairs(n)
    grid_spec = pltpu.PrefetchScalarGridSpec(
        num_scalar_prefetch=2,
        grid=(B, qtab.shape[0]),
        in_specs=[pl.BlockSpec((4, 32), lambda b, p, qt, kt: (0, 0)),
                  pl.BlockSpec((1, GROUP), lambda b, p, qt, kt: (0, 0)),
                  pl.BlockSpec((1, GROUP, t), lambda b, p, qt, kt: (b, 0, qt[p])),
                  pl.BlockSpec((1, t, GROUP), lambda b, p, qt, kt: (b, kt[p], 0)),
                  pl.BlockSpec((1, 4 * VROWS, t), lambda b, p, qt, kt: (b, 0, kt[p]))],
        out_specs=pl.BlockSpec((1, t, GROUP), lambda b, p, qt, kt: (b, qt[p], 0)),
        scratch_shapes=[pltpu.VMEM((8, GROUP, t), BF),
                        pltpu.VMEM((8, t), F32),
                        pltpu.VMEM((2, 4 * VROWS, t), F32)])
    return pl.pallas_call(
        functools.partial(_attn_a_body, t=t, lam_init=lam_init),
        grid_spec=grid_spec,
        out_shape=jax.ShapeDtypeStruct((B, S, GROUP), BF),
        compiler_params=pltpu.CompilerParams(
            dimension_semantics=("arbitrary", "arbitrary"), vmem_limit_bytes=VMEM_LIMIT),
        name="attn_diff",
    )(qtab, ktab, lam_p, gh, q, k, v)


def _attn_b_body(qtab_ref, ktab_ref, g_ref, q_ref, k_ref, v_ref, o_ref, m_ref, acc_ref, *, t):
    qi = qtab_ref[pl.program_id(1)]
    ki = ktab_ref[pl.program_id(1)]

    @pl.when(ki == 0)
    def _init():
        m_ref[...] = jnp.full(m_ref.shape, NEG, F32)
        acc_ref[...] = jnp.zeros(acc_ref.shape, F32)

    def step(diag):
        jobs = _attn_jobs(t // QB, diag, [(h,) for h in range(4)])
        kblk = [k_ref[0, kc * QB:(kc + 1) * QB, :] for kc in range(t // QB)]
        keep = _causal_keep() if diag else None

        def qk(h, qc, kc):
            return _dot(kblk[kc][:, h * 128:(h + 1) * 128],
                        q_ref[0, h * 128:(h + 1) * 128, qc * QB:(qc + 1) * QB])

        def pv(h, qc, kc, pt, alpha):
            hs, cs = slice(VROWS * h, VROWS * (h + 1)), slice(qc * QB, (qc + 1) * QB)
            acc_ref[hs, cs] = acc_ref[hs, cs] * alpha + _dot(v_ref[0, hs, kc * QB:(kc + 1) * QB], pt)

        st_next = qk(*jobs[0])
        pending = None
        for n, (h, qc, kc) in enumerate(jobs):
            st = st_next
            if n + 1 < len(jobs):
                st_next = qk(*jobs[n + 1])
            if diag and kc == qc:
                st = jnp.where(keep, st, NEG)
            pt, alpha = _online_softmax_t(st, m_ref, h, slice(qc * QB, (qc + 1) * QB))
            if pending is not None:
                pv(*pending)
            pending = (h, qc, kc, pt, alpha)
        pv(*pending)

    @pl.when(ki < qi)
    def _off():
        step(False)

    @pl.when(ki == qi)
    def _diag():
        step(True)
        rows = [acc_ref[VROWS * h:VROWS * h + 64, :] * (1.0 / acc_ref[VROWS * h + 64:VROWS * h + 65, :])
                for h in range(4)]
        ot = jnp.concatenate(rows, axis=0)
        o_ref[0] = _rms(ot.T, g_ref[...]).astype(BF)


def _attn_b(g, q, k, v, *, t):
    B, S, _ = k.shape
    n = S // t
    qtab, ktab = _causal_pairs(n)
    grid_spec = pltpu.PrefetchScalarGridSpec(
        num_scalar_prefetch=2,
        grid=(B, qtab.shape[0]),
        in_specs=[pl.BlockSpec((1, GROUP), lambda b, p, qt, kt: (0, 0)),
                  pl.BlockSpec((1, 512, t), lambda b, p, qt, kt: (b, 0, qt[p])),
                  pl.BlockSpec((1, t, 512), lambda b, p, qt, kt: (b, kt[p], 0)),
                  pl.BlockSpec((1, 4 * VROWS, t), lambda b, p, qt, kt: (b, 0, kt[p]))],
        out_specs=pl.BlockSpec((1, t, GROUP), lambda b, p, qt, kt: (b, qt[p], 0)),
        scratch_shapes=[pltpu.VMEM((4, t), F32),
                        pltpu.VMEM((4 * VROWS, t), F32)])
    return pl.pallas_call(
        functools.partial(_attn_b_body, t=t),
        grid_spec=grid_spec,
        out_shape=jax.ShapeDtypeStruct((B, S, GROUP), BF),
        compiler_params=pltpu.CompilerParams(
            dimension_semantics=("arbitrary", "arbitrary"), vmem_limit_bytes=VMEM_LIMIT),
        name="attn_mla",
    )(qtab, ktab, g, q, k, v)


DIL_TILE = 2048
DILATIONS = (1, 4, 16)


def _dil_body(g_ref, q_ref, kp_ref, kc_ref, vp_ref, vc_ref, o_ref, ob_ref, lb_ref, *, td):
    i = pl.program_id(1)
    lane = lax.broadcasted_iota(jnp.int32, (1, GROUP), 1)
    a = lax.broadcasted_iota(jnp.int32, (128, 256), 0)
    c = lax.broadcasted_iota(jnp.int32, (128, 256), 1)
    dist = jnp.where(c - a >= 0, c - a, 1000)

    def rows(ref, start, d):
        idx = pl.ds(start, 128) if d == 1 else pl.ds(start, 128, stride=d)
        return jnp.concatenate([ref.at[0, 0][idx, :], ref.at[0, 1][idx, :]], axis=1).astype(BF)

    for bi, d in enumerate(DILATIONS):
        def block(n, carry, bi=bi, d=d):
            if d == 1:
                r, blk = 0, n
            elif td // d == 128:
                r, blk = n, 0
            else:
                r, blk = n % d, n // d
            q0 = r + d * 128 * blk
            if d == 1:
                q0 = pl.multiple_of(q0, 128)
            q = rows(q_ref, q0, d)
            k_hi, v_hi = rows(kc_ref, q0, d), rows(vc_ref, q0, d)
            lo_prev = td - d * 128 + r
            if isinstance(blk, int):
                first = True
                k_lo, v_lo = rows(kp_ref, lo_prev, d), rows(vp_ref, lo_prev, d)
            else:
                first = blk == 0
                lo_cur = jnp.maximum(q0 - d * 128, 0)
                if d == 1:
                    lo_cur = pl.multiple_of(lo_cur, 128)
                k_lo = jnp.where(first, rows(kp_ref, lo_prev, d), rows(kc_ref, lo_cur, d))
                v_lo = jnp.where(first, rows(vp_ref, lo_prev, d), rows(vc_ref, lo_cur, d))
            kwin = jnp.concatenate([k_lo, k_hi], axis=0)
            vst = _vstack(jnp.concatenate([v_lo, v_hi], axis=0), lane)
            cmin = jnp.where(jnp.logical_and(first, i == 0), 128, 0)
            ok = jnp.where(c >= cmin, dist, 1000) <= 128
            es, ils, lses = [], [], []
            for h in range(4):
                s = _nt_dot(jnp.where(lane // 64 == h, q, jnp.zeros_like(q)), kwin)
                s = jnp.where(ok, s, NEG)
                m = jnp.max(s, axis=-1, keepdims=True)
                ex = jnp.exp(s - m)
                den = jnp.sum(ex, axis=-1, keepdims=True)
                es.append(ex.astype(BF))
                ils.append(1.0 / den)
                lses.append(m + jnp.log(den))
            o = _dot(jnp.concatenate(es, axis=1), vst) * _head_expand(ils, lane, 64)
            lse = _head_expand(lses, lane, 64)
            idx = pl.ds(q0, 128) if d == 1 else pl.ds(q0, 128, stride=d)
            for hf in range(2):
                ob_ref.at[bi, hf][idx, :] = o[:, hf * LANES:(hf + 1) * LANES]
                lb_ref.at[bi, hf][idx, :] = lse[:, hf * LANES:(hf + 1) * LANES]
            return carry

        lax.fori_loop(0, td // 128, block, 0, unroll=8)

    for ch in range(td // 256):
        rs = slice(ch * 256, (ch + 1) * 256)
        halves = []
        for hf in range(2):
            ls = [lb_ref[bi, hf, rs, :] for bi in range(3)]
            mx = jnp.maximum(jnp.maximum(ls[0], ls[1]), ls[2])
            ws = [jnp.exp(l - mx) for l in ls]
            num = ws[0] * ob_ref[0, hf, rs, :] + ws[1] * ob_ref[1, hf, rs, :] + ws[2] * ob_ref[2, hf, rs, :]
            halves.append(num / (ws[0] + ws[1] + ws[2]))
        o_ref[0, rs, :] = _rms(jnp.concatenate(halves, axis=1), g_ref[...]).astype(BF)


def _dil(g, q, k, v, *, td):
    B, _, S, _ = q.shape
    cur = pl.BlockSpec((1, 2, td, LANES), lambda b, i: (b, 0, i, 0))
    prev = pl.BlockSpec((1, 2, td, LANES), lambda b, i: (b, 0, jnp.maximum(i - 1, 0), 0))
    return pl.pallas_call(
        functools.partial(_dil_body, td=td),
        grid=(B, S // td),
        in_specs=[pl.BlockSpec((1, GROUP), lambda b, i: (0, 0)), cur, prev, cur, prev, cur],
        out_specs=pl.BlockSpec((1, td, GROUP), lambda b, i: (b, i, 0)),
        out_shape=jax.ShapeDtypeStruct((B, S, GROUP), BF),
        scratch_shapes=[pltpu.VMEM((3, 2, td, LANES), F32),
                        pltpu.VMEM((3, 2, td, LANES), F32)],
        compiler_params=pltpu.CompilerParams(
            dimension_semantics=("arbitrary", "arbitrary"), vmem_limit_bytes=VMEM_LIMIT),
        name="dilated",
    )(g, q, k, k, v, v)


def _outproj_body(x_ref, oa_ref, ob_ref, oc_ref, od_ref, w_ref, y_ref):
    y = _dot(oa_ref[...], w_ref[0:256, :])
    y += _dot(ob_ref[...], w_ref[256:512, :])
    y += _dot(oc_ref[...], w_ref[512:768, :])
    y += _dot(od_ref[...], w_ref[768:1024, :])
    y_ref[...] = x_ref[...] + y


def _outproj(x, oa, ob, oc, od, w_out, *, tm):
    T, D = x.shape
    row = lambda w: pl.BlockSpec((tm, w), lambda i: (i, 0))
    full = lambda shape: pl.BlockSpec(shape, lambda i: (0,) * len(shape))
    return pl.pallas_call(
        _outproj_body,
        grid=(T // tm,),
        in_specs=[row(D)] + [row(GROUP)] * 4 + [full((D, D))],
        out_specs=row(D),
        out_shape=jax.ShapeDtypeStruct((T, D), F32),
        compiler_params=pltpu.CompilerParams(
            dimension_semantics=("arbitrary",), vmem_limit_bytes=VMEM_LIMIT),
        name="outproj",
    )(x, oa, ob, oc, od, w_out)


PEER_EB = 2048
GELU_C1 = math.sqrt(2.0 / math.pi)
GELU_C2 = 0.044715 * GELU_C1
N_CAND = PEER_TOP + 1


def _peer_body(x_ref, g_ref, wqt_ref, sk_ref, u_ref, vt_ref, gf_ref, o_ref,
               ht_ref, st_ref, e2_ref, th_ref, cw_ref, hw_ref, acc_ref,
               *, tt, final):
    e = pl.program_id(1)
    nch = tt // LANES

    @pl.when(e == 0)
    def _prologue():
        h = _rms(x_ref[...], g_ref[...])
        ht_ref[...] = h.T.astype(BF)
        qt = _dot(wqt_ref[...], ht_ref[...]).astype(BF)
        for hh in range(PEER_HEADS):
            for c in range(2):
                r0 = hh * N_KEYS + c * 64
                sc = _dot(sk_ref[c], qt[r0:r0 + 64, :])
                for tc in range(nch):
                    blk = sc[:, tc * LANES:(tc + 1) * LANES]
                    st_ref.at[tc][pl.ds(c * 1024 + hh, N_KEYS, stride=PEER_HEADS), :] = blk
                    if c == 1:
                        st_ref[tc, 2048 + hh * N_KEYS:2048 + (hh + 1) * N_KEYS, :] = blk
        acc_ref[...] = jnp.zeros(acc_ref.shape, F32)

        def top_vals(load, n_rows, n_out):
            top = []
            for r in range(n_rows):
                v = load(r)
                for q in range(len(top)):
                    top[q], v = jnp.maximum(top[q], v), jnp.minimum(top[q], v)
                if len(top) < n_out:
                    top.append(v)
            return top

        def token_chunk(tc, carry):
            ls = pl.ds(pl.multiple_of(tc * LANES, LANES), LANES)
            v1 = top_vals(lambda r: st_ref[tc, r * 8:(r + 1) * 8, :], N_KEYS, N_CAND)
            v2 = top_vals(lambda r: st_ref[tc, 1024 + r * 8:1024 + (r + 1) * 8, :], N_KEYS, N_CAND)
            cands = [v1[a] + v2[b] for a in range(N_CAND) for b in range(N_CAND)
                     if (a + 1) * (b + 1) <= N_CAND]
            top = top_vals(lambda r: cands[r], len(cands), N_CAND)
            z = jnp.ones_like(top[0])
            for r in range(1, PEER_TOP):
                z = z + jnp.exp(top[r] - top[0])
            tau = 0.5 * (top[PEER_TOP - 1] + top[PEER_TOP])
            iz = 1.0 / z
            for r in range(N_KEYS):
                s1 = st_ref[tc, r * 8:(r + 1) * 8, :]
                th_ref[r, :, ls] = tau - s1
                cw_ref[r, :, ls] = jnp.exp(s1 - v1[0]) * (0.5 * iz)
            for hh in range(PEER_HEADS):
                rows = slice(2048 + hh * 128, 2048 + (hh + 1) * 128)
                e2_ref[hh * 128:(hh + 1) * 128, ls] = jnp.exp(st_ref[tc, rows, :] - v2[0][hh:hh + 1, :])
            return carry

        lax.fori_loop(0, nch, token_chunk, 0)

    nblk = PEER_EB // 256

    def pre_act(b):
        return _dot(u_ref[b * 256:(b + 1) * 256, :], ht_ref[...])

    def gate(b, a):
        i0 = e * (PEER_EB // N_KEYS) + 2 * b
        th = [th_ref[i0], th_ref[i0 + 1]]
        cw = [cw_ref[i0], cw_ref[i0 + 1]]
        for tc in range(nch):
            ls = slice(tc * LANES, (tc + 1) * LANES)
            for sub in range(2):
                w = [None, None]
                for hh in range(PEER_HEADS):
                    r0 = hh * N_KEYS + sub * 64
                    s2 = st_ref[tc, 2048 + r0:2048 + r0 + 64, :]
                    e2 = e2_ref[r0:r0 + 64, ls]
                    for half in range(2):
                        sel = jnp.where(s2 >= th[half][hh:hh + 1, ls], e2, 0.0) * cw[half][hh:hh + 1, ls]
                        w[half] = sel if w[half] is None else w[half] + sel
                for half in range(2):
                    r1 = half * N_KEYS + sub * 64
                    x = a[r1:r1 + 64, ls]
                    xw = x * w[half]
                    t = jnp.tanh(x * (GELU_C1 + GELU_C2 * (x * x)))
                    hw_ref[b % 2, r1:r1 + 64, ls] = (xw + xw * t).astype(BF)

    a = pre_act(0)
    for b in range(nblk):
        a_next = pre_act(b + 1) if b + 1 < nblk else None
        gate(b, a)
        if b > 0:
            acc_ref[...] += _dot(vt_ref[b - 1], hw_ref[(b - 1) % 2])
        a = a_next
    acc_ref[...] += _dot(vt_ref[nblk - 1], hw_ref[(nblk - 1) % 2])

    @pl.when(e == pl.num_programs(1) - 1)
    def _finish():
        y = x_ref[...] + acc_ref[...].T
        if final:
            y = _rms(y, gf_ref[...])
        o_ref[...] = y


def _peer(x, g, wqt, sk, u, vt, gf, *, tt, final):
    T, D = x.shape
    ne = u.shape[0] // PEER_EB
    const = lambda shape: pl.BlockSpec(shape, lambda i, e: (0,) * len(shape),
                                       pipeline_mode=pl.Buffered(1))
    return pl.pallas_call(
        functools.partial(_peer_body, tt=tt, final=final),
        grid=(T // tt, ne),
        in_specs=[pl.BlockSpec((tt, D), lambda i, e: (i, 0)),
                  const((1, D)), const((D, D)), const((2, N_KEYS, 64)),
                  pl.BlockSpec((PEER_EB, D), lambda i, e: (e, 0)),
                  pl.BlockSpec((PEER_EB // 256, D, 256), lambda i, e: (e, 0, 0)),
                  const((1, D))],
        out_specs=pl.BlockSpec((tt, D), lambda i, e: (i, 0)),
        out_shape=jax.ShapeDtypeStruct((T, D), F32),
        scratch_shapes=[pltpu.VMEM((D, tt), BF),
                        pltpu.VMEM((tt // LANES, 3 * 1024, LANES), F32),
                        pltpu.VMEM((1024, tt), F32),
                        pltpu.VMEM((N_KEYS, PEER_HEADS, tt), F32),
                        pltpu.VMEM((N_KEYS, PEER_HEADS, tt), F32),
                        pltpu.VMEM((2, 256, tt), BF),
                        pltpu.VMEM((D, tt), F32)],
        compiler_params=pltpu.CompilerParams(
            dimension_semantics=("arbitrary", "arbitrary"), vmem_limit_bytes=VMEM_LIMIT),
        name="peer",
    )(x, g, wqt, sk, u, vt, gf)


def _pack_in_weights(w_in):
    aq, ak, av, cq, ckv, kpe, zc, dq, dk, dv = jnp.split(
        w_in, np.cumsum([256, 256, 256, 256, 128, 32, 512, 256, 256])[:].tolist(), axis=1)
    kpe_rep = jnp.zeros((D_MODEL, 4, 128), w_in.dtype).at[:, :, 64:96].set(kpe[:, None, :])
    return jnp.concatenate(
        [aq, ak, av, cq, ckv, kpe_rep.reshape(D_MODEL, 512), zc, dq, dk, dv], axis=1).astype(BF)


def _pack_mla_weights(w_uq, w_ukv):
    uq = jnp.pad(w_uq.reshape(256, 4, 96), ((0, 0), (0, 0), (0, 32))).reshape(256, 512)
    ukv = w_ukv.reshape(128, 4, 128)
    uk = jnp.pad(ukv[:, :, :64], ((0, 0), (0, 0), (0, 64))).reshape(128, 512)
    uv = ukv[:, :, 64:].reshape(128, 256)
    return uq.astype(BF), uk.astype(BF), uv.astype(BF)


def _pack_expert_out(v):
    n, d = v.shape
    return v.reshape(n // 256, 256, d).transpose(0, 2, 1).astype(BF)


def kernel(x, w_in, w_out, norm_mix, norm_ffn, diff_lambda, diff_head_norm, mla_q_norm, mla_kv_norm, mla_w_uq, mla_w_ukv, mla_out_norm, sgu_v_norm, sgu_w, sgu_b, sgu_out_norm, dil_out_norm, peer_w_q, peer_sub_keys, peer_u, peer_v, final_norm):
    B, S, D = x.shape
    depth = w_in.shape[0]
    T = B * S
    tm = min(1024, S)
    ta, tb = min(DIFF_TILE, S), min(MLA_TILE, S)
    tt = min(512, T)
    row = lambda a: a.reshape(1, -1)

    tabs = (_rope_tables(S, 8, 32, 0) + _rope_tables(S, 32, 128, 64) + _rope_tables(S, 16, 64, 0))

    for l in range(depth):
        lam_init = 0.8 - 0.6 * math.exp(-0.3 * l)
        w_cat = _pack_in_weights(w_in[l])
        wuq, wuk, wuv = _pack_mla_weights(mla_w_uq[l], mla_w_ukv[l])
        sgub = jnp.repeat(sgu_b[l].T, 64, axis=1)
        qa, ka, va, qb, kb, vb, oc, qd, kd, vd = _inproj(
            x, row(norm_mix[l]), w_cat, tabs, row(mla_q_norm[l]), row(mla_kv_norm[l]),
            wuq, wuk, wuv, row(sgu_v_norm[l]), sgu_w[l], sgub, row(sgu_out_norm[l]), tm=tm)

        oa = _attn_a(diff_lambda[l], row(diff_head_norm[l]), qa, ka, va, t=ta, lam_init=lam_init)
        ob = _attn_b(row(mla_out_norm[l]), qb, kb, vb, t=tb)

        od = _dil(row(dil_out_norm[l]), qd, kd, vd, td=DIL_TILE)

        x2 = _outproj(x.reshape(T, D), oa.reshape(T, GROUP), ob.reshape(T, GROUP),
                      oc.reshape(T, GROUP), od.reshape(T, GROUP),
                      w_out[l].astype(BF), tm=min(512, T))

        x2 = _peer(x2, row(norm_ffn[l]), peer_w_q[l].T.astype(BF), peer_sub_keys[l].astype(BF),
                   peer_u[l].astype(BF), _pack_expert_out(peer_v[l]), row(final_norm),
                   tt=tt, final=(l == depth - 1))
        x = x2.reshape(B, S, D)
    return x
```

```python
import functools
import math

import numpy as np
import jax
import jax.numpy as jnp
from jax import lax
from jax.experimental import pallas as pl
from jax.experimental.pallas import tpu as pltpu

D_MODEL = 1024
GROUP = 256
ROPE_THETA = 500000.0
NEG = -1e30
EPS = 1e-6
LANES = 128
LOG2E = math.log2(math.e)

N_KEYS = 128
PEER_HEADS = 8
PEER_TOP = 16

BF = jnp.bfloat16
F32 = jnp.float32

C_AQ, C_AK, C_AV, C_CQ, C_CKV, C_KPE, C_ZC, C_DQ, C_DK, C_DV, C_END = (
    0, 256, 512, 768, 1024, 1152, 1664, 2176, 2432, 2688, 2944)

VMEM_LIMIT = 56 * 1024 * 1024


def _rms(x, g):
    return x * lax.rsqrt(jnp.mean(x * x, axis=-1, keepdims=True) + EPS) * g


def _nt_dot(a, b):
    return lax.dot_general(a, b, (((1,), (1,)), ((), ())), preferred_element_type=F32)


def _dot(a, b):
    return jnp.dot(a, b, preferred_element_type=F32)


def _rope_tables(seq, n_rot, period, offset):
    half = n_rot // 2
    pos = jnp.arange(seq, dtype=F32)
    inv = ROPE_THETA ** (-jnp.arange(half, dtype=F32) * (2.0 / n_rot))
    ang = pos[:, None] * inv[None, :]
    cos, sin = jnp.cos(ang), jnp.sin(ang)
    g = np.arange(LANES) % period - offset
    rot = (g >= 0) & (g < n_rot)
    idx = np.where(rot, g % half, 0)
    sign = np.where(g < half, -1.0, 1.0).astype(np.float32)
    cos_t = jnp.where(rot[None, :], cos[:, idx], 1.0)
    sin_t = jnp.where(rot[None, :], sin[:, idx] * sign[None, :], 0.0)
    return cos_t.astype(F32), sin_t.astype(F32)


def _rope_apply(x, cos_t, sin_t, n_rot, period, offset):
    half = n_rot // 2
    lane = lax.broadcasted_iota(jnp.int32, (1, LANES), 1)
    first = (lane % period - offset) < half
    outs = []
    for c in range(x.shape[1] // LANES):
        xc = x[:, c * LANES:(c + 1) * LANES]
        fwd = pltpu.roll(xc, LANES - half, 1)
        bwd = pltpu.roll(xc, half, 1)
        outs.append(xc * cos_t + jnp.where(first, fwd, bwd) * sin_t)
    return outs[0] if len(outs) == 1 else jnp.concatenate(outs, axis=1)


VROWS = 80


def _value_rows(vt):
    one = (lax.broadcasted_iota(jnp.int32, (VROWS - 64, vt.shape[1]), 0) == 0).astype(F32)
    parts = []
    for h in range(4):
        parts += [vt[64 * h:64 * (h + 1)], one]
    return jnp.concatenate(parts, axis=0).astype(BF)


def _inproj_body(x_ref, gmix_ref, w_ref, ca_ref, sa_ref, cb_ref, sb_ref, cd_ref, sd_ref,
                 gq_ref, gkv_ref, wuq_ref, wuk_ref, wuv_ref,
                 gsv_ref, sguw_ref, sgub_ref, gso_ref,
                 qa_ref, ka_ref, va_ref, qb_ref, kb_ref, vb_ref, oc_ref,
                 qd_ref, kd_ref, vd_ref, *, tm):
    x = x_ref[0]
    h = _rms(x, gmix_ref[...]).astype(BF)

    def proj(a, b):
        return _dot(h, w_ref[:, a:b])

    ca, sa = ca_ref[...], sa_ref[...]
    aq = _rope_apply(proj(C_AQ, C_AK), ca, sa, 8, 32, 0)
    qa_ref[0] = (aq * (32.0 ** -0.5 * LOG2E)).T.astype(BF)
    ka_ref[0] = _rope_apply(proj(C_AK, C_AV), ca, sa, 8, 32, 0).astype(BF)
    va_ref[0] = _value_rows(proj(C_AV, C_CQ).T)

    cb, sb = cb_ref[...], sb_ref[...]
    cq = _rms(proj(C_CQ, C_CKV), gq_ref[...]).astype(BF)
    qb = _rope_apply(_dot(cq, wuq_ref[...]), cb, sb, 32, 128, 64)
    qb_ref[0] = (qb * (96.0 ** -0.5 * LOG2E)).T.astype(BF)
    ckv = _rms(proj(C_CKV, C_KPE), gkv_ref[...]).astype(BF)
    kpe = _rope_apply(proj(C_KPE, C_ZC), cb, sb, 32, 128, 64)
    kb_ref[0] = (_dot(ckv, wuk_ref[...]) + kpe).astype(BF)
    vb_ref[0] = _value_rows(_dot(ckv, wuv_ref[...]).T)

    zc = jax.nn.gelu(proj(C_ZC, C_DQ))
    u = zc[:, :GROUP]
    vn = _rms(zc[:, GROUP:], gsv_ref[...]).astype(BF)
    r = lax.broadcasted_iota(jnp.int32, (128, 128), 0)
    c = lax.broadcasted_iota(jnp.int32, (128, 128), 1)
    wcat = jnp.concatenate(
        [jnp.where(r >= c, sguw_ref[g], 0.0).astype(BF) for g in range(4)], axis=1)
    lane = lax.broadcasted_iota(jnp.int32, (1, GROUP), 1)
    bias = sgub_ref[...]
    gso = gso_ref[...]
    for ch in range(tm // 128):
        vc = vn[ch * 128:(ch + 1) * 128]
        vst = jnp.concatenate(
            [jnp.where(lane // 64 == g, vc, jnp.zeros_like(vc)) for g in range(4)], axis=0)
        sv = _dot(wcat, vst) + bias
        oc = u[ch * 128:(ch + 1) * 128] * sv
        oc_ref[0, ch * 128:(ch + 1) * 128, :] = _rms(oc, gso).astype(BF)

    cd, sd = cd_ref[...], sd_ref[...]
    dq = _rope_apply(proj(C_DQ, C_DK), cd, sd, 16, 64, 0)
    dk = _rope_apply(proj(C_DK, C_DV), cd, sd, 16, 64, 0)
    dv = proj(C_DV, C_END)
    for hf in range(2):
        qd_ref[0, hf] = dq[:, hf * LANES:(hf + 1) * LANES] * 0.125
        kd_ref[0, hf] = dk[:, hf * LANES:(hf + 1) * LANES]
        vd_ref[0, hf] = dv[:, hf * LANES:(hf + 1) * LANES]


def _inproj(x, gmix, w_cat, tabs, gq, gkv, wuq, wuk, wuv, gsv, sguw, sgub, gso, *, tm):
    B, S, D = x.shape
    ns = S // tm
    full = lambda shape: pl.BlockSpec(shape, lambda s, b: (0,) * len(shape))
    tab = pl.BlockSpec((tm, LANES), lambda s, b: (s, 0))
    seq = lambda w: pl.BlockSpec((1, tm, w), lambda s, b: (b, s, 0))
    seq_t = lambda w: pl.BlockSpec((1, w, tm), lambda s, b: (b, 0, s))
    halves = pl.BlockSpec((1, 2, tm, LANES), lambda s, b: (b, 0, s, 0))
    out_w = [256, 256, 4 * VROWS, 512, 512, 4 * VROWS, 256]
    transposed = [True, False, True, True, False, True, False]
    return pl.pallas_call(
        functools.partial(_inproj_body, tm=tm),
        grid=(ns, B),
        in_specs=[seq(D), full((1, D)), full((D, C_END))] + [tab] * 6 + [
            full((1, 256)), full((1, 128)), full((256, 512)), full((128, 512)), full((128, 256)),
            full((1, 256)), full((4, 128, 128)), full((128, 256)), full((1, 256))],
        out_specs=[seq_t(w) if tr else seq(w) for w, tr in zip(out_w, transposed)] + [halves] * 3,
        out_shape=[jax.ShapeDtypeStruct((B, w, S) if tr else (B, S, w), BF)
                   for w, tr in zip(out_w, transposed)]
        + [jax.ShapeDtypeStruct((B, 2, S, LANES), F32)] * 3,
        compiler_params=pltpu.CompilerParams(
            dimension_semantics=("arbitrary", "arbitrary"), vmem_limit_bytes=VMEM_LIMIT),
        name="inproj",
    )(x, gmix, w_cat, *tabs, gq, gkv, wuq, wuk, wuv, gsv, sguw, sgub, gso)


def _head_expand(cols, lane, width):
    out = cols[-1]
    for h in range(len(cols) - 2, -1, -1):
        out = jnp.where(lane // width == h, cols[h], out)
    return out


def _vstack(v, lane):
    return jnp.concatenate(
        [jnp.where(lane // 64 == h, v, jnp.zeros_like(v)) for h in range(4)], axis=0)


DIFF_TILE = 1024
MLA_TILE = 1024
QB = 512


def _online_softmax_t(st, m_ref, j, cs):
    m_prev = m_ref[j:j + 1, cs]
    m_new = jnp.maximum(m_prev, jnp.max(st, axis=0, keepdims=True))
    alpha = jnp.exp2(m_prev - m_new)
    et = jnp.exp2(st - m_new)
    m_ref[j:j + 1, cs] = m_new
    return et.astype(BF), alpha


def _attn_jobs(ns, diag, heads):
    return [(*hd, qc, kc) for kc in range(ns) for qc in range(ns) for hd in heads
            if not (diag and kc > qc)]


def _causal_keep():
    return (lax.broadcasted_iota(jnp.int32, (QB, QB), 0)
            <= lax.broadcasted_iota(jnp.int32, (QB, QB), 1))


def _causal_pairs(n):
    pairs = [(i, j) for i in range(n) for j in range(i + 1)]
    return (jnp.asarray([p[0] for p in pairs], jnp.int32),
            jnp.asarray([p[1] for p in pairs], jnp.int32))


def _attn_a_body(qtab_ref, ktab_ref, lam_ref, gh_ref, q_ref, k_ref, v_ref, o_ref,
                 qs_ref, m_ref, acc_ref, *, t, lam_init):
    qi = qtab_ref[pl.program_id(1)]
    ki = ktab_ref[pl.program_id(1)]

    @pl.when(ki == 0)
    def _init():
        m_ref[...] = jnp.full(m_ref.shape, NEG, F32)
        acc_ref[...] = jnp.zeros(acc_ref.shape, F32)
        qt = q_ref[0]
        row = lax.broadcasted_iota(jnp.int32, (GROUP, 1), 0)
        for j in range(8):
            qs_ref[j] = jnp.where(row // 32 == j, qt, jnp.zeros_like(qt))

    def step(diag):
        jobs = _attn_jobs(t // QB, diag, [(mp, h) for mp in range(2) for h in range(4)])
        kblk = [k_ref[0, kc * QB:(kc + 1) * QB, :] for kc in range(t // QB)]
        keep = _causal_keep() if diag else None

        def qk(mp, h, qc, kc):
            return _dot(kblk[kc], qs_ref[2 * h + mp, :, qc * QB:(qc + 1) * QB])

        def pv(mp, h, qc, kc, pt, alpha):
            hs, cs = slice(VROWS * h, VROWS * (h + 1)), slice(qc * QB, (qc + 1) * QB)
            acc_ref[mp, hs, cs] = (acc_ref[mp, hs, cs] * alpha
                                   + _dot(v_ref[0, hs, kc * QB:(kc + 1) * QB], pt))

        st_next = qk(*jobs[0])
        pending = None
        for n, (mp, h, qc, kc) in enumerate(jobs):
            st = st_next
            if n + 1 < len(jobs):
                st_next = qk(*jobs[n + 1])
            if diag and kc == qc:
                st = jnp.where(keep, st, NEG)
            pt, alpha = _online_softmax_t(st, m_ref, 2 * h + mp, slice(qc * QB, (qc + 1) * QB))
            if pending is not None:
                pv(*pending)
            pending = (mp, h, qc, kc, pt, alpha)
        pv(*pending)

    @pl.when(ki < qi)
    def _off():
        step(False)

    @pl.when(ki == qi)
    def _diag():
        step(True)
        lf = lam_ref[...]
        lam = (jnp.exp(jnp.sum(lf[0:1] * lf[1:2], axis=-1, keepdims=True))
               - jnp.exp(jnp.sum(lf[2:3] * lf[3:4], axis=-1, keepdims=True)) + lam_init)
        rows = []
        for h in range(4):
            hs, ls = slice(VROWS * h, VROWS * h + 64), slice(VROWS * h + 64, VROWS * h + 65)
            oh = (acc_ref[0, hs, :] * (1.0 / acc_ref[0, ls, :])
                  - lam * (acc_ref[1, hs, :] * (1.0 / acc_ref[1, ls, :])))
            ms = jnp.mean(oh * oh, axis=0, keepdims=True)
            rows.append(oh * lax.rsqrt(ms + EPS))
        ot = jnp.concatenate(rows, axis=0)
        o_ref[0] = (ot.T * gh_ref[...] * (1.0 - lam_init)).astype(BF)


def _attn_a(lam_p, gh, q, k, v, *, t, lam_init):
    B, S, _ = k.shape
    n = S // t
    qtab, ktab = _causal_pairs(n)
    grid_spec = pltpu.PrefetchScalarGridSpec(
        num_scalar_prefetch=2,
        grid=(B, qtab.shape[0]),
        in_specs=[pl.BlockSpec((4, 32), lambda b, p, qt, kt: (0, 0)),
                  pl.BlockSpec((1, GROUP), lambda b, p, qt, kt: (0, 0)),
                  pl.BlockSpec((1, GROUP, t), lambda b, p, qt, kt: (b, 0, qt[p])),
                  pl.BlockSpec((1, t, GROUP), lambda b, p, qt, kt: (b, kt[p], 0)),
                  pl.BlockSpec((1, 4 * VROWS, t), lambda b, p, qt, kt: (b, 0, kt[p]))],
        out_specs=pl.BlockSpec((1, t, GROUP), lambda b, p, qt, kt: (b, qt[p], 0)),
        scratch_shapes=[pltpu.VMEM((8, GROUP, t), BF),
                        pltpu.VMEM((8, t), F32),
                        pltpu.VMEM((2, 4 * VROWS, t), F32)])
    return pl.pallas_call(
        functools.partial(_attn_a_body, t=t, lam_init=lam_init),
        grid_spec=grid_spec,
        out_shape=jax.ShapeDtypeStruct((B, S, GROUP), BF),
        compiler_params=pltpu.CompilerParams(
            dimension_semantics=("arbitrary", "arbitrary"), vmem_limit_bytes=VMEM_LIMIT),
        name="attn_diff",
    )(qtab, ktab, lam_p, gh, q, k, v)


def _attn_b_body(qtab_ref, ktab_ref, g_ref, q_ref, k_ref, v_ref, o_ref, m_ref, acc_ref, *, t):
    qi = qtab_ref[pl.program_id(1)]
    ki = ktab_ref[pl.program_id(1)]

    @pl.when(ki == 0)
    def _init():
        m_ref[...] = jnp.full(m_ref.shape, NEG, F32)
        acc_ref[...] = jnp.zeros(acc_ref.shape, F32)

    def step(diag):
        jobs = _attn_jobs(t // QB, diag, [(h,) for h in range(4)])
        kblk = [k_ref[0, kc * QB:(kc + 1) * QB, :] for kc in range(t // QB)]
        keep = _causal_keep() if diag else None

        def qk(h, qc, kc):
            return _dot(kblk[kc][:, h * 128:(h + 1) * 128],
                        q_ref[0, h * 128:(h + 1) * 128, qc * QB:(qc + 1) * QB])

        def pv(h, qc, kc, pt, alpha):
            hs, cs = slice(VROWS * h, VROWS * (h + 1)), slice(qc * QB, (qc + 1) * QB)
            acc_ref[hs, cs] = acc_ref[hs, cs] * alpha + _dot(v_ref[0, hs, kc * QB:(kc + 1) * QB], pt)

        st_next = qk(*jobs[0])
        pending = None
        for n, (h, qc, kc) in enumerate(jobs):
            st = st_next
            if n + 1 < len(jobs):
                st_next = qk(*jobs[n + 1])
            if diag and kc == qc:
                st = jnp.where(keep, st, NEG)
            pt, alpha = _online_softmax_t(st, m_ref, h, slice(qc * QB, (qc + 1) * QB))
            if pending is not None:
                pv(*pending)
            pending = (h, qc, kc, pt, alpha)
        pv(*pending)

    @pl.when(ki < qi)
    def _off():
        step(False)

    @pl.when(ki == qi)
    def _diag():
        step(True)
        rows = [acc_ref[VROWS * h:VROWS * h + 64, :] * (1.0 / acc_ref[VROWS * h + 64:VROWS * h + 65, :])
                for h in range(4)]
        ot = jnp.concatenate(rows, axis=0)
        o_ref[0] = _rms(ot.T, g_ref[...]).astype(BF)


def _attn_b(g, q, k, v, *, t):
    B, S, _ = k.shape
    n = S // t
    qtab, ktab = _causal_pairs(n)
    grid_spec = pltpu.PrefetchScalarGridSpec(
        num_scalar_prefetch=2,
        grid=(B, qtab.shape[0]),
        in_specs=[pl.BlockSpec((1, GROUP), lambda b, p, qt, kt: (0, 0)),
                  pl.BlockSpec((1, 512, t), lambda b, p, qt, kt: (b, 0, qt[p])),
                  pl.BlockSpec((1, t, 512), lambda b, p, qt, kt: (b, kt[p], 0)),
                  pl.BlockSpec((1, 4 * VROWS, t), lambda b, p, qt, kt: (b, 0, kt[p]))],
        out_specs=pl.BlockSpec((1, t, GROUP), lambda b, p, qt, kt: (b, qt[p], 0)),
        scratch_shapes=[pltpu.VMEM((4, t), F32),
                        pltpu.VMEM((4 * VROWS, t), F32)])
    return pl.pallas_call(
        functools.partial(_attn_b_body, t=t),
        grid_spec=grid_spec,
        out_shape=jax.ShapeDtypeStruct((B, S, GROUP), BF),
        compiler_params=pltpu.CompilerParams(
            dimension_semantics=("arbitrary", "arbitrary"), vmem_limit_bytes=VMEM_LIMIT),
        name="attn_mla",
    )(qtab, ktab, g, q, k, v)


DIL_TILE = 2048
DILATIONS = (1, 4, 16)


def _dil_body(g_ref, q_ref, kp_ref, kc_ref, vp_ref, vc_ref, o_ref, ob_ref, lb_ref, *, td):
    i = pl.program_id(1)
    lane = lax.broadcasted_iota(jnp.int32, (1, GROUP), 1)
    a = lax.broadcasted_iota(jnp.int32, (128, 256), 0)
    c = lax.broadcasted_iota(jnp.int32, (128, 256), 1)
    dist = jnp.where(c - a >= 0, c - a, 1000)

    def rows(ref, start, d):
        idx = pl.ds(start, 128) if d == 1 else pl.ds(start, 128, stride=d)
        return jnp.concatenate([ref.at[0, 0][idx, :], ref.at[0, 1][idx, :]], axis=1).astype(BF)

    for bi, d in enumerate(DILATIONS):
        def block(n, carry, bi=bi, d=d):
            if d == 1:
                r, blk = 0, n
            elif td // d == 128:
                r, blk = n, 0
            else:
                r, blk = n % d, n // d
            q0 = r + d * 128 * blk
            if d == 1:
                q0 = pl.multiple_of(q0, 128)
            q = rows(q_ref, q0, d)
            k_hi, v_hi = rows(kc_ref, q0, d), rows(vc_ref, q0, d)
            lo_prev = td - d * 128 + r
            if isinstance(blk, int):
                first = True
                k_lo, v_lo = rows(kp_ref, lo_prev, d), rows(vp_ref, lo_prev, d)
            else:
                first = blk == 0
                lo_cur = jnp.maximum(q0 - d * 128, 0)
                if d == 1:
                    lo_cur = pl.multiple_of(lo_cur, 128)
                k_lo = jnp.where(first, rows(kp_ref, lo_prev, d), rows(kc_ref, lo_cur, d))
                v_lo = jnp.where(first, rows(vp_ref, lo_prev, d), rows(vc_ref, lo_cur, d))
            kwin = jnp.concatenate([k_lo, k_hi], axis=0)
            vst = _vstack(jnp.concatenate([v_lo, v_hi], axis=0), lane)
            cmin = jnp.where(jnp.logical_and(first, i == 0), 128, 0)
            ok = jnp.where(c >= cmin, dist, 1000) <= 128
            es, ils, lses = [], [], []
            for h in range(4):
                s = _nt_dot(jnp.where(lane // 64 == h, q, jnp.zeros_like(q)), kwin)
                s = jnp.where(ok, s, NEG)
                m = jnp.max(s, axis=-1, keepdims=True)
                ex = jnp.exp(s - m)
                den = jnp.sum(ex, axis=-1, keepdims=True)
                es.append(ex.astype(BF))
                ils.append(1.0 / den)
                lses.append(m + jnp.log(den))
            o = _dot(jnp.concatenate(es, axis=1), vst) * _head_expand(ils, lane, 64)
            lse = _head_expand(lses, lane, 64)
            idx = pl.ds(q0, 128) if d == 1 else pl.ds(q0, 128, stride=d)
            for hf in range(2):
                ob_ref.at[bi, hf][idx, :] = o[:, hf * LANES:(hf + 1) * LANES]
                lb_ref.at[bi, hf][idx, :] = lse[:, hf * LANES:(hf + 1) * LANES]
            return carry

        lax.fori_loop(0, td // 128, block, 0, unroll=8)

    for ch in range(td // 256):
        rs = slice(ch * 256, (ch + 1) * 256)
        halves = []
        for hf in range(2):
            ls = [lb_ref[bi, hf, rs, :] for bi in range(3)]
            mx = jnp.maximum(jnp.maximum(ls[0], ls[1]), ls[2])
            ws = [jnp.exp(l - mx) for l in ls]
            num = ws[0] * ob_ref[0, hf, rs, :] + ws[1] * ob_ref[1, hf, rs, :] + ws[2] * ob_ref[2, hf, rs, :]
            halves.append(num / (ws[0] + ws[1] + ws[2]))
        o_ref[0, rs, :] = _rms(jnp.concatenate(halves, axis=1), g_ref[...]).astype(BF)


def _dil(g, q, k, v, *, td):
    B, _, S, _ = q.shape
    cur = pl.BlockSpec((1, 2, td, LANES), lambda b, i: (b, 0, i, 0))
    prev = pl.BlockSpec((1, 2, td, LANES), lambda b, i: (b, 0, jnp.maximum(i - 1, 0), 0))
    return pl.pallas_call(
        functools.partial(_dil_body, td=td),
        grid=(B, S // td),
        in_specs=[pl.BlockSpec((1, GROUP), lambda b, i: (0, 0)), cur, prev, cur, prev, cur],
        out_specs=pl.BlockSpec((1, td, GROUP), lambda b, i: (b, i, 0)),
        out_shape=jax.ShapeDtypeStruct((B, S, GROUP), BF),
        scratch_shapes=[pltpu.VMEM((3, 2, td, LANES), F32),
                        pltpu.VMEM((3, 2, td, LANES), F32)],
        compiler_params=pltpu.CompilerParams(
            dimension_semantics=("arbitrary", "arbitrary"), vmem_limit_bytes=VMEM_LIMIT),
        name="dilated",
    )(g, q, k, k, v, v)


PEER_EB = 2048
GELU_C1 = math.sqrt(2.0 / math.pi)
GELU_C2 = 0.044715 * GELU_C1
N_CAND = PEER_TOP + 1


def _peer_body(x_ref, oa_ref, ob_ref, oc_ref, od_ref, wo_ref, g_ref, wqt_ref, sk_ref, u_ref, vt_ref,
               gf_ref, o_ref, x2_ref, ht_ref, st_ref, e2_ref, th_ref, cw_ref, hw_ref, acc_ref,
               *, tt, final):
    e = pl.program_id(1)
    nch = tt // LANES

    @pl.when(e == 0)
    def _prologue():
        y = _dot(oa_ref[...], wo_ref[0:256, :])
        y += _dot(ob_ref[...], wo_ref[256:512, :])
        y += _dot(oc_ref[...], wo_ref[512:768, :])
        y += _dot(od_ref[...], wo_ref[768:1024, :])
        x2_ref[...] = x_ref[...] + y
        h = _rms(x2_ref[...], g_ref[...])
        ht_ref[...] = h.T.astype(BF)
        qt = _dot(wqt_ref[...], ht_ref[...]).astype(BF)
        for hh in range(PEER_HEADS):
            for c in range(2):
                r0 = hh * N_KEYS + c * 64
                sc = _dot(sk_ref[c], qt[r0:r0 + 64, :])
                for tc in range(nch):
                    blk = sc[:, tc * LANES:(tc + 1) * LANES]
                    st_ref.at[tc][pl.ds(c * 1024 + hh, N_KEYS, stride=PEER_HEADS), :] = blk
                    if c == 1:
                        st_ref[tc, 2048 + hh * N_KEYS:2048 + (hh + 1) * N_KEYS, :] = blk
        acc_ref[...] = jnp.zeros(acc_ref.shape, F32)

        def top_vals(load, n_rows, n_out):
            top = []
            for r in range(n_rows):
                v = load(r)
                for q in range(len(top)):
                    top[q], v = jnp.maximum(top[q], v), jnp.minimum(top[q], v)
                if len(top) < n_out:
                    top.append(v)
            return top

        def token_chunk(tc, carry):
            ls = pl.ds(pl.multiple_of(tc * LANES, LANES), LANES)
            v1 = top_vals(lambda r: st_ref[tc, r * 8:(r + 1) * 8, :], N_KEYS, N_CAND)
            v2 = top_vals(lambda r: st_ref[tc, 1024 + r * 8:1024 + (r + 1) * 8, :], N_KEYS, N_CAND)
            cands = [v1[a] + v2[b] for a in range(N_CAND) for b in range(N_CAND)
                     if (a + 1) * (b + 1) <= N_CAND]
            top = top_vals(lambda r: cands[r], len(cands), N_CAND)
            z = jnp.ones_like(top[0])
            for r in range(1, PEER_TOP):
                z = z + jnp.exp(top[r] - top[0])
            tau = 0.5 * (top[PEER_TOP - 1] + top[PEER_TOP])
            iz = 1.0 / z
            for r in range(N_KEYS):
                s1 = st_ref[tc, r * 8:(r + 1) * 8, :]
                th_ref[r, :, ls] = tau - s1
                cw_ref[r, :, ls] = jnp.exp(s1 - v1[0]) * (0.5 * iz)
            for hh in range(PEER_HEADS):
                rows = slice(2048 + hh * 128, 2048 + (hh + 1) * 128)
                e2_ref[hh * 128:(hh + 1) * 128, ls] = jnp.exp(st_ref[tc, rows, :] - v2[0][hh:hh + 1, :])
            return carry

        lax.fori_loop(0, nch, token_chunk, 0)

    nblk = PEER_EB // 256

    def pre_act(b):
        return _dot(u_ref[b * 256:(b + 1) * 256, :], ht_ref[...])

    def gate(b, a):
        i0 = e * (PEER_EB // N_KEYS) + 2 * b
        th = [th_ref[i0], th_ref[i0 + 1]]
        cw = [cw_ref[i0], cw_ref[i0 + 1]]
        for tc in range(nch):
            ls = slice(tc * LANES, (tc + 1) * LANES)
            for sub in range(2):
                w = [None, None]
                for hh in range(PEER_HEADS):
                    r0 = hh * N_KEYS + sub * 64
                    s2 = st_ref[tc, 2048 + r0:2048 + r0 + 64, :]
                    e2 = e2_ref[r0:r0 + 64, ls]
                    for half in range(2):
                        sel = jnp.where(s2 >= th[half][hh:hh + 1, ls], e2, 0.0) * cw[half][hh:hh + 1, ls]
                        w[half] = sel if w[half] is None else w[half] + sel
                for half in range(2):
                    r1 = half * N_KEYS + sub * 64
                    x = a[r1:r1 + 64, ls]
                    xw = x * w[half]
                    t = jnp.tanh(x * (GELU_C1 + GELU_C2 * (x * x)))
                    hw_ref[b % 2, r1:r1 + 64, ls] = (xw + xw * t).astype(BF)

    a = pre_act(0)
    for b in range(nblk):
        a_next = pre_act(b + 1) if b + 1 < nblk else None
        gate(b, a)
        if b > 0:
            acc_ref[...] += _dot(vt_ref[b - 1], hw_ref[(b - 1) % 2])
        a = a_next
    acc_ref[...] += _dot(vt_ref[nblk - 1], hw_ref[(nblk - 1) % 2])

    @pl.when(e == pl.num_programs(1) - 1)
    def _finish():
        y = x2_ref[...] + acc_ref[...].T
        if final:
            y = _rms(y, gf_ref[...])
        o_ref[...] = y


def _peer(x, oa, ob, oc, od, w_out, g, wqt, sk, u, vt, gf, *, tt, final):
    T, D = x.shape
    ne = u.shape[0] // PEER_EB
    const = lambda shape: pl.BlockSpec(shape, lambda i, e: (0,) * len(shape),
                                       pipeline_mode=pl.Buffered(1))
    return pl.pallas_call(
        functools.partial(_peer_body, tt=tt, final=final),
        grid=(T // tt, ne),
        in_specs=[pl.BlockSpec((tt, D), lambda i, e: (i, 0))]
        + [pl.BlockSpec((tt, GROUP), lambda i, e: (i, 0))] * 4
        + [const((D, D)), const((1, D)), const((D, D)), const((2, N_KEYS, 64)),
                  pl.BlockSpec((PEER_EB, D), lambda i, e: (e, 0)),
                  pl.BlockSpec((PEER_EB // 256, D, 256), lambda i, e: (e, 0, 0)),
                  const((1, D))],
        out_specs=pl.BlockSpec((tt, D), lambda i, e: (i, 0)),
        out_shape=jax.ShapeDtypeStruct((T, D), F32),
        scratch_shapes=[pltpu.VMEM((tt, D), F32),
                        pltpu.VMEM((D, tt), BF),
                        pltpu.VMEM((tt // LANES, 3 * 1024, LANES), F32),
                        pltpu.VMEM((1024, tt), F32),
                        pltpu.VMEM((N_KEYS, PEER_HEADS, tt), F32),
                        pltpu.VMEM((N_KEYS, PEER_HEADS, tt), F32),
                        pltpu.VMEM((2, 256, tt), BF),
                        pltpu.VMEM((D, tt), F32)],
        compiler_params=pltpu.CompilerParams(
            dimension_semantics=("arbitrary", "arbitrary"), vmem_limit_bytes=VMEM_LIMIT),
        name="peer",
    )(x, oa, ob, oc, od, w_out, g, wqt, sk, u, vt, gf)


def _pack_in_weights(w_in):
    aq, ak, av, cq, ckv, kpe, zc, dq, dk, dv = jnp.split(
        w_in, np.cumsum([256, 256, 256, 256, 128, 32, 512, 256, 256])[:].tolist(), axis=1)
    kpe_rep = jnp.zeros((D_MODEL, 4, 128), w_in.dtype).at[:, :, 64:96].set(kpe[:, None, :])
    return jnp.concatenate(
        [aq, ak, av, cq, ckv, kpe_rep.reshape(D_MODEL, 512), zc, dq, dk, dv], axis=1).astype(BF)


def _pack_mla_weights(w_uq, w_ukv):
    uq = jnp.pad(w_uq.reshape(256, 4, 96), ((0, 0), (0, 0), (0, 32))).reshape(256, 512)
    ukv = w_ukv.reshape(128, 4, 128)
    uk = jnp.pad(ukv[:, :, :64], ((0, 0), (0, 0), (0, 64))).reshape(128, 512)
    uv = ukv[:, :, 64:].reshape(128, 256)
    return uq.astype(BF), uk.astype(BF), uv.astype(BF)


def _pack_expert_out(v):
    n, d = v.shape
    return v.reshape(n // 256, 256, d).transpose(0, 2, 1).astype(BF)


def kernel(x, w_in, w_out, norm_mix, norm_ffn, diff_lambda, diff_head_norm, mla_q_norm, mla_kv_norm, mla_w_uq, mla_w_ukv, mla_out_norm, sgu_v_norm, sgu_w, sgu_b, sgu_out_norm, dil_out_norm, peer_w_q, peer_sub_keys, peer_u, peer_v, final_norm):
    B, S, D = x.shape
    depth = w_in.shape[0]
    T = B * S
    tm = min(1024, S)
    ta, tb = min(DIFF_TILE, S), min(MLA_TILE, S)
    tt = min(512, T)
    row = lambda a: a.reshape(1, -1)

    tabs = (_rope_tables(S, 8, 32, 0) + _rope_tables(S, 32, 128, 64) + _rope_tables(S, 16, 64, 0))

    for l in range(depth):
        lam_init = 0.8 - 0.6 * math.exp(-0.3 * l)
        w_cat = _pack_in_weights(w_in[l])
        wuq, wuk, wuv = _pack_mla_weights(mla_w_uq[l], mla_w_ukv[l])
        sgub = jnp.repeat(sgu_b[l].T, 64, axis=1)
        qa, ka, va, qb, kb, vb, oc, qd, kd, vd = _inproj(
            x, row(norm_mix[l]), w_cat, tabs, row(mla_q_norm[l]), row(mla_kv_norm[l]),
            wuq, wuk, wuv, row(sgu_v_norm[l]), sgu_w[l], sgub, row(sgu_out_norm[l]), tm=tm)

        oa = _attn_a(diff_lambda[l], row(diff_head_norm[l]), qa, ka, va, t=ta, lam_init=lam_init)
        ob = _attn_b(row(mla_out_norm[l]), qb, kb, vb, t=tb)

        od = _dil(row(dil_out_norm[l]), qd, kd, vd, td=DIL_TILE)

        x2 = _peer(x.reshape(T, D), oa.reshape(T, GROUP), ob.reshape(T, GROUP), oc.reshape(T, GROUP),
                   od.reshape(T, GROUP), w_out[l].astype(BF), row(norm_ffn[l]), peer_w_q[l].T.astype(BF), peer_sub_keys[l].astype(BF),
                   peer_u[l].astype(BF), _pack_expert_out(peer_v[l]), row(final_norm),
                   tt=tt, final=(l == depth - 1))
        x = x2.reshape(B, S, D)
    return x
```

```python
import functools
import math

import numpy as np
import jax
import jax.numpy as jnp
from jax import lax
from jax.experimental import pallas as pl
from jax.experimental.pallas import tpu as pltpu

D_MODEL = 1024
GROUP = 256
ROPE_THETA = 500000.0
NEG = -1e30
EPS = 1e-6
LANES = 128
LOG2E = math.log2(math.e)

N_KEYS = 128
PEER_HEADS = 8
PEER_TOP = 16

BF = jnp.bfloat16
F32 = jnp.float32

C_AQ, C_AK, C_AV, C_CQ, C_CKV, C_KPE, C_ZC, C_DQ, C_DK, C_DV, C_END = (
    0, 256, 512, 768, 1024, 1152, 1664, 2176, 2432, 2688, 2944)

VMEM_LIMIT = 56 * 1024 * 1024
INPROJ_TILE = 1024
PEER_TOKENS = 512


def _rms(x, g):
    return x * lax.rsqrt(jnp.mean(x * x, axis=-1, keepdims=True) + EPS) * g


def _nt_dot(a, b):
    return lax.dot_general(a, b, (((1,), (1,)), ((), ())), preferred_element_type=F32)


def _dot(a, b):
    return jnp.dot(a, b, preferred_element_type=F32)


def _rope_tables(seq, n_rot, period, offset):
    half = n_rot // 2
    pos = jnp.arange(seq, dtype=F32)
    inv = ROPE_THETA ** (-jnp.arange(half, dtype=F32) * (2.0 / n_rot))
    ang = pos[:, None] * inv[None, :]
    cos, sin = jnp.cos(ang), jnp.sin(ang)
    g = np.arange(LANES) % period - offset
    rot = (g >= 0) & (g < n_rot)
    idx = np.where(rot, g % half, 0)
    sign = np.where(g < half, -1.0, 1.0).astype(np.float32)
    cos_t = jnp.where(rot[None, :], cos[:, idx], 1.0)
    sin_t = jnp.where(rot[None, :], sin[:, idx] * sign[None, :], 0.0)
    return cos_t.astype(F32), sin_t.astype(F32)


def _rope_apply(x, cos_t, sin_t, n_rot, period, offset):
    half = n_rot // 2
    lane = lax.broadcasted_iota(jnp.int32, (1, LANES), 1)
    first = (lane % period - offset) < half
    outs = []
    for c in range(x.shape[1] // LANES):
        xc = x[:, c * LANES:(c + 1) * LANES]
        fwd = pltpu.roll(xc, LANES - half, 1)
        bwd = pltpu.roll(xc, half, 1)
        outs.append(xc * cos_t + jnp.where(first, fwd, bwd) * sin_t)
    return outs[0] if len(outs) == 1 else jnp.concatenate(outs, axis=1)


VROWS = 80


def _value_rows(vt):
    one = (lax.broadcasted_iota(jnp.int32, (VROWS - 64, vt.shape[1]), 0) == 0).astype(F32)
    parts = []
    for h in range(4):
        parts += [vt[64 * h:64 * (h + 1)], one]
    return jnp.concatenate(parts, axis=0).astype(BF)


def _inproj_body(x_ref, gmix_ref, w_ref, ca_ref, sa_ref, cb_ref, sb_ref, cd_ref, sd_ref,
                 gq_ref, gkv_ref, wuq_ref, wuk_ref, wuv_ref,
                 gsv_ref, sguw_ref, sgub_ref, gso_ref,
                 qa_ref, ka_ref, va_ref, qb_ref, kb_ref, vb_ref, oc_ref,
                 qd_ref, kd_ref, vd_ref, *, tm):
    x = x_ref[0]
    h = _rms(x, gmix_ref[...]).astype(BF)

    def proj(a, b):
        return _dot(h, w_ref[:, a:b])

    ca, sa = ca_ref[...], sa_ref[...]
    aq = _rope_apply(proj(C_AQ, C_AK), ca, sa, 8, 32, 0)
    qa_ref[0] = (aq * (32.0 ** -0.5 * LOG2E)).T.astype(BF)
    ka_ref[0] = _rope_apply(proj(C_AK, C_AV), ca, sa, 8, 32, 0).astype(BF)
    va_ref[0] = _value_rows(proj(C_AV, C_CQ).T)

    cb, sb = cb_ref[...], sb_ref[...]
    cq = _rms(proj(C_CQ, C_CKV), gq_ref[...]).astype(BF)
    qb = _rope_apply(_dot(cq, wuq_ref[...]), cb, sb, 32, 128, 64)
    qb_ref[0] = (qb * (96.0 ** -0.5 * LOG2E)).T.astype(BF)
    ckv = _rms(proj(C_CKV, C_KPE), gkv_ref[...]).astype(BF)
    kpe = _rope_apply(proj(C_KPE, C_ZC), cb, sb, 32, 128, 64)
    kb_ref[0] = (_dot(ckv, wuk_ref[...]) + kpe).astype(BF)
    vb_ref[0] = _value_rows(_dot(ckv, wuv_ref[...]).T)

    zc = jax.nn.gelu(proj(C_ZC, C_DQ))
    u = zc[:, :GROUP]
    vn = _rms(zc[:, GROUP:], gsv_ref[...]).astype(BF)
    r = lax.broadcasted_iota(jnp.int32, (128, 128), 0)
    c = lax.broadcasted_iota(jnp.int32, (128, 128), 1)
    wcat = jnp.concatenate(
        [jnp.where(r >= c, sguw_ref[g], 0.0).astype(BF) for g in range(4)], axis=1)
    lane = lax.broadcasted_iota(jnp.int32, (1, GROUP), 1)
    bias = sgub_ref[...]
    gso = gso_ref[...]
    for ch in range(tm // 128):
        vc = vn[ch * 128:(ch + 1) * 128]
        vst = jnp.concatenate(
            [jnp.where(lane // 64 == g, vc, jnp.zeros_like(vc)) for g in range(4)], axis=0)
        sv = _dot(wcat, vst) + bias
        oc = u[ch * 128:(ch + 1) * 128] * sv
        oc_ref[0, ch * 128:(ch + 1) * 128, :] = _rms(oc, gso).astype(BF)

    cd, sd = cd_ref[...], sd_ref[...]
    dq = _rope_apply(proj(C_DQ, C_DK), cd, sd, 16, 64, 0)
    dk = _rope_apply(proj(C_DK, C_DV), cd, sd, 16, 64, 0)
    dv = proj(C_DV, C_END)
    for hf in range(2):
        qd_ref[0, hf] = dq[:, hf * LANES:(hf + 1) * LANES] * 0.125
        kd_ref[0, hf] = dk[:, hf * LANES:(hf + 1) * LANES]
        vd_ref[0, hf] = dv[:, hf * LANES:(hf + 1) * LANES]


def _inproj(x, gmix, w_cat, tabs, gq, gkv, wuq, wuk, wuv, gsv, sguw, sgub, gso, *, tm):
    B, S, D = x.shape
    ns = S // tm
    full = lambda shape: pl.BlockSpec(shape, lambda s, b: (0,) * len(shape))
    tab = pl.BlockSpec((tm, LANES), lambda s, b: (s, 0))
    seq = lambda w: pl.BlockSpec((1, tm, w), lambda s, b: (b, s, 0))
    seq_t = lambda w: pl.BlockSpec((1, w, tm), lambda s, b: (b, 0, s))
    halves = pl.BlockSpec((1, 2, tm, LANES), lambda s, b: (b, 0, s, 0))
    out_w = [256, 256, 4 * VROWS, 512, 512, 4 * VROWS, 256]
    transposed = [True, False, True, True, False, True, False]
    return pl.pallas_call(
        functools.partial(_inproj_body, tm=tm),
        grid=(ns, B),
        in_specs=[seq(D), full((1, D)), full((D, C_END))] + [tab] * 6 + [
            full((1, 256)), full((1, 128)), full((256, 512)), full((128, 512)), full((128, 256)),
            full((1, 256)), full((4, 128, 128)), full((128, 256)), full((1, 256))],
        out_specs=[seq_t(w) if tr else seq(w) for w, tr in zip(out_w, transposed)] + [halves] * 3,
        out_shape=[jax.ShapeDtypeStruct((B, w, S) if tr else (B, S, w), BF)
                   for w, tr in zip(out_w, transposed)]
        + [jax.ShapeDtypeStruct((B, 2, S, LANES), F32)] * 3,
        compiler_params=pltpu.CompilerParams(
            dimension_semantics=("arbitrary", "arbitrary"), vmem_limit_bytes=VMEM_LIMIT),
        name="inproj",
    )(x, gmix, w_cat, *tabs, gq, gkv, wuq, wuk, wuv, gsv, sguw, sgub, gso)


def _head_expand(cols, lane, width):
    out = cols[-1]
    for h in range(len(cols) - 2, -1, -1):
        out = jnp.where(lane // width == h, cols[h], out)
    return out


def _vstack(v, lane):
    return jnp.concatenate(
        [jnp.where(lane // 64 == h, v, jnp.zeros_like(v)) for h in range(4)], axis=0)


DIFF_TILE = 1024
MLA_TILE = 1024
QB = 512


def _online_softmax_t(st, m_ref, j, cs):
    m_prev = m_ref[j:j + 1, cs]
    m_new = jnp.maximum(m_prev, jnp.max(st, axis=0, keepdims=True))
    alpha = jnp.exp2(m_prev - m_new)
    et = jnp.exp2(st - m_new)
    m_ref[j:j + 1, cs] = m_new
    return et.astype(BF), alpha


def _attn_jobs(ns, diag, heads):
    return [(*hd, qc, kc) for kc in range(ns) for qc in range(ns) for hd in heads
            if not (diag and kc > qc)]


def _causal_keep():
    return (lax.broadcasted_iota(jnp.int32, (QB, QB), 0)
            <= lax.broadcasted_iota(jnp.int32, (QB, QB), 1))


def _causal_pairs(n):
    pairs = [(i, j) for i in range(n) for j in range(i + 1)]
    return (jnp.asarray([p[0] for p in pairs], jnp.int32),
            jnp.asarray([p[1] for p in pairs], jnp.int32))


def _attn_a_body(qtab_ref, ktab_ref, lam_ref, gh_ref, q_ref, k_ref, v_ref, o_ref,
                 qs_ref, m_ref, acc_ref, *, t, lam_init):
    qi = qtab_ref[pl.program_id(1)]
    ki = ktab_ref[pl.program_id(1)]

    @pl.when(ki == 0)
    def _init():
        m_ref[...] = jnp.full(m_ref.shape, NEG, F32)
        acc_ref[...] = jnp.zeros(acc_ref.shape, F32)
        qt = q_ref[0]
        row = lax.broadcasted_iota(jnp.int32, (GROUP, 1), 0)
        for j in range(8):
            qs_ref[j] = jnp.where(row // 32 == j, qt, jnp.zeros_like(qt))

    def step(diag):
        jobs = _attn_jobs(t // QB, diag, [(mp, h) for mp in range(2) for h in range(4)])
        kblk = [k_ref[0, kc * QB:(kc + 1) * QB, :] for kc in range(t // QB)]
        keep = _causal_keep() if diag else None

        def qk(mp, h, qc, kc):
            return _dot(kblk[kc], qs_ref[2 * h + mp, :, qc * QB:(qc + 1) * QB])

        def pv(mp, h, qc, kc, pt, alpha):
            hs, cs = slice(VROWS * h, VROWS * (h + 1)), slice(qc * QB, (qc + 1) * QB)
            acc_ref[mp, hs, cs] = (acc_ref[mp, hs, cs] * alpha
                                   + _dot(v_ref[0, hs, kc * QB:(kc + 1) * QB], pt))

        st_next = qk(*jobs[0])
        pending = None
        for n, (mp, h, qc, kc) in enumerate(jobs):
            st = st_next
            if n + 1 < len(jobs):
                st_next = qk(*jobs[n + 1])
            if diag and kc == qc:
                st = jnp.where(keep, st, NEG)
            pt, alpha = _online_softmax_t(st, m_ref, 2 * h + mp, slice(qc * QB, (qc + 1) * QB))
            if pending is not None:
                pv(*pending)
            pending = (mp, h, qc, kc, pt, alpha)
        pv(*pending)

    @pl.when(ki < qi)
    def _off():
        step(False)

    @pl.when(ki == qi)
    def _diag():
        step(True)
        lf = lam_ref[...]
        lam = (jnp.exp(jnp.sum(lf[0:1] * lf[1:2], axis=-1, keepdims=True))
               - jnp.exp(jnp.sum(lf[2:3] * lf[3:4], axis=-1, keepdims=True)) + lam_init)
        rows = []
        for h in range(4):
            hs, ls = slice(VROWS * h, VROWS * h + 64), slice(VROWS * h + 64, VROWS * h + 65)
            oh = (acc_ref[0, hs, :] * (1.0 / acc_ref[0, ls, :])
                  - lam * (acc_ref[1, hs, :] * (1.0 / acc_ref[1, ls, :])))
            ms = jnp.mean(oh * oh, axis=0, keepdims=True)
            rows.append(oh * lax.rsqrt(ms + EPS))
        ot = jnp.concatenate(rows, axis=0)
        o_ref[0] = (ot.T * gh_ref[...] * (1.0 - lam_init)).astype(BF)


def _attn_a(lam_p, gh, q, k, v, *, t, lam_init):
    B, S, _ = k.shape
    n = S // t
    qtab, ktab = _causal_pairs(n)
    grid_spec = pltpu.PrefetchScalarGridSpec(
        num_scalar_prefetch=2,
        grid=(B, qtab.shape[0]),
        in_specs=[pl.BlockSpec((4, 32), lambda b, p, qt, kt: (0, 0)),
                  pl.BlockSpec((1, GROUP), lambda b, p, qt, kt: (0, 0)),
                  pl.BlockSpec((1, GROUP, t), lambda b, p, qt, kt: (b, 0, qt[p])),
                  pl.BlockSpec((1, t, GROUP), lambda b, p, qt, kt: (b, kt[p], 0)),
                  pl.BlockSpec((1, 4 * VROWS, t), lambda b, p, qt, kt: (b, 0, kt[p]))],
        out_specs=pl.BlockSpec((1, t, GROUP), lambda b, p, qt, kt: (b, qt[p], 0)),
        scratch_shapes=[pltpu.VMEM((8, GROUP, t), BF),
                        pltpu.VMEM((8, t), F32),
                        pltpu.VMEM((2, 4 * VROWS, t), F32)])
    return pl.pallas_call(
        functools.partial(_attn_a_body, t=t, lam_init=lam_init),
        grid_spec=grid_spec,
        out_shape=jax.ShapeDtypeStruct((B, S, GROUP), BF),
        compiler_params=pltpu.CompilerParams(
            dimension_semantics=("arbitrary", "arbitrary"), vmem_limit_bytes=VMEM_LIMIT),
        name="attn_diff",
    )(qtab, ktab, lam_p, gh, q, k, v)


def _attn_b_body(qtab_ref, ktab_ref, g_ref, q_ref, k_ref, v_ref, o_ref, m_ref, acc_ref, *, t):
    qi = qtab_ref[pl.program_id(1)]
    ki = ktab_ref[pl.program_id(1)]

    @pl.when(ki == 0)
    def _init():
        m_ref[...] = jnp.full(m_ref.shape, NEG, F32)
        acc_ref[...] = jnp.zeros(acc_ref.shape, F32)

    def step(diag):
        jobs = _attn_jobs(t // QB, diag, [(h,) for h in range(4)])
        kblk = [k_ref[0, kc * QB:(kc + 1) * QB, :] for kc in range(t // QB)]
        keep = _causal_keep() if diag else None

        def qk(h, qc, kc):
            return _dot(kblk[kc][:, h * 128:(h + 1) * 128],
                        q_ref[0, h * 128:(h + 1) * 128, qc * QB:(qc + 1) * QB])

        def pv(h, qc, kc, pt, alpha):
            hs, cs = slice(VROWS * h, VROWS * (h + 1)), slice(qc * QB, (qc + 1) * QB)
            acc_ref[hs, cs] = acc_ref[hs, cs] * alpha + _dot(v_ref[0, hs, kc * QB:(kc + 1) * QB], pt)

        st_next = qk(*jobs[0])
        pending = None
        for n, (h, qc, kc) in enumerate(jobs):
            st = st_next
            if n + 1 < len(jobs):
                st_next = qk(*jobs[n + 1])
            if diag and kc == qc:
                st = jnp.where(keep, st, NEG)
            pt, alpha = _online_softmax_t(st, m_ref, h, slice(qc * QB, (qc + 1) * QB))
            if pending is not None:
                pv(*pending)
            pending = (h, qc, kc, pt, alpha)
        pv(*pending)

    @pl.when(ki < qi)
    def _off():
        step(False)

    @pl.when(ki == qi)
    def _diag():
        step(True)
        rows = [acc_ref[VROWS * h:VROWS * h + 64, :] * (1.0 / acc_ref[VROWS * h + 64:VROWS * h + 65, :])
                for h in range(4)]
        ot = jnp.concatenate(rows, axis=0)
        o_ref[0] = _rms(ot.T, g_ref[...]).astype(BF)


def _attn_b(g, q, k, v, *, t):
    B, S, _ = k.shape
    n = S // t
    qtab, ktab = _causal_pairs(n)
    grid_spec = pltpu.PrefetchScalarGridSpec(
        num_scalar_prefetch=2,
        grid=(B, qtab.shape[0]),
        in_specs=[pl.BlockSpec((1, GROUP), lambda b, p, qt, kt: (0, 0)),
                  pl.BlockSpec((1, 512, t), lambda b, p, qt, kt: (b, 0, qt[p])),
                  pl.BlockSpec((1, t, 512), lambda b, p, qt, kt: (b, kt[p], 0)),
                  pl.BlockSpec((1, 4 * VROWS, t), lambda b, p, qt, kt: (b, 0, kt[p]))],
        out_specs=pl.BlockSpec((1, t, GROUP), lambda b, p, qt, kt: (b, qt[p], 0)),
        scratch_shapes=[pltpu.VMEM((4, t), F32),
                        pltpu.VMEM((4 * VROWS, t), F32)])
    return pl.pallas_call(
        functools.partial(_attn_b_body, t=t),
        grid_spec=grid_spec,
        out_shape=jax.ShapeDtypeStruct((B, S, GROUP), BF),
        compiler_params=pltpu.CompilerParams(
            dimension_semantics=("arbitrary", "arbitrary"), vmem_limit_bytes=VMEM_LIMIT),
        name="attn_mla",
    )(qtab, ktab, g, q, k, v)


DIL_TILE = 2048
DILATIONS = (1, 4, 16)


def _dil_body(g_ref, q_ref, kp_ref, kc_ref, vp_ref, vc_ref, o_ref, ob_ref, lb_ref, *, td):
    i = pl.program_id(1)
    lane = lax.broadcasted_iota(jnp.int32, (1, GROUP), 1)
    a = lax.broadcasted_iota(jnp.int32, (128, 256), 0)
    c = lax.broadcasted_iota(jnp.int32, (128, 256), 1)
    dist = jnp.where(c - a >= 0, c - a, 1000)

    def rows(ref, start, d):
        idx = pl.ds(start, 128) if d == 1 else pl.ds(start, 128, stride=d)
        return jnp.concatenate([ref.at[0, 0][idx, :], ref.at[0, 1][idx, :]], axis=1).astype(BF)

    for bi, d in enumerate(DILATIONS):
        def block(n, carry, bi=bi, d=d):
            if d == 1:
                r, blk = 0, n
            elif td // d == 128:
                r, blk = n, 0
            else:
                r, blk = n % d, n // d
            q0 = r + d * 128 * blk
            if d == 1:
                q0 = pl.multiple_of(q0, 128)
            q = rows(q_ref, q0, d)
            k_hi, v_hi = rows(kc_ref, q0, d), rows(vc_ref, q0, d)
            lo_prev = td - d * 128 + r
            if isinstance(blk, int):
                first = True
                k_lo, v_lo = rows(kp_ref, lo_prev, d), rows(vp_ref, lo_prev, d)
            else:
                first = blk == 0
                lo_cur = jnp.maximum(q0 - d * 128, 0)
                if d == 1:
                    lo_cur = pl.multiple_of(lo_cur, 128)
                k_lo = jnp.where(first, rows(kp_ref, lo_prev, d), rows(kc_ref, lo_cur, d))
                v_lo = jnp.where(first, rows(vp_ref, lo_prev, d), rows(vc_ref, lo_cur, d))
            kwin = jnp.concatenate([k_lo, k_hi], axis=0)
            vst = _vstack(jnp.concatenate([v_lo, v_hi], axis=0), lane)
            cmin = jnp.where(jnp.logical_and(first, i == 0), 128, 0)
            ok = jnp.where(c >= cmin, dist, 1000) <= 128
            es, ils, lses = [], [], []
            for h in range(4):
                s = _nt_dot(jnp.where(lane // 64 == h, q, jnp.zeros_like(q)), kwin)
                s = jnp.where(ok, s, NEG)
                m = jnp.max(s, axis=-1, keepdims=True)
                ex = jnp.exp(s - m)
                den = jnp.sum(ex, axis=-1, keepdims=True)
                es.append(ex.astype(BF))
                ils.append(1.0 / den)
                lses.append(m + jnp.log(den))
            o = _dot(jnp.concatenate(es, axis=1), vst) * _head_expand(ils, lane, 64)
            lse = _head_expand(lses, lane, 64)
            idx = pl.ds(q0, 128) if d == 1 else pl.ds(q0, 128, stride=d)
            for hf in range(2):
                ob_ref.at[bi, hf][idx, :] = o[:, hf * LANES:(hf + 1) * LANES]
                lb_ref.at[bi, hf][idx, :] = lse[:, hf * LANES:(hf + 1) * LANES]
            return carry

        lax.fori_loop(0, td // 128, block, 0, unroll=8)

    for ch in range(td // 256):
        rs = slice(ch * 256, (ch + 1) * 256)
        halves = []
        for hf in range(2):
            ls = [lb_ref[bi, hf, rs, :] for bi in range(3)]
            mx = jnp.maximum(jnp.maximum(ls[0], ls[1]), ls[2])
            ws = [jnp.exp(l - mx) for l in ls]
            num = ws[0] * ob_ref[0, hf, rs, :] + ws[1] * ob_ref[1, hf, rs, :] + ws[2] * ob_ref[2, hf, rs, :]
            halves.append(num / (ws[0] + ws[1] + ws[2]))
        o_ref[0, rs, :] = _rms(jnp.concatenate(halves, axis=1), g_ref[...]).astype(BF)


def _dil(g, q, k, v, *, td):
    B, _, S, _ = q.shape
    cur = pl.BlockSpec((1, 2, td, LANES), lambda b, i: (b, 0, i, 0))
    prev = pl.BlockSpec((1, 2, td, LANES), lambda b, i: (b, 0, jnp.maximum(i - 1, 0), 0))
    return pl.pallas_call(
        functools.partial(_dil_body, td=td),
        grid=(B, S // td),
        in_specs=[pl.BlockSpec((1, GROUP), lambda b, i: (0, 0)), cur, prev, cur, prev, cur],
        out_specs=pl.BlockSpec((1, td, GROUP), lambda b, i: (b, i, 0)),
        out_shape=jax.ShapeDtypeStruct((B, S, GROUP), BF),
        scratch_shapes=[pltpu.VMEM((3, 2, td, LANES), F32),
                        pltpu.VMEM((3, 2, td, LANES), F32)],
        compiler_params=pltpu.CompilerParams(
            dimension_semantics=("arbitrary", "arbitrary"), vmem_limit_bytes=VMEM_LIMIT),
        name="dilated",
    )(g, q, k, k, v, v)


PEER_EB = 2048
GELU_C1 = math.sqrt(2.0 / math.pi)
GELU_C2 = 0.044715 * GELU_C1
N_CAND = PEER_TOP + 1


def _peer_body(x_ref, oa_ref, ob_ref, oc_ref, od_ref, wo_ref, g_ref, wqt_ref, sk_ref, u_ref, vt_ref,
               gf_ref, o_ref, x2_ref, ht_ref, st_ref, e2_ref, th_ref, cw_ref, hw_ref, acc_ref,
               *, tt, final):
    e = pl.program_id(1)
    nch = tt // LANES

    @pl.when(e == 0)
    def _prologue():
        y = _dot(oa_ref[...], wo_ref[0:256, :])
        y += _dot(ob_ref[...], wo_ref[256:512, :])
        y += _dot(oc_ref[...], wo_ref[512:768, :])
        y += _dot(od_ref[...], wo_ref[768:1024, :])
        x2_ref[...] = x_ref[...] + y
        h = _rms(x2_ref[...], g_ref[...])
        ht_ref[...] = h.T.astype(BF)
        qt = _dot(wqt_ref[...], ht_ref[...]).astype(BF)
        for hh in range(PEER_HEADS):
            for c in range(2):
                r0 = hh * N_KEYS + c * 64
                sc = _dot(sk_ref[c], qt[r0:r0 + 64, :])
                for tc in range(nch):
                    blk = sc[:, tc * LANES:(tc + 1) * LANES]
                    st_ref.at[tc][pl.ds(c * 1024 + hh, N_KEYS, stride=PEER_HEADS), :] = blk
                    if c == 1:
                        st_ref[tc, 2048 + hh * N_KEYS:2048 + (hh + 1) * N_KEYS, :] = blk
        acc_ref[...] = jnp.zeros(acc_ref.shape, F32)

        def top_vals(load, n_rows, n_out):
            top = []
            for r in range(n_rows):
                v = load(r)
                for q in range(len(top)):
                    top[q], v = jnp.maximum(top[q], v), jnp.minimum(top[q], v)
                if len(top) < n_out:
                    top.append(v)
            return top

        def token_chunk(tc, carry):
            ls = pl.ds(pl.multiple_of(tc * LANES, LANES), LANES)
            v1 = top_vals(lambda r: st_ref[tc, r * 8:(r + 1) * 8, :], N_KEYS, N_CAND)
            v2 = top_vals(lambda r: st_ref[tc, 1024 + r * 8:1024 + (r + 1) * 8, :], N_KEYS, N_CAND)
            cands = [v1[a] + v2[b] for a in range(N_CAND) for b in range(N_CAND)
                     if (a + 1) * (b + 1) <= N_CAND]
            top = top_vals(lambda r: cands[r], len(cands), N_CAND)
            z = jnp.ones_like(top[0])
            for r in range(1, PEER_TOP):
                z = z + jnp.exp(top[r] - top[0])
            tau = 0.5 * (top[PEER_TOP - 1] + top[PEER_TOP])
            iz = 1.0 / z
            for r in range(N_KEYS):
                s1 = st_ref[tc, r * 8:(r + 1) * 8, :]
                th_ref[r, :, ls] = tau - s1
                cw_ref[r, :, ls] = jnp.exp(s1 - v1[0]) * (0.5 * iz)
            for hh in range(PEER_HEADS):
                rows = slice(2048 + hh * 128, 2048 + (hh + 1) * 128)
                e2_ref[hh * 128:(hh + 1) * 128, ls] = jnp.exp(st_ref[tc, rows, :] - v2[0][hh:hh + 1, :])
            return carry

        lax.fori_loop(0, nch, token_chunk, 0)

    nblk = PEER_EB // 256

    def pre_act(b):
        return _dot(u_ref[b * 256:(b + 1) * 256, :], ht_ref[...])

    def gate(b, a):
        i0 = e * (PEER_EB // N_KEYS) + 2 * b
        th = [th_ref[i0], th_ref[i0 + 1]]
        cw = [cw_ref[i0], cw_ref[i0 + 1]]
        for tc in range(nch):
            ls = slice(tc * LANES, (tc + 1) * LANES)
            for sub in range(2):
                w = [None, None]
                for hh in range(PEER_HEADS):
                    r0 = hh * N_KEYS + sub * 64
                    s2 = st_ref[tc, 2048 + r0:2048 + r0 + 64, :]
                    e2 = e2_ref[r0:r0 + 64, ls]
                    for half in range(2):
                        sel = jnp.where(s2 >= th[half][hh:hh + 1, ls], e2, 0.0) * cw[half][hh:hh + 1, ls]
                        w[half] = sel if w[half] is None else w[half] + sel
                for half in range(2):
                    r1 = half * N_KEYS + sub * 64
                    x = a[r1:r1 + 64, ls]
                    xw = x * w[half]
                    t = jnp.tanh(x * (GELU_C1 + GELU_C2 * (x * x)))
                    hw_ref[b % 2, r1:r1 + 64, ls] = (xw + xw * t).astype(BF)

    a = pre_act(0)
    for b in range(nblk):
        a_next = pre_act(b + 1) if b + 1 < nblk else None
        gate(b, a)
        if b > 0:
            acc_ref[...] += _dot(vt_ref[b - 1], hw_ref[(b - 1) % 2])
        a = a_next
    acc_ref[...] += _dot(vt_ref[nblk - 1], hw_ref[(nblk - 1) % 2])

    @pl.when(e == pl.num_programs(1) - 1)
    def _finish():
        y = x2_ref[...] + acc_ref[...].T
        if final:
            y = _rms(y, gf_ref[...])
        o_ref[...] = y


def _peer(x, oa, ob, oc, od, w_out, g, wqt, sk, u, vt, gf, *, tt, final):
    T, D = x.shape
    ne = u.shape[0] // PEER_EB
    const = lambda shape: pl.BlockSpec(shape, lambda i, e: (0,) * len(shape),
                                       pipeline_mode=pl.Buffered(1))
    return pl.pallas_call(
        functools.partial(_peer_body, tt=tt, final=final),
        grid=(T // tt, ne),
        in_specs=[pl.BlockSpec((tt, D), lambda i, e: (i, 0))]
        + [pl.BlockSpec((tt, GROUP), lambda i, e: (i, 0))] * 4
        + [const((D, D)), const((1, D)), const((D, D)), const((2, N_KEYS, 64)),
                  pl.BlockSpec((PEER_EB, D), lambda i, e: (e, 0)),
                  pl.BlockSpec((PEER_EB // 256, D, 256), lambda i, e: (e, 0, 0)),
                  const((1, D))],
        out_specs=pl.BlockSpec((tt, D), lambda i, e: (i, 0)),
        out_shape=jax.ShapeDtypeStruct((T, D), F32),
        scratch_shapes=[pltpu.VMEM((tt, D), F32),
                        pltpu.VMEM((D, tt), BF),
                        pltpu.VMEM((tt // LANES, 3 * 1024, LANES), F32),
                        pltpu.VMEM((1024, tt), F32),
                        pltpu.VMEM((N_KEYS, PEER_HEADS, tt), F32),
                        pltpu.VMEM((N_KEYS, PEER_HEADS, tt), F32),
                        pltpu.VMEM((2, 256, tt), BF),
                        pltpu.VMEM((D, tt), F32)],
        compiler_params=pltpu.CompilerParams(
            dimension_semantics=("arbitrary", "arbitrary"), vmem_limit_bytes=VMEM_LIMIT),
        name="peer",
    )(x, oa, ob, oc, od, w_out, g, wqt, sk, u, vt, gf)


def _pack_in_weights(w_in):
    aq, ak, av, cq, ckv, kpe, zc, dq, dk, dv = jnp.split(
        w_in, np.cumsum([256, 256, 256, 256, 128, 32, 512, 256, 256])[:].tolist(), axis=1)
    kpe_rep = jnp.zeros((D_MODEL, 4, 128), w_in.dtype).at[:, :, 64:96].set(kpe[:, None, :])
    return jnp.concatenate(
        [aq, ak, av, cq, ckv, kpe_rep.reshape(D_MODEL, 512), zc, dq, dk, dv], axis=1).astype(BF)


def _pack_mla_weights(w_uq, w_ukv):
    uq = jnp.pad(w_uq.reshape(256, 4, 96), ((0, 0), (0, 0), (0, 32))).reshape(256, 512)
    ukv = w_ukv.reshape(128, 4, 128)
    uk = jnp.pad(ukv[:, :, :64], ((0, 0), (0, 0), (0, 64))).reshape(128, 512)
    uv = ukv[:, :, 64:].reshape(128, 256)
    return uq.astype(BF), uk.astype(BF), uv.astype(BF)


def _pack_expert_out(v):
    n, d = v.shape
    return v.reshape(n // 256, 256, d).transpose(0, 2, 1).astype(BF)


def kernel(x, w_in, w_out, norm_mix, norm_ffn, diff_lambda, diff_head_norm, mla_q_norm, mla_kv_norm, mla_w_uq, mla_w_ukv, mla_out_norm, sgu_v_norm, sgu_w, sgu_b, sgu_out_norm, dil_out_norm, peer_w_q, peer_sub_keys, peer_u, peer_v, final_norm):
    B, S, D = x.shape
    depth = w_in.shape[0]
    T = B * S
    tm = min(INPROJ_TILE, S)
    ta, tb = min(DIFF_TILE, S), min(MLA_TILE, S)
    tt = min(PEER_TOKENS, T)
    assert S % tm == 0 and S % ta == 0 and S % tb == 0 and S % DIL_TILE == 0 and T % tt == 0, (B, S)
    assert ta % QB == 0 and tb % QB == 0 and tt % LANES == 0 and D == D_MODEL
    row = lambda a: a.reshape(1, -1)

    tabs = (_rope_tables(S, 8, 32, 0) + _rope_tables(S, 32, 128, 64) + _rope_tables(S, 16, 64, 0))

    for l in range(depth):
        lam_init = 0.8 - 0.6 * math.exp(-0.3 * l)
        w_cat = _pack_in_weights(w_in[l])
        wuq, wuk, wuv = _pack_mla_weights(mla_w_uq[l], mla_w_ukv[l])
        sgub = jnp.repeat(sgu_b[l].T, 64, axis=1)
        qa, ka, va, qb, kb, vb, oc, qd, kd, vd = _inproj(
            x, row(norm_mix[l]), w_cat, tabs, row(mla_q_norm[l]), row(mla_kv_norm[l]),
            wuq, wuk, wuv, row(sgu_v_norm[l]), sgu_w[l], sgub, row(sgu_out_norm[l]), tm=tm)

        oa = _attn_a(diff_lambda[l], row(diff_head_norm[l]), qa, ka, va, t=ta, lam_init=lam_init)
        ob = _attn_b(row(mla_out_norm[l]), qb, kb, vb, t=tb)

        od = _dil(row(dil_out_norm[l]), qd, kd, vd, td=DIL_TILE)

        x2 = _peer(x.reshape(T, D), oa.reshape(T, GROUP), ob.reshape(T, GROUP), oc.reshape(T, GROUP),
                   od.reshape(T, GROUP), w_out[l].astype(BF), row(norm_ffn[l]), peer_w_q[l].T.astype(BF), peer_sub_keys[l].astype(BF),
                   peer_u[l].astype(BF), _pack_expert_out(peer_v[l]), row(final_norm),
                   tt=tt, final=(l == depth - 1))
        x = x2.reshape(B, S, D)
    return x
```

```python
import functools
import math

import numpy as np
import jax
import jax.numpy as jnp
from jax import lax
from jax.experimental import pallas as pl
from jax.experimental.pallas import tpu as pltpu

D_MODEL = 1024
GROUP = 256
ROPE_THETA = 500000.0
NEG = -1e30
EPS = 1e-6
LANES = 128
LOG2E = math.log2(math.e)

N_KEYS = 128
PEER_HEADS = 8
PEER_TOP = 16

BF = jnp.bfloat16
F32 = jnp.float32

C_AQ, C_AK, C_AV, C_CQ, C_CKV, C_KPE, C_ZC, C_DQ, C_DK, C_DV, C_END = (
    0, 256, 512, 768, 1024, 1152, 1664, 2176, 2432, 2688, 2944)

VMEM_LIMIT = 56 * 1024 * 1024
INPROJ_TILE = 1024
PEER_TOKENS = 512


def _rms(x, g):
    return x * lax.rsqrt(jnp.mean(x * x, axis=-1, keepdims=True) + EPS) * g


def _nt_dot(a, b):
    return lax.dot_general(a, b, (((1,), (1,)), ((), ())), preferred_element_type=F32)


def _dot(a, b):
    return jnp.dot(a, b, preferred_element_type=F32)


def _rope_tables(seq, n_rot, period, offset):
    half = n_rot // 2
    pos = jnp.arange(seq, dtype=F32)
    inv = ROPE_THETA ** (-jnp.arange(half, dtype=F32) * (2.0 / n_rot))
    ang = pos[:, None] * inv[None, :]
    cos, sin = jnp.cos(ang), jnp.sin(ang)
    g = np.arange(LANES) % period - offset
    rot = (g >= 0) & (g < n_rot)
    idx = np.where(rot, g % half, 0)
    sign = np.where(g < half, -1.0, 1.0).astype(np.float32)
    cos_t = jnp.where(rot[None, :], cos[:, idx], 1.0)
    sin_t = jnp.where(rot[None, :], sin[:, idx] * sign[None, :], 0.0)
    return cos_t.astype(F32), sin_t.astype(F32)


def _rope_apply(x, cos_t, sin_t, n_rot, period, offset):
    half = n_rot // 2
    lane = lax.broadcasted_iota(jnp.int32, (1, LANES), 1)
    first = (lane % period - offset) < half
    outs = []
    for c in range(x.shape[1] // LANES):
        xc = x[:, c * LANES:(c + 1) * LANES]
        fwd = pltpu.roll(xc, LANES - half, 1)
        bwd = pltpu.roll(xc, half, 1)
        outs.append(xc * cos_t + jnp.where(first, fwd, bwd) * sin_t)
    return outs[0] if len(outs) == 1 else jnp.concatenate(outs, axis=1)


VROWS = 80


def _value_rows(vt):
    one = (lax.broadcasted_iota(jnp.int32, (VROWS - 64, vt.shape[1]), 0) == 0).astype(F32)
    parts = []
    for h in range(4):
        parts += [vt[64 * h:64 * (h + 1)], one]
    return jnp.concatenate(parts, axis=0).astype(BF)


def _inproj_body(x_ref, gmix_ref, w_ref, ca_ref, sa_ref, cb_ref, sb_ref, cd_ref, sd_ref,
                 gq_ref, gkv_ref, wuq_ref, wuk_ref, wuv_ref,
                 gsv_ref, sguw_ref, sgub_ref, gso_ref,
                 qa_ref, ka_ref, va_ref, qb_ref, kb_ref, vb_ref, oc_ref,
                 qd_ref, kd_ref, vd_ref, *, tm):
    x = x_ref[0]
    h = _rms(x, gmix_ref[...]).astype(BF)

    def proj(a, b):
        return _dot(h, w_ref[:, a:b])

    ca, sa = ca_ref[...], sa_ref[...]
    aq = _rope_apply(proj(C_AQ, C_AK), ca, sa, 8, 32, 0)
    qa_ref[0] = (aq * (32.0 ** -0.5 * LOG2E)).T.astype(BF)
    ka_ref[0] = _rope_apply(proj(C_AK, C_AV), ca, sa, 8, 32, 0).astype(BF)
    va_ref[0] = _value_rows(proj(C_AV, C_CQ).T)

    cb, sb = cb_ref[...], sb_ref[...]
    cq = _rms(proj(C_CQ, C_CKV), gq_ref[...]).astype(BF)
    qb = _rope_apply(_dot(cq, wuq_ref[...]), cb, sb, 32, 128, 64)
    qb_ref[0] = (qb * (96.0 ** -0.5 * LOG2E)).T.astype(BF)
    ckv = _rms(proj(C_CKV, C_KPE), gkv_ref[...]).astype(BF)
    kpe = _rope_apply(proj(C_KPE, C_ZC), cb, sb, 32, 128, 64)
    kb_ref[0] = (_dot(ckv, wuk_ref[...]) + kpe).astype(BF)
    vb_ref[0] = _value_rows(_dot(ckv, wuv_ref[...]).T)

    zc = jax.nn.gelu(proj(C_ZC, C_DQ))
    u = zc[:, :GROUP]
    vn = _rms(zc[:, GROUP:], gsv_ref[...]).astype(BF)
    r = lax.broadcasted_iota(jnp.int32, (128, 128), 0)
    c = lax.broadcasted_iota(jnp.int32, (128, 128), 1)
    wcat = jnp.concatenate(
        [jnp.where(r >= c, sguw_ref[g], 0.0).astype(BF) for g in range(4)], axis=1)
    lane = lax.broadcasted_iota(jnp.int32, (1, GROUP), 1)
    bias = sgub_ref[...]
    gso = gso_ref[...]
    for ch in range(tm // 128):
        vc = vn[ch * 128:(ch + 1) * 128]
        vst = jnp.concatenate(
            [jnp.where(lane // 64 == g, vc, jnp.zeros_like(vc)) for g in range(4)], axis=0)
        sv = _dot(wcat, vst) + bias
        oc = u[ch * 128:(ch + 1) * 128] * sv
        oc_ref[0, ch * 128:(ch + 1) * 128, :] = _rms(oc, gso).astype(BF)

    cd, sd = cd_ref[...], sd_ref[...]
    dq = _rope_apply(proj(C_DQ, C_DK), cd, sd, 16, 64, 0)
    dk = _rope_apply(proj(C_DK, C_DV), cd, sd, 16, 64, 0)
    dv = proj(C_DV, C_END)
    for hf in range(2):
        qd_ref[0, hf] = dq[:, hf * LANES:(hf + 1) * LANES] * 0.125
        kd_ref[0, hf] = dk[:, hf * LANES:(hf + 1) * LANES]
        vd_ref[0, hf] = dv[:, hf * LANES:(hf + 1) * LANES]


def _inproj(x, gmix, w_cat, tabs, gq, gkv, wuq, wuk, wuv, gsv, sguw, sgub, gso, *, tm):
    B, S, D = x.shape
    ns = S // tm
    full = lambda shape: pl.BlockSpec(shape, lambda s, b: (0,) * len(shape))
    tab = pl.BlockSpec((tm, LANES), lambda s, b: (s, 0))
    seq = lambda w: pl.BlockSpec((1, tm, w), lambda s, b: (b, s, 0))
    seq_t = lambda w: pl.BlockSpec((1, w, tm), lambda s, b: (b, 0, s))
    halves = pl.BlockSpec((1, 2, tm, LANES), lambda s, b: (b, 0, s, 0))
    out_w = [256, 256, 4 * VROWS, 512, 512, 4 * VROWS, 256]
    transposed = [True, False, True, True, False, True, False]
    return pl.pallas_call(
        functools.partial(_inproj_body, tm=tm),
        grid=(ns, B),
        in_specs=[seq(D), full((1, D)), full((D, C_END))] + [tab] * 6 + [
            full((1, 256)), full((1, 128)), full((256, 512)), full((128, 512)), full((128, 256)),
            full((1, 256)), full((4, 128, 128)), full((128, 256)), full((1, 256))],
        out_specs=[seq_t(w) if tr else seq(w) for w, tr in zip(out_w, transposed)] + [halves] * 3,
        out_shape=[jax.ShapeDtypeStruct((B, w, S) if tr else (B, S, w), BF)
                   for w, tr in zip(out_w, transposed)]
        + [jax.ShapeDtypeStruct((B, 2, S, LANES), F32)] * 3,
        compiler_params=pltpu.CompilerParams(
            dimension_semantics=("arbitrary", "arbitrary"), vmem_limit_bytes=VMEM_LIMIT),
        name="inproj",
    )(x, gmix, w_cat, *tabs, gq, gkv, wuq, wuk, wuv, gsv, sguw, sgub, gso)


def _head_expand(cols, lane, width):
    out = cols[-1]
    for h in range(len(cols) - 2, -1, -1):
        out = jnp.where(lane // width == h, cols[h], out)
    return out


def _vstack(v, lane):
    return jnp.concatenate(
        [jnp.where(lane // 64 == h, v, jnp.zeros_like(v)) for h in range(4)], axis=0)


DIFF_TILE = 1024
MLA_TILE = 1024
QB = 512


def _online_softmax_t(st, m_ref, j, cs):
    m_prev = m_ref[j:j + 1, cs]
    m_new = jnp.maximum(m_prev, jnp.max(st, axis=0, keepdims=True))
    alpha = jnp.exp2(m_prev - m_new)
    et = jnp.exp2(st - m_new)
    m_ref[j:j + 1, cs] = m_new
    return et.astype(BF), alpha


def _attn_jobs(ns, diag, heads):
    return [(*hd, qc, kc) for kc in range(ns) for qc in range(ns) for hd in heads
            if not (diag and kc > qc)]


def _causal_keep():
    return (lax.broadcasted_iota(jnp.int32, (QB, QB), 0)
            <= lax.broadcasted_iota(jnp.int32, (QB, QB), 1))


def _causal_pairs(n):
    pairs = [(i, j) for i in range(n) for j in range(i + 1)]
    return (jnp.asarray([p[0] for p in pairs], jnp.int32),
            jnp.asarray([p[1] for p in pairs], jnp.int32))


def _attn_a_body(qtab_ref, ktab_ref, lam_ref, gh_ref, q_ref, k_ref, v_ref, o_ref,
                 qs_ref, m_ref, acc_ref, *, t, lam_init):
    qi = qtab_ref[pl.program_id(1)]
    ki = ktab_ref[pl.program_id(1)]

    @pl.when(ki == 0)
    def _init():
        m_ref[...] = jnp.full(m_ref.shape, NEG, F32)
        acc_ref[...] = jnp.zeros(acc_ref.shape, F32)
        row = lax.broadcasted_iota(jnp.int32, (LANES, 1), 0)
        for j in range(8):
            grp = q_ref[0, (j // 4) * LANES:(j // 4 + 1) * LANES, :]
            qs_ref[j] = jnp.where(row // 32 == j % 4, grp, jnp.zeros_like(grp))

    def step(diag):
        jobs = _attn_jobs(t // QB, diag, [(mp, h) for mp in range(2) for h in range(4)])
        kblk = [k_ref[0, kc * QB:(kc + 1) * QB, :] for kc in range(t // QB)]
        keep = _causal_keep() if diag else None

        def qk(mp, h, qc, kc):
            j = 2 * h + mp
            return _dot(kblk[kc][:, (j // 4) * LANES:(j // 4 + 1) * LANES],
                        qs_ref[j, :, qc * QB:(qc + 1) * QB])

        def pv(mp, h, qc, kc, pt, alpha):
            hs, cs = slice(VROWS * h, VROWS * (h + 1)), slice(qc * QB, (qc + 1) * QB)
            acc_ref[mp, hs, cs] = (acc_ref[mp, hs, cs] * alpha
                                   + _dot(v_ref[0, hs, kc * QB:(kc + 1) * QB], pt))

        st_next = qk(*jobs[0])
        pending = None
        for n, (mp, h, qc, kc) in enumerate(jobs):
            st = st_next
            if n + 1 < len(jobs):
                st_next = qk(*jobs[n + 1])
            if diag and kc == qc:
                st = jnp.where(keep, st, NEG)
            pt, alpha = _online_softmax_t(st, m_ref, 2 * h + mp, slice(qc * QB, (qc + 1) * QB))
            if pending is not None:
                pv(*pending)
            pending = (mp, h, qc, kc, pt, alpha)
        pv(*pending)

    @pl.when(ki < qi)
    def _off():
        step(False)

    @pl.when(ki == qi)
    def _diag():
        step(True)
        lf = lam_ref[...]
        lam = (jnp.exp(jnp.sum(lf[0:1] * lf[1:2], axis=-1, keepdims=True))
               - jnp.exp(jnp.sum(lf[2:3] * lf[3:4], axis=-1, keepdims=True)) + lam_init)
        rows = []
        for h in range(4):
            hs, ls = slice(VROWS * h, VROWS * h + 64), slice(VROWS * h + 64, VROWS * h + 65)
            oh = (acc_ref[0, hs, :] * (1.0 / acc_ref[0, ls, :])
                  - lam * (acc_ref[1, hs, :] * (1.0 / acc_ref[1, ls, :])))
            ms = jnp.mean(oh * oh, axis=0, keepdims=True)
            rows.append(oh * lax.rsqrt(ms + EPS))
        ot = jnp.concatenate(rows, axis=0)
        o_ref[0] = (ot.T * gh_ref[...] * (1.0 - lam_init)).astype(BF)


def _attn_a(lam_p, gh, q, k, v, *, t, lam_init):
    B, S, _ = k.shape
    n = S // t
    qtab, ktab = _causal_pairs(n)
    grid_spec = pltpu.PrefetchScalarGridSpec(
        num_scalar_prefetch=2,
        grid=(B, qtab.shape[0]),
        in_specs=[pl.BlockSpec((4, 32), lambda b, p, qt, kt: (0, 0)),
                  pl.BlockSpec((1, GROUP), lambda b, p, qt, kt: (0, 0)),
                  pl.BlockSpec((1, GROUP, t), lambda b, p, qt, kt: (b, 0, qt[p])),
                  pl.BlockSpec((1, t, GROUP), lambda b, p, qt, kt: (b, kt[p], 0)),
                  pl.BlockSpec((1, 4 * VROWS, t), lambda b, p, qt, kt: (b, 0, kt[p]))],
        out_specs=pl.BlockSpec((1, t, GROUP), lambda b, p, qt, kt: (b, qt[p], 0)),
        scratch_shapes=[pltpu.VMEM((8, LANES, t), BF),
                        pltpu.VMEM((8, t), F32),
                        pltpu.VMEM((2, 4 * VROWS, t), F32)])
    return pl.pallas_call(
        functools.partial(_attn_a_body, t=t, lam_init=lam_init),
        grid_spec=grid_spec,
        out_shape=jax.ShapeDtypeStruct((B, S, GROUP), BF),
        compiler_params=pltpu.CompilerParams(
            dimension_semantics=("arbitrary", "arbitrary"), vmem_limit_bytes=VMEM_LIMIT),
        name="attn_diff",
    )(qtab, ktab, lam_p, gh, q, k, v)


def _attn_b_body(qtab_ref, ktab_ref, g_ref, q_ref, k_ref, v_ref, o_ref, m_ref, acc_ref, *, t):
    qi = qtab_ref[pl.program_id(1)]
    ki = ktab_ref[pl.program_id(1)]

    @pl.when(ki == 0)
    def _init():
        m_ref[...] = jnp.full(m_ref.shape, NEG, F32)
        acc_ref[...] = jnp.zeros(acc_ref.shape, F32)

    def step(diag):
        jobs = _attn_jobs(t // QB, diag, [(h,) for h in range(4)])
        kblk = [k_ref[0, kc * QB:(kc + 1) * QB, :] for kc in range(t // QB)]
        keep = _causal_keep() if diag else None

        def qk(h, qc, kc):
            return _dot(kblk[kc][:, h * 128:(h + 1) * 128],
                        q_ref[0, h * 128:(h + 1) * 128, qc * QB:(qc + 1) * QB])

        def pv(h, qc, kc, pt, alpha):
            hs, cs = slice(VROWS * h, VROWS * (h + 1)), slice(qc * QB, (qc + 1) * QB)
            acc_ref[hs, cs] = acc_ref[hs, cs] * alpha + _dot(v_ref[0, hs, kc * QB:(kc + 1) * QB], pt)

        st_next = qk(*jobs[0])
        pending = None
        for n, (h, qc, kc) in enumerate(jobs):
            st = st_next
            if n + 1 < len(jobs):
                st_next = qk(*jobs[n + 1])
            if diag and kc == qc:
                st = jnp.where(keep, st, NEG)
            pt, alpha = _online_softmax_t(st, m_ref, h, slice(qc * QB, (qc + 1) * QB))
            if pending is not None:
                pv(*pending)
            pending = (h, qc, kc, pt, alpha)
        pv(*pending)

    @pl.when(ki < qi)
    def _off():
        step(False)

    @pl.when(ki == qi)
    def _diag():
        step(True)
        rows = [acc_ref[VROWS * h:VROWS * h + 64, :] * (1.0 / acc_ref[VROWS * h + 64:VROWS * h + 65, :])
                for h in range(4)]
        ot = jnp.concatenate(rows, axis=0)
        o_ref[0] = _rms(ot.T, g_ref[...]).astype(BF)


def _attn_b(g, q, k, v, *, t):
    B, S, _ = k.shape
    n = S // t
    qtab, ktab = _causal_pairs(n)
    grid_spec = pltpu.PrefetchScalarGridSpec(
        num_scalar_prefetch=2,
        grid=(B, qtab.shape[0]),
        in_specs=[pl.BlockSpec((1, GROUP), lambda b, p, qt, kt: (0, 0)),
                  pl.BlockSpec((1, 512, t), lambda b, p, qt, kt: (b, 0, qt[p])),
                  pl.BlockSpec((1, t, 512), lambda b, p, qt, kt: (b, kt[p], 0)),
                  pl.BlockSpec((1, 4 * VROWS, t), lambda b, p, qt, kt: (b, 0, kt[p]))],
        out_specs=pl.BlockSpec((1, t, GROUP), lambda b, p, qt, kt: (b, qt[p], 0)),
        scratch_shapes=[pltpu.VMEM((4, t), F32),
                        pltpu.VMEM((4 * VROWS, t), F32)])
    return pl.pallas_call(
        functools.partial(_attn_b_body, t=t),
        grid_spec=grid_spec,
        out_shape=jax.ShapeDtypeStruct((B, S, GROUP), BF),
        compiler_params=pltpu.CompilerParams(
            dimension_semantics=("arbitrary", "arbitrary"), vmem_limit_bytes=VMEM_LIMIT),
        name="attn_mla",
    )(qtab, ktab, g, q, k, v)


DIL_TILE = 2048
DILATIONS = (1, 4, 16)


def _dil_body(g_ref, q_ref, kp_ref, kc_ref, vp_ref, vc_ref, o_ref, ob_ref, lb_ref, *, td):
    i = pl.program_id(1)
    lane = lax.broadcasted_iota(jnp.int32, (1, GROUP), 1)
    a = lax.broadcasted_iota(jnp.int32, (128, 256), 0)
    c = lax.broadcasted_iota(jnp.int32, (128, 256), 1)
    dist = jnp.where(c - a >= 0, c - a, 1000)

    def rows(ref, start, d):
        idx = pl.ds(start, 128) if d == 1 else pl.ds(start, 128, stride=d)
        return jnp.concatenate([ref.at[0, 0][idx, :], ref.at[0, 1][idx, :]], axis=1).astype(BF)

    for bi, d in enumerate(DILATIONS):
        def block(n, carry, bi=bi, d=d):
            if d == 1:
                r, blk = 0, n
            elif td // d == 128:
                r, blk = n, 0
            else:
                r, blk = n % d, n // d
            q0 = r + d * 128 * blk
            if d == 1:
                q0 = pl.multiple_of(q0, 128)
            q = rows(q_ref, q0, d)
            k_hi, v_hi = rows(kc_ref, q0, d), rows(vc_ref, q0, d)
            lo_prev = td - d * 128 + r
            if isinstance(blk, int):
                first = True
                k_lo, v_lo = rows(kp_ref, lo_prev, d), rows(vp_ref, lo_prev, d)
            else:
                first = blk == 0
                lo_cur = jnp.maximum(q0 - d * 128, 0)
                if d == 1:
                    lo_cur = pl.multiple_of(lo_cur, 128)
                k_lo = jnp.where(first, rows(kp_ref, lo_prev, d), rows(kc_ref, lo_cur, d))
                v_lo = jnp.where(first, rows(vp_ref, lo_prev, d), rows(vc_ref, lo_cur, d))
            kwin = jnp.concatenate([k_lo, k_hi], axis=0)
            vst = _vstack(jnp.concatenate([v_lo, v_hi], axis=0), lane)
            cmin = jnp.where(jnp.logical_and(first, i == 0), 128, 0)
            ok = jnp.where(c >= cmin, dist, 1000) <= 128
            es, ils, lses = [], [], []
            for h in range(4):
                s = _nt_dot(jnp.where(lane // 64 == h, q, jnp.zeros_like(q)), kwin)
                s = jnp.where(ok, s, NEG)
                m = jnp.max(s, axis=-1, keepdims=True)
                ex = jnp.exp(s - m)
                den = jnp.sum(ex, axis=-1, keepdims=True)
                es.append(ex.astype(BF))
                ils.append(1.0 / den)
                lses.append(m + jnp.log(den))
            o = _dot(jnp.concatenate(es, axis=1), vst) * _head_expand(ils, lane, 64)
            lse = _head_expand(lses, lane, 64)
            idx = pl.ds(q0, 128) if d == 1 else pl.ds(q0, 128, stride=d)
            for hf in range(2):
                ob_ref.at[bi, hf][idx, :] = o[:, hf * LANES:(hf + 1) * LANES]
                lb_ref.at[bi, hf][idx, :] = lse[:, hf * LANES:(hf + 1) * LANES]
            return carry

        lax.fori_loop(0, td // 128, block, 0, unroll=8)

    for ch in range(td // 256):
        rs = slice(ch * 256, (ch + 1) * 256)
        halves = []
        for hf in range(2):
            ls = [lb_ref[bi, hf, rs, :] for bi in range(3)]
            mx = jnp.maximum(jnp.maximum(ls[0], ls[1]), ls[2])
            ws = [jnp.exp(l - mx) for l in ls]
            num = ws[0] * ob_ref[0, hf, rs, :] + ws[1] * ob_ref[1, hf, rs, :] + ws[2] * ob_ref[2, hf, rs, :]
            halves.append(num / (ws[0] + ws[1] + ws[2]))
        o_ref[0, rs, :] = _rms(jnp.concatenate(halves, axis=1), g_ref[...]).astype(BF)


def _dil(g, q, k, v, *, td):
    B, _, S, _ = q.shape
    cur = pl.BlockSpec((1, 2, td, LANES), lambda b, i: (b, 0, i, 0))
    prev = pl.BlockSpec((1, 2, td, LANES), lambda b, i: (b, 0, jnp.maximum(i - 1, 0), 0))
    return pl.pallas_call(
        functools.partial(_dil_body, td=td),
        grid=(B, S // td),
        in_specs=[pl.BlockSpec((1, GROUP), lambda b, i: (0, 0)), cur, prev, cur, prev, cur],
        out_specs=pl.BlockSpec((1, td, GROUP), lambda b, i: (b, i, 0)),
        out_shape=jax.ShapeDtypeStruct((B, S, GROUP), BF),
        scratch_shapes=[pltpu.VMEM((3, 2, td, LANES), F32),
                        pltpu.VMEM((3, 2, td, LANES), F32)],
        compiler_params=pltpu.CompilerParams(
            dimension_semantics=("arbitrary", "arbitrary"), vmem_limit_bytes=VMEM_LIMIT),
        name="dilated",
    )(g, q, k, k, v, v)


PEER_EB = 2048
GELU_C1 = math.sqrt(2.0 / math.pi)
GELU_C2 = 0.044715 * GELU_C1
N_CAND = PEER_TOP + 1


def _peer_body(x_ref, oa_ref, ob_ref, oc_ref, od_ref, wo_ref, g_ref, wqt_ref, sk_ref, u_ref, vt_ref,
               gf_ref, o_ref, x2_ref, ht_ref, st_ref, e2_ref, th_ref, cw_ref, hw_ref, acc_ref,
               *, tt, final):
    e = pl.program_id(1)
    nch = tt // LANES

    @pl.when(e == 0)
    def _prologue():
        y = _dot(oa_ref[...], wo_ref[0:256, :])
        y += _dot(ob_ref[...], wo_ref[256:512, :])
        y += _dot(oc_ref[...], wo_ref[512:768, :])
        y += _dot(od_ref[...], wo_ref[768:1024, :])
        x2_ref[...] = x_ref[...] + y
        h = _rms(x2_ref[...], g_ref[...])
        ht_ref[...] = h.T.astype(BF)
        qt = _dot(wqt_ref[...], ht_ref[...]).astype(BF)
        for hh in range(PEER_HEADS):
            for c in range(2):
                r0 = hh * N_KEYS + c * 64
                sc = _dot(sk_ref[c], qt[r0:r0 + 64, :])
                for tc in range(nch):
                    blk = sc[:, tc * LANES:(tc + 1) * LANES]
                    st_ref.at[tc][pl.ds(c * 1024 + hh, N_KEYS, stride=PEER_HEADS), :] = blk
                    if c == 1:
                        st_ref[tc, 2048 + hh * N_KEYS:2048 + (hh + 1) * N_KEYS, :] = blk
        acc_ref[...] = jnp.zeros(acc_ref.shape, F32)

        def top_vals(load, n_rows, n_out):
            top = []
            for r in range(n_rows):
                v = load(r)
                for q in range(len(top)):
                    top[q], v = jnp.maximum(top[q], v), jnp.minimum(top[q], v)
                if len(top) < n_out:
                    top.append(v)
            return top

        def token_chunk(tc, carry):
            ls = pl.ds(pl.multiple_of(tc * LANES, LANES), LANES)
            v1 = top_vals(lambda r: st_ref[tc, r * 8:(r + 1) * 8, :], N_KEYS, N_CAND)
            v2 = top_vals(lambda r: st_ref[tc, 1024 + r * 8:1024 + (r + 1) * 8, :], N_KEYS, N_CAND)
            cands = [v1[a] + v2[b] for a in range(N_CAND) for b in range(N_CAND)
                     if (a + 1) * (b + 1) <= N_CAND]
            top = top_vals(lambda r: cands[r], len(cands), N_CAND)
            z = jnp.ones_like(top[0])
            for r in range(1, PEER_TOP):
                z = z + jnp.exp(top[r] - top[0])
            tau = 0.5 * (top[PEER_TOP - 1] + top[PEER_TOP])
            iz = 1.0 / z
            for r in range(N_KEYS):
                s1 = st_ref[tc, r * 8:(r + 1) * 8, :]
                th_ref[r, :, ls] = tau - s1
                cw_ref[r, :, ls] = jnp.exp(s1 - v1[0]) * (0.5 * iz)
            for hh in range(PEER_HEADS):
                rows = slice(2048 + hh * 128, 2048 + (hh + 1) * 128)
                e2_ref[hh * 128:(hh + 1) * 128, ls] = jnp.exp(st_ref[tc, rows, :] - v2[0][hh:hh + 1, :])
            return carry

        lax.fori_loop(0, nch, token_chunk, 0)

    nblk = PEER_EB // 256

    def pre_act(b):
        return _dot(u_ref[b * 256:(b + 1) * 256, :], ht_ref[...])

    def gate(b, a):
        i0 = e * (PEER_EB // N_KEYS) + 2 * b
        th = [th_ref[i0], th_ref[i0 + 1]]
        cw = [cw_ref[i0], cw_ref[i0 + 1]]
        for tc in range(nch):
            ls = slice(tc * LANES, (tc + 1) * LANES)
            for sub in range(2):
                w = [None, None]
                for hh in range(PEER_HEADS):
                    r0 = hh * N_KEYS + sub * 64
                    s2 = st_ref[tc, 2048 + r0:2048 + r0 + 64, :]
                    e2 = e2_ref[r0:r0 + 64, ls]
                    for half in range(2):
                        sel = jnp.where(s2 >= th[half][hh:hh + 1, ls], e2, 0.0) * cw[half][hh:hh + 1, ls]
                        w[half] = sel if w[half] is None else w[half] + sel
                for half in range(2):
                    r1 = half * N_KEYS + sub * 64
                    x = a[r1:r1 + 64, ls]
                    xw = x * w[half]
                    t = jnp.tanh(x * (GELU_C1 + GELU_C2 * (x * x)))
                    hw_ref[b % 2, r1:r1 + 64, ls] = (xw + xw * t).astype(BF)

    a = pre_act(0)
    for b in range(nblk):
        a_next = pre_act(b + 1) if b + 1 < nblk else None
        gate(b, a)
        if b > 0:
            acc_ref[...] += _dot(vt_ref[b - 1], hw_ref[(b - 1) % 2])
        a = a_next
    acc_ref[...] += _dot(vt_ref[nblk - 1], hw_ref[(nblk - 1) % 2])

    @pl.when(e == pl.num_programs(1) - 1)
    def _finish():
        y = x2_ref[...] + acc_ref[...].T
        if final:
            y = _rms(y, gf_ref[...])
        o_ref[...] = y


def _peer(x, oa, ob, oc, od, w_out, g, wqt, sk, u, vt, gf, *, tt, final):
    T, D = x.shape
    ne = u.shape[0] // PEER_EB
    const = lambda shape: pl.BlockSpec(shape, lambda i, e: (0,) * len(shape),
                                       pipeline_mode=pl.Buffered(1))
    return pl.pallas_call(
        functools.partial(_peer_body, tt=tt, final=final),
        grid=(T // tt, ne),
        in_specs=[pl.BlockSpec((tt, D), lambda i, e: (i, 0))]
        + [pl.BlockSpec((tt, GROUP), lambda i, e: (i, 0))] * 4
        + [const((D, D)), const((1, D)), const((D, D)), const((2, N_KEYS, 64)),
                  pl.BlockSpec((PEER_EB, D), lambda i, e: (e, 0)),
                  pl.BlockSpec((PEER_EB // 256, D, 256), lambda i, e: (e, 0, 0)),
                  const((1, D))],
        out_specs=pl.BlockSpec((tt, D), lambda i, e: (i, 0)),
        out_shape=jax.ShapeDtypeStruct((T, D), F32),
        scratch_shapes=[pltpu.VMEM((tt, D), F32),
                        pltpu.VMEM((D, tt), BF),
                        pltpu.VMEM((tt // LANES, 3 * 1024, LANES), F32),
                        pltpu.VMEM((1024, tt), F32),
                        pltpu.VMEM((N_KEYS, PEER_HEADS, tt), F32),
                        pltpu.VMEM((N_KEYS, PEER_HEADS, tt), F32),
                        pltpu.VMEM((2, 256, tt), BF),
                        pltpu.VMEM((D, tt), F32)],
        compiler_params=pltpu.CompilerParams(
            dimension_semantics=("arbitrary", "arbitrary"), vmem_limit_bytes=VMEM_LIMIT),
        name="peer",
    )(x, oa, ob, oc, od, w_out, g, wqt, sk, u, vt, gf)


def _pack_in_weights(w_in):
    aq, ak, av, cq, ckv, kpe, zc, dq, dk, dv = jnp.split(
        w_in, np.cumsum([256, 256, 256, 256, 128, 32, 512, 256, 256])[:].tolist(), axis=1)
    kpe_rep = jnp.zeros((D_MODEL, 4, 128), w_in.dtype).at[:, :, 64:96].set(kpe[:, None, :])
    return jnp.concatenate(
        [aq, ak, av, cq, ckv, kpe_rep.reshape(D_MODEL, 512), zc, dq, dk, dv], axis=1).astype(BF)


def _pack_mla_weights(w_uq, w_ukv):
    uq = jnp.pad(w_uq.reshape(256, 4, 96), ((0, 0), (0, 0), (0, 32))).reshape(256, 512)
    ukv = w_ukv.reshape(128, 4, 128)
    uk = jnp.pad(ukv[:, :, :64], ((0, 0), (0, 0), (0, 64))).reshape(128, 512)
    uv = ukv[:, :, 64:].reshape(128, 256)
    return uq.astype(BF), uk.astype(BF), uv.astype(BF)


def _pack_expert_out(v):
    n, d = v.shape
    return v.reshape(n // 256, 256, d).transpose(0, 2, 1).astype(BF)


def kernel(x, w_in, w_out, norm_mix, norm_ffn, diff_lambda, diff_head_norm, mla_q_norm, mla_kv_norm, mla_w_uq, mla_w_ukv, mla_out_norm, sgu_v_norm, sgu_w, sgu_b, sgu_out_norm, dil_out_norm, peer_w_q, peer_sub_keys, peer_u, peer_v, final_norm):
    B, S, D = x.shape
    depth = w_in.shape[0]
    T = B * S
    tm = min(INPROJ_TILE, S)
    ta, tb = min(DIFF_TILE, S), min(MLA_TILE, S)
    tt = min(PEER_TOKENS, T)
    assert S % tm == 0 and S % ta == 0 and S % tb == 0 and S % DIL_TILE == 0 and T % tt == 0, (B, S)
    assert ta % QB == 0 and tb % QB == 0 and tt % LANES == 0 and D == D_MODEL
    row = lambda a: a.reshape(1, -1)

    tabs = (_rope_tables(S, 8, 32, 0) + _rope_tables(S, 32, 128, 64) + _rope_tables(S, 16, 64, 0))

    for l in range(depth):
        lam_init = 0.8 - 0.6 * math.exp(-0.3 * l)
        w_cat = _pack_in_weights(w_in[l])
        wuq, wuk, wuv = _pack_mla_weights(mla_w_uq[l], mla_w_ukv[l])
        sgub = jnp.repeat(sgu_b[l].T, 64, axis=1)
        qa, ka, va, qb, kb, vb, oc, qd, kd, vd = _inproj(
            x, row(norm_mix[l]), w_cat, tabs, row(mla_q_norm[l]), row(mla_kv_norm[l]),
            wuq, wuk, wuv, row(sgu_v_norm[l]), sgu_w[l], sgub, row(sgu_out_norm[l]), tm=tm)

        oa = _attn_a(diff_lambda[l], row(diff_head_norm[l]), qa, ka, va, t=ta, lam_init=lam_init)
        ob = _attn_b(row(mla_out_norm[l]), qb, kb, vb, t=tb)

        od = _dil(row(dil_out_norm[l]), qd, kd, vd, td=DIL_TILE)

        x2 = _peer(x.reshape(T, D), oa.reshape(T, GROUP), ob.reshape(T, GROUP), oc.reshape(T, GROUP),
                   od.reshape(T, GROUP), w_out[l].astype(BF), row(norm_ffn[l]), peer_w_q[l].T.astype(BF), peer_sub_keys[l].astype(BF),
                   peer_u[l].astype(BF), _pack_expert_out(peer_v[l]), row(final_norm),
                   tt=tt, final=(l == depth - 1))
        x = x2.reshape(B, S, D)
    return x
```

```python
import functools
import math

import numpy as np
import jax
import jax.numpy as jnp
from jax import lax
from jax.experimental import pallas as pl
from jax.experimental.pallas import tpu as pltpu

D_MODEL = 1024
GROUP = 256
ROPE_THETA = 500000.0
NEG = -1e30
EPS = 1e-6
LANES = 128
LOG2E = math.log2(math.e)

N_KEYS = 128
PEER_HEADS = 8
PEER_TOP = 16

BF = jnp.bfloat16
F32 = jnp.float32

C_AQ, C_AK, C_AV, C_CQ, C_CKV, C_KPE, C_ZC, C_DQ, C_DK, C_DV, C_END = (
    0, 256, 512, 768, 1024, 1152, 1664, 2176, 2432, 2688, 2944)

VMEM_LIMIT = 56 * 1024 * 1024
INPROJ_TILE = 1024
PEER_TOKENS = 512


def _rms(x, g):
    return x * lax.rsqrt(jnp.mean(x * x, axis=-1, keepdims=True) + EPS) * g


def _nt_dot(a, b):
    return lax.dot_general(a, b, (((1,), (1,)), ((), ())), preferred_element_type=F32)


def _dot(a, b):
    return jnp.dot(a, b, preferred_element_type=F32)


def _rope_tables(seq, n_rot, period, offset):
    half = n_rot // 2
    pos = jnp.arange(seq, dtype=F32)
    inv = ROPE_THETA ** (-jnp.arange(half, dtype=F32) * (2.0 / n_rot))
    ang = pos[:, None] * inv[None, :]
    cos, sin = jnp.cos(ang), jnp.sin(ang)
    g = np.arange(LANES) % period - offset
    rot = (g >= 0) & (g < n_rot)
    idx = np.where(rot, g % half, 0)
    sign = np.where(g < half, -1.0, 1.0).astype(np.float32)
    cos_t = jnp.where(rot[None, :], cos[:, idx], 1.0)
    sin_t = jnp.where(rot[None, :], sin[:, idx] * sign[None, :], 0.0)
    return cos_t.astype(F32), sin_t.astype(F32)


def _rope_apply(x, cos_t, sin_t, n_rot, period, offset):
    half = n_rot // 2
    lane = lax.broadcasted_iota(jnp.int32, (1, LANES), 1)
    first = (lane % period - offset) < half
    outs = []
    for c in range(x.shape[1] // LANES):
        xc = x[:, c * LANES:(c + 1) * LANES]
        fwd = pltpu.roll(xc, LANES - half, 1)
        bwd = pltpu.roll(xc, half, 1)
        outs.append(xc * cos_t + jnp.where(first, fwd, bwd) * sin_t)
    return outs[0] if len(outs) == 1 else jnp.concatenate(outs, axis=1)


VROWS = 80


def _value_rows(vt):
    one = (lax.broadcasted_iota(jnp.int32, (VROWS - 64, vt.shape[1]), 0) == 0).astype(F32)
    parts = []
    for h in range(4):
        parts += [vt[64 * h:64 * (h + 1)], one]
    return jnp.concatenate(parts, axis=0).astype(BF)


def _inproj_body(x_ref, gmix_ref, w_ref, ca_ref, sa_ref, cb_ref, sb_ref, cd_ref, sd_ref,
                 gq_ref, gkv_ref, wuq_ref, wuk_ref, wuv_ref,
                 gsv_ref, sguw_ref, sgub_ref, gso_ref,
                 qa_ref, ka_ref, va_ref, qb_ref, kb_ref, vb_ref, oc_ref,
                 qd_ref, kd_ref, vd_ref, *, tm):
    x = x_ref[0]
    h = _rms(x, gmix_ref[...]).astype(BF)

    def proj(a, b):
        return _dot(h, w_ref[:, a:b])

    ca, sa = ca_ref[...], sa_ref[...]
    aq = _rope_apply(proj(C_AQ, C_AK), ca, sa, 8, 32, 0)
    qa_ref[0] = (aq * (32.0 ** -0.5 * LOG2E)).T.astype(BF)
    ka_ref[0] = _rope_apply(proj(C_AK, C_AV), ca, sa, 8, 32, 0).astype(BF)
    va_ref[0] = _value_rows(proj(C_AV, C_CQ).T)

    cb, sb = cb_ref[...], sb_ref[...]
    cq = _rms(proj(C_CQ, C_CKV), gq_ref[...]).astype(BF)
    qb = _rope_apply(_dot(cq, wuq_ref[...]), cb, sb, 32, 128, 64)
    qb_ref[0] = (qb * (96.0 ** -0.5 * LOG2E)).T.astype(BF)
    ckv = _rms(proj(C_CKV, C_KPE), gkv_ref[...]).astype(BF)
    kpe = _rope_apply(proj(C_KPE, C_ZC), cb, sb, 32, 128, 64)
    kb_ref[0] = (_dot(ckv, wuk_ref[...]) + kpe).astype(BF)
    vb_ref[0] = _value_rows(_dot(ckv, wuv_ref[...]).T)

    zc = jax.nn.gelu(proj(C_ZC, C_DQ))
    u = zc[:, :GROUP]
    vn = _rms(zc[:, GROUP:], gsv_ref[...]).astype(BF)
    r = lax.broadcasted_iota(jnp.int32, (128, 128), 0)
    c = lax.broadcasted_iota(jnp.int32, (128, 128), 1)
    wcat = jnp.concatenate(
        [jnp.where(r >= c, sguw_ref[g], 0.0).astype(BF) for g in range(4)], axis=1)
    lane = lax.broadcasted_iota(jnp.int32, (1, GROUP), 1)
    bias = sgub_ref[...]
    gso = gso_ref[...]
    for ch in range(tm // 128):
        vc = vn[ch * 128:(ch + 1) * 128]
        vst = jnp.concatenate(
            [jnp.where(lane // 64 == g, vc, jnp.zeros_like(vc)) for g in range(4)], axis=0)
        sv = _dot(wcat, vst) + bias
        oc = u[ch * 128:(ch + 1) * 128] * sv
        oc_ref[0, ch * 128:(ch + 1) * 128, :] = _rms(oc, gso).astype(BF)

    cd, sd = cd_ref[...], sd_ref[...]
    dq = _rope_apply(proj(C_DQ, C_DK), cd, sd, 16, 64, 0)
    dk = _rope_apply(proj(C_DK, C_DV), cd, sd, 16, 64, 0)
    dv = proj(C_DV, C_END)
    for hf in range(2):
        qd_ref[0, hf] = dq[:, hf * LANES:(hf + 1) * LANES] * 0.125
        kd_ref[0, hf] = dk[:, hf * LANES:(hf + 1) * LANES]
        vd_ref[0, hf] = dv[:, hf * LANES:(hf + 1) * LANES]


def _inproj(x, gmix, w_cat, tabs, gq, gkv, wuq, wuk, wuv, gsv, sguw, sgub, gso, *, tm):
    B, S, D = x.shape
    ns = S // tm
    full = lambda shape: pl.BlockSpec(shape, lambda s, b: (0,) * len(shape))
    tab = pl.BlockSpec((tm, LANES), lambda s, b: (s, 0))
    seq = lambda w: pl.BlockSpec((1, tm, w), lambda s, b: (b, s, 0))
    seq_t = lambda w: pl.BlockSpec((1, w, tm), lambda s, b: (b, 0, s))
    halves = pl.BlockSpec((1, 2, tm, LANES), lambda s, b: (b, 0, s, 0))
    out_w = [256, 256, 4 * VROWS, 512, 512, 4 * VROWS, 256]
    transposed = [True, False, True, True, False, True, False]
    return pl.pallas_call(
        functools.partial(_inproj_body, tm=tm),
        grid=(ns, B),
        in_specs=[seq(D), full((1, D)), full((D, C_END))] + [tab] * 6 + [
            full((1, 256)), full((1, 128)), full((256, 512)), full((128, 512)), full((128, 256)),
            full((1, 256)), full((4, 128, 128)), full((128, 256)), full((1, 256))],
        out_specs=[seq_t(w) if tr else seq(w) for w, tr in zip(out_w, transposed)] + [halves] * 3,
        out_shape=[jax.ShapeDtypeStruct((B, w, S) if tr else (B, S, w), BF)
                   for w, tr in zip(out_w, transposed)]
        + [jax.ShapeDtypeStruct((B, 2, S, LANES), F32)] * 3,
        compiler_params=pltpu.CompilerParams(
            dimension_semantics=("arbitrary", "arbitrary"), vmem_limit_bytes=VMEM_LIMIT),
        name="inproj",
    )(x, gmix, w_cat, *tabs, gq, gkv, wuq, wuk, wuv, gsv, sguw, sgub, gso)


def _head_expand(cols, lane, width):
    out = cols[-1]
    for h in range(len(cols) - 2, -1, -1):
        out = jnp.where(lane // width == h, cols[h], out)
    return out


DIFF_TILE = 1024
MLA_TILE = 1024
QB = 512


def _online_softmax_t(st, m_ref, j, cs):
    m_prev = m_ref[j:j + 1, cs]
    m_new = jnp.maximum(m_prev, jnp.max(st, axis=0, keepdims=True))
    alpha = jnp.exp2(m_prev - m_new)
    et = jnp.exp2(st - m_new)
    m_ref[j:j + 1, cs] = m_new
    return et.astype(BF), alpha


def _attn_jobs(ns, diag, heads):
    return [(*hd, qc, kc) for kc in range(ns) for qc in range(ns) for hd in heads
            if not (diag and kc > qc)]


def _causal_keep():
    return (lax.broadcasted_iota(jnp.int32, (QB, QB), 0)
            <= lax.broadcasted_iota(jnp.int32, (QB, QB), 1))


def _causal_pairs(n):
    pairs = [(i, j) for i in range(n) for j in range(i + 1)]
    return (jnp.asarray([p[0] for p in pairs], jnp.int32),
            jnp.asarray([p[1] for p in pairs], jnp.int32))


def _attn_a_body(qtab_ref, ktab_ref, lam_ref, gh_ref, q_ref, k_ref, v_ref, o_ref,
                 qs_ref, m_ref, acc_ref, *, t, lam_init):
    qi = qtab_ref[pl.program_id(1)]
    ki = ktab_ref[pl.program_id(1)]

    @pl.when(ki == 0)
    def _init():
        m_ref[...] = jnp.full(m_ref.shape, NEG, F32)
        acc_ref[...] = jnp.zeros(acc_ref.shape, F32)
        row = lax.broadcasted_iota(jnp.int32, (LANES, 1), 0)
        for j in range(8):
            grp = q_ref[0, (j // 4) * LANES:(j // 4 + 1) * LANES, :]
            qs_ref[j] = jnp.where(row // 32 == j % 4, grp, jnp.zeros_like(grp))

    def step(diag):
        jobs = _attn_jobs(t // QB, diag, [(mp, h) for mp in range(2) for h in range(4)])
        kblk = [k_ref[0, kc * QB:(kc + 1) * QB, :] for kc in range(t // QB)]
        keep = _causal_keep() if diag else None

        def qk(mp, h, qc, kc):
            j = 2 * h + mp
            return _dot(kblk[kc][:, (j // 4) * LANES:(j // 4 + 1) * LANES],
                        qs_ref[j, :, qc * QB:(qc + 1) * QB])

        def pv(mp, h, qc, kc, pt, alpha):
            hs, cs = slice(VROWS * h, VROWS * (h + 1)), slice(qc * QB, (qc + 1) * QB)
            acc_ref[mp, hs, cs] = (acc_ref[mp, hs, cs] * alpha
                                   + _dot(v_ref[0, hs, kc * QB:(kc + 1) * QB], pt))

        st_next = qk(*jobs[0])
        pending = None
        for n, (mp, h, qc, kc) in enumerate(jobs):
            st = st_next
            if n + 1 < len(jobs):
                st_next = qk(*jobs[n + 1])
            if diag and kc == qc:
                st = jnp.where(keep, st, NEG)
            pt, alpha = _online_softmax_t(st, m_ref, 2 * h + mp, slice(qc * QB, (qc + 1) * QB))
            if pending is not None:
                pv(*pending)
            pending = (mp, h, qc, kc, pt, alpha)
        pv(*pending)

    @pl.when(ki < qi)
    def _off():
        step(False)

    @pl.when(ki == qi)
    def _diag():
        step(True)
        lf = lam_ref[...]
        lam = (jnp.exp(jnp.sum(lf[0:1] * lf[1:2], axis=-1, keepdims=True))
               - jnp.exp(jnp.sum(lf[2:3] * lf[3:4], axis=-1, keepdims=True)) + lam_init)
        rows = []
        for h in range(4):
            hs, ls = slice(VROWS * h, VROWS * h + 64), slice(VROWS * h + 64, VROWS * h + 65)
            oh = (acc_ref[0, hs, :] * (1.0 / acc_ref[0, ls, :])
                  - lam * (acc_ref[1, hs, :] * (1.0 / acc_ref[1, ls, :])))
            ms = jnp.mean(oh * oh, axis=0, keepdims=True)
            rows.append(oh * lax.rsqrt(ms + EPS))
        ot = jnp.concatenate(rows, axis=0)
        o_ref[0] = (ot.T * gh_ref[...] * (1.0 - lam_init)).astype(BF)


def _attn_a(lam_p, gh, q, k, v, *, t, lam_init):
    B, S, _ = k.shape
    n = S // t
    qtab, ktab = _causal_pairs(n)
    grid_spec = pltpu.PrefetchScalarGridSpec(
        num_scalar_prefetch=2,
        grid=(B, qtab.shape[0]),
        in_specs=[pl.BlockSpec((4, 32), lambda b, p, qt, kt: (0, 0)),
                  pl.BlockSpec((1, GROUP), lambda b, p, qt, kt: (0, 0)),
                  pl.BlockSpec((1, GROUP, t), lambda b, p, qt, kt: (b, 0, qt[p])),
                  pl.BlockSpec((1, t, GROUP), lambda b, p, qt, kt: (b, kt[p], 0)),
                  pl.BlockSpec((1, 4 * VROWS, t), lambda b, p, qt, kt: (b, 0, kt[p]))],
        out_specs=pl.BlockSpec((1, t, GROUP), lambda b, p, qt, kt: (b, qt[p], 0)),
        scratch_shapes=[pltpu.VMEM((8, LANES, t), BF),
                        pltpu.VMEM((8, t), F32),
                        pltpu.VMEM((2, 4 * VROWS, t), F32)])
    return pl.pallas_call(
        functools.partial(_attn_a_body, t=t, lam_init=lam_init),
        grid_spec=grid_spec,
        out_shape=jax.ShapeDtypeStruct((B, S, GROUP), BF),
        compiler_params=pltpu.CompilerParams(
            dimension_semantics=("arbitrary", "arbitrary"), vmem_limit_bytes=VMEM_LIMIT),
        name="attn_diff",
    )(qtab, ktab, lam_p, gh, q, k, v)


def _attn_b_body(qtab_ref, ktab_ref, g_ref, q_ref, k_ref, v_ref, o_ref, m_ref, acc_ref, *, t):
    qi = qtab_ref[pl.program_id(1)]
    ki = ktab_ref[pl.program_id(1)]

    @pl.when(ki == 0)
    def _init():
        m_ref[...] = jnp.full(m_ref.shape, NEG, F32)
        acc_ref[...] = jnp.zeros(acc_ref.shape, F32)

    def step(diag):
        jobs = _attn_jobs(t // QB, diag, [(h,) for h in range(4)])
        kblk = [k_ref[0, kc * QB:(kc + 1) * QB, :] for kc in range(t // QB)]
        keep = _causal_keep() if diag else None

        def qk(h, qc, kc):
            return _dot(kblk[kc][:, h * 128:(h + 1) * 128],
                        q_ref[0, h * 128:(h + 1) * 128, qc * QB:(qc + 1) * QB])

        def pv(h, qc, kc, pt, alpha):
            hs, cs = slice(VROWS * h, VROWS * (h + 1)), slice(qc * QB, (qc + 1) * QB)
            acc_ref[hs, cs] = acc_ref[hs, cs] * alpha + _dot(v_ref[0, hs, kc * QB:(kc + 1) * QB], pt)

        st_next = qk(*jobs[0])
        pending = None
        for n, (h, qc, kc) in enumerate(jobs):
            st = st_next
            if n + 1 < len(jobs):
                st_next = qk(*jobs[n + 1])
            if diag and kc == qc:
                st = jnp.where(keep, st, NEG)
            pt, alpha = _online_softmax_t(st, m_ref, h, slice(qc * QB, (qc + 1) * QB))
            if pending is not None:
                pv(*pending)
            pending = (h, qc, kc, pt, alpha)
        pv(*pending)

    @pl.when(ki < qi)
    def _off():
        step(False)

    @pl.when(ki == qi)
    def _diag():
        step(True)
        rows = [acc_ref[VROWS * h:VROWS * h + 64, :] * (1.0 / acc_ref[VROWS * h + 64:VROWS * h + 65, :])
                for h in range(4)]
        ot = jnp.concatenate(rows, axis=0)
        o_ref[0] = _rms(ot.T, g_ref[...]).astype(BF)


def _attn_b(g, q, k, v, *, t):
    B, S, _ = k.shape
    n = S // t
    qtab, ktab = _causal_pairs(n)
    grid_spec = pltpu.PrefetchScalarGridSpec(
        num_scalar_prefetch=2,
        grid=(B, qtab.shape[0]),
        in_specs=[pl.BlockSpec((1, GROUP), lambda b, p, qt, kt: (0, 0)),
                  pl.BlockSpec((1, 512, t), lambda b, p, qt, kt: (b, 0, qt[p])),
                  pl.BlockSpec((1, t, 512), lambda b, p, qt, kt: (b, kt[p], 0)),
                  pl.BlockSpec((1, 4 * VROWS, t), lambda b, p, qt, kt: (b, 0, kt[p]))],
        out_specs=pl.BlockSpec((1, t, GROUP), lambda b, p, qt, kt: (b, qt[p], 0)),
        scratch_shapes=[pltpu.VMEM((4, t), F32),
                        pltpu.VMEM((4 * VROWS, t), F32)])
    return pl.pallas_call(
        functools.partial(_attn_b_body, t=t),
        grid_spec=grid_spec,
        out_shape=jax.ShapeDtypeStruct((B, S, GROUP), BF),
        compiler_params=pltpu.CompilerParams(
            dimension_semantics=("arbitrary", "arbitrary"), vmem_limit_bytes=VMEM_LIMIT),
        name="attn_mla",
    )(qtab, ktab, g, q, k, v)


DIL_TILE = 2048
DILATIONS = (1, 4, 16)


def _dil_body(g_ref, q_ref, kp_ref, kc_ref, vp_ref, vc_ref, o_ref, ob_ref, lb_ref, *, td):
    i = pl.program_id(1)
    lane = lax.broadcasted_iota(jnp.int32, (1, GROUP), 1)
    a = lax.broadcasted_iota(jnp.int32, (128, 256), 0)
    c = lax.broadcasted_iota(jnp.int32, (128, 256), 1)
    dist = jnp.where(c - a >= 0, c - a, 1000)

    def rows(ref, start, d):
        idx = pl.ds(start, 128) if d == 1 else pl.ds(start, 128, stride=d)
        return jnp.concatenate([ref.at[0, 0][idx, :], ref.at[0, 1][idx, :]], axis=1).astype(BF)

    for bi, d in enumerate(DILATIONS):
        def block(n, carry, bi=bi, d=d):
            if d == 1:
                r, blk = 0, n
            elif td // d == 128:
                r, blk = n, 0
            else:
                r, blk = n % d, n // d
            q0 = r + d * 128 * blk
            if d == 1:
                q0 = pl.multiple_of(q0, 128)
            q = rows(q_ref, q0, d)
            k_hi, v_hi = rows(kc_ref, q0, d), rows(vc_ref, q0, d)
            lo_prev = td - d * 128 + r
            if isinstance(blk, int):
                first = True
                k_lo, v_lo = rows(kp_ref, lo_prev, d), rows(vp_ref, lo_prev, d)
            else:
                first = blk == 0
                lo_cur = jnp.maximum(q0 - d * 128, 0)
                if d == 1:
                    lo_cur = pl.multiple_of(lo_cur, 128)
                k_lo = jnp.where(first, rows(kp_ref, lo_prev, d), rows(kc_ref, lo_cur, d))
                v_lo = jnp.where(first, rows(vp_ref, lo_prev, d), rows(vc_ref, lo_cur, d))
            kwin = jnp.concatenate([k_lo, k_hi], axis=0)
            vwin = jnp.concatenate([v_lo, v_hi], axis=0)
            cmin = jnp.where(jnp.logical_and(first, i == 0), 128, 0)
            ok = jnp.where(c >= cmin, dist, 1000) <= 128
            ils, lses, outs = [], [], []
            for pair in range(2):
                ps = slice(pair * LANES, (pair + 1) * LANES)
                qg, kg, vg = q[:, ps], kwin[:, ps], vwin[:, ps]
                es = []
                for h in range(2):
                    mine = lane[:, :LANES] // 64 == h
                    s = _nt_dot(jnp.where(mine, qg, jnp.zeros_like(qg)), kg)
                    s = jnp.where(ok, s, NEG)
                    m = jnp.max(s, axis=-1, keepdims=True)
                    ex = jnp.exp(s - m)
                    den = jnp.sum(ex, axis=-1, keepdims=True)
                    es.append(ex.astype(BF))
                    ils.append(1.0 / den)
                    lses.append(m + jnp.log(den))
                vst = jnp.concatenate([jnp.where(lane[:, :LANES] // 64 == h, vg, jnp.zeros_like(vg))
                                       for h in range(2)], axis=0)
                outs.append(_dot(jnp.concatenate(es, axis=1), vst))
            o = jnp.concatenate(outs, axis=1) * _head_expand(ils, lane, 64)
            lse = _head_expand(lses, lane, 64)
            idx = pl.ds(q0, 128) if d == 1 else pl.ds(q0, 128, stride=d)
            for hf in range(2):
                ob_ref.at[bi, hf][idx, :] = o[:, hf * LANES:(hf + 1) * LANES]
                lb_ref.at[bi, hf][idx, :] = lse[:, hf * LANES:(hf + 1) * LANES]
            return carry

        lax.fori_loop(0, td // 128, block, 0, unroll=8)

    for ch in range(td // 256):
        rs = slice(ch * 256, (ch + 1) * 256)
        halves = []
        for hf in range(2):
            ls = [lb_ref[bi, hf, rs, :] for bi in range(3)]
            mx = jnp.maximum(jnp.maximum(ls[0], ls[1]), ls[2])
            ws = [jnp.exp(l - mx) for l in ls]
            num = ws[0] * ob_ref[0, hf, rs, :] + ws[1] * ob_ref[1, hf, rs, :] + ws[2] * ob_ref[2, hf, rs, :]
            halves.append(num / (ws[0] + ws[1] + ws[2]))
        o_ref[0, rs, :] = _rms(jnp.concatenate(halves, axis=1), g_ref[...]).astype(BF)


def _dil(g, q, k, v, *, td):
    B, _, S, _ = q.shape
    cur = pl.BlockSpec((1, 2, td, LANES), lambda b, i: (b, 0, i, 0))
    prev = pl.BlockSpec((1, 2, td, LANES), lambda b, i: (b, 0, jnp.maximum(i - 1, 0), 0))
    return pl.pallas_call(
        functools.partial(_dil_body, td=td),
        grid=(B, S // td),
        in_specs=[pl.BlockSpec((1, GROUP), lambda b, i: (0, 0)), cur, prev, cur, prev, cur],
        out_specs=pl.BlockSpec((1, td, GROUP), lambda b, i: (b, i, 0)),
        out_shape=jax.ShapeDtypeStruct((B, S, GROUP), BF),
        scratch_shapes=[pltpu.VMEM((3, 2, td, LANES), F32),
                        pltpu.VMEM((3, 2, td, LANES), F32)],
        compiler_params=pltpu.CompilerParams(
            dimension_semantics=("arbitrary", "arbitrary"), vmem_limit_bytes=VMEM_LIMIT),
        name="dilated",
    )(g, q, k, k, v, v)


PEER_EB = 2048
GELU_C1 = math.sqrt(2.0 / math.pi)
GELU_C2 = 0.044715 * GELU_C1
N_CAND = PEER_TOP + 1


def _peer_body(x_ref, oa_ref, ob_ref, oc_ref, od_ref, wo_ref, g_ref, wqt_ref, sk_ref, u_ref, vt_ref,
               gf_ref, o_ref, x2_ref, ht_ref, st_ref, e2_ref, th_ref, cw_ref, hw_ref, acc_ref,
               *, tt, final):
    e = pl.program_id(1)
    nch = tt // LANES

    @pl.when(e == 0)
    def _prologue():
        y = _dot(oa_ref[...], wo_ref[0:256, :])
        y += _dot(ob_ref[...], wo_ref[256:512, :])
        y += _dot(oc_ref[...], wo_ref[512:768, :])
        y += _dot(od_ref[...], wo_ref[768:1024, :])
        x2_ref[...] = x_ref[...] + y
        h = _rms(x2_ref[...], g_ref[...])
        ht_ref[...] = h.T.astype(BF)
        qt = _dot(wqt_ref[...], ht_ref[...]).astype(BF)
        for hh in range(PEER_HEADS):
            for c in range(2):
                r0 = hh * N_KEYS + c * 64
                sc = _dot(sk_ref[c], qt[r0:r0 + 64, :])
                for tc in range(nch):
                    blk = sc[:, tc * LANES:(tc + 1) * LANES]
                    st_ref.at[tc][pl.ds(c * 1024 + hh, N_KEYS, stride=PEER_HEADS), :] = blk
                    if c == 1:
                        st_ref[tc, 2048 + hh * N_KEYS:2048 + (hh + 1) * N_KEYS, :] = blk
        acc_ref[...] = jnp.zeros(acc_ref.shape, F32)

        def top_vals(load, n_rows, n_out):
            top = []
            for r in range(n_rows):
                v = load(r)
                for q in range(len(top)):
                    top[q], v = jnp.maximum(top[q], v), jnp.minimum(top[q], v)
                if len(top) < n_out:
                    top.append(v)
            return top

        def token_chunk(tc, carry):
            ls = pl.ds(pl.multiple_of(tc * LANES, LANES), LANES)
            v1 = top_vals(lambda r: st_ref[tc, r * 8:(r + 1) * 8, :], N_KEYS, N_CAND)
            v2 = top_vals(lambda r: st_ref[tc, 1024 + r * 8:1024 + (r + 1) * 8, :], N_KEYS, N_CAND)
            cands = [v1[a] + v2[b] for a in range(N_CAND) for b in range(N_CAND)
                     if (a + 1) * (b + 1) <= N_CAND]
            top = top_vals(lambda r: cands[r], len(cands), N_CAND)
            z = jnp.ones_like(top[0])
            for r in range(1, PEER_TOP):
                z = z + jnp.exp(top[r] - top[0])
            tau = 0.5 * (top[PEER_TOP - 1] + top[PEER_TOP])
            iz = 1.0 / z
            for r in range(N_KEYS):
                s1 = st_ref[tc, r * 8:(r + 1) * 8, :]
                th_ref[r, :, ls] = tau - s1
                cw_ref[r, :, ls] = jnp.exp(s1 - v1[0]) * (0.5 * iz)
            for hh in range(PEER_HEADS):
                rows = slice(2048 + hh * 128, 2048 + (hh + 1) * 128)
                e2_ref[hh * 128:(hh + 1) * 128, ls] = jnp.exp(st_ref[tc, rows, :] - v2[0][hh:hh + 1, :])
            return carry

        lax.fori_loop(0, nch, token_chunk, 0)

    nblk = PEER_EB // 256

    def pre_act(b):
        return _dot(u_ref[b * 256:(b + 1) * 256, :], ht_ref[...])

    def gate(b, a):
        i0 = e * (PEER_EB // N_KEYS) + 2 * b
        th = [th_ref[i0], th_ref[i0 + 1]]
        cw = [cw_ref[i0], cw_ref[i0 + 1]]
        for tc in range(nch):
            ls = slice(tc * LANES, (tc + 1) * LANES)
            for sub in range(2):
                w = [None, None]
                for hh in range(PEER_HEADS):
                    r0 = hh * N_KEYS + sub * 64
                    s2 = st_ref[tc, 2048 + r0:2048 + r0 + 64, :]
                    e2 = e2_ref[r0:r0 + 64, ls]
                    for half in range(2):
                        sel = jnp.where(s2 >= th[half][hh:hh + 1, ls], e2, 0.0) * cw[half][hh:hh + 1, ls]
                        w[half] = sel if w[half] is None else w[half] + sel
                for half in range(2):
                    r1 = half * N_KEYS + sub * 64
                    x = a[r1:r1 + 64, ls]
                    xw = x * w[half]
                    t = jnp.tanh(x * (GELU_C1 + GELU_C2 * (x * x)))
                    hw_ref[b % 2, r1:r1 + 64, ls] = (xw + xw * t).astype(BF)

    a = pre_act(0)
    for b in range(nblk):
        a_next = pre_act(b + 1) if b + 1 < nblk else None
        gate(b, a)
        if b > 0:
            acc_ref[...] += _dot(vt_ref[b - 1], hw_ref[(b - 1) % 2])
        a = a_next
    acc_ref[...] += _dot(vt_ref[nblk - 1], hw_ref[(nblk - 1) % 2])

    @pl.when(e == pl.num_programs(1) - 1)
    def _finish():
        y = x2_ref[...] + acc_ref[...].T
        if final:
            y = _rms(y, gf_ref[...])
        o_ref[...] = y


def _peer(x, oa, ob, oc, od, w_out, g, wqt, sk, u, vt, gf, *, tt, final):
    T, D = x.shape
    ne = u.shape[0] // PEER_EB
    const = lambda shape: pl.BlockSpec(shape, lambda i, e: (0,) * len(shape),
                                       pipeline_mode=pl.Buffered(1))
    return pl.pallas_call(
        functools.partial(_peer_body, tt=tt, final=final),
        grid=(T // tt, ne),
        in_specs=[pl.BlockSpec((tt, D), lambda i, e: (i, 0))]
        + [pl.BlockSpec((tt, GROUP), lambda i, e: (i, 0))] * 4
        + [const((D, D)), const((1, D)), const((D, D)), const((2, N_KEYS, 64)),
                  pl.BlockSpec((PEER_EB, D), lambda i, e: (e, 0)),
                  pl.BlockSpec((PEER_EB // 256, D, 256), lambda i, e: (e, 0, 0)),
                  const((1, D))],
        out_specs=pl.BlockSpec((tt, D), lambda i, e: (i, 0)),
        out_shape=jax.ShapeDtypeStruct((T, D), F32),
        scratch_shapes=[pltpu.VMEM((tt, D), F32),
                        pltpu.VMEM((D, tt), BF),
                        pltpu.VMEM((tt // LANES, 3 * 1024, LANES), F32),
                        pltpu.VMEM((1024, tt), F32),
                        pltpu.VMEM((N_KEYS, PEER_HEADS, tt), F32),
                        pltpu.VMEM((N_KEYS, PEER_HEADS, tt), F32),
                        pltpu.VMEM((2, 256, tt), BF),
                        pltpu.VMEM((D, tt), F32)],
        compiler_params=pltpu.CompilerParams(
            dimension_semantics=("arbitrary", "arbitrary"), vmem_limit_bytes=VMEM_LIMIT),
        name="peer",
    )(x, oa, ob, oc, od, w_out, g, wqt, sk, u, vt, gf)


def _pack_in_weights(w_in):
    aq, ak, av, cq, ckv, kpe, zc, dq, dk, dv = jnp.split(
        w_in, np.cumsum([256, 256, 256, 256, 128, 32, 512, 256, 256])[:].tolist(), axis=1)
    kpe_rep = jnp.zeros((D_MODEL, 4, 128), w_in.dtype).at[:, :, 64:96].set(kpe[:, None, :])
    return jnp.concatenate(
        [aq, ak, av, cq, ckv, kpe_rep.reshape(D_MODEL, 512), zc, dq, dk, dv], axis=1).astype(BF)


def _pack_mla_weights(w_uq, w_ukv):
    uq = jnp.pad(w_uq.reshape(256, 4, 96), ((0, 0), (0, 0), (0, 32))).reshape(256, 512)
    ukv = w_ukv.reshape(128, 4, 128)
    uk = jnp.pad(ukv[:, :, :64], ((0, 0), (0, 0), (0, 64))).reshape(128, 512)
    uv = ukv[:, :, 64:].reshape(128, 256)
    return uq.astype(BF), uk.astype(BF), uv.astype(BF)


def _pack_expert_out(v):
    n, d = v.shape
    return v.reshape(n // 256, 256, d).transpose(0, 2, 1).astype(BF)


def kernel(x, w_in, w_out, norm_mix, norm_ffn, diff_lambda, diff_head_norm, mla_q_norm, mla_kv_norm, mla_w_uq, mla_w_ukv, mla_out_norm, sgu_v_norm, sgu_w, sgu_b, sgu_out_norm, dil_out_norm, peer_w_q, peer_sub_keys, peer_u, peer_v, final_norm):
    B, S, D = x.shape
    depth = w_in.shape[0]
    T = B * S
    tm = min(INPROJ_TILE, S)
    ta, tb = min(DIFF_TILE, S), min(MLA_TILE, S)
    tt = min(PEER_TOKENS, T)
    assert S % tm == 0 and S % ta == 0 and S % tb == 0 and S % DIL_TILE == 0 and T % tt == 0, (B, S)
    assert ta % QB == 0 and tb % QB == 0 and tt % LANES == 0 and D == D_MODEL
    row = lambda a: a.reshape(1, -1)

    tabs = (_rope_tables(S, 8, 32, 0) + _rope_tables(S, 32, 128, 64) + _rope_tables(S, 16, 64, 0))

    for l in range(depth):
        lam_init = 0.8 - 0.6 * math.exp(-0.3 * l)
        w_cat = _pack_in_weights(w_in[l])
        wuq, wuk, wuv = _pack_mla_weights(mla_w_uq[l], mla_w_ukv[l])
        sgub = jnp.repeat(sgu_b[l].T, 64, axis=1)
        qa, ka, va, qb, kb, vb, oc, qd, kd, vd = _inproj(
            x, row(norm_mix[l]), w_cat, tabs, row(mla_q_norm[l]), row(mla_kv_norm[l]),
            wuq, wuk, wuv, row(sgu_v_norm[l]), sgu_w[l], sgub, row(sgu_out_norm[l]), tm=tm)

        oa = _attn_a(diff_lambda[l], row(diff_head_norm[l]), qa, ka, va, t=ta, lam_init=lam_init)
        ob = _attn_b(row(mla_out_norm[l]), qb, kb, vb, t=tb)

        od = _dil(row(dil_out_norm[l]), qd, kd, vd, td=DIL_TILE)

        x2 = _peer(x.reshape(T, D), oa.reshape(T, GROUP), ob.reshape(T, GROUP), oc.reshape(T, GROUP),
                   od.reshape(T, GROUP), w_out[l].astype(BF), row(norm_ffn[l]), peer_w_q[l].T.astype(BF), peer_sub_keys[l].astype(BF),
                   peer_u[l].astype(BF), _pack_expert_out(peer_v[l]), row(final_norm),
                   tt=tt, final=(l == depth - 1))
        x = x2.reshape(B, S, D)
    return x
```

```python
import functools
import math

import numpy as np
import jax
import jax.numpy as jnp
from jax import lax
from jax.experimental import pallas as pl
from jax.experimental.pallas import tpu as pltpu

D_MODEL = 1024
GROUP = 256
ROPE_THETA = 500000.0
NEG = -1e30
EPS = 1e-6
LANES = 128
LOG2E = math.log2(math.e)

N_KEYS = 128
PEER_HEADS = 8
PEER_TOP = 16

BF = jnp.bfloat16
F32 = jnp.float32

C_AQ, C_AK, C_AV, C_CQ, C_CKV, C_KPE, C_ZC, C_DQ, C_DK, C_DV, C_END = (
    0, 256, 512, 768, 1024, 1152, 1664, 2176, 2432, 2688, 2944)

VMEM_LIMIT = 56 * 1024 * 1024
INPROJ_TILE = 1024
PEER_TOKENS = 512


def _rms(x, g):
    return x * lax.rsqrt(jnp.mean(x * x, axis=-1, keepdims=True) + EPS) * g


def _nt_dot(a, b):
    return lax.dot_general(a, b, (((1,), (1,)), ((), ())), preferred_element_type=F32)


def _dot(a, b):
    return jnp.dot(a, b, preferred_element_type=F32)


def _rope_tables(seq, n_rot, period, offset):
    half = n_rot // 2
    pos = jnp.arange(seq, dtype=F32)
    inv = ROPE_THETA ** (-jnp.arange(half, dtype=F32) * (2.0 / n_rot))
    ang = pos[:, None] * inv[None, :]
    cos, sin = jnp.cos(ang), jnp.sin(ang)
    g = np.arange(LANES) % period - offset
    rot = (g >= 0) & (g < n_rot)
    idx = np.where(rot, g % half, 0)
    sign = np.where(g < half, -1.0, 1.0).astype(np.float32)
    cos_t = jnp.where(rot[None, :], cos[:, idx], 1.0)
    sin_t = jnp.where(rot[None, :], sin[:, idx] * sign[None, :], 0.0)
    return cos_t.astype(F32), sin_t.astype(F32)


def _rope_apply(x, cos_t, sin_t, n_rot, period, offset):
    half = n_rot // 2
    lane = lax.broadcasted_iota(jnp.int32, (1, LANES), 1)
    first = (lane % period - offset) < half
    outs = []
    for c in range(x.shape[1] // LANES):
        xc = x[:, c * LANES:(c + 1) * LANES]
        fwd = pltpu.roll(xc, LANES - half, 1)
        bwd = pltpu.roll(xc, half, 1)
        outs.append(xc * cos_t + jnp.where(first, fwd, bwd) * sin_t)
    return outs[0] if len(outs) == 1 else jnp.concatenate(outs, axis=1)


VROWS = 80


def _value_rows(vt):
    one = (lax.broadcasted_iota(jnp.int32, (VROWS - 64, vt.shape[1]), 0) == 0).astype(F32)
    parts = []
    for h in range(4):
        parts += [vt[64 * h:64 * (h + 1)], one]
    return jnp.concatenate(parts, axis=0).astype(BF)


def _inproj_body(x_ref, gmix_ref, w_ref, ca_ref, sa_ref, cb_ref, sb_ref, cd_ref, sd_ref,
                 gq_ref, gkv_ref, wuq_ref, wuk_ref, wuv_ref,
                 gsv_ref, sguw_ref, sgub_ref, gso_ref,
                 qa_ref, ka_ref, va_ref, qb_ref, kb_ref, vb_ref, oc_ref,
                 qd_ref, kd_ref, vd_ref, *, tm):
    x = x_ref[0]
    h = _rms(x, gmix_ref[...]).astype(BF)

    def proj(a, b):
        return _dot(h, w_ref[:, a:b])

    ca, sa = ca_ref[...], sa_ref[...]
    aq = _rope_apply(proj(C_AQ, C_AK), ca, sa, 8, 32, 0)
    qa_ref[0] = (aq * (32.0 ** -0.5 * LOG2E)).T.astype(BF)
    ka_ref[0] = _rope_apply(proj(C_AK, C_AV), ca, sa, 8, 32, 0).astype(BF)
    va_ref[0] = _value_rows(proj(C_AV, C_CQ).T)

    cb, sb = cb_ref[...], sb_ref[...]
    cq = _rms(proj(C_CQ, C_CKV), gq_ref[...]).astype(BF)
    qb = _rope_apply(_dot(cq, wuq_ref[...]), cb, sb, 32, 128, 64)
    qb_ref[0] = (qb * (96.0 ** -0.5 * LOG2E)).T.astype(BF)
    ckv = _rms(proj(C_CKV, C_KPE), gkv_ref[...]).astype(BF)
    kpe = _rope_apply(proj(C_KPE, C_ZC), cb, sb, 32, 128, 64)
    kb_ref[0] = (_dot(ckv, wuk_ref[...]) + kpe).astype(BF)
    vb_ref[0] = _value_rows(_dot(ckv, wuv_ref[...]).T)

    zc = jax.nn.gelu(proj(C_ZC, C_DQ))
    u = zc[:, :GROUP]
    vn = _rms(zc[:, GROUP:], gsv_ref[...]).astype(BF)
    r = lax.broadcasted_iota(jnp.int32, (128, 128), 0)
    c = lax.broadcasted_iota(jnp.int32, (128, 128), 1)
    wcat = jnp.concatenate(
        [jnp.where(r >= c, sguw_ref[g], 0.0).astype(BF) for g in range(4)], axis=1)
    lane = lax.broadcasted_iota(jnp.int32, (1, GROUP), 1)
    bias = sgub_ref[...]
    gso = gso_ref[...]
    for ch in range(tm // 128):
        vc = vn[ch * 128:(ch + 1) * 128]
        vst = jnp.concatenate(
            [jnp.where(lane // 64 == g, vc, jnp.zeros_like(vc)) for g in range(4)], axis=0)
        sv = _dot(wcat, vst) + bias
        oc = u[ch * 128:(ch + 1) * 128] * sv
        oc_ref[0, ch * 128:(ch + 1) * 128, :] = _rms(oc, gso).astype(BF)

    cd, sd = cd_ref[...], sd_ref[...]
    dq = _rope_apply(proj(C_DQ, C_DK), cd, sd, 16, 64, 0)
    dk = _rope_apply(proj(C_DK, C_DV), cd, sd, 16, 64, 0)
    dv = proj(C_DV, C_END)
    for hf in range(2):
        qd_ref[0, hf] = dq[:, hf * LANES:(hf + 1) * LANES] * 0.125
        kd_ref[0, hf] = dk[:, hf * LANES:(hf + 1) * LANES]
        vd_ref[0, hf] = dv[:, hf * LANES:(hf + 1) * LANES]


def _inproj(x, gmix, w_cat, tabs, gq, gkv, wuq, wuk, wuv, gsv, sguw, sgub, gso, *, tm):
    B, S, D = x.shape
    ns = S // tm
    full = lambda shape: pl.BlockSpec(shape, lambda s, b: (0,) * len(shape))
    tab = pl.BlockSpec((tm, LANES), lambda s, b: (s, 0))
    seq = lambda w: pl.BlockSpec((1, tm, w), lambda s, b: (b, s, 0))
    seq_t = lambda w: pl.BlockSpec((1, w, tm), lambda s, b: (b, 0, s))
    halves = pl.BlockSpec((1, 2, tm, LANES), lambda s, b: (b, 0, s, 0))
    out_w = [256, 256, 4 * VROWS, 512, 512, 4 * VROWS, 256]
    transposed = [True, False, True, True, False, True, False]
    return pl.pallas_call(
        functools.partial(_inproj_body, tm=tm),
        grid=(ns, B),
        in_specs=[seq(D), full((1, D)), full((D, C_END))] + [tab] * 6 + [
            full((1, 256)), full((1, 128)), full((256, 512)), full((128, 512)), full((128, 256)),
            full((1, 256)), full((4, 128, 128)), full((128, 256)), full((1, 256))],
        out_specs=[seq_t(w) if tr else seq(w) for w, tr in zip(out_w, transposed)] + [halves] * 3,
        out_shape=[jax.ShapeDtypeStruct((B, w, S) if tr else (B, S, w), BF)
                   for w, tr in zip(out_w, transposed)]
        + [jax.ShapeDtypeStruct((B, 2, S, LANES), F32)] * 3,
        compiler_params=pltpu.CompilerParams(
            dimension_semantics=("arbitrary", "arbitrary"), vmem_limit_bytes=VMEM_LIMIT),
        name="inproj",
    )(x, gmix, w_cat, *tabs, gq, gkv, wuq, wuk, wuv, gsv, sguw, sgub, gso)


def _head_expand(cols, lane, width):
    out = cols[-1]
    for h in range(len(cols) - 2, -1, -1):
        out = jnp.where(lane // width == h, cols[h], out)
    return out


def _vstack(v, lane):
    return jnp.concatenate(
        [jnp.where(lane // 64 == h, v, jnp.zeros_like(v)) for h in range(4)], axis=0)


DIFF_TILE = 1024
MLA_TILE = 1024
QB = 512


def _online_softmax_t(st, m_ref, j, cs):
    m_prev = m_ref[j:j + 1, cs]
    m_new = jnp.maximum(m_prev, jnp.max(st, axis=0, keepdims=True))
    alpha = jnp.exp2(m_prev - m_new)
    et = jnp.exp2(st - m_new)
    m_ref[j:j + 1, cs] = m_new
    return et.astype(BF), alpha


def _attn_jobs(ns, diag, heads):
    return [(*hd, qc, kc) for kc in range(ns) for qc in range(ns) for hd in heads
            if not (diag and kc > qc)]


def _causal_keep():
    return (lax.broadcasted_iota(jnp.int32, (QB, QB), 0)
            <= lax.broadcasted_iota(jnp.int32, (QB, QB), 1))


def _causal_pairs(n):
    pairs = [(i, j) for i in range(n) for j in range(i + 1)]
    return (jnp.asarray([p[0] for p in pairs], jnp.int32),
            jnp.asarray([p[1] for p in pairs], jnp.int32))


def _attn_a_body(qtab_ref, ktab_ref, lam_ref, gh_ref, q_ref, k_ref, v_ref, o_ref,
                 qs_ref, m_ref, acc_ref, *, t, lam_init):
    qi = qtab_ref[pl.program_id(1)]
    ki = ktab_ref[pl.program_id(1)]

    @pl.when(ki == 0)
    def _init():
        m_ref[...] = jnp.full(m_ref.shape, NEG, F32)
        acc_ref[...] = jnp.zeros(acc_ref.shape, F32)
        row = lax.broadcasted_iota(jnp.int32, (LANES, 1), 0)
        for j in range(8):
            grp = q_ref[0, (j // 4) * LANES:(j // 4 + 1) * LANES, :]
            qs_ref[j] = jnp.where(row // 32 == j % 4, grp, jnp.zeros_like(grp))

    def step(diag):
        jobs = _attn_jobs(t // QB, diag, [(mp, h) for mp in range(2) for h in range(4)])
        kblk = [k_ref[0, kc * QB:(kc + 1) * QB, :] for kc in range(t // QB)]
        keep = _causal_keep() if diag else None

        def qk(mp, h, qc, kc):
            j = 2 * h + mp
            return _dot(kblk[kc][:, (j // 4) * LANES:(j // 4 + 1) * LANES],
                        qs_ref[j, :, qc * QB:(qc + 1) * QB])

        def pv(mp, h, qc, kc, pt, alpha):
            hs, cs = slice(VROWS * h, VROWS * (h + 1)), slice(qc * QB, (qc + 1) * QB)
            acc_ref[mp, hs, cs] = (acc_ref[mp, hs, cs] * alpha
                                   + _dot(v_ref[0, hs, kc * QB:(kc + 1) * QB], pt))

        st_next = qk(*jobs[0])
        pending = None
        for n, (mp, h, qc, kc) in enumerate(jobs):
            st = st_next
            if n + 1 < len(jobs):
                st_next = qk(*jobs[n + 1])
            if diag and kc == qc:
                st = jnp.where(keep, st, NEG)
            pt, alpha = _online_softmax_t(st, m_ref, 2 * h + mp, slice(qc * QB, (qc + 1) * QB))
            if pending is not None:
                pv(*pending)
            pending = (mp, h, qc, kc, pt, alpha)
        pv(*pending)

    @pl.when(ki < qi)
    def _off():
        step(False)

    @pl.when(ki == qi)
    def _diag():
        step(True)
        lf = lam_ref[...]
        lam = (jnp.exp(jnp.sum(lf[0:1] * lf[1:2], axis=-1, keepdims=True))
               - jnp.exp(jnp.sum(lf[2:3] * lf[3:4], axis=-1, keepdims=True)) + lam_init)
        rows = []
        for h in range(4):
            hs, ls = slice(VROWS * h, VROWS * h + 64), slice(VROWS * h + 64, VROWS * h + 65)
            oh = (acc_ref[0, hs, :] * (1.0 / acc_ref[0, ls, :])
                  - lam * (acc_ref[1, hs, :] * (1.0 / acc_ref[1, ls, :])))
            ms = jnp.mean(oh * oh, axis=0, keepdims=True)
            rows.append(oh * lax.rsqrt(ms + EPS))
        ot = jnp.concatenate(rows, axis=0)
        o_ref[0] = (ot.T * gh_ref[...] * (1.0 - lam_init)).astype(BF)


def _attn_a(lam_p, gh, q, k, v, *, t, lam_init):
    B, S, _ = k.shape
    n = S // t
    qtab, ktab = _causal_pairs(n)
    grid_spec = pltpu.PrefetchScalarGridSpec(
        num_scalar_prefetch=2,
        grid=(B, qtab.shape[0]),
        in_specs=[pl.BlockSpec((4, 32), lambda b, p, qt, kt: (0, 0)),
                  pl.BlockSpec((1, GROUP), lambda b, p, qt, kt: (0, 0)),
                  pl.BlockSpec((1, GROUP, t), lambda b, p, qt, kt: (b, 0, qt[p])),
                  pl.BlockSpec((1, t, GROUP), lambda b, p, qt, kt: (b, kt[p], 0)),
                  pl.BlockSpec((1, 4 * VROWS, t), lambda b, p, qt, kt: (b, 0, kt[p]))],
        out_specs=pl.BlockSpec((1, t, GROUP), lambda b, p, qt, kt: (b, qt[p], 0)),
        scratch_shapes=[pltpu.VMEM((8, LANES, t), BF),
                        pltpu.VMEM((8, t), F32),
                        pltpu.VMEM((2, 4 * VROWS, t), F32)])
    return pl.pallas_call(
        functools.partial(_attn_a_body, t=t, lam_init=lam_init),
        grid_spec=grid_spec,
        out_shape=jax.ShapeDtypeStruct((B, S, GROUP), BF),
        compiler_params=pltpu.CompilerParams(
            dimension_semantics=("arbitrary", "arbitrary"), vmem_limit_bytes=VMEM_LIMIT),
        name="attn_diff",
    )(qtab, ktab, lam_p, gh, q, k, v)


def _attn_b_body(qtab_ref, ktab_ref, g_ref, q_ref, k_ref, v_ref, o_ref, m_ref, acc_ref, *, t):
    qi = qtab_ref[pl.program_id(1)]
    ki = ktab_ref[pl.program_id(1)]

    @pl.when(ki == 0)
    def _init():
        m_ref[...] = jnp.full(m_ref.shape, NEG, F32)
        acc_ref[...] = jnp.zeros(acc_ref.shape, F32)

    def step(diag):
        jobs = _attn_jobs(t // QB, diag, [(h,) for h in range(4)])
        kblk = [k_ref[0, kc * QB:(kc + 1) * QB, :] for kc in range(t // QB)]
        keep = _causal_keep() if diag else None

        def qk(h, qc, kc):
            return _dot(kblk[kc][:, h * 128:(h + 1) * 128],
                        q_ref[0, h * 128:(h + 1) * 128, qc * QB:(qc + 1) * QB])

        def pv(h, qc, kc, pt, alpha):
            hs, cs = slice(VROWS * h, VROWS * (h + 1)), slice(qc * QB, (qc + 1) * QB)
            acc_ref[hs, cs] = acc_ref[hs, cs] * alpha + _dot(v_ref[0, hs, kc * QB:(kc + 1) * QB], pt)

        st_next = qk(*jobs[0])
        pending = None
        for n, (h, qc, kc) in enumerate(jobs):
            st = st_next
            if n + 1 < len(jobs):
                st_next = qk(*jobs[n + 1])
            if diag and kc == qc:
                st = jnp.where(keep, st, NEG)
            pt, alpha = _online_softmax_t(st, m_ref, h, slice(qc * QB, (qc + 1) * QB))
            if pending is not None:
                pv(*pending)
            pending = (h, qc, kc, pt, alpha)
        pv(*pending)

    @pl.when(ki < qi)
    def _off():
        step(False)

    @pl.when(ki == qi)
    def _diag():
        step(True)
        rows = [acc_ref[VROWS * h:VROWS * h + 64, :] * (1.0 / acc_ref[VROWS * h + 64:VROWS * h + 65, :])
                for h in range(4)]
        ot = jnp.concatenate(rows, axis=0)
        o_ref[0] = _rms(ot.T, g_ref[...]).astype(BF)


def _attn_b(g, q, k, v, *, t):
    B, S, _ = k.shape
    n = S // t
    qtab, ktab = _causal_pairs(n)
    grid_spec = pltpu.PrefetchScalarGridSpec(
        num_scalar_prefetch=2,
        grid=(B, qtab.shape[0]),
        in_specs=[pl.BlockSpec((1, GROUP), lambda b, p, qt, kt: (0, 0)),
                  pl.BlockSpec((1, 512, t), lambda b, p, qt, kt: (b, 0, qt[p])),
                  pl.BlockSpec((1, t, 512), lambda b, p, qt, kt: (b, kt[p], 0)),
                  pl.BlockSpec((1, 4 * VROWS, t), lambda b, p, qt, kt: (b, 0, kt[p]))],
        out_specs=pl.BlockSpec((1, t, GROUP), lambda b, p, qt, kt: (b, qt[p], 0)),
        scratch_shapes=[pltpu.VMEM((4, t), F32),
                        pltpu.VMEM((4 * VROWS, t), F32)])
    return pl.pallas_call(
        functools.partial(_attn_b_body, t=t),
        grid_spec=grid_spec,
        out_shape=jax.ShapeDtypeStruct((B, S, GROUP), BF),
        compiler_params=pltpu.CompilerParams(
            dimension_semantics=("arbitrary", "arbitrary"), vmem_limit_bytes=VMEM_LIMIT),
        name="attn_mla",
    )(qtab, ktab, g, q, k, v)


DIL_TILE = 2048
DILATIONS = (1, 4, 16)


def _dil_body(g_ref, q_ref, kp_ref, kc_ref, vp_ref, vc_ref, o_ref, ob_ref, lb_ref, *, td):
    i = pl.program_id(1)
    lane = lax.broadcasted_iota(jnp.int32, (1, GROUP), 1)
    a = lax.broadcasted_iota(jnp.int32, (128, 256), 0)
    c = lax.broadcasted_iota(jnp.int32, (128, 256), 1)
    dist = jnp.where(c - a >= 0, c - a, 1000)

    def rows(ref, start, d):
        idx = pl.ds(start, 128) if d == 1 else pl.ds(start, 128, stride=d)
        return jnp.concatenate([ref.at[0, 0][idx, :], ref.at[0, 1][idx, :]], axis=1).astype(BF)

    for bi, d in enumerate(DILATIONS):
        def block(n, carry, bi=bi, d=d):
            if d == 1:
                r, blk = 0, n
            elif td // d == 128:
                r, blk = n, 0
            else:
                r, blk = n % d, n // d
            q0 = r + d * 128 * blk
            if d == 1:
                q0 = pl.multiple_of(q0, 128)
            q = rows(q_ref, q0, d)
            k_hi, v_hi = rows(kc_ref, q0, d), rows(vc_ref, q0, d)
            lo_prev = td - d * 128 + r
            if isinstance(blk, int):
                first = True
                k_lo, v_lo = rows(kp_ref, lo_prev, d), rows(vp_ref, lo_prev, d)
            else:
                first = blk == 0
                lo_cur = jnp.maximum(q0 - d * 128, 0)
                if d == 1:
                    lo_cur = pl.multiple_of(lo_cur, 128)
                k_lo = jnp.where(first, rows(kp_ref, lo_prev, d), rows(kc_ref, lo_cur, d))
                v_lo = jnp.where(first, rows(vp_ref, lo_prev, d), rows(vc_ref, lo_cur, d))
            kwin = jnp.concatenate([k_lo, k_hi], axis=0)
            vst = _vstack(jnp.concatenate([v_lo, v_hi], axis=0), lane)
            cmin = jnp.where(jnp.logical_and(first, i == 0), 128, 0)
            ok = jnp.where(c >= cmin, dist, 1000) <= 128
            es, ils, lses = [], [], []
            for h in range(4):
                ps = slice((h // 2) * LANES, (h // 2 + 1) * LANES)
                qg = q[:, ps]
                s = _nt_dot(jnp.where(lane[:, ps] // 64 == h, qg, jnp.zeros_like(qg)), kwin[:, ps])
                s = jnp.where(ok, s, NEG)
                m = jnp.max(s, axis=-1, keepdims=True)
                ex = jnp.exp(s - m)
                den = jnp.sum(ex, axis=-1, keepdims=True)
                es.append(ex.astype(BF))
                ils.append(1.0 / den)
                lses.append(m + jnp.log(den))
            o = _dot(jnp.concatenate(es, axis=1), vst) * _head_expand(ils, lane, 64)
            lse = _head_expand(lses, lane, 64)
            idx = pl.ds(q0, 128) if d == 1 else pl.ds(q0, 128, stride=d)
            for hf in range(2):
                ob_ref.at[bi, hf][idx, :] = o[:, hf * LANES:(hf + 1) * LANES]
                lb_ref.at[bi, hf][idx, :] = lse[:, hf * LANES:(hf + 1) * LANES]
            return carry

        lax.fori_loop(0, td // 128, block, 0, unroll=8)

    for ch in range(td // 256):
        rs = slice(ch * 256, (ch + 1) * 256)
        halves = []
        for hf in range(2):
            ls = [lb_ref[bi, hf, rs, :] for bi in range(3)]
            mx = jnp.maximum(jnp.maximum(ls[0], ls[1]), ls[2])
            ws = [jnp.exp(l - mx) for l in ls]
            num = ws[0] * ob_ref[0, hf, rs, :] + ws[1] * ob_ref[1, hf, rs, :] + ws[2] * ob_ref[2, hf, rs, :]
            halves.append(num / (ws[0] + ws[1] + ws[2]))
        o_ref[0, rs, :] = _rms(jnp.concatenate(halves, axis=1), g_ref[...]).astype(BF)


def _dil(g, q, k, v, *, td):
    B, _, S, _ = q.shape
    cur = pl.BlockSpec((1, 2, td, LANES), lambda b, i: (b, 0, i, 0))
    prev = pl.BlockSpec((1, 2, td, LANES), lambda b, i: (b, 0, jnp.maximum(i - 1, 0), 0))
    return pl.pallas_call(
        functools.partial(_dil_body, td=td),
        grid=(B, S // td),
        in_specs=[pl.BlockSpec((1, GROUP), lambda b, i: (0, 0)), cur, prev, cur, prev, cur],
        out_specs=pl.BlockSpec((1, td, GROUP), lambda b, i: (b, i, 0)),
        out_shape=jax.ShapeDtypeStruct((B, S, GROUP), BF),
        scratch_shapes=[pltpu.VMEM((3, 2, td, LANES), F32),
                        pltpu.VMEM((3, 2, td, LANES), F32)],
        compiler_params=pltpu.CompilerParams(
            dimension_semantics=("arbitrary", "arbitrary"), vmem_limit_bytes=VMEM_LIMIT),
        name="dilated",
    )(g, q, k, k, v, v)


PEER_EB = 2048
GELU_C1 = math.sqrt(2.0 / math.pi)
GELU_C2 = 0.044715 * GELU_C1
N_CAND = PEER_TOP + 1


def _peer_body(x_ref, oa_ref, ob_ref, oc_ref, od_ref, wo_ref, g_ref, wqt_ref, sk_ref, u_ref, vt_ref,
               gf_ref, o_ref, x2_ref, ht_ref, st_ref, e2_ref, th_ref, cw_ref, hw_ref, acc_ref,
               *, tt, final):
    e = pl.program_id(1)
    nch = tt // LANES

    @pl.when(e == 0)
    def _prologue():
        y = _dot(oa_ref[...], wo_ref[0:256, :])
        y += _dot(ob_ref[...], wo_ref[256:512, :])
        y += _dot(oc_ref[...], wo_ref[512:768, :])
        y += _dot(od_ref[...], wo_ref[768:1024, :])
        x2_ref[...] = x_ref[...] + y
        h = _rms(x2_ref[...], g_ref[...])
        ht_ref[...] = h.T.astype(BF)
        qt = _dot(wqt_ref[...], ht_ref[...]).astype(BF)
        for hh in range(PEER_HEADS):
            for c in range(2):
                r0 = hh * N_KEYS + c * 64
                sc = _dot(sk_ref[c], qt[r0:r0 + 64, :])
                for tc in range(nch):
                    blk = sc[:, tc * LANES:(tc + 1) * LANES]
                    st_ref.at[tc][pl.ds(c * 1024 + hh, N_KEYS, stride=PEER_HEADS), :] = blk
                    if c == 1:
                        st_ref[tc, 2048 + hh * N_KEYS:2048 + (hh + 1) * N_KEYS, :] = blk
        acc_ref[...] = jnp.zeros(acc_ref.shape, F32)

        def top_vals(load, n_rows, n_out):
            top = []
            for r in range(n_rows):
                v = load(r)
                for q in range(len(top)):
                    top[q], v = jnp.maximum(top[q], v), jnp.minimum(top[q], v)
                if len(top) < n_out:
                    top.append(v)
            return top

        def token_chunk(tc, carry):
            ls = pl.ds(pl.multiple_of(tc * LANES, LANES), LANES)
            v1 = top_vals(lambda r: st_ref[tc, r * 8:(r + 1) * 8, :], N_KEYS, N_CAND)
            v2 = top_vals(lambda r: st_ref[tc, 1024 + r * 8:1024 + (r + 1) * 8, :], N_KEYS, N_CAND)
            cands = [v1[a] + v2[b] for a in range(N_CAND) for b in range(N_CAND)
                     if (a + 1) * (b + 1) <= N_CAND]
            top = top_vals(lambda r: cands[r], len(cands), N_CAND)
            z = jnp.ones_like(top[0])
            for r in range(1, PEER_TOP):
                z = z + jnp.exp(top[r] - top[0])
            tau = 0.5 * (top[PEER_TOP - 1] + top[PEER_TOP])
            iz = 1.0 / z
            for r in range(N_KEYS):
                s1 = st_ref[tc, r * 8:(r + 1) * 8, :]
                th_ref[r, :, ls] = tau - s1
                cw_ref[r, :, ls] = jnp.exp(s1 - v1[0]) * (0.5 * iz)
            for hh in range(PEER_HEADS):
                rows = slice(2048 + hh * 128, 2048 + (hh + 1) * 128)
                e2_ref[hh * 128:(hh + 1) * 128, ls] = jnp.exp(st_ref[tc, rows, :] - v2[0][hh:hh + 1, :])
            return carry

        lax.fori_loop(0, nch, token_chunk, 0)

    nblk = PEER_EB // 256

    def pre_act(b):
        return _dot(u_ref[b * 256:(b + 1) * 256, :], ht_ref[...])

    def gate(b, a):
        i0 = e * (PEER_EB // N_KEYS) + 2 * b
        th = [th_ref[i0], th_ref[i0 + 1]]
        cw = [cw_ref[i0], cw_ref[i0 + 1]]
        for tc in range(nch):
            ls = slice(tc * LANES, (tc + 1) * LANES)
            for sub in range(2):
                w = [None, None]
                for hh in range(PEER_HEADS):
                    r0 = hh * N_KEYS + sub * 64
                    s2 = st_ref[tc, 2048 + r0:2048 + r0 + 64, :]
                    e2 = e2_ref[r0:r0 + 64, ls]
                    for half in range(2):
                        sel = jnp.where(s2 >= th[half][hh:hh + 1, ls], e2, 0.0) * cw[half][hh:hh + 1, ls]
                        w[half] = sel if w[half] is None else w[half] + sel
                for half in range(2):
                    r1 = half * N_KEYS + sub * 64
                    x = a[r1:r1 + 64, ls]
                    xw = x * w[half]
                    t = jnp.tanh(x * (GELU_C1 + GELU_C2 * (x * x)))
                    hw_ref[b % 2, r1:r1 + 64, ls] = (xw + xw * t).astype(BF)

    a = pre_act(0)
    for b in range(nblk):
        a_next = pre_act(b + 1) if b + 1 < nblk else None
        gate(b, a)
        if b > 0:
            acc_ref[...] += _dot(vt_ref[b - 1], hw_ref[(b - 1) % 2])
        a = a_next
    acc_ref[...] += _dot(vt_ref[nblk - 1], hw_ref[(nblk - 1) % 2])

    @pl.when(e == pl.num_programs(1) - 1)
    def _finish():
        y = x2_ref[...] + acc_ref[...].T
        if final:
            y = _rms(y, gf_ref[...])
        o_ref[...] = y


def _peer(x, oa, ob, oc, od, w_out, g, wqt, sk, u, vt, gf, *, tt, final):
    T, D = x.shape
    ne = u.shape[0] // PEER_EB
    const = lambda shape: pl.BlockSpec(shape, lambda i, e: (0,) * len(shape),
                                       pipeline_mode=pl.Buffered(1))
    return pl.pallas_call(
        functools.partial(_peer_body, tt=tt, final=final),
        grid=(T // tt, ne),
        in_specs=[pl.BlockSpec((tt, D), lambda i, e: (i, 0))]
        + [pl.BlockSpec((tt, GROUP), lambda i, e: (i, 0))] * 4
        + [const((D, D)), const((1, D)), const((D, D)), const((2, N_KEYS, 64)),
                  pl.BlockSpec((PEER_EB, D), lambda i, e: (e, 0)),
                  pl.BlockSpec((PEER_EB // 256, D, 256), lambda i, e: (e, 0, 0)),
                  const((1, D))],
        out_specs=pl.BlockSpec((tt, D), lambda i, e: (i, 0)),
        out_shape=jax.ShapeDtypeStruct((T, D), F32),
        scratch_shapes=[pltpu.VMEM((tt, D), F32),
                        pltpu.VMEM((D, tt), BF),
                        pltpu.VMEM((tt // LANES, 3 * 1024, LANES), F32),
                        pltpu.VMEM((1024, tt), F32),
                        pltpu.VMEM((N_KEYS, PEER_HEADS, tt), F32),
                        pltpu.VMEM((N_KEYS, PEER_HEADS, tt), F32),
                        pltpu.VMEM((2, 256, tt), BF),
                        pltpu.VMEM((D, tt), F32)],
        compiler_params=pltpu.CompilerParams(
            dimension_semantics=("arbitrary", "arbitrary"), vmem_limit_bytes=VMEM_LIMIT),
        name="peer",
    )(x, oa, ob, oc, od, w_out, g, wqt, sk, u, vt, gf)


def _pack_in_weights(w_in):
    aq, ak, av, cq, ckv, kpe, zc, dq, dk, dv = jnp.split(
        w_in, np.cumsum([256, 256, 256, 256, 128, 32, 512, 256, 256])[:].tolist(), axis=1)
    kpe_rep = jnp.zeros((D_MODEL, 4, 128), w_in.dtype).at[:, :, 64:96].set(kpe[:, None, :])
    return jnp.concatenate(
        [aq, ak, av, cq, ckv, kpe_rep.reshape(D_MODEL, 512), zc, dq, dk, dv], axis=1).astype(BF)


def _pack_mla_weights(w_uq, w_ukv):
    uq = jnp.pad(w_uq.reshape(256, 4, 96), ((0, 0), (0, 0), (0, 32))).reshape(256, 512)
    ukv = w_ukv.reshape(128, 4, 128)
    uk = jnp.pad(ukv[:, :, :64], ((0, 0), (0, 0), (0, 64))).reshape(128, 512)
    uv = ukv[:, :, 64:].reshape(128, 256)
    return uq.astype(BF), uk.astype(BF), uv.astype(BF)


def _pack_expert_out(v):
    n, d = v.shape
    return v.reshape(n // 256, 256, d).transpose(0, 2, 1).astype(BF)


def kernel(x, w_in, w_out, norm_mix, norm_ffn, diff_lambda, diff_head_norm, mla_q_norm, mla_kv_norm, mla_w_uq, mla_w_ukv, mla_out_norm, sgu_v_norm, sgu_w, sgu_b, sgu_out_norm, dil_out_norm, peer_w_q, peer_sub_keys, peer_u, peer_v, final_norm):
    B, S, D = x.shape
    depth = w_in.shape[0]
    T = B * S
    tm = min(INPROJ_TILE, S)
    ta, tb = min(DIFF_TILE, S), min(MLA_TILE, S)
    tt = min(PEER_TOKENS, T)
    assert S % tm == 0 and S % ta == 0 and S % tb == 0 and S % DIL_TILE == 0 and T % tt == 0, (B, S)
    assert ta % QB == 0 and tb % QB == 0 and tt % LANES == 0 and D == D_MODEL
    row = lambda a: a.reshape(1, -1)

    tabs = (_rope_tables(S, 8, 32, 0) + _rope_tables(S, 32, 128, 64) + _rope_tables(S, 16, 64, 0))

    for l in range(depth):
        lam_init = 0.8 - 0.6 * math.exp(-0.3 * l)
        w_cat = _pack_in_weights(w_in[l])
        wuq, wuk, wuv = _pack_mla_weights(mla_w_uq[l], mla_w_ukv[l])
        sgub = jnp.repeat(sgu_b[l].T, 64, axis=1)
        qa, ka, va, qb, kb, vb, oc, qd, kd, vd = _inproj(
            x, row(norm_mix[l]), w_cat, tabs, row(mla_q_norm[l]), row(mla_kv_norm[l]),
            wuq, wuk, wuv, row(sgu_v_norm[l]), sgu_w[l], sgub, row(sgu_out_norm[l]), tm=tm)

        oa = _attn_a(diff_lambda[l], row(diff_head_norm[l]), qa, ka, va, t=ta, lam_init=lam_init)
        ob = _attn_b(row(mla_out_norm[l]), qb, kb, vb, t=tb)

        od = _dil(row(dil_out_norm[l]), qd, kd, vd, td=DIL_TILE)

        x2 = _peer(x.reshape(T, D), oa.reshape(T, GROUP), ob.reshape(T, GROUP), oc.reshape(T, GROUP),
                   od.reshape(T, GROUP), w_out[l].astype(BF), row(norm_ffn[l]), peer_w_q[l].T.astype(BF), peer_sub_keys[l].astype(BF),
                   peer_u[l].astype(BF), _pack_expert_out(peer_v[l]), row(final_norm),
                   tt=tt, final=(l == depth - 1))
        x = x2.reshape(B, S, D)
    return x
```

```python
import functools
import math

import numpy as np
import jax
import jax.numpy as jnp
from jax import lax
from jax.experimental import pallas as pl
from jax.experimental.pallas import tpu as pltpu

D_MODEL = 1024
GROUP = 256
ROPE_THETA = 500000.0
NEG = -1e30
EPS = 1e-6
LANES = 128
LOG2E = math.log2(math.e)

N_KEYS = 128
PEER_HEADS = 8
PEER_TOP = 16

BF = jnp.bfloat16
F32 = jnp.float32

C_AQ, C_AK, C_AV, C_CQ, C_CKV, C_KPE, C_ZC, C_DQ, C_DK, C_DV, C_END = (
    0, 256, 512, 768, 1024, 1152, 1664, 2176, 2432, 2688, 2944)

VMEM_LIMIT = 56 * 1024 * 1024
INPROJ_TILE = 1024
PEER_TOKENS = 512


def _rms(x, g):
    return x * lax.rsqrt(jnp.mean(x * x, axis=-1, keepdims=True) + EPS) * g


def _nt_dot(a, b):
    return lax.dot_general(a, b, (((1,), (1,)), ((), ())), preferred_element_type=F32)


def _dot(a, b):
    return jnp.dot(a, b, preferred_element_type=F32)


def _rope_tables(seq, n_rot, period, offset):
    half = n_rot // 2
    pos = jnp.arange(seq, dtype=F32)
    inv = ROPE_THETA ** (-jnp.arange(half, dtype=F32) * (2.0 / n_rot))
    ang = pos[:, None] * inv[None, :]
    cos, sin = jnp.cos(ang), jnp.sin(ang)
    g = np.arange(LANES) % period - offset
    rot = (g >= 0) & (g < n_rot)
    idx = np.where(rot, g % half, 0)
    sign = np.where(g < half, -1.0, 1.0).astype(np.float32)
    cos_t = jnp.where(rot[None, :], cos[:, idx], 1.0)
    sin_t = jnp.where(rot[None, :], sin[:, idx] * sign[None, :], 0.0)
    return cos_t.astype(F32), sin_t.astype(F32)


def _rope_apply(x, cos_t, sin_t, n_rot, period, offset):
    half = n_rot // 2
    lane = lax.broadcasted_iota(jnp.int32, (1, LANES), 1)
    first = (lane % period - offset) < half
    outs = []
    for c in range(x.shape[1] // LANES):
        xc = x[:, c * LANES:(c + 1) * LANES]
        fwd = pltpu.roll(xc, LANES - half, 1)
        bwd = pltpu.roll(xc, half, 1)
        outs.append(xc * cos_t + jnp.where(first, fwd, bwd) * sin_t)
    return outs[0] if len(outs) == 1 else jnp.concatenate(outs, axis=1)


VROWS = 80


def _value_rows(vt):
    one = (lax.broadcasted_iota(jnp.int32, (VROWS - 64, vt.shape[1]), 0) == 0).astype(F32)
    parts = []
    for h in range(4):
        parts += [vt[64 * h:64 * (h + 1)], one]
    return jnp.concatenate(parts, axis=0).astype(BF)


def _inproj_body(x_ref, gmix_ref, w_ref, ca_ref, sa_ref, cb_ref, sb_ref, cd_ref, sd_ref,
                 gq_ref, gkv_ref, wuq_ref, wuk_ref, wuv_ref,
                 gsv_ref, sguw_ref, sgub_ref, gso_ref,
                 qa_ref, ka_ref, va_ref, qb_ref, kb_ref, vb_ref, oc_ref,
                 qd_ref, kd_ref, vd_ref, *, tm):
    x = x_ref[0]
    h = _rms(x, gmix_ref[...]).astype(BF)

    def proj(a, b):
        return _dot(h, w_ref[:, a:b])

    ca, sa = ca_ref[...], sa_ref[...]
    aq = _rope_apply(proj(C_AQ, C_AK), ca, sa, 8, 32, 0)
    qa_ref[0] = (aq * (32.0 ** -0.5 * LOG2E)).T.astype(BF)
    ka_ref[0] = _rope_apply(proj(C_AK, C_AV), ca, sa, 8, 32, 0).astype(BF)
    va_ref[0] = _value_rows(proj(C_AV, C_CQ).T)

    cb, sb = cb_ref[...], sb_ref[...]
    cq = _rms(proj(C_CQ, C_CKV), gq_ref[...]).astype(BF)
    qb = _rope_apply(_dot(cq, wuq_ref[...]), cb, sb, 32, 128, 64)
    qb_ref[0] = (qb * (96.0 ** -0.5 * LOG2E)).T.astype(BF)
    ckv = _rms(proj(C_CKV, C_KPE), gkv_ref[...]).astype(BF)
    kpe = _rope_apply(proj(C_KPE, C_ZC), cb, sb, 32, 128, 64)
    kb_ref[0] = (_dot(ckv, wuk_ref[...]) + kpe).astype(BF)
    vb_ref[0] = _value_rows(_dot(ckv, wuv_ref[...]).T)

    zc = jax.nn.gelu(proj(C_ZC, C_DQ))
    u = zc[:, :GROUP]
    vn = _rms(zc[:, GROUP:], gsv_ref[...]).astype(BF)
    r = lax.broadcasted_iota(jnp.int32, (128, 128), 0)
    c = lax.broadcasted_iota(jnp.int32, (128, 128), 1)
    wcat = jnp.concatenate(
        [jnp.where(r >= c, sguw_ref[g], 0.0).astype(BF) for g in range(4)], axis=1)
    lane = lax.broadcasted_iota(jnp.int32, (1, GROUP), 1)
    bias = sgub_ref[...]
    gso = gso_ref[...]
    for ch in range(tm // 128):
        vc = vn[ch * 128:(ch + 1) * 128]
        vst = jnp.concatenate(
            [jnp.where(lane // 64 == g, vc, jnp.zeros_like(vc)) for g in range(4)], axis=0)
        sv = _dot(wcat, vst) + bias
        oc = u[ch * 128:(ch + 1) * 128] * sv
        oc_ref[0, ch * 128:(ch + 1) * 128, :] = _rms(oc, gso).astype(BF)

    cd, sd = cd_ref[...], sd_ref[...]
    dq = _rope_apply(proj(C_DQ, C_DK), cd, sd, 16, 64, 0)
    dk = _rope_apply(proj(C_DK, C_DV), cd, sd, 16, 64, 0)
    dv = proj(C_DV, C_END)
    for hf in range(2):
        qd_ref[0, hf] = dq[:, hf * LANES:(hf + 1) * LANES] * 0.125
        kd_ref[0, hf] = dk[:, hf * LANES:(hf + 1) * LANES]
        vd_ref[0, hf] = dv[:, hf * LANES:(hf + 1) * LANES]


def _inproj(x, gmix, w_cat, tabs, gq, gkv, wuq, wuk, wuv, gsv, sguw, sgub, gso, *, tm):
    B, S, D = x.shape
    ns = S // tm
    full = lambda shape: pl.BlockSpec(shape, lambda s, b: (0,) * len(shape))
    tab = pl.BlockSpec((tm, LANES), lambda s, b: (s, 0))
    seq = lambda w: pl.BlockSpec((1, tm, w), lambda s, b: (b, s, 0))
    seq_t = lambda w: pl.BlockSpec((1, w, tm), lambda s, b: (b, 0, s))
    halves = pl.BlockSpec((1, 2, tm, LANES), lambda s, b: (b, 0, s, 0))
    out_w = [256, 256, 4 * VROWS, 512, 512, 4 * VROWS, 256]
    transposed = [True, False, True, True, False, True, False]
    return pl.pallas_call(
        functools.partial(_inproj_body, tm=tm),
        grid=(ns, B),
        in_specs=[seq(D), full((1, D)), full((D, C_END))] + [tab] * 6 + [
            full((1, 256)), full((1, 128)), full((256, 512)), full((128, 512)), full((128, 256)),
            full((1, 256)), full((4, 128, 128)), full((128, 256)), full((1, 256))],
        out_specs=[seq_t(w) if tr else seq(w) for w, tr in zip(out_w, transposed)] + [halves] * 3,
        out_shape=[jax.ShapeDtypeStruct((B, w, S) if tr else (B, S, w), BF)
                   for w, tr in zip(out_w, transposed)]
        + [jax.ShapeDtypeStruct((B, 2, S, LANES), F32)] * 3,
        compiler_params=pltpu.CompilerParams(
            dimension_semantics=("arbitrary", "arbitrary"), vmem_limit_bytes=VMEM_LIMIT),
        name="inproj",
    )(x, gmix, w_cat, *tabs, gq, gkv, wuq, wuk, wuv, gsv, sguw, sgub, gso)


def _head_expand(cols, lane, width):
    out = cols[-1]
    for h in range(len(cols) - 2, -1, -1):
        out = jnp.where(lane // width == h, cols[h], out)
    return out


def _vstack(v, lane):
    return jnp.concatenate(
        [jnp.where(lane // 64 == h, v, jnp.zeros_like(v)) for h in range(4)], axis=0)


DIFF_TILE = 1024
MLA_TILE = 1024
QB = 512


def _online_softmax_t(st, m_ref, j, cs):
    m_prev = m_ref[j:j + 1, cs]
    m_new = jnp.maximum(m_prev, jnp.max(st, axis=0, keepdims=True))
    alpha = jnp.exp2(m_prev - m_new)
    et = jnp.exp2(st - m_new)
    m_ref[j:j + 1, cs] = m_new
    return et.astype(BF), alpha


def _attn_jobs(ns, diag, heads):
    return [(*hd, qc, kc) for kc in range(ns) for qc in range(ns) for hd in heads
            if not (diag and kc > qc)]


def _causal_keep():
    return (lax.broadcasted_iota(jnp.int32, (QB, QB), 0)
            <= lax.broadcasted_iota(jnp.int32, (QB, QB), 1))


def _causal_pairs(n):
    pairs = [(i, j) for i in range(n) for j in range(i + 1)]
    return (jnp.asarray([p[0] for p in pairs], jnp.int32),
            jnp.asarray([p[1] for p in pairs], jnp.int32))


def _attn_a_body(qtab_ref, ktab_ref, lam_ref, gh_ref, q_ref, k_ref, v_ref, o_ref,
                 qs_ref, m_ref, acc_ref, *, t, lam_init):
    qi = qtab_ref[pl.program_id(1)]
    ki = ktab_ref[pl.program_id(1)]

    @pl.when(ki == 0)
    def _init():
        m_ref[...] = jnp.full(m_ref.shape, NEG, F32)
        acc_ref[...] = jnp.zeros(acc_ref.shape, F32)
        row = lax.broadcasted_iota(jnp.int32, (LANES, 1), 0)
        for j in range(8):
            grp = q_ref[0, (j // 4) * LANES:(j // 4 + 1) * LANES, :]
            qs_ref[j] = jnp.where(row // 32 == j % 4, grp, jnp.zeros_like(grp))

    def step(diag):
        jobs = _attn_jobs(t // QB, diag, [(mp, h) for mp in range(2) for h in range(4)])
        kblk = [k_ref[0, kc * QB:(kc + 1) * QB, :] for kc in range(t // QB)]
        keep = _causal_keep() if diag else None

        def qk(mp, h, qc, kc):
            j = 2 * h + mp
            return _dot(kblk[kc][:, (j // 4) * LANES:(j // 4 + 1) * LANES],
                        qs_ref[j, :, qc * QB:(qc + 1) * QB])

        def pv(mp, h, qc, kc, pt, alpha):
            hs, cs = slice(VROWS * h, VROWS * (h + 1)), slice(qc * QB, (qc + 1) * QB)
            acc_ref[mp, hs, cs] = (acc_ref[mp, hs, cs] * alpha
                                   + _dot(v_ref[0, hs, kc * QB:(kc + 1) * QB], pt))

        st_next = qk(*jobs[0])
        pending = None
        for n, (mp, h, qc, kc) in enumerate(jobs):
            st = st_next
            if n + 1 < len(jobs):
                st_next = qk(*jobs[n + 1])
            if diag and kc == qc:
                st = jnp.where(keep, st, NEG)
            pt, alpha = _online_softmax_t(st, m_ref, 2 * h + mp, slice(qc * QB, (qc + 1) * QB))
            if pending is not None:
                pv(*pending)
            pending = (mp, h, qc, kc, pt, alpha)
        pv(*pending)

    @pl.when(ki < qi)
    def _off():
        step(False)

    @pl.when(ki == qi)
    def _diag():
        step(True)
        lf = lam_ref[...]
        lam = (jnp.exp(jnp.sum(lf[0:1] * lf[1:2], axis=-1, keepdims=True))
               - jnp.exp(jnp.sum(lf[2:3] * lf[3:4], axis=-1, keepdims=True)) + lam_init)
        rows = []
        for h in range(4):
            hs, ls = slice(VROWS * h, VROWS * h + 64), slice(VROWS * h + 64, VROWS * h + 65)
            oh = (acc_ref[0, hs, :] * (1.0 / acc_ref[0, ls, :])
                  - lam * (acc_ref[1, hs, :] * (1.0 / acc_ref[1, ls, :])))
            ms = jnp.mean(oh * oh, axis=0, keepdims=True)
            rows.append(oh * lax.rsqrt(ms + EPS))
        ot = jnp.concatenate(rows, axis=0)
        o_ref[0] = (ot.T * gh_ref[...] * (1.0 - lam_init)).astype(BF)


def _attn_a(lam_p, gh, q, k, v, *, t, lam_init):
    B, S, _ = k.shape
    n = S // t
    qtab, ktab = _causal_pairs(n)
    grid_spec = pltpu.PrefetchScalarGridSpec(
        num_scalar_prefetch=2,
        grid=(B, qtab.shape[0]),
        in_specs=[pl.BlockSpec((4, 32), lambda b, p, qt, kt: (0, 0)),
                  pl.BlockSpec((1, GROUP), lambda b, p, qt, kt: (0, 0)),
                  pl.BlockSpec((1, GROUP, t), lambda b, p, qt, kt: (b, 0, qt[p])),
                  pl.BlockSpec((1, t, GROUP), lambda b, p, qt, kt: (b, kt[p], 0)),
                  pl.BlockSpec((1, 4 * VROWS, t), lambda b, p, qt, kt: (b, 0, kt[p]))],
        out_specs=pl.BlockSpec((1, t, GROUP), lambda b, p, qt, kt: (b, qt[p], 0)),
        scratch_shapes=[pltpu.VMEM((8, LANES, t), BF),
                        pltpu.VMEM((8, t), F32),
                        pltpu.VMEM((2, 4 * VROWS, t), F32)])
    return pl.pallas_call(
        functools.partial(_attn_a_body, t=t, lam_init=lam_init),
        grid_spec=grid_spec,
        out_shape=jax.ShapeDtypeStruct((B, S, GROUP), BF),
        compiler_params=pltpu.CompilerParams(
            dimension_semantics=("arbitrary", "arbitrary"), vmem_limit_bytes=VMEM_LIMIT),
        name="attn_diff",
    )(qtab, ktab, lam_p, gh, q, k, v)


def _attn_b_body(qtab_ref, ktab_ref, g_ref, q_ref, k_ref, v_ref, o_ref, m_ref, acc_ref, *, t):
    qi = qtab_ref[pl.program_id(1)]
    ki = ktab_ref[pl.program_id(1)]

    @pl.when(ki == 0)
    def _init():
        m_ref[...] = jnp.full(m_ref.shape, NEG, F32)
        acc_ref[...] = jnp.zeros(acc_ref.shape, F32)

    def step(diag):
        jobs = _attn_jobs(t // QB, diag, [(h,) for h in range(4)])
        kblk = [k_ref[0, kc * QB:(kc + 1) * QB, :] for kc in range(t // QB)]
        keep = _causal_keep() if diag else None

        def qk(h, qc, kc):
            return _dot(kblk[kc][:, h * 128:(h + 1) * 128],
                        q_ref[0, h * 128:(h + 1) * 128, qc * QB:(qc + 1) * QB])

        def pv(h, qc, kc, pt, alpha):
            hs, cs = slice(VROWS * h, VROWS * (h + 1)), slice(qc * QB, (qc + 1) * QB)
            acc_ref[hs, cs] = acc_ref[hs, cs] * alpha + _dot(v_ref[0, hs, kc * QB:(kc + 1) * QB], pt)

        st_next = qk(*jobs[0])
        pending = None
        for n, (h, qc, kc) in enumerate(jobs):
            st = st_next
            if n + 1 < len(jobs):
                st_next = qk(*jobs[n + 1])
            if diag and kc == qc:
                st = jnp.where(keep, st, NEG)
            pt, alpha = _online_softmax_t(st, m_ref, h, slice(qc * QB, (qc + 1) * QB))
            if pending is not None:
                pv(*pending)
            pending = (h, qc, kc, pt, alpha)
        pv(*pending)

    @pl.when(ki < qi)
    def _off():
        step(False)

    @pl.when(ki == qi)
    def _diag():
        step(True)
        rows = [acc_ref[VROWS * h:VROWS * h + 64, :] * (1.0 / acc_ref[VROWS * h + 64:VROWS * h + 65, :])
                for h in range(4)]
        ot = jnp.concatenate(rows, axis=0)
        o_ref[0] = _rms(ot.T, g_ref[...]).astype(BF)


def _attn_b(g, q, k, v, *, t):
    B, S, _ = k.shape
    n = S // t
    qtab, ktab = _causal_pairs(n)
    grid_spec = pltpu.PrefetchScalarGridSpec(
        num_scalar_prefetch=2,
        grid=(B, qtab.shape[0]),
        in_specs=[pl.BlockSpec((1, GROUP), lambda b, p, qt, kt: (0, 0)),
                  pl.BlockSpec((1, 512, t), lambda b, p, qt, kt: (b, 0, qt[p])),
                  pl.BlockSpec((1, t, 512), lambda b, p, qt, kt: (b, kt[p], 0)),
                  pl.BlockSpec((1, 4 * VROWS, t), lambda b, p, qt, kt: (b, 0, kt[p]))],
        out_specs=pl.BlockSpec((1, t, GROUP), lambda b, p, qt, kt: (b, qt[p], 0)),
        scratch_shapes=[pltpu.VMEM((4, t), F32),
                        pltpu.VMEM((4 * VROWS, t), F32)])
    return pl.pallas_call(
        functools.partial(_attn_b_body, t=t),
        grid_spec=grid_spec,
        out_shape=jax.ShapeDtypeStruct((B, S, GROUP), BF),
        compiler_params=pltpu.CompilerParams(
            dimension_semantics=("arbitrary", "arbitrary"), vmem_limit_bytes=VMEM_LIMIT),
        name="attn_mla",
    )(qtab, ktab, g, q, k, v)


DIL_TILE = 2048
DILATIONS = (1, 4, 16)


def _dil_body(g_ref, q_ref, kp_ref, kc_ref, vp_ref, vc_ref, o_ref, ob_ref, lb_ref, *, td):
    i = pl.program_id(1)
    lane = lax.broadcasted_iota(jnp.int32, (1, GROUP), 1)
    a = lax.broadcasted_iota(jnp.int32, (128, 256), 0)
    c = lax.broadcasted_iota(jnp.int32, (128, 256), 1)
    dist = jnp.where(c - a >= 0, c - a, 1000)

    def rows(ref, start, d):
        idx = pl.ds(start, 128) if d == 1 else pl.ds(start, 128, stride=d)
        return jnp.concatenate([ref.at[0, 0][idx, :], ref.at[0, 1][idx, :]], axis=1).astype(BF)

    for bi, d in enumerate(DILATIONS):
        def block(n, carry, bi=bi, d=d):
            if d == 1:
                r, blk = 0, n
            elif td // d == 128:
                r, blk = n, 0
            else:
                r, blk = n % d, n // d
            q0 = r + d * 128 * blk
            if d == 1:
                q0 = pl.multiple_of(q0, 128)
            q = rows(q_ref, q0, d)
            k_hi, v_hi = rows(kc_ref, q0, d), rows(vc_ref, q0, d)
            lo_prev = td - d * 128 + r
            if isinstance(blk, int):
                first = True
                k_lo, v_lo = rows(kp_ref, lo_prev, d), rows(vp_ref, lo_prev, d)
            else:
                first = blk == 0
                lo_cur = jnp.maximum(q0 - d * 128, 0)
                if d == 1:
                    lo_cur = pl.multiple_of(lo_cur, 128)
                k_lo = jnp.where(first, rows(kp_ref, lo_prev, d), rows(kc_ref, lo_cur, d))
                v_lo = jnp.where(first, rows(vp_ref, lo_prev, d), rows(vc_ref, lo_cur, d))
            kwin = jnp.concatenate([k_lo, k_hi], axis=0)
            vst = _vstack(jnp.concatenate([v_lo, v_hi], axis=0), lane)
            cmin = jnp.where(jnp.logical_and(first, i == 0), 128, 0)
            ok = jnp.where(c >= cmin, dist, 1000) <= 128
            es, ils, lses = [], [], []
            for h in range(4):
                s = _nt_dot(q[:, 64 * h:64 * (h + 1)], kwin[:, 64 * h:64 * (h + 1)])
                s = jnp.where(ok, s, NEG)
                m = jnp.max(s, axis=-1, keepdims=True)
                ex = jnp.exp(s - m)
                den = jnp.sum(ex, axis=-1, keepdims=True)
                es.append(ex.astype(BF))
                ils.append(1.0 / den)
                lses.append(m + jnp.log(den))
            o = _dot(jnp.concatenate(es, axis=1), vst) * _head_expand(ils, lane, 64)
            lse = _head_expand(lses, lane, 64)
            idx = pl.ds(q0, 128) if d == 1 else pl.ds(q0, 128, stride=d)
            for hf in range(2):
                ob_ref.at[bi, hf][idx, :] = o[:, hf * LANES:(hf + 1) * LANES]
                lb_ref.at[bi, hf][idx, :] = lse[:, hf * LANES:(hf + 1) * LANES]
            return carry

        lax.fori_loop(0, td // 128, block, 0, unroll=8)

    for ch in range(td // 256):
        rs = slice(ch * 256, (ch + 1) * 256)
        halves = []
        for hf in range(2):
            ls = [lb_ref[bi, hf, rs, :] for bi in range(3)]
            mx = jnp.maximum(jnp.maximum(ls[0], ls[1]), ls[2])
            ws = [jnp.exp(l - mx) for l in ls]
            num = ws[0] * ob_ref[0, hf, rs, :] + ws[1] * ob_ref[1, hf, rs, :] + ws[2] * ob_ref[2, hf, rs, :]
            halves.append(num / (ws[0] + ws[1] + ws[2]))
        o_ref[0, rs, :] = _rms(jnp.concatenate(halves, axis=1), g_ref[...]).astype(BF)


def _dil(g, q, k, v, *, td):
    B, _, S, _ = q.shape
    cur = pl.BlockSpec((1, 2, td, LANES), lambda b, i: (b, 0, i, 0))
    prev = pl.BlockSpec((1, 2, td, LANES), lambda b, i: (b, 0, jnp.maximum(i - 1, 0), 0))
    return pl.pallas_call(
        functools.partial(_dil_body, td=td),
        grid=(B, S // td),
        in_specs=[pl.BlockSpec((1, GROUP), lambda b, i: (0, 0)), cur, prev, cur, prev, cur],
        out_specs=pl.BlockSpec((1, td, GROUP), lambda b, i: (b, i, 0)),
        out_shape=jax.ShapeDtypeStruct((B, S, GROUP), BF),
        scratch_shapes=[pltpu.VMEM((3, 2, td, LANES), F32),
                        pltpu.VMEM((3, 2, td, LANES), F32)],
        compiler_params=pltpu.CompilerParams(
            dimension_semantics=("arbitrary", "arbitrary"), vmem_limit_bytes=VMEM_LIMIT),
        name="dilated",
    )(g, q, k, k, v, v)


PEER_EB = 2048
GELU_C1 = math.sqrt(2.0 / math.pi)
GELU_C2 = 0.044715 * GELU_C1
N_CAND = PEER_TOP + 1


def _peer_body(x_ref, oa_ref, ob_ref, oc_ref, od_ref, wo_ref, g_ref, wqt_ref, sk_ref, u_ref, vt_ref,
               gf_ref, o_ref, x2_ref, ht_ref, st_ref, e2_ref, th_ref, cw_ref, hw_ref, acc_ref,
               *, tt, final):
    e = pl.program_id(1)
    nch = tt // LANES

    @pl.when(e == 0)
    def _prologue():
        y = _dot(oa_ref[...], wo_ref[0:256, :])
        y += _dot(ob_ref[...], wo_ref[256:512, :])
        y += _dot(oc_ref[...], wo_ref[512:768, :])
        y += _dot(od_ref[...], wo_ref[768:1024, :])
        x2_ref[...] = x_ref[...] + y
        h = _rms(x2_ref[...], g_ref[...])
        ht_ref[...] = h.T.astype(BF)
        qt = _dot(wqt_ref[...], ht_ref[...]).astype(BF)
        for hh in range(PEER_HEADS):
            for c in range(2):
                r0 = hh * N_KEYS + c * 64
                sc = _dot(sk_ref[c], qt[r0:r0 + 64, :])
                for tc in range(nch):
                    blk = sc[:, tc * LANES:(tc + 1) * LANES]
                    st_ref.at[tc][pl.ds(c * 1024 + hh, N_KEYS, stride=PEER_HEADS), :] = blk
                    if c == 1:
                        st_ref[tc, 2048 + hh * N_KEYS:2048 + (hh + 1) * N_KEYS, :] = blk
        acc_ref[...] = jnp.zeros(acc_ref.shape, F32)

        def top_vals(load, n_rows, n_out):
            top = []
            for r in range(n_rows):
                v = load(r)
                for q in range(len(top)):
                    top[q], v = jnp.maximum(top[q], v), jnp.minimum(top[q], v)
                if len(top) < n_out:
                    top.append(v)
            return top

        def token_chunk(tc, carry):
            ls = pl.ds(pl.multiple_of(tc * LANES, LANES), LANES)
            v1 = top_vals(lambda r: st_ref[tc, r * 8:(r + 1) * 8, :], N_KEYS, N_CAND)
            v2 = top_vals(lambda r: st_ref[tc, 1024 + r * 8:1024 + (r + 1) * 8, :], N_KEYS, N_CAND)
            cands = [v1[a] + v2[b] for a in range(N_CAND) for b in range(N_CAND)
                     if (a + 1) * (b + 1) <= N_CAND]
            top = top_vals(lambda r: cands[r], len(cands), N_CAND)
            z = jnp.ones_like(top[0])
            for r in range(1, PEER_TOP):
                z = z + jnp.exp(top[r] - top[0])
            tau = 0.5 * (top[PEER_TOP - 1] + top[PEER_TOP])
            iz = 1.0 / z
            for r in range(N_KEYS):
                s1 = st_ref[tc, r * 8:(r + 1) * 8, :]
                th_ref[r, :, ls] = tau - s1
                cw_ref[r, :, ls] = jnp.exp(s1 - v1[0]) * (0.5 * iz)
            for hh in range(PEER_HEADS):
                rows = slice(2048 + hh * 128, 2048 + (hh + 1) * 128)
                e2_ref[hh * 128:(hh + 1) * 128, ls] = jnp.exp(st_ref[tc, rows, :] - v2[0][hh:hh + 1, :])
            return carry

        lax.fori_loop(0, nch, token_chunk, 0)

    nblk = PEER_EB // 256

    def pre_act(b):
        return _dot(u_ref[b * 256:(b + 1) * 256, :], ht_ref[...])

    def gate(b, a):
        i0 = e * (PEER_EB // N_KEYS) + 2 * b
        th = [th_ref[i0], th_ref[i0 + 1]]
        cw = [cw_ref[i0], cw_ref[i0 + 1]]
        for tc in range(nch):
            ls = slice(tc * LANES, (tc + 1) * LANES)
            for sub in range(2):
                w = [None, None]
                for hh in range(PEER_HEADS):
                    r0 = hh * N_KEYS + sub * 64
                    s2 = st_ref[tc, 2048 + r0:2048 + r0 + 64, :]
                    e2 = e2_ref[r0:r0 + 64, ls]
                    for half in range(2):
                        sel = jnp.where(s2 >= th[half][hh:hh + 1, ls], e2, 0.0) * cw[half][hh:hh + 1, ls]
                        w[half] = sel if w[half] is None else w[half] + sel
                for half in range(2):
                    r1 = half * N_KEYS + sub * 64
                    x = a[r1:r1 + 64, ls]
                    xw = x * w[half]
                    t = jnp.tanh(x * (GELU_C1 + GELU_C2 * (x * x)))
                    hw_ref[b % 2, r1:r1 + 64, ls] = (xw + xw * t).astype(BF)

    a = pre_act(0)
    for b in range(nblk):
        a_next = pre_act(b + 1) if b + 1 < nblk else None
        gate(b, a)
        if b > 0:
            acc_ref[...] += _dot(vt_ref[b - 1], hw_ref[(b - 1) % 2])
        a = a_next
    acc_ref[...] += _dot(vt_ref[nblk - 1], hw_ref[(nblk - 1) % 2])

    @pl.when(e == pl.num_programs(1) - 1)
    def _finish():
        y = x2_ref[...] + acc_ref[...].T
        if final:
            y = _rms(y, gf_ref[...])
        o_ref[...] = y


def _peer(x, oa, ob, oc, od, w_out, g, wqt, sk, u, vt, gf, *, tt, final):
    T, D = x.shape
    ne = u.shape[0] // PEER_EB
    const = lambda shape: pl.BlockSpec(shape, lambda i, e: (0,) * len(shape),
                                       pipeline_mode=pl.Buffered(1))
    return pl.pallas_call(
        functools.partial(_peer_body, tt=tt, final=final),
        grid=(T // tt, ne),
        in_specs=[pl.BlockSpec((tt, D), lambda i, e: (i, 0))]
        + [pl.BlockSpec((tt, GROUP), lambda i, e: (i, 0))] * 4
        + [const((D, D)), const((1, D)), const((D, D)), const((2, N_KEYS, 64)),
                  pl.BlockSpec((PEER_EB, D), lambda i, e: (e, 0)),
                  pl.BlockSpec((PEER_EB // 256, D, 256), lambda i, e: (e, 0, 0)),
                  const((1, D))],
        out_specs=pl.BlockSpec((tt, D), lambda i, e: (i, 0)),
        out_shape=jax.ShapeDtypeStruct((T, D), F32),
        scratch_shapes=[pltpu.VMEM((tt, D), F32),
                        pltpu.VMEM((D, tt), BF),
                        pltpu.VMEM((tt // LANES, 3 * 1024, LANES), F32),
                        pltpu.VMEM((1024, tt), F32),
                        pltpu.VMEM((N_KEYS, PEER_HEADS, tt), F32),
                        pltpu.VMEM((N_KEYS, PEER_HEADS, tt), F32),
                        pltpu.VMEM((2, 256, tt), BF),
                        pltpu.VMEM((D, tt), F32)],
        compiler_params=pltpu.CompilerParams(
            dimension_semantics=("arbitrary", "arbitrary"), vmem_limit_bytes=VMEM_LIMIT),
        name="peer",
    )(x, oa, ob, oc, od, w_out, g, wqt, sk, u, vt, gf)


def _pack_in_weights(w_in):
    aq, ak, av, cq, ckv, kpe, zc, dq, dk, dv = jnp.split(
        w_in, np.cumsum([256, 256, 256, 256, 128, 32, 512, 256, 256])[:].tolist(), axis=1)
    kpe_rep = jnp.zeros((D_MODEL, 4, 128), w_in.dtype).at[:, :, 64:96].set(kpe[:, None, :])
    return jnp.concatenate(
        [aq, ak, av, cq, ckv, kpe_rep.reshape(D_MODEL, 512), zc, dq, dk, dv], axis=1).astype(BF)


def _pack_mla_weights(w_uq, w_ukv):
    uq = jnp.pad(w_uq.reshape(256, 4, 96), ((0, 0), (0, 0), (0, 32))).reshape(256, 512)
    ukv = w_ukv.reshape(128, 4, 128)
    uk = jnp.pad(ukv[:, :, :64], ((0, 0), (0, 0), (0, 64))).reshape(128, 512)
    uv = ukv[:, :, 64:].reshape(128, 256)
    return uq.astype(BF), uk.astype(BF), uv.astype(BF)


def _pack_expert_out(v):
    n, d = v.shape
    return v.reshape(n // 256, 256, d).transpose(0, 2, 1).astype(BF)


def kernel(x, w_in, w_out, norm_mix, norm_ffn, diff_lambda, diff_head_norm, mla_q_norm, mla_kv_norm, mla_w_uq, mla_w_ukv, mla_out_norm, sgu_v_norm, sgu_w, sgu_b, sgu_out_norm, dil_out_norm, peer_w_q, peer_sub_keys, peer_u, peer_v, final_norm):
    B, S, D = x.shape
    depth = w_in.shape[0]
    T = B * S
    tm = min(INPROJ_TILE, S)
    ta, tb = min(DIFF_TILE, S), min(MLA_TILE, S)
    tt = min(PEER_TOKENS, T)
    assert S % tm == 0 and S % ta == 0 and S % tb == 0 and S % DIL_TILE == 0 and T % tt == 0, (B, S)
    assert ta % QB == 0 and tb % QB == 0 and tt % LANES == 0 and D == D_MODEL
    row = lambda a: a.reshape(1, -1)

    tabs = (_rope_tables(S, 8, 32, 0) + _rope_tables(S, 32, 128, 64) + _rope_tables(S, 16, 64, 0))

    for l in range(depth):
        lam_init = 0.8 - 0.6 * math.exp(-0.3 * l)
        w_cat = _pack_in_weights(w_in[l])
        wuq, wuk, wuv = _pack_mla_weights(mla_w_uq[l], mla_w_ukv[l])
        sgub = jnp.repeat(sgu_b[l].T, 64, axis=1)
        qa, ka, va, qb, kb, vb, oc, qd, kd, vd = _inproj(
            x, row(norm_mix[l]), w_cat, tabs, row(mla_q_norm[l]), row(mla_kv_norm[l]),
            wuq, wuk, wuv, row(sgu_v_norm[l]), sgu_w[l], sgub, row(sgu_out_norm[l]), tm=tm)

        oa = _attn_a(diff_lambda[l], row(diff_head_norm[l]), qa, ka, va, t=ta, lam_init=lam_init)
        ob = _attn_b(row(mla_out_norm[l]), qb, kb, vb, t=tb)

        od = _dil(row(dil_out_norm[l]), qd, kd, vd, td=DIL_TILE)

        x2 = _peer(x.reshape(T, D), oa.reshape(T, GROUP), ob.reshape(T, GROUP), oc.reshape(T, GROUP),
                   od.reshape(T, GROUP), w_out[l].astype(BF), row(norm_ffn[l]), peer_w_q[l].T.astype(BF), peer_sub_keys[l].astype(BF),
                   peer_u[l].astype(BF), _pack_expert_out(peer_v[l]), row(final_norm),
                   tt=tt, final=(l == depth - 1))
        x = x2.reshape(B, S, D)
    return x
```

```python
import functools
import math

import numpy as np
import jax
import jax.numpy as jnp
from jax import lax
from jax.experimental import pallas as pl
from jax.experimental.pallas import tpu as pltpu

D_MODEL = 1024
GROUP = 256
ROPE_THETA = 500000.0
NEG = -1e30
EPS = 1e-6
LANES = 128
LOG2E = math.log2(math.e)

N_KEYS = 128
PEER_HEADS = 8
PEER_TOP = 16

BF = jnp.bfloat16
F32 = jnp.float32

C_AQ, C_AK, C_AV, C_CQ, C_CKV, C_KPE, C_ZC, C_DQ, C_DK, C_DV, C_END = (
    0, 256, 512, 768, 1024, 1152, 1664, 2176, 2432, 2688, 2944)

VMEM_LIMIT = 56 * 1024 * 1024
INPROJ_TILE = 1024
PEER_TOKENS = 512


def _rms(x, g):
    return x * lax.rsqrt(jnp.mean(x * x, axis=-1, keepdims=True) + EPS) * g


def _nt_dot(a, b):
    return lax.dot_general(a, b, (((1,), (1,)), ((), ())), preferred_element_type=F32)


def _dot(a, b):
    return jnp.dot(a, b, preferred_element_type=F32)


def _rope_tables(seq, n_rot, period, offset):
    half = n_rot // 2
    pos = jnp.arange(seq, dtype=F32)
    inv = ROPE_THETA ** (-jnp.arange(half, dtype=F32) * (2.0 / n_rot))
    ang = pos[:, None] * inv[None, :]
    cos, sin = jnp.cos(ang), jnp.sin(ang)
    g = np.arange(LANES) % period - offset
    rot = (g >= 0) & (g < n_rot)
    idx = np.where(rot, g % half, 0)
    sign = np.where(g < half, -1.0, 1.0).astype(np.float32)
    cos_t = jnp.where(rot[None, :], cos[:, idx], 1.0)
    sin_t = jnp.where(rot[None, :], sin[:, idx] * sign[None, :], 0.0)
    return cos_t.astype(F32), sin_t.astype(F32)


def _rope_apply(x, cos_t, sin_t, n_rot, period, offset):
    half = n_rot // 2
    lane = lax.broadcasted_iota(jnp.int32, (1, LANES), 1)
    first = (lane % period - offset) < half
    outs = []
    for c in range(x.shape[1] // LANES):
        xc = x[:, c * LANES:(c + 1) * LANES]
        fwd = pltpu.roll(xc, LANES - half, 1)
        bwd = pltpu.roll(xc, half, 1)
        outs.append(xc * cos_t + jnp.where(first, fwd, bwd) * sin_t)
    return outs[0] if len(outs) == 1 else jnp.concatenate(outs, axis=1)


VROWS = 80


def _value_rows(vt):
    one = (lax.broadcasted_iota(jnp.int32, (VROWS - 64, vt.shape[1]), 0) == 0).astype(F32)
    parts = []
    for h in range(4):
        parts += [vt[64 * h:64 * (h + 1)], one]
    return jnp.concatenate(parts, axis=0).astype(BF)


def _inproj_body(x_ref, gmix_ref, w_ref, ca_ref, sa_ref, cb_ref, sb_ref, cd_ref, sd_ref,
                 gq_ref, gkv_ref, wuq_ref, wuk_ref, wuv_ref,
                 gsv_ref, sguw_ref, sgub_ref, gso_ref,
                 qa_ref, ka_ref, va_ref, qb_ref, kb_ref, vb_ref, oc_ref,
                 qd_ref, kd_ref, vd_ref, *, tm):
    x = x_ref[0]
    h = _rms(x, gmix_ref[...]).astype(BF)

    def proj(a, b):
        return _dot(h, w_ref[:, a:b])

    ca, sa = ca_ref[...], sa_ref[...]
    aq = _rope_apply(proj(C_AQ, C_AK), ca, sa, 8, 32, 0)
    qa_ref[0] = (aq * (32.0 ** -0.5 * LOG2E)).T.astype(BF)
    ka_ref[0] = _rope_apply(proj(C_AK, C_AV), ca, sa, 8, 32, 0).astype(BF)
    va_ref[0] = _value_rows(proj(C_AV, C_CQ).T)

    cb, sb = cb_ref[...], sb_ref[...]
    cq = _rms(proj(C_CQ, C_CKV), gq_ref[...]).astype(BF)
    qb = _rope_apply(_dot(cq, wuq_ref[...]), cb, sb, 32, 128, 64)
    qb_ref[0] = (qb * (96.0 ** -0.5 * LOG2E)).T.astype(BF)
    ckv = _rms(proj(C_CKV, C_KPE), gkv_ref[...]).astype(BF)
    kpe = _rope_apply(proj(C_KPE, C_ZC), cb, sb, 32, 128, 64)
    kb_ref[0] = (_dot(ckv, wuk_ref[...]) + kpe).astype(BF)
    vb_ref[0] = _value_rows(_dot(ckv, wuv_ref[...]).T)

    zc = jax.nn.gelu(proj(C_ZC, C_DQ))
    u = zc[:, :GROUP]
    vn = _rms(zc[:, GROUP:], gsv_ref[...]).astype(BF)
    r = lax.broadcasted_iota(jnp.int32, (128, 128), 0)
    c = lax.broadcasted_iota(jnp.int32, (128, 128), 1)
    wcat = jnp.concatenate(
        [jnp.where(r >= c, sguw_ref[g], 0.0).astype(BF) for g in range(4)], axis=1)
    lane = lax.broadcasted_iota(jnp.int32, (1, GROUP), 1)
    bias = sgub_ref[...]
    gso = gso_ref[...]
    for ch in range(tm // 128):
        vc = vn[ch * 128:(ch + 1) * 128]
        vst = jnp.concatenate(
            [jnp.where(lane // 64 == g, vc, jnp.zeros_like(vc)) for g in range(4)], axis=0)
        sv = _dot(wcat, vst) + bias
        oc = u[ch * 128:(ch + 1) * 128] * sv
        oc_ref[0, ch * 128:(ch + 1) * 128, :] = _rms(oc, gso).astype(BF)

    cd, sd = cd_ref[...], sd_ref[...]
    dq = _rope_apply(proj(C_DQ, C_DK), cd, sd, 16, 64, 0)
    dk = _rope_apply(proj(C_DK, C_DV), cd, sd, 16, 64, 0)
    dv = proj(C_DV, C_END)
    for hf in range(2):
        qd_ref[0, hf] = dq[:, hf * LANES:(hf + 1) * LANES] * 0.125
        kd_ref[0, hf] = dk[:, hf * LANES:(hf + 1) * LANES]
        vd_ref[0, hf] = dv[:, hf * LANES:(hf + 1) * LANES]


def _inproj(x, gmix, w_cat, tabs, gq, gkv, wuq, wuk, wuv, gsv, sguw, sgub, gso, *, tm):
    B, S, D = x.shape
    ns = S // tm
    full = lambda shape: pl.BlockSpec(shape, lambda s, b: (0,) * len(shape))
    tab = pl.BlockSpec((tm, LANES), lambda s, b: (s, 0))
    seq = lambda w: pl.BlockSpec((1, tm, w), lambda s, b: (b, s, 0))
    seq_t = lambda w: pl.BlockSpec((1, w, tm), lambda s, b: (b, 0, s))
    halves = pl.BlockSpec((1, 2, tm, LANES), lambda s, b: (b, 0, s, 0))
    out_w = [256, 256, 4 * VROWS, 512, 512, 4 * VROWS, 256]
    transposed = [True, False, True, True, False, True, False]
    return pl.pallas_call(
        functools.partial(_inproj_body, tm=tm),
        grid=(ns, B),
        in_specs=[seq(D), full((1, D)), full((D, C_END))] + [tab] * 6 + [
            full((1, 256)), full((1, 128)), full((256, 512)), full((128, 512)), full((128, 256)),
            full((1, 256)), full((4, 128, 128)), full((128, 256)), full((1, 256))],
        out_specs=[seq_t(w) if tr else seq(w) for w, tr in zip(out_w, transposed)] + [halves] * 3,
        out_shape=[jax.ShapeDtypeStruct((B, w, S) if tr else (B, S, w), BF)
                   for w, tr in zip(out_w, transposed)]
        + [jax.ShapeDtypeStruct((B, 2, S, LANES), F32)] * 3,
        compiler_params=pltpu.CompilerParams(
            dimension_semantics=("arbitrary", "arbitrary"), vmem_limit_bytes=VMEM_LIMIT),
        name="inproj",
    )(x, gmix, w_cat, *tabs, gq, gkv, wuq, wuk, wuv, gsv, sguw, sgub, gso)


def _head_expand(cols, lane, width):
    out = cols[-1]
    for h in range(len(cols) - 2, -1, -1):
        out = jnp.where(lane // width == h, cols[h], out)
    return out


def _vstack(v, lane):
    return jnp.concatenate(
        [jnp.where(lane // 64 == h, v, jnp.zeros_like(v)) for h in range(4)], axis=0)


DIFF_TILE = 1024
MLA_TILE = 1024
QB = 512


def _online_softmax_t(st, m_ref, j, cs):
    m_prev = m_ref[j:j + 1, cs]
    m_new = jnp.maximum(m_prev, jnp.max(st, axis=0, keepdims=True))
    alpha = jnp.exp2(m_prev - m_new)
    et = jnp.exp2(st - m_new)
    m_ref[j:j + 1, cs] = m_new
    return et.astype(BF), alpha


def _attn_jobs(ns, diag, heads):
    return [(*hd, qc, kc) for kc in range(ns) for qc in range(ns) for hd in heads
            if not (diag and kc > qc)]


def _causal_keep():
    return (lax.broadcasted_iota(jnp.int32, (QB, QB), 0)
            <= lax.broadcasted_iota(jnp.int32, (QB, QB), 1))


def _causal_pairs(n):
    pairs = [(i, j) for i in range(n) for j in range(i + 1)]
    return (jnp.asarray([p[0] for p in pairs], jnp.int32),
            jnp.asarray([p[1] for p in pairs], jnp.int32))


def _attn_a_body(qtab_ref, ktab_ref, lam_ref, gh_ref, q_ref, k_ref, v_ref, o_ref,
                 qs_ref, m_ref, acc_ref, *, t, lam_init):
    qi = qtab_ref[pl.program_id(1)]
    ki = ktab_ref[pl.program_id(1)]

    @pl.when(ki == 0)
    def _init():
        m_ref[...] = jnp.full(m_ref.shape, NEG, F32)
        acc_ref[...] = jnp.zeros(acc_ref.shape, F32)
        row = lax.broadcasted_iota(jnp.int32, (64, 1), 0)
        for j in range(8):
            grp = q_ref[0, (j // 2) * 64:(j // 2 + 1) * 64, :]
            qs_ref[j] = jnp.where(row // 32 == j % 2, grp, jnp.zeros_like(grp))

    def step(diag):
        jobs = _attn_jobs(t // QB, diag, [(mp, h) for mp in range(2) for h in range(4)])
        kblk = [[k_ref[0, kc * QB:(kc + 1) * QB, 64 * h:64 * (h + 1)] for h in range(4)]
                for kc in range(t // QB)]
        keep = _causal_keep() if diag else None

        def qk(mp, h, qc, kc):
            return _dot(kblk[kc][h], qs_ref[2 * h + mp, :, qc * QB:(qc + 1) * QB])

        def pv(mp, h, qc, kc, pt, alpha):
            hs, cs = slice(VROWS * h, VROWS * (h + 1)), slice(qc * QB, (qc + 1) * QB)
            acc_ref[mp, hs, cs] = (acc_ref[mp, hs, cs] * alpha
                                   + _dot(v_ref[0, hs, kc * QB:(kc + 1) * QB], pt))

        st_next = qk(*jobs[0])
        pending = None
        for n, (mp, h, qc, kc) in enumerate(jobs):
            st = st_next
            if n + 1 < len(jobs):
                st_next = qk(*jobs[n + 1])
            if diag and kc == qc:
                st = jnp.where(keep, st, NEG)
            pt, alpha = _online_softmax_t(st, m_ref, 2 * h + mp, slice(qc * QB, (qc + 1) * QB))
            if pending is not None:
                pv(*pending)
            pending = (mp, h, qc, kc, pt, alpha)
        pv(*pending)

    @pl.when(ki < qi)
    def _off():
        step(False)

    @pl.when(ki == qi)
    def _diag():
        step(True)
        lf = lam_ref[...]
        lam = (jnp.exp(jnp.sum(lf[0:1] * lf[1:2], axis=-1, keepdims=True))
               - jnp.exp(jnp.sum(lf[2:3] * lf[3:4], axis=-1, keepdims=True)) + lam_init)
        rows = []
        for h in range(4):
            hs, ls = slice(VROWS * h, VROWS * h + 64), slice(VROWS * h + 64, VROWS * h + 65)
            oh = (acc_ref[0, hs, :] * (1.0 / acc_ref[0, ls, :])
                  - lam * (acc_ref[1, hs, :] * (1.0 / acc_ref[1, ls, :])))
            ms = jnp.mean(oh * oh, axis=0, keepdims=True)
            rows.append(oh * lax.rsqrt(ms + EPS))
        ot = jnp.concatenate(rows, axis=0)
        o_ref[0] = (ot.T * gh_ref[...] * (1.0 - lam_init)).astype(BF)


def _attn_a(lam_p, gh, q, k, v, *, t, lam_init):
    B, S, _ = k.shape
    n = S // t
    qtab, ktab = _causal_pairs(n)
    grid_spec = pltpu.PrefetchScalarGridSpec(
        num_scalar_prefetch=2,
        grid=(B, qtab.shape[0]),
        in_specs=[pl.BlockSpec((4, 32), lambda b, p, qt, kt: (0, 0)),
                  pl.BlockSpec((1, GROUP), lambda b, p, qt, kt: (0, 0)),
                  pl.BlockSpec((1, GROUP, t), lambda b, p, qt, kt: (b, 0, qt[p])),
                  pl.BlockSpec((1, t, GROUP), lambda b, p, qt, kt: (b, kt[p], 0)),
                  pl.BlockSpec((1, 4 * VROWS, t), lambda b, p, qt, kt: (b, 0, kt[p]))],
        out_specs=pl.BlockSpec((1, t, GROUP), lambda b, p, qt, kt: (b, qt[p], 0)),
        scratch_shapes=[pltpu.VMEM((8, 64, t), BF),
                        pltpu.VMEM((8, t), F32),
                        pltpu.VMEM((2, 4 * VROWS, t), F32)])
    return pl.pallas_call(
        functools.partial(_attn_a_body, t=t, lam_init=lam_init),
        grid_spec=grid_spec,
        out_shape=jax.ShapeDtypeStruct((B, S, GROUP), BF),
        compiler_params=pltpu.CompilerParams(
            dimension_semantics=("arbitrary", "arbitrary"), vmem_limit_bytes=VMEM_LIMIT),
        name="attn_diff",
    )(qtab, ktab, lam_p, gh, q, k, v)


def _attn_b_body(qtab_ref, ktab_ref, g_ref, q_ref, k_ref, v_ref, o_ref, m_ref, acc_ref, *, t):
    qi = qtab_ref[pl.program_id(1)]
    ki = ktab_ref[pl.program_id(1)]

    @pl.when(ki == 0)
    def _init():
        m_ref[...] = jnp.full(m_ref.shape, NEG, F32)
        acc_ref[...] = jnp.zeros(acc_ref.shape, F32)

    def step(diag):
        jobs = _attn_jobs(t // QB, diag, [(h,) for h in range(4)])
        kblk = [k_ref[0, kc * QB:(kc + 1) * QB, :] for kc in range(t // QB)]
        keep = _causal_keep() if diag else None

        def qk(h, qc, kc):
            return _dot(kblk[kc][:, h * 128:(h + 1) * 128],
                        q_ref[0, h * 128:(h + 1) * 128, qc * QB:(qc + 1) * QB])

        def pv(h, qc, kc, pt, alpha):
            hs, cs = slice(VROWS * h, VROWS * (h + 1)), slice(qc * QB, (qc + 1) * QB)
            acc_ref[hs, cs] = acc_ref[hs, cs] * alpha + _dot(v_ref[0, hs, kc * QB:(kc + 1) * QB], pt)

        st_next = qk(*jobs[0])
        pending = None
        for n, (h, qc, kc) in enumerate(jobs):
            st = st_next
            if n + 1 < len(jobs):
                st_next = qk(*jobs[n + 1])
            if diag and kc == qc:
                st = jnp.where(keep, st, NEG)
            pt, alpha = _online_softmax_t(st, m_ref, h, slice(qc * QB, (qc + 1) * QB))
            if pending is not None:
                pv(*pending)
            pending = (h, qc, kc, pt, alpha)
        pv(*pending)

    @pl.when(ki < qi)
    def _off():
        step(False)

    @pl.when(ki == qi)
    def _diag():
        step(True)
        rows = [acc_ref[VROWS * h:VROWS * h + 64, :] * (1.0 / acc_ref[VROWS * h + 64:VROWS * h + 65, :])
                for h in range(4)]
        ot = jnp.concatenate(rows, axis=0)
        o_ref[0] = _rms(ot.T, g_ref[...]).astype(BF)


def _attn_b(g, q, k, v, *, t):
    B, S, _ = k.shape
    n = S // t
    qtab, ktab = _causal_pairs(n)
    grid_spec = pltpu.PrefetchScalarGridSpec(
        num_scalar_prefetch=2,
        grid=(B, qtab.shape[0]),
        in_specs=[pl.BlockSpec((1, GROUP), lambda b, p, qt, kt: (0, 0)),
                  pl.BlockSpec((1, 512, t), lambda b, p, qt, kt: (b, 0, qt[p])),
                  pl.BlockSpec((1, t, 512), lambda b, p, qt, kt: (b, kt[p], 0)),
                  pl.BlockSpec((1, 4 * VROWS, t), lambda b, p, qt, kt: (b, 0, kt[p]))],
        out_specs=pl.BlockSpec((1, t, GROUP), lambda b, p, qt, kt: (b, qt[p], 0)),
        scratch_shapes=[pltpu.VMEM((4, t), F32),
                        pltpu.VMEM((4 * VROWS, t), F32)])
    return pl.pallas_call(
        functools.partial(_attn_b_body, t=t),
        grid_spec=grid_spec,
        out_shape=jax.ShapeDtypeStruct((B, S, GROUP), BF),
        compiler_params=pltpu.CompilerParams(
            dimension_semantics=("arbitrary", "arbitrary"), vmem_limit_bytes=VMEM_LIMIT),
        name="attn_mla",
    )(qtab, ktab, g, q, k, v)


DIL_TILE = 2048
DILATIONS = (1, 4, 16)


def _dil_body(g_ref, q_ref, kp_ref, kc_ref, vp_ref, vc_ref, o_ref, ob_ref, lb_ref, *, td):
    i = pl.program_id(1)
    lane = lax.broadcasted_iota(jnp.int32, (1, GROUP), 1)
    a = lax.broadcasted_iota(jnp.int32, (128, 256), 0)
    c = lax.broadcasted_iota(jnp.int32, (128, 256), 1)
    dist = jnp.where(c - a >= 0, c - a, 1000)

    def rows(ref, start, d):
        idx = pl.ds(start, 128) if d == 1 else pl.ds(start, 128, stride=d)
        return jnp.concatenate([ref.at[0, 0][idx, :], ref.at[0, 1][idx, :]], axis=1).astype(BF)

    for bi, d in enumerate(DILATIONS):
        def block(n, carry, bi=bi, d=d):
            if d == 1:
                r, blk = 0, n
            elif td // d == 128:
                r, blk = n, 0
            else:
                r, blk = n % d, n // d
            q0 = r + d * 128 * blk
            if d == 1:
                q0 = pl.multiple_of(q0, 128)
            q = rows(q_ref, q0, d)
            k_hi, v_hi = rows(kc_ref, q0, d), rows(vc_ref, q0, d)
            lo_prev = td - d * 128 + r
            if isinstance(blk, int):
                first = True
                k_lo, v_lo = rows(kp_ref, lo_prev, d), rows(vp_ref, lo_prev, d)
            else:
                first = blk == 0
                lo_cur = jnp.maximum(q0 - d * 128, 0)
                if d == 1:
                    lo_cur = pl.multiple_of(lo_cur, 128)
                k_lo = jnp.where(first, rows(kp_ref, lo_prev, d), rows(kc_ref, lo_cur, d))
                v_lo = jnp.where(first, rows(vp_ref, lo_prev, d), rows(vc_ref, lo_cur, d))
            kwin = jnp.concatenate([k_lo, k_hi], axis=0)
            vst = _vstack(jnp.concatenate([v_lo, v_hi], axis=0), lane)
            cmin = jnp.where(jnp.logical_and(first, i == 0), 128, 0)
            ok = jnp.where(c >= cmin, dist, 1000) <= 128
            es, ils, lses = [], [], []
            for h in range(4):
                ps = slice((h // 2) * LANES, (h // 2 + 1) * LANES)
                qg = q[:, ps]
                s = _nt_dot(jnp.where(lane[:, ps] // 64 == h, qg, jnp.zeros_like(qg)), kwin[:, ps])
                s = jnp.where(ok, s, NEG)
                m = jnp.max(s, axis=-1, keepdims=True)
                ex = jnp.exp(s - m)
                den = jnp.sum(ex, axis=-1, keepdims=True)
                es.append(ex.astype(BF))
                ils.append(1.0 / den)
                lses.append(m + jnp.log(den))
            o = _dot(jnp.concatenate(es, axis=1), vst) * _head_expand(ils, lane, 64)
            lse = _head_expand(lses, lane, 64)
            idx = pl.ds(q0, 128) if d == 1 else pl.ds(q0, 128, stride=d)
            for hf in range(2):
                ob_ref.at[bi, hf][idx, :] = o[:, hf * LANES:(hf + 1) * LANES]
                lb_ref.at[bi, hf][idx, :] = lse[:, hf * LANES:(hf + 1) * LANES]
            return carry

        lax.fori_loop(0, td // 128, block, 0, unroll=8)

    for ch in range(td // 256):
        rs = slice(ch * 256, (ch + 1) * 256)
        halves = []
        for hf in range(2):
            ls = [lb_ref[bi, hf, rs, :] for bi in range(3)]
            mx = jnp.maximum(jnp.maximum(ls[0], ls[1]), ls[2])
            ws = [jnp.exp(l - mx) for l in ls]
            num = ws[0] * ob_ref[0, hf, rs, :] + ws[1] * ob_ref[1, hf, rs, :] + ws[2] * ob_ref[2, hf, rs, :]
            halves.append(num / (ws[0] + ws[1] + ws[2]))
        o_ref[0, rs, :] = _rms(jnp.concatenate(halves, axis=1), g_ref[...]).astype(BF)


def _dil(g, q, k, v, *, td):
    B, _, S, _ = q.shape
    cur = pl.BlockSpec((1, 2, td, LANES), lambda b, i: (b, 0, i, 0))
    prev = pl.BlockSpec((1, 2, td, LANES), lambda b, i: (b, 0, jnp.maximum(i - 1, 0), 0))
    return pl.pallas_call(
        functools.partial(_dil_body, td=td),
        grid=(B, S // td),
        in_specs=[pl.BlockSpec((1, GROUP), lambda b, i: (0, 0)), cur, prev, cur, prev, cur],
        out_specs=pl.BlockSpec((1, td, GROUP), lambda b, i: (b, i, 0)),
        out_shape=jax.ShapeDtypeStruct((B, S, GROUP), BF),
        scratch_shapes=[pltpu.VMEM((3, 2, td, LANES), F32),
                        pltpu.VMEM((3, 2, td, LANES), F32)],
        compiler_params=pltpu.CompilerParams(
            dimension_semantics=("arbitrary", "arbitrary"), vmem_limit_bytes=VMEM_LIMIT),
        name="dilated",
    )(g, q, k, k, v, v)


PEER_EB = 2048
GELU_C1 = math.sqrt(2.0 / math.pi)
GELU_C2 = 0.044715 * GELU_C1
N_CAND = PEER_TOP + 1


def _peer_body(x_ref, oa_ref, ob_ref, oc_ref, od_ref, wo_ref, g_ref, wqt_ref, sk_ref, u_ref, vt_ref,
               gf_ref, o_ref, x2_ref, ht_ref, st_ref, e2_ref, th_ref, cw_ref, hw_ref, acc_ref,
               *, tt, final):
    e = pl.program_id(1)
    nch = tt // LANES

    @pl.when(e == 0)
    def _prologue():
        y = _dot(oa_ref[...], wo_ref[0:256, :])
        y += _dot(ob_ref[...], wo_ref[256:512, :])
        y += _dot(oc_ref[...], wo_ref[512:768, :])
        y += _dot(od_ref[...], wo_ref[768:1024, :])
        x2_ref[...] = x_ref[...] + y
        h = _rms(x2_ref[...], g_ref[...])
        ht_ref[...] = h.T.astype(BF)
        qt = _dot(wqt_ref[...], ht_ref[...]).astype(BF)
        for hh in range(PEER_HEADS):
            for c in range(2):
                r0 = hh * N_KEYS + c * 64
                sc = _dot(sk_ref[c], qt[r0:r0 + 64, :])
                for tc in range(nch):
                    blk = sc[:, tc * LANES:(tc + 1) * LANES]
                    st_ref.at[tc][pl.ds(c * 1024 + hh, N_KEYS, stride=PEER_HEADS), :] = blk
                    if c == 1:
                        st_ref[tc, 2048 + hh * N_KEYS:2048 + (hh + 1) * N_KEYS, :] = blk
        acc_ref[...] = jnp.zeros(acc_ref.shape, F32)

        def top_vals(load, n_rows, n_out):
            top = []
            for r in range(n_rows):
                v = load(r)
                for q in range(len(top)):
                    top[q], v = jnp.maximum(top[q], v), jnp.minimum(top[q], v)
                if len(top) < n_out:
                    top.append(v)
            return top

        def token_chunk(tc, carry):
            ls = pl.ds(pl.multiple_of(tc * LANES, LANES), LANES)
            v1 = top_vals(lambda r: st_ref[tc, r * 8:(r + 1) * 8, :], N_KEYS, N_CAND)
            v2 = top_vals(lambda r: st_ref[tc, 1024 + r * 8:1024 + (r + 1) * 8, :], N_KEYS, N_CAND)
            cands = [v1[a] + v2[b] for a in range(N_CAND) for b in range(N_CAND)
                     if (a + 1) * (b + 1) <= N_CAND]
            top = top_vals(lambda r: cands[r], len(cands), N_CAND)
            z = jnp.ones_like(top[0])
            for r in range(1, PEER_TOP):
                z = z + jnp.exp(top[r] - top[0])
            tau = 0.5 * (top[PEER_TOP - 1] + top[PEER_TOP])
            iz = 1.0 / z
            for r in range(N_KEYS):
                s1 = st_ref[tc, r * 8:(r + 1) * 8, :]
                th_ref[r, :, ls] = tau - s1
                cw_ref[r, :, ls] = jnp.exp(s1 - v1[0]) * (0.5 * iz)
            for hh in range(PEER_HEADS):
                rows = slice(2048 + hh * 128, 2048 + (hh + 1) * 128)
                e2_ref[hh * 128:(hh + 1) * 128, ls] = jnp.exp(st_ref[tc, rows, :] - v2[0][hh:hh + 1, :])
            return carry

        lax.fori_loop(0, nch, token_chunk, 0)

    nblk = PEER_EB // 256

    def pre_act(b):
        return _dot(u_ref[b * 256:(b + 1) * 256, :], ht_ref[...])

    def gate(b, a):
        i0 = e * (PEER_EB // N_KEYS) + 2 * b
        th = [th_ref[i0], th_ref[i0 + 1]]
        cw = [cw_ref[i0], cw_ref[i0 + 1]]
        for tc in range(nch):
            ls = slice(tc * LANES, (tc + 1) * LANES)
            for sub in range(2):
                w = [None, None]
                for hh in range(PEER_HEADS):
                    r0 = hh * N_KEYS + sub * 64
                    s2 = st_ref[tc, 2048 + r0:2048 + r0 + 64, :]
                    e2 = e2_ref[r0:r0 + 64, ls]
                    for half in range(2):
                        sel = jnp.where(s2 >= th[half][hh:hh + 1, ls], e2, 0.0) * cw[half][hh:hh + 1, ls]
                        w[half] = sel if w[half] is None else w[half] + sel
                for half in range(2):
                    r1 = half * N_KEYS + sub * 64
                    x = a[r1:r1 + 64, ls]
                    xw = x * w[half]
                    t = jnp.tanh(x * (GELU_C1 + GELU_C2 * (x * x)))
                    hw_ref[b % 2, r1:r1 + 64, ls] = (xw + xw * t).astype(BF)

    a = pre_act(0)
    for b in range(nblk):
        a_next = pre_act(b + 1) if b + 1 < nblk else None
        gate(b, a)
        if b > 0:
            acc_ref[...] += _dot(vt_ref[b - 1], hw_ref[(b - 1) % 2])
        a = a_next
    acc_ref[...] += _dot(vt_ref[nblk - 1], hw_ref[(nblk - 1) % 2])

    @pl.when(e == pl.num_programs(1) - 1)
    def _finish():
        y = x2_ref[...] + acc_ref[...].T
        if final:
            y = _rms(y, gf_ref[...])
        o_ref[...] = y


def _peer(x, oa, ob, oc, od, w_out, g, wqt, sk, u, vt, gf, *, tt, final):
    T, D = x.shape
    ne = u.shape[0] // PEER_EB
    const = lambda shape: pl.BlockSpec(shape, lambda i, e: (0,) * len(shape),
                                       pipeline_mode=pl.Buffered(1))
    return pl.pallas_call(
        functools.partial(_peer_body, tt=tt, final=final),
        grid=(T // tt, ne),
        in_specs=[pl.BlockSpec((tt, D), lambda i, e: (i, 0))]
        + [pl.BlockSpec((tt, GROUP), lambda i, e: (i, 0))] * 4
        + [const((D, D)), const((1, D)), const((D, D)), const((2, N_KEYS, 64)),
                  pl.BlockSpec((PEER_EB, D), lambda i, e: (e, 0)),
                  pl.BlockSpec((PEER_EB // 256, D, 256), lambda i, e: (e, 0, 0)),
                  const((1, D))],
        out_specs=pl.BlockSpec((tt, D), lambda i, e: (i, 0)),
        out_shape=jax.ShapeDtypeStruct((T, D), F32),
        scratch_shapes=[pltpu.VMEM((tt, D), F32),
                        pltpu.VMEM((D, tt), BF),
                        pltpu.VMEM((tt // LANES, 3 * 1024, LANES), F32),
                        pltpu.VMEM((1024, tt), F32),
                        pltpu.VMEM((N_KEYS, PEER_HEADS, tt), F32),
                        pltpu.VMEM((N_KEYS, PEER_HEADS, tt), F32),
                        pltpu.VMEM((2, 256, tt), BF),
                        pltpu.VMEM((D, tt), F32)],
        compiler_params=pltpu.CompilerParams(
            dimension_semantics=("arbitrary", "arbitrary"), vmem_limit_bytes=VMEM_LIMIT),
        name="peer",
    )(x, oa, ob, oc, od, w_out, g, wqt, sk, u, vt, gf)


def _pack_in_weights(w_in):
    aq, ak, av, cq, ckv, kpe, zc, dq, dk, dv = jnp.split(
        w_in, np.cumsum([256, 256, 256, 256, 128, 32, 512, 256, 256])[:].tolist(), axis=1)
    kpe_rep = jnp.zeros((D_MODEL, 4, 128), w_in.dtype).at[:, :, 64:96].set(kpe[:, None, :])
    return jnp.concatenate(
        [aq, ak, av, cq, ckv, kpe_rep.reshape(D_MODEL, 512), zc, dq, dk, dv], axis=1).astype(BF)


def _pack_mla_weights(w_uq, w_ukv):
    uq = jnp.pad(w_uq.reshape(256, 4, 96), ((0, 0), (0, 0), (0, 32))).reshape(256, 512)
    ukv = w_ukv.reshape(128, 4, 128)
    uk = jnp.pad(ukv[:, :, :64], ((0, 0), (0, 0), (0, 64))).reshape(128, 512)
    uv = ukv[:, :, 64:].reshape(128, 256)
    return uq.astype(BF), uk.astype(BF), uv.astype(BF)


def _pack_expert_out(v):
    n, d = v.shape
    return v.reshape(n // 256, 256, d).transpose(0, 2, 1).astype(BF)


def kernel(x, w_in, w_out, norm_mix, norm_ffn, diff_lambda, diff_head_norm, mla_q_norm, mla_kv_norm, mla_w_uq, mla_w_ukv, mla_out_norm, sgu_v_norm, sgu_w, sgu_b, sgu_out_norm, dil_out_norm, peer_w_q, peer_sub_keys, peer_u, peer_v, final_norm):
    B, S, D = x.shape
    depth = w_in.shape[0]
    T = B * S
    tm = min(INPROJ_TILE, S)
    ta, tb = min(DIFF_TILE, S), min(MLA_TILE, S)
    tt = min(PEER_TOKENS, T)
    assert S % tm == 0 and S % ta == 0 and S % tb == 0 and S % DIL_TILE == 0 and T % tt == 0, (B, S)
    assert ta % QB == 0 and tb % QB == 0 and tt % LANES == 0 and D == D_MODEL
    row = lambda a: a.reshape(1, -1)

    tabs = (_rope_tables(S, 8, 32, 0) + _rope_tables(S, 32, 128, 64) + _rope_tables(S, 16, 64, 0))

    for l in range(depth):
        lam_init = 0.8 - 0.6 * math.exp(-0.3 * l)
        w_cat = _pack_in_weights(w_in[l])
        wuq, wuk, wuv = _pack_mla_weights(mla_w_uq[l], mla_w_ukv[l])
        sgub = jnp.repeat(sgu_b[l].T, 64, axis=1)
        qa, ka, va, qb, kb, vb, oc, qd, kd, vd = _inproj(
            x, row(norm_mix[l]), w_cat, tabs, row(mla_q_norm[l]), row(mla_kv_norm[l]),
            wuq, wuk, wuv, row(sgu_v_norm[l]), sgu_w[l], sgub, row(sgu_out_norm[l]), tm=tm)

        oa = _attn_a(diff_lambda[l], row(diff_head_norm[l]), qa, ka, va, t=ta, lam_init=lam_init)
        ob = _attn_b(row(mla_out_norm[l]), qb, kb, vb, t=tb)

        od = _dil(row(dil_out_norm[l]), qd, kd, vd, td=DIL_TILE)

        x2 = _peer(x.reshape(T, D), oa.reshape(T, GROUP), ob.reshape(T, GROUP), oc.reshape(T, GROUP),
                   od.reshape(T, GROUP), w_out[l].astype(BF), row(norm_ffn[l]), peer_w_q[l].T.astype(BF), peer_sub_keys[l].astype(BF),
                   peer_u[l].astype(BF), _pack_expert_out(peer_v[l]), row(final_norm),
                   tt=tt, final=(l == depth - 1))
        x = x2.reshape(B, S, D)
    return x
```

```python
import functools
import math

import numpy as np
import jax
import jax.numpy as jnp
from jax import lax
from jax.experimental import pallas as pl
from jax.experimental.pallas import tpu as pltpu

D_MODEL = 1024
GROUP = 256
ROPE_THETA = 500000.0
NEG = -1e30
EPS = 1e-6
LANES = 128
LOG2E = math.log2(math.e)

N_KEYS = 128
PEER_HEADS = 8
PEER_TOP = 16

BF = jnp.bfloat16
F32 = jnp.float32

C_AQ, C_AK, C_AV, C_CQ, C_CKV, C_KPE, C_ZC, C_DQ, C_DK, C_DV, C_END = (
    0, 256, 512, 768, 1024, 1152, 1664, 2176, 2432, 2688, 2944)

VMEM_LIMIT = 56 * 1024 * 1024
INPROJ_TILE = 1024
PEER_TOKENS = 512


def _rms(x, g):
    return x * lax.rsqrt(jnp.mean(x * x, axis=-1, keepdims=True) + EPS) * g


def _nt_dot(a, b):
    return lax.dot_general(a, b, (((1,), (1,)), ((), ())), preferred_element_type=F32)


def _dot(a, b):
    return jnp.dot(a, b, preferred_element_type=F32)


def _rope_tables(seq, n_rot, period, offset):
    half = n_rot // 2
    pos = jnp.arange(seq, dtype=F32)
    inv = ROPE_THETA ** (-jnp.arange(half, dtype=F32) * (2.0 / n_rot))
    ang = pos[:, None] * inv[None, :]
    cos, sin = jnp.cos(ang), jnp.sin(ang)
    g = np.arange(LANES) % period - offset
    rot = (g >= 0) & (g < n_rot)
    idx = np.where(rot, g % half, 0)
    sign = np.where(g < half, -1.0, 1.0).astype(np.float32)
    cos_t = jnp.where(rot[None, :], cos[:, idx], 1.0)
    sin_t = jnp.where(rot[None, :], sin[:, idx] * sign[None, :], 0.0)
    return cos_t.astype(F32), sin_t.astype(F32)


def _rope_apply(x, cos_t, sin_t, n_rot, period, offset):
    half = n_rot // 2
    lane = lax.broadcasted_iota(jnp.int32, (1, LANES), 1)
    first = (lane % period - offset) < half
    outs = []
    for c in range(x.shape[1] // LANES):
        xc = x[:, c * LANES:(c + 1) * LANES]
        fwd = pltpu.roll(xc, LANES - half, 1)
        bwd = pltpu.roll(xc, half, 1)
        outs.append(xc * cos_t + jnp.where(first, fwd, bwd) * sin_t)
    return outs[0] if len(outs) == 1 else jnp.concatenate(outs, axis=1)


VROWS = 80


def _value_rows(vt):
    one = (lax.broadcasted_iota(jnp.int32, (VROWS - 64, vt.shape[1]), 0) == 0).astype(F32)
    parts = []
    for h in range(4):
        parts += [vt[64 * h:64 * (h + 1)], one]
    return jnp.concatenate(parts, axis=0).astype(BF)


def _inproj_body(x_ref, gmix_ref, w_ref, ca_ref, sa_ref, cb_ref, sb_ref, cd_ref, sd_ref,
                 gq_ref, gkv_ref, wuq_ref, wuk_ref, wuv_ref,
                 gsv_ref, sguw_ref, sgub_ref, gso_ref,
                 qa_ref, ka_ref, va_ref, qb_ref, kb_ref, vb_ref, oc_ref,
                 qd_ref, kd_ref, vd_ref, *, tm):
    x = x_ref[0]
    h = _rms(x, gmix_ref[...]).astype(BF)

    def proj(a, b):
        return _dot(h, w_ref[:, a:b])

    ca, sa = ca_ref[...], sa_ref[...]
    aq = _rope_apply(proj(C_AQ, C_AK), ca, sa, 8, 32, 0)
    qa_ref[0] = (aq * (32.0 ** -0.5 * LOG2E)).T.astype(BF)
    ka_ref[0] = _rope_apply(proj(C_AK, C_AV), ca, sa, 8, 32, 0).astype(BF)
    va_ref[0] = _value_rows(proj(C_AV, C_CQ).T)

    cb, sb = cb_ref[...], sb_ref[...]
    cq = _rms(proj(C_CQ, C_CKV), gq_ref[...]).astype(BF)
    qb = _rope_apply(_dot(cq, wuq_ref[...]), cb, sb, 32, 128, 64)
    qb_ref[0] = (qb * (96.0 ** -0.5 * LOG2E)).T.astype(BF)
    ckv = _rms(proj(C_CKV, C_KPE), gkv_ref[...]).astype(BF)
    kpe = _rope_apply(proj(C_KPE, C_ZC), cb, sb, 32, 128, 64)
    kb_ref[0] = (_dot(ckv, wuk_ref[...]) + kpe).astype(BF)
    vb_ref[0] = _value_rows(_dot(ckv, wuv_ref[...]).T)

    zc = jax.nn.gelu(proj(C_ZC, C_DQ))
    u = zc[:, :GROUP]
    vn = _rms(zc[:, GROUP:], gsv_ref[...]).astype(BF)
    r = lax.broadcasted_iota(jnp.int32, (128, 128), 0)
    c = lax.broadcasted_iota(jnp.int32, (128, 128), 1)
    wcat = jnp.concatenate(
        [jnp.where(r >= c, sguw_ref[g], 0.0).astype(BF) for g in range(4)], axis=1)
    lane = lax.broadcasted_iota(jnp.int32, (1, GROUP), 1)
    bias = sgub_ref[...]
    gso = gso_ref[...]
    for ch in range(tm // 128):
        vc = vn[ch * 128:(ch + 1) * 128]
        vst = jnp.concatenate(
            [jnp.where(lane // 64 == g, vc, jnp.zeros_like(vc)) for g in range(4)], axis=0)
        sv = _dot(wcat, vst) + bias
        oc = u[ch * 128:(ch + 1) * 128] * sv
        oc_ref[0, ch * 128:(ch + 1) * 128, :] = _rms(oc, gso).astype(BF)

    cd, sd = cd_ref[...], sd_ref[...]
    dq = _rope_apply(proj(C_DQ, C_DK), cd, sd, 16, 64, 0)
    dk = _rope_apply(proj(C_DK, C_DV), cd, sd, 16, 64, 0)
    dv = proj(C_DV, C_END)
    for hf in range(2):
        qd_ref[0, hf] = dq[:, hf * LANES:(hf + 1) * LANES] * 0.125
        kd_ref[0, hf] = dk[:, hf * LANES:(hf + 1) * LANES]
        vd_ref[0, hf] = dv[:, hf * LANES:(hf + 1) * LANES]


def _inproj(x, gmix, w_cat, tabs, gq, gkv, wuq, wuk, wuv, gsv, sguw, sgub, gso, *, tm):
    B, S, D = x.shape
    ns = S // tm
    full = lambda shape: pl.BlockSpec(shape, lambda s, b: (0,) * len(shape))
    tab = pl.BlockSpec((tm, LANES), lambda s, b: (s, 0))
    seq = lambda w: pl.BlockSpec((1, tm, w), lambda s, b: (b, s, 0))
    seq_t = lambda w: pl.BlockSpec((1, w, tm), lambda s, b: (b, 0, s))
    halves = pl.BlockSpec((1, 2, tm, LANES), lambda s, b: (b, 0, s, 0))
    out_w = [256, 256, 4 * VROWS, 512, 512, 4 * VROWS, 256]
    transposed = [True, False, True, True, False, True, False]
    return pl.pallas_call(
        functools.partial(_inproj_body, tm=tm),
        grid=(ns, B),
        in_specs=[seq(D), full((1, D)), full((D, C_END))] + [tab] * 6 + [
            full((1, 256)), full((1, 128)), full((256, 512)), full((128, 512)), full((128, 256)),
            full((1, 256)), full((4, 128, 128)), full((128, 256)), full((1, 256))],
        out_specs=[seq_t(w) if tr else seq(w) for w, tr in zip(out_w, transposed)] + [halves] * 3,
        out_shape=[jax.ShapeDtypeStruct((B, w, S) if tr else (B, S, w), BF)
                   for w, tr in zip(out_w, transposed)]
        + [jax.ShapeDtypeStruct((B, 2, S, LANES), F32)] * 3,
        compiler_params=pltpu.CompilerParams(
            dimension_semantics=("arbitrary", "arbitrary"), vmem_limit_bytes=VMEM_LIMIT),
        name="inproj",
    )(x, gmix, w_cat, *tabs, gq, gkv, wuq, wuk, wuv, gsv, sguw, sgub, gso)


def _head_expand(cols, lane, width):
    out = cols[-1]
    for h in range(len(cols) - 2, -1, -1):
        out = jnp.where(lane // width == h, cols[h], out)
    return out


def _vstack(v, lane):
    return jnp.concatenate(
        [jnp.where(lane // 64 == h, v, jnp.zeros_like(v)) for h in range(4)], axis=0)


DIFF_TILE = 1024
MLA_TILE = 1024
QB = 512


def _online_softmax_t(st, m_ref, j, cs):
    m_prev = m_ref[j:j + 1, cs]
    m_new = jnp.maximum(m_prev, jnp.max(st, axis=0, keepdims=True))
    alpha = jnp.exp2(m_prev - m_new)
    et = jnp.exp2(st - m_new)
    m_ref[j:j + 1, cs] = m_new
    return et.astype(BF), alpha


def _attn_jobs(ns, diag, heads):
    return [(*hd, qc, kc) for kc in range(ns) for qc in range(ns) for hd in heads
            if not (diag and kc > qc)]


def _causal_keep():
    return (lax.broadcasted_iota(jnp.int32, (QB, QB), 0)
            <= lax.broadcasted_iota(jnp.int32, (QB, QB), 1))


def _causal_pairs(n):
    pairs = [(i, j) for i in range(n) for j in range(i + 1)]
    return (jnp.asarray([p[0] for p in pairs], jnp.int32),
            jnp.asarray([p[1] for p in pairs], jnp.int32))


def _attn_a_body(qtab_ref, ktab_ref, lam_ref, gh_ref, q_ref, k_ref, v_ref, o_ref,
                 m_ref, acc_ref, *, t, lam_init):
    qi = qtab_ref[pl.program_id(1)]
    ki = ktab_ref[pl.program_id(1)]

    @pl.when(ki == 0)
    def _init():
        m_ref[...] = jnp.full(m_ref.shape, NEG, F32)
        acc_ref[...] = jnp.zeros(acc_ref.shape, F32)

    def step(diag):
        jobs = _attn_jobs(t // QB, diag, [(mp, h) for mp in range(2) for h in range(4)])
        kblk = [[k_ref[0, kc * QB:(kc + 1) * QB, 32 * j:32 * (j + 1)] for j in range(8)]
                for kc in range(t // QB)]
        keep = _causal_keep() if diag else None

        def qk(mp, h, qc, kc):
            j = 2 * h + mp
            return _dot(kblk[kc][j], q_ref[0, 32 * j:32 * (j + 1), qc * QB:(qc + 1) * QB])

        def pv(mp, h, qc, kc, pt, alpha):
            hs, cs = slice(VROWS * h, VROWS * (h + 1)), slice(qc * QB, (qc + 1) * QB)
            acc_ref[mp, hs, cs] = (acc_ref[mp, hs, cs] * alpha
                                   + _dot(v_ref[0, hs, kc * QB:(kc + 1) * QB], pt))

        st_next = qk(*jobs[0])
        pending = None
        for n, (mp, h, qc, kc) in enumerate(jobs):
            st = st_next
            if n + 1 < len(jobs):
                st_next = qk(*jobs[n + 1])
            if diag and kc == qc:
                st = jnp.where(keep, st, NEG)
            pt, alpha = _online_softmax_t(st, m_ref, 2 * h + mp, slice(qc * QB, (qc + 1) * QB))
            if pending is not None:
                pv(*pending)
            pending = (mp, h, qc, kc, pt, alpha)
        pv(*pending)

    @pl.when(ki < qi)
    def _off():
        step(False)

    @pl.when(ki == qi)
    def _diag():
        step(True)
        lf = lam_ref[...]
        lam = (jnp.exp(jnp.sum(lf[0:1] * lf[1:2], axis=-1, keepdims=True))
               - jnp.exp(jnp.sum(lf[2:3] * lf[3:4], axis=-1, keepdims=True)) + lam_init)
        rows = []
        for h in range(4):
            hs, ls = slice(VROWS * h, VROWS * h + 64), slice(VROWS * h + 64, VROWS * h + 65)
            oh = (acc_ref[0, hs, :] * (1.0 / acc_ref[0, ls, :])
                  - lam * (acc_ref[1, hs, :] * (1.0 / acc_ref[1, ls, :])))
            ms = jnp.mean(oh * oh, axis=0, keepdims=True)
            rows.append(oh * lax.rsqrt(ms + EPS))
        ot = jnp.concatenate(rows, axis=0)
        o_ref[0] = (ot.T * gh_ref[...] * (1.0 - lam_init)).astype(BF)


def _attn_a(lam_p, gh, q, k, v, *, t, lam_init):
    B, S, _ = k.shape
    n = S // t
    qtab, ktab = _causal_pairs(n)
    grid_spec = pltpu.PrefetchScalarGridSpec(
        num_scalar_prefetch=2,
        grid=(B, qtab.shape[0]),
        in_specs=[pl.BlockSpec((4, 32), lambda b, p, qt, kt: (0, 0)),
                  pl.BlockSpec((1, GROUP), lambda b, p, qt, kt: (0, 0)),
                  pl.BlockSpec((1, GROUP, t), lambda b, p, qt, kt: (b, 0, qt[p])),
                  pl.BlockSpec((1, t, GROUP), lambda b, p, qt, kt: (b, kt[p], 0)),
                  pl.BlockSpec((1, 4 * VROWS, t), lambda b, p, qt, kt: (b, 0, kt[p]))],
        out_specs=pl.BlockSpec((1, t, GROUP), lambda b, p, qt, kt: (b, qt[p], 0)),
        scratch_shapes=[pltpu.VMEM((8, t), F32),
                        pltpu.VMEM((2, 4 * VROWS, t), F32)])
    return pl.pallas_call(
        functools.partial(_attn_a_body, t=t, lam_init=lam_init),
        grid_spec=grid_spec,
        out_shape=jax.ShapeDtypeStruct((B, S, GROUP), BF),
        compiler_params=pltpu.CompilerParams(
            dimension_semantics=("arbitrary", "arbitrary"), vmem_limit_bytes=VMEM_LIMIT),
        name="attn_diff",
    )(qtab, ktab, lam_p, gh, q, k, v)


def _attn_b_body(qtab_ref, ktab_ref, g_ref, q_ref, k_ref, v_ref, o_ref, m_ref, acc_ref, *, t):
    qi = qtab_ref[pl.program_id(1)]
    ki = ktab_ref[pl.program_id(1)]

    @pl.when(ki == 0)
    def _init():
        m_ref[...] = jnp.full(m_ref.shape, NEG, F32)
        acc_ref[...] = jnp.zeros(acc_ref.shape, F32)

    def step(diag):
        jobs = _attn_jobs(t // QB, diag, [(h,) for h in range(4)])
        kblk = [k_ref[0, kc * QB:(kc + 1) * QB, :] for kc in range(t // QB)]
        keep = _causal_keep() if diag else None

        def qk(h, qc, kc):
            return _dot(kblk[kc][:, h * 128:(h + 1) * 128],
                        q_ref[0, h * 128:(h + 1) * 128, qc * QB:(qc + 1) * QB])

        def pv(h, qc, kc, pt, alpha):
            hs, cs = slice(VROWS * h, VROWS * (h + 1)), slice(qc * QB, (qc + 1) * QB)
            acc_ref[hs, cs] = acc_ref[hs, cs] * alpha + _dot(v_ref[0, hs, kc * QB:(kc + 1) * QB], pt)

        st_next = qk(*jobs[0])
        pending = None
        for n, (h, qc, kc) in enumerate(jobs):
            st = st_next
            if n + 1 < len(jobs):
                st_next = qk(*jobs[n + 1])
            if diag and kc == qc:
                st = jnp.where(keep, st, NEG)
            pt, alpha = _online_softmax_t(st, m_ref, h, slice(qc * QB, (qc + 1) * QB))
            if pending is not None:
                pv(*pending)
            pending = (h, qc, kc, pt, alpha)
        pv(*pending)

    @pl.when(ki < qi)
    def _off():
        step(False)

    @pl.when(ki == qi)
    def _diag():
        step(True)
        rows = [acc_ref[VROWS * h:VROWS * h + 64, :] * (1.0 / acc_ref[VROWS * h + 64:VROWS * h + 65, :])
                for h in range(4)]
        ot = jnp.concatenate(rows, axis=0)
        o_ref[0] = _rms(ot.T, g_ref[...]).astype(BF)


def _attn_b(g, q, k, v, *, t):
    B, S, _ = k.shape
    n = S // t
    qtab, ktab = _causal_pairs(n)
    grid_spec = pltpu.PrefetchScalarGridSpec(
        num_scalar_prefetch=2,
        grid=(B, qtab.shape[0]),
        in_specs=[pl.BlockSpec((1, GROUP), lambda b, p, qt, kt: (0, 0)),
                  pl.BlockSpec((1, 512, t), lambda b, p, qt, kt: (b, 0, qt[p])),
                  pl.BlockSpec((1, t, 512), lambda b, p, qt, kt: (b, kt[p], 0)),
                  pl.BlockSpec((1, 4 * VROWS, t), lambda b, p, qt, kt: (b, 0, kt[p]))],
        out_specs=pl.BlockSpec((1, t, GROUP), lambda b, p, qt, kt: (b, qt[p], 0)),
        scratch_shapes=[pltpu.VMEM((4, t), F32),
                        pltpu.VMEM((4 * VROWS, t), F32)])
    return pl.pallas_call(
        functools.partial(_attn_b_body, t=t),
        grid_spec=grid_spec,
        out_shape=jax.ShapeDtypeStruct((B, S, GROUP), BF),
        compiler_params=pltpu.CompilerParams(
            dimension_semantics=("arbitrary", "arbitrary"), vmem_limit_bytes=VMEM_LIMIT),
        name="attn_mla",
    )(qtab, ktab, g, q, k, v)


DIL_TILE = 2048
DILATIONS = (1, 4, 16)


def _dil_body(g_ref, q_ref, kp_ref, kc_ref, vp_ref, vc_ref, o_ref, ob_ref, lb_ref, *, td):
    i = pl.program_id(1)
    lane = lax.broadcasted_iota(jnp.int32, (1, GROUP), 1)
    a = lax.broadcasted_iota(jnp.int32, (128, 256), 0)
    c = lax.broadcasted_iota(jnp.int32, (128, 256), 1)
    dist = jnp.where(c - a >= 0, c - a, 1000)

    def rows(ref, start, d):
        idx = pl.ds(start, 128) if d == 1 else pl.ds(start, 128, stride=d)
        return jnp.concatenate([ref.at[0, 0][idx, :], ref.at[0, 1][idx, :]], axis=1).astype(BF)

    for bi, d in enumerate(DILATIONS):
        def block(n, carry, bi=bi, d=d):
            if d == 1:
                r, blk = 0, n
            elif td // d == 128:
                r, blk = n, 0
            else:
                r, blk = n % d, n // d
            q0 = r + d * 128 * blk
            if d == 1:
                q0 = pl.multiple_of(q0, 128)
            q = rows(q_ref, q0, d)
            k_hi, v_hi = rows(kc_ref, q0, d), rows(vc_ref, q0, d)
            lo_prev = td - d * 128 + r
            if isinstance(blk, int):
                first = True
                k_lo, v_lo = rows(kp_ref, lo_prev, d), rows(vp_ref, lo_prev, d)
            else:
                first = blk == 0
                lo_cur = jnp.maximum(q0 - d * 128, 0)
                if d == 1:
                    lo_cur = pl.multiple_of(lo_cur, 128)
                k_lo = jnp.where(first, rows(kp_ref, lo_prev, d), rows(kc_ref, lo_cur, d))
                v_lo = jnp.where(first, rows(vp_ref, lo_prev, d), rows(vc_ref, lo_cur, d))
            kwin = jnp.concatenate([k_lo, k_hi], axis=0)
            vst = _vstack(jnp.concatenate([v_lo, v_hi], axis=0), lane)
            cmin = jnp.where(jnp.logical_and(first, i == 0), 128, 0)
            ok = jnp.where(c >= cmin, dist, 1000) <= 128
            es, ils, lses = [], [], []
            for h in range(4):
                ps = slice((h // 2) * LANES, (h // 2 + 1) * LANES)
                qg = q[:, ps]
                s = _nt_dot(jnp.where(lane[:, ps] // 64 == h, qg, jnp.zeros_like(qg)), kwin[:, ps])
                s = jnp.where(ok, s, NEG)
                m = jnp.max(s, axis=-1, keepdims=True)
                ex = jnp.exp(s - m)
                den = jnp.sum(ex, axis=-1, keepdims=True)
                es.append(ex.astype(BF))
                ils.append(1.0 / den)
                lses.append(m + jnp.log(den))
            o = _dot(jnp.concatenate(es, axis=1), vst) * _head_expand(ils, lane, 64)
            lse = _head_expand(lses, lane, 64)
            idx = pl.ds(q0, 128) if d == 1 else pl.ds(q0, 128, stride=d)
            for hf in range(2):
                ob_ref.at[bi, hf][idx, :] = o[:, hf * LANES:(hf + 1) * LANES]
                lb_ref.at[bi, hf][idx, :] = lse[:, hf * LANES:(hf + 1) * LANES]
            return carry

        lax.fori_loop(0, td // 128, block, 0, unroll=8)

    for ch in range(td // 256):
        rs = slice(ch * 256, (ch + 1) * 256)
        halves = []
        for hf in range(2):
            ls = [lb_ref[bi, hf, rs, :] for bi in range(3)]
            mx = jnp.maximum(jnp.maximum(ls[0], ls[1]), ls[2])
            ws = [jnp.exp(l - mx) for l in ls]
            num = ws[0] * ob_ref[0, hf, rs, :] + ws[1] * ob_ref[1, hf, rs, :] + ws[2] * ob_ref[2, hf, rs, :]
            halves.append(num / (ws[0] + ws[1] + ws[2]))
        o_ref[0, rs, :] = _rms(jnp.concatenate(halves, axis=1), g_ref[...]).astype(BF)


def _dil(g, q, k, v, *, td):
    B, _, S, _ = q.shape
    cur = pl.BlockSpec((1, 2, td, LANES), lambda b, i: (b, 0, i, 0))
    prev = pl.BlockSpec((1, 2, td, LANES), lambda b, i: (b, 0, jnp.maximum(i - 1, 0), 0))
    return pl.pallas_call(
        functools.partial(_dil_body, td=td),
        grid=(B, S // td),
        in_specs=[pl.BlockSpec((1, GROUP), lambda b, i: (0, 0)), cur, prev, cur, prev, cur],
        out_specs=pl.BlockSpec((1, td, GROUP), lambda b, i: (b, i, 0)),
        out_shape=jax.ShapeDtypeStruct((B, S, GROUP), BF),
        scratch_shapes=[pltpu.VMEM((3, 2, td, LANES), F32),
                        pltpu.VMEM((3, 2, td, LANES), F32)],
        compiler_params=pltpu.CompilerParams(
            dimension_semantics=("arbitrary", "arbitrary"), vmem_limit_bytes=VMEM_LIMIT),
        name="dilated",
    )(g, q, k, k, v, v)


PEER_EB = 2048
GELU_C1 = math.sqrt(2.0 / math.pi)
GELU_C2 = 0.044715 * GELU_C1
N_CAND = PEER_TOP + 1


def _peer_body(x_ref, oa_ref, ob_ref, oc_ref, od_ref, wo_ref, g_ref, wqt_ref, sk_ref, u_ref, vt_ref,
               gf_ref, o_ref, x2_ref, ht_ref, st_ref, e2_ref, th_ref, cw_ref, hw_ref, acc_ref,
               *, tt, final):
    e = pl.program_id(1)
    nch = tt // LANES

    @pl.when(e == 0)
    def _prologue():
        y = _dot(oa_ref[...], wo_ref[0:256, :])
        y += _dot(ob_ref[...], wo_ref[256:512, :])
        y += _dot(oc_ref[...], wo_ref[512:768, :])
        y += _dot(od_ref[...], wo_ref[768:1024, :])
        x2_ref[...] = x_ref[...] + y
        h = _rms(x2_ref[...], g_ref[...])
        ht_ref[...] = h.T.astype(BF)
        qt = _dot(wqt_ref[...], ht_ref[...]).astype(BF)
        for hh in range(PEER_HEADS):
            for c in range(2):
                r0 = hh * N_KEYS + c * 64
                sc = _dot(sk_ref[c], qt[r0:r0 + 64, :])
                for tc in range(nch):
                    blk = sc[:, tc * LANES:(tc + 1) * LANES]
                    st_ref.at[tc][pl.ds(c * 1024 + hh, N_KEYS, stride=PEER_HEADS), :] = blk
                    if c == 1:
                        st_ref[tc, 2048 + hh * N_KEYS:2048 + (hh + 1) * N_KEYS, :] = blk
        acc_ref[...] = jnp.zeros(acc_ref.shape, F32)

        def top_vals(load, n_rows, n_out):
            top = []
            for r in range(n_rows):
                v = load(r)
                for q in range(len(top)):
                    top[q], v = jnp.maximum(top[q], v), jnp.minimum(top[q], v)
                if len(top) < n_out:
                    top.append(v)
            return top

        def token_chunk(tc, carry):
            ls = pl.ds(pl.multiple_of(tc * LANES, LANES), LANES)
            v1 = top_vals(lambda r: st_ref[tc, r * 8:(r + 1) * 8, :], N_KEYS, N_CAND)
            v2 = top_vals(lambda r: st_ref[tc, 1024 + r * 8:1024 + (r + 1) * 8, :], N_KEYS, N_CAND)
            cands = [v1[a] + v2[b] for a in range(N_CAND) for b in range(N_CAND)
                     if (a + 1) * (b + 1) <= N_CAND]
            top = top_vals(lambda r: cands[r], len(cands), N_CAND)
            z = jnp.ones_like(top[0])
            for r in range(1, PEER_TOP):
                z = z + jnp.exp(top[r] - top[0])
            tau = 0.5 * (top[PEER_TOP - 1] + top[PEER_TOP])
            iz = 1.0 / z
            for r in range(N_KEYS):
                s1 = st_ref[tc, r * 8:(r + 1) * 8, :]
                th_ref[r, :, ls] = tau - s1
                cw_ref[r, :, ls] = jnp.exp(s1 - v1[0]) * (0.5 * iz)
            for hh in range(PEER_HEADS):
                rows = slice(2048 + hh * 128, 2048 + (hh + 1) * 128)
                e2_ref[hh * 128:(hh + 1) * 128, ls] = jnp.exp(st_ref[tc, rows, :] - v2[0][hh:hh + 1, :])
            return carry

        lax.fori_loop(0, nch, token_chunk, 0)

    nblk = PEER_EB // 256

    def pre_act(b):
        return _dot(u_ref[b * 256:(b + 1) * 256, :], ht_ref[...])

    def gate(b, a):
        i0 = e * (PEER_EB // N_KEYS) + 2 * b
        th = [th_ref[i0], th_ref[i0 + 1]]
        cw = [cw_ref[i0], cw_ref[i0 + 1]]
        for tc in range(nch):
            ls = slice(tc * LANES, (tc + 1) * LANES)
            for sub in range(2):
                w = [None, None]
                for hh in range(PEER_HEADS):
                    r0 = hh * N_KEYS + sub * 64
                    s2 = st_ref[tc, 2048 + r0:2048 + r0 + 64, :]
                    e2 = e2_ref[r0:r0 + 64, ls]
                    for half in range(2):
                        sel = jnp.where(s2 >= th[half][hh:hh + 1, ls], e2, 0.0) * cw[half][hh:hh + 1, ls]
                        w[half] = sel if w[half] is None else w[half] + sel
                for half in range(2):
                    r1 = half * N_KEYS + sub * 64
                    x = a[r1:r1 + 64, ls]
                    xw = x * w[half]
                    t = jnp.tanh(x * (GELU_C1 + GELU_C2 * (x * x)))
                    hw_ref[b % 2, r1:r1 + 64, ls] = (xw + xw * t).astype(BF)

    a = pre_act(0)
    for b in range(nblk):
        a_next = pre_act(b + 1) if b + 1 < nblk else None
        gate(b, a)
        if b > 0:
            acc_ref[...] += _dot(vt_ref[b - 1], hw_ref[(b - 1) % 2])
        a = a_next
    acc_ref[...] += _dot(vt_ref[nblk - 1], hw_ref[(nblk - 1) % 2])

    @pl.when(e == pl.num_programs(1) - 1)
    def _finish():
        y = x2_ref[...] + acc_ref[...].T
        if final:
            y = _rms(y, gf_ref[...])
        o_ref[...] = y


def _peer(x, oa, ob, oc, od, w_out, g, wqt, sk, u, vt, gf, *, tt, final):
    T, D = x.shape
    ne = u.shape[0] // PEER_EB
    const = lambda shape: pl.BlockSpec(shape, lambda i, e: (0,) * len(shape),
                                       pipeline_mode=pl.Buffered(1))
    return pl.pallas_call(
        functools.partial(_peer_body, tt=tt, final=final),
        grid=(T // tt, ne),
        in_specs=[pl.BlockSpec((tt, D), lambda i, e: (i, 0))]
        + [pl.BlockSpec((tt, GROUP), lambda i, e: (i, 0))] * 4
        + [const((D, D)), const((1, D)), const((D, D)), const((2, N_KEYS, 64)),
                  pl.BlockSpec((PEER_EB, D), lambda i, e: (e, 0)),
                  pl.BlockSpec((PEER_EB // 256, D, 256), lambda i, e: (e, 0, 0)),
                  const((1, D))],
        out_specs=pl.BlockSpec((tt, D), lambda i, e: (i, 0)),
        out_shape=jax.ShapeDtypeStruct((T, D), F32),
        scratch_shapes=[pltpu.VMEM((tt, D), F32),
                        pltpu.VMEM((D, tt), BF),
                        pltpu.VMEM((tt // LANES, 3 * 1024, LANES), F32),
                        pltpu.VMEM((1024, tt), F32),
                        pltpu.VMEM((N_KEYS, PEER_HEADS, tt), F32),
                        pltpu.VMEM((N_KEYS, PEER_HEADS, tt), F32),
                        pltpu.VMEM((2, 256, tt), BF),
                        pltpu.VMEM((D, tt), F32)],
        compiler_params=pltpu.CompilerParams(
            dimension_semantics=("arbitrary", "arbitrary"), vmem_limit_bytes=VMEM_LIMIT),
        name="peer",
    )(x, oa, ob, oc, od, w_out, g, wqt, sk, u, vt, gf)


def _pack_in_weights(w_in):
    aq, ak, av, cq, ckv, kpe, zc, dq, dk, dv = jnp.split(
        w_in, np.cumsum([256, 256, 256, 256, 128, 32, 512, 256, 256])[:].tolist(), axis=1)
    kpe_rep = jnp.zeros((D_MODEL, 4, 128), w_in.dtype).at[:, :, 64:96].set(kpe[:, None, :])
    return jnp.concatenate(
        [aq, ak, av, cq, ckv, kpe_rep.reshape(D_MODEL, 512), zc, dq, dk, dv], axis=1).astype(BF)


def _pack_mla_weights(w_uq, w_ukv):
    uq = jnp.pad(w_uq.reshape(256, 4, 96), ((0, 0), (0, 0), (0, 32))).reshape(256, 512)
    ukv = w_ukv.reshape(128, 4, 128)
    uk = jnp.pad(ukv[:, :, :64], ((0, 0), (0, 0), (0, 64))).reshape(128, 512)
    uv = ukv[:, :, 64:].reshape(128, 256)
    return uq.astype(BF), uk.astype(BF), uv.astype(BF)


def _pack_expert_out(v):
    n, d = v.shape
    return v.reshape(n // 256, 256, d).transpose(0, 2, 1).astype(BF)


def kernel(x, w_in, w_out, norm_mix, norm_ffn, diff_lambda, diff_head_norm, mla_q_norm, mla_kv_norm, mla_w_uq, mla_w_ukv, mla_out_norm, sgu_v_norm, sgu_w, sgu_b, sgu_out_norm, dil_out_norm, peer_w_q, peer_sub_keys, peer_u, peer_v, final_norm):
    B, S, D = x.shape
    depth = w_in.shape[0]
    T = B * S
    tm = min(INPROJ_TILE, S)
    ta, tb = min(DIFF_TILE, S), min(MLA_TILE, S)
    tt = min(PEER_TOKENS, T)
    assert S % tm == 0 and S % ta == 0 and S % tb == 0 and S % DIL_TILE == 0 and T % tt == 0, (B, S)
    assert ta % QB == 0 and tb % QB == 0 and tt % LANES == 0 and D == D_MODEL
    row = lambda a: a.reshape(1, -1)

    tabs = (_rope_tables(S, 8, 32, 0) + _rope_tables(S, 32, 128, 64) + _rope_tables(S, 16, 64, 0))

    for l in range(depth):
        lam_init = 0.8 - 0.6 * math.exp(-0.3 * l)
        w_cat = _pack_in_weights(w_in[l])
        wuq, wuk, wuv = _pack_mla_weights(mla_w_uq[l], mla_w_ukv[l])
        sgub = jnp.repeat(sgu_b[l].T, 64, axis=1)
        qa, ka, va, qb, kb, vb, oc, qd, kd, vd = _inproj(
            x, row(norm_mix[l]), w_cat, tabs, row(mla_q_norm[l]), row(mla_kv_norm[l]),
            wuq, wuk, wuv, row(sgu_v_norm[l]), sgu_w[l], sgub, row(sgu_out_norm[l]), tm=tm)

        oa = _attn_a(diff_lambda[l], row(diff_head_norm[l]), qa, ka, va, t=ta, lam_init=lam_init)
        ob = _attn_b(row(mla_out_norm[l]), qb, kb, vb, t=tb)

        od = _dil(row(dil_out_norm[l]), qd, kd, vd, td=DIL_TILE)

        x2 = _peer(x.reshape(T, D), oa.reshape(T, GROUP), ob.reshape(T, GROUP), oc.reshape(T, GROUP),
                   od.reshape(T, GROUP), w_out[l].astype(BF), row(norm_ffn[l]), peer_w_q[l].T.astype(BF), peer_sub_keys[l].astype(BF),
                   peer_u[l].astype(BF), _pack_expert_out(peer_v[l]), row(final_norm),
                   tt=tt, final=(l == depth - 1))
        x = x2.reshape(B, S, D)
    return x
```
